```python
import jax, jax.numpy as jnp
from jax import lax
import numpy as np

D_MODEL = 4096
BATCH = 2
SEQ = 4096
DEPTH = 2

CHUNK = 64
MIX_WIDTH = D_MODEL
SGU_BLOCK = 128
SGU_WIDTH = MIX_WIDTH // 4
SGU_HEAD_DIM = 128
SGU_HEADS = SGU_WIDTH // SGU_HEAD_DIM
MLA_NOPE_DIM = 128
MLA_ROPE_DIM = 64
MLA_V_DIM = 128
MLA_WIDTH = MIX_WIDTH // 2
MLA_HEADS = MLA_WIDTH // MLA_V_DIM
MLA_Q_RANK = D_MODEL // 4
MLA_KV_RANK = D_MODEL // 8
Q_BLOCK = 128
RET_WIDTH = MIX_WIDTH // 4
RET_V_DIM = 128
RET_HEADS = RET_WIDTH // RET_V_DIM
RET_QK_DIM = RET_V_DIM // 2
FF_DIM = 4 * D_MODEL
ROPE_THETA = 10000.0
NORM_EPS = 1e-6

IN_SIZES = (SGU_WIDTH, SGU_WIDTH,
            MLA_Q_RANK, MLA_KV_RANK, MLA_ROPE_DIM,
            RET_HEADS * RET_QK_DIM, RET_HEADS * RET_QK_DIM, RET_WIDTH, RET_WIDTH)
IN_WIDTH = sum(IN_SIZES)

kernel_name = "hybrid_sgu_mla_retention_sandwich"


def rms_norm(x, g):
    xf = x.astype(jnp.float32)
    y = xf * lax.rsqrt(jnp.mean(xf * xf, axis=-1, keepdims=True) + NORM_EPS)
    return (y * g.astype(jnp.float32)).astype(x.dtype)


def layer_norm(x, g, b):
    xf = x.astype(jnp.float32)
    mu = jnp.mean(xf, axis=-1, keepdims=True)
    xc = xf - mu
    y = xc * lax.rsqrt(jnp.mean(xc * xc, axis=-1, keepdims=True) + NORM_EPS)
    return (y * g.astype(jnp.float32) + b.astype(jnp.float32)).astype(x.dtype)


def rope_tables(seq_len, dim):
    inv_freq = 1.0 / (ROPE_THETA ** (jnp.arange(0, dim, 2, dtype=jnp.float32) / dim))
    ang = jnp.arange(seq_len, dtype=jnp.float32)[:, None] * inv_freq[None, :]
    return jnp.cos(ang), jnp.sin(ang)


def apply_rope(x, cos, sin):
    x1, x2 = jnp.split(x, 2, axis=-1)
    c = cos[None, :, None, :].astype(x.dtype)
    s = sin[None, :, None, :].astype(x.dtype)
    return jnp.concatenate([x1 * c - x2 * s, x1 * s + x2 * c], axis=-1)


def sgu_mixer(u, v, ln_g, ln_b, w_s, b_s):
    bsz, s_len, _ = u.shape
    v = layer_norm(v.reshape(bsz, s_len, SGU_HEADS, SGU_HEAD_DIM),
                   ln_g.reshape(SGU_HEADS, SGU_HEAD_DIM), ln_b.reshape(SGU_HEADS, SGU_HEAD_DIM))
    v = v.reshape(bsz, s_len // SGU_BLOCK, SGU_BLOCK, SGU_HEADS, SGU_HEAD_DIM)
    pos_chunk = jnp.arange(SGU_BLOCK) // CHUNK
    mask = pos_chunk[None, :] <= pos_chunk[:, None]
    w = jnp.where(mask[None], w_s, 0.0).astype(v.dtype)
    mixed = jnp.einsum('gij,bnjgc->bnigc', w, v) + b_s.T[None, None, :, :, None].astype(v.dtype)
    return u * mixed.reshape(bsz, s_len, SGU_WIDTH)


def mla_mixer(c_q, c_kv, k_rope, q_norm, wq_b, kv_norm, wkv_b, cos, sin):
    bsz, s_len, _ = c_q.shape
    q = (rms_norm(c_q, q_norm) @ wq_b).reshape(bsz, s_len, MLA_HEADS, MLA_NOPE_DIM + MLA_ROPE_DIM)
    q_nope = q[..., :MLA_NOPE_DIM]
    q_rope = apply_rope(q[..., MLA_NOPE_DIM:], cos, sin)
    kv = (rms_norm(c_kv, kv_norm) @ wkv_b).reshape(bsz, s_len, MLA_HEADS, MLA_NOPE_DIM + MLA_V_DIM)
    k_nope = kv[..., :MLA_NOPE_DIM]
    val = kv[..., MLA_NOPE_DIM:]
    k_rope = apply_rope(k_rope[:, :, None, :], cos, sin)[:, :, 0, :]
    scale = (MLA_NOPE_DIM + MLA_ROPE_DIM) ** -0.5
    n_blk = s_len // Q_BLOCK
    qn_blocks = q_nope.reshape(bsz, n_blk, Q_BLOCK, MLA_HEADS, MLA_NOPE_DIM).transpose(1, 0, 2, 3, 4)
    qr_blocks = q_rope.reshape(bsz, n_blk, Q_BLOCK, MLA_HEADS, MLA_ROPE_DIM).transpose(1, 0, 2, 3, 4)
    k_chunk = jnp.arange(s_len) // CHUNK

    def attend(args):
        qn, qr, blk = args
        s = (jnp.einsum('bqhd,bkhd->bhqk', qn, k_nope)
             + jnp.einsum('bqhr,bkr->bhqk', qr, k_rope)).astype(jnp.float32) * scale
        q_chunk = (blk * Q_BLOCK + jnp.arange(Q_BLOCK)) // CHUNK
        mask = k_chunk[None, :] <= q_chunk[:, None]
        s = jnp.where(mask[None, None], s, -jnp.inf)
        p = jax.nn.softmax(s, axis=-1).astype(val.dtype)
        return jnp.einsum('bhqk,bkhd->bqhd', p, val)

    out = lax.map(attend, (qn_blocks, qr_blocks, jnp.arange(n_blk)))
    return out.transpose(1, 0, 2, 3, 4).reshape(bsz, s_len, MLA_WIDTH)


def retention_mixer(q, k, v, gate, cos, sin):
    bsz, s_len, _ = q.shape
    n_chunk = s_len // CHUNK
    dt = q.dtype
    q = apply_rope(q.reshape(bsz, s_len, RET_HEADS, RET_QK_DIM), cos, sin) * (RET_QK_DIM ** -0.5)
    k = apply_rope(k.reshape(bsz, s_len, RET_HEADS, RET_QK_DIM), cos, sin)
    qc = q.reshape(bsz, n_chunk, CHUNK, RET_HEADS, RET_QK_DIM)
    kc = k.reshape(bsz, n_chunk, CHUNK, RET_HEADS, RET_QK_DIM)
    vc = v.reshape(bsz, n_chunk, CHUNK, RET_HEADS, RET_V_DIM)
    log_gamma = jnp.log1p(-jnp.exp2(-5.0 - jnp.arange(RET_HEADS, dtype=jnp.float32)))
    idx = jnp.arange(CHUNK, dtype=jnp.float32)
    intra_decay = jnp.exp(log_gamma[:, None, None] * jnp.abs(idx[:, None] - idx[None, :]))
    q_decay = jnp.exp(log_gamma[None, :] * (idx[:, None] + 1.0))
    k_decay = jnp.exp(log_gamma[None, :] * (CHUNK - 1.0 - idx[:, None]))
    chunk_decay = jnp.exp(log_gamma * CHUNK)
    scores = jnp.einsum('bclhd,bcmhd->bchlm', qc, kc) * intra_decay.astype(dt)
    intra_out = jnp.einsum('bchlm,bcmhe->bclhe', scores, vc)
    kv_chunks = jnp.einsum('bclhd,bclhe->cbhde', kc * k_decay[None, None, :, :, None].astype(dt), vc)

    def step(state, kv_c):
        return chunk_decay[None, :, None, None] * state + kv_c, state

    init = jnp.zeros((bsz, RET_HEADS, RET_QK_DIM, RET_V_DIM), jnp.float32)
    _, prev_states = lax.scan(step, init, kv_chunks.astype(jnp.float32))
    cross_out = jnp.einsum('bclhd,cbhde->bclhe', qc * q_decay[None, None, :, :, None].astype(dt),
                           prev_states.astype(dt))
    o = (intra_out + cross_out).reshape(bsz, s_len, RET_HEADS, RET_V_DIM).astype(jnp.float32)
    o = o * lax.rsqrt(jnp.mean(o * o, axis=-1, keepdims=True) + NORM_EPS)
    return jax.nn.silu(gate) * o.reshape(bsz, s_len, RET_WIDTH).astype(dt)


def hybrid_layer(x, n_mix_pre, n_mix_post, n_ffn_pre, n_ffn_post, w_in,
                 sgu_ln_g, sgu_ln_b, sgu_w, sgu_b,
                 mla_q_norm, mla_wq_b, mla_kv_norm, mla_wkv_b,
                 w_out, w_up, w_down, cos, sin):
    h = rms_norm(x, n_mix_pre)
    proj = h @ w_in
    offsets = [int(o) for o in np.cumsum(IN_SIZES)[:-1]]
    u, v, c_q, c_kv, k_rope, r_q, r_k, r_v, r_g = jnp.split(proj, offsets, axis=-1)
    out_a = sgu_mixer(u, v, sgu_ln_g, sgu_ln_b, sgu_w, sgu_b)
    out_b = mla_mixer(c_q, c_kv, k_rope, mla_q_norm, mla_wq_b, mla_kv_norm, mla_wkv_b, cos, sin)
    out_c = retention_mixer(r_q, r_k, r_v, r_g, cos, sin)
    mixed = jnp.concatenate([out_a, out_b, out_c], axis=-1) @ w_out
    x = x + rms_norm(mixed, n_mix_post)
    h = rms_norm(x, n_ffn_pre)
    f = jnp.square(jax.nn.relu(h @ w_up)) @ w_down
    return x + rms_norm(f, n_ffn_post)


def setup_inputs(seed: int = 0) -> dict:
    key = jax.random.key(seed)
    ks = jax.random.split(key, 20)

    def nrm(k, shape, scale):
        return jax.random.normal(k, shape, jnp.float32) * scale

    L = DEPTH
    return {
        "x": nrm(ks[0], (BATCH, SEQ, D_MODEL), 1.0),
        "norm_mix_pre": 1.0 + nrm(ks[1], (L, D_MODEL), 0.02),
        "norm_mix_post": 1.0 + nrm(ks[2], (L, D_MODEL), 0.02),
        "norm_ffn_pre": 1.0 + nrm(ks[3], (L, D_MODEL), 0.02),
        "norm_ffn_post": 1.0 + nrm(ks[4], (L, D_MODEL), 0.02),
        "w_in": nrm(ks[5], (L, D_MODEL, IN_WIDTH), D_MODEL ** -0.5),
        "sgu_ln_g": 1.0 + nrm(ks[6], (L, SGU_WIDTH), 0.02),
        "sgu_ln_b": nrm(ks[7], (L, SGU_WIDTH), 0.02),
        "sgu_w": nrm(ks[8], (L, SGU_HEADS, SGU_BLOCK, SGU_BLOCK), SGU_BLOCK ** -0.5),
        "sgu_b": 1.0 + nrm(ks[9], (L, SGU_HEADS, SGU_BLOCK), 0.1),
        "mla_q_norm": 1.0 + nrm(ks[10], (L, MLA_Q_RANK), 0.02),
        "mla_wq_b": nrm(ks[11], (L, MLA_Q_RANK, MLA_HEADS * (MLA_NOPE_DIM + MLA_ROPE_DIM)), MLA_Q_RANK ** -0.5),
        "mla_kv_norm": 1.0 + nrm(ks[12], (L, MLA_KV_RANK), 0.02),
        "mla_wkv_b": nrm(ks[13], (L, MLA_KV_RANK, MLA_HEADS * (MLA_NOPE_DIM + MLA_V_DIM)), MLA_KV_RANK ** -0.5),
        "w_out": nrm(ks[14], (L, MIX_WIDTH, D_MODEL), MIX_WIDTH ** -0.5),
        "w_up": nrm(ks[15], (L, D_MODEL, FF_DIM), D_MODEL ** -0.5),
        "w_down": nrm(ks[16], (L, FF_DIM, D_MODEL), FF_DIM ** -0.5),
    }


def reference(x, norm_mix_pre, norm_mix_post, norm_ffn_pre, norm_ffn_post, w_in,
              sgu_ln_g, sgu_ln_b, sgu_w, sgu_b,
              mla_q_norm, mla_wq_b, mla_kv_norm, mla_wkv_b,
              w_out, w_up, w_down):
    cos, sin = rope_tables(x.shape[1], MLA_ROPE_DIM)
    for l in range(DEPTH):
        x = hybrid_layer(x, norm_mix_pre[l], norm_mix_post[l], norm_ffn_pre[l], norm_ffn_post[l], w_in[l],
                         sgu_ln_g[l], sgu_ln_b[l], sgu_w[l], sgu_b[l],
                         mla_q_norm[l], mla_wq_b[l], mla_kv_norm[l], mla_wkv_b[l],
                         w_out[l], w_up[l], w_down[l], cos, sin)
    return x
```

```python
import functools
import math

import numpy as np
import jax
import jax.numpy as jnp
from jax import lax
from jax.experimental import pallas as pl
from jax.experimental.pallas import tpu as pltpu

F32 = jnp.float32
BF16 = jnp.bfloat16

D_MODEL = 4096
CHUNK = 64
EPS = 1e-6
ROPE_THETA = 10000.0
SGU_BLOCK = 128
SGU_WIDTH = 1024
SGU_HEADS = 8
MLA_HEADS = 16
MLA_NOPE = 128
MLA_ROPE = 64
MLA_V = 128
MLA_Q_RANK = 1024
MLA_KV_RANK = 512
MLA_WIDTH = MLA_HEADS * MLA_V
RET_HEADS = 8
RET_QK = 64
RET_V = 128
RET_WIDTH = RET_HEADS * RET_V
FF_DIM = 4 * D_MODEL

LANES = 128
VMEM_CAP = 64 * 1024 * 1024

IN_TILE = 1024
COL_U = 0 * IN_TILE
COL_V = 1 * IN_TILE
COL_CQ = 2 * IN_TILE
COL_RV = 3 * IN_TILE
COL_RG = 4 * IN_TILE
COL_CKV = 5 * IN_TILE
COL_KR = COL_CKV + MLA_KV_RANK
COL_RQ = 6 * IN_TILE
COL_RK = COL_RQ + RET_HEADS * RET_QK
IN_WIDTH_PADDED = 7 * IN_TILE
ROPE_MIXED_TILE = 5
ROPE_FULL_TILE = 6

TM = 1024
TM_NORM = 256
TM_FFN = 512
TF_FFN = 512
TB_SGU = 512
TQ = 512
RET_L = 256


def _vmem_limit(nbytes):
    return int(min(VMEM_CAP - (2 << 20), nbytes))


def _params(sem, nbytes):
    return pltpu.CompilerParams(dimension_semantics=sem, vmem_limit_bytes=_vmem_limit(nbytes))


def _rms(x, g):
    return x * lax.rsqrt(jnp.mean(x * x, axis=-1, keepdims=True) + EPS) * g


def _rmsnorm_kernel(x_ref, g_ref, o_ref):
    o_ref[...] = _rms(x_ref[...], g_ref[...]).astype(o_ref.dtype)


def rmsnorm_rows(x, g):
    t, d = x.shape
    return pl.pallas_call(
        _rmsnorm_kernel,
        out_shape=jax.ShapeDtypeStruct((t, d), BF16),
        grid=(t // TM_NORM,),
        in_specs=[pl.BlockSpec((TM_NORM, d), lambda i: (i, 0)),
                  pl.BlockSpec((1, d), lambda i: (0, 0))],
        out_specs=pl.BlockSpec((TM_NORM, d), lambda i: (i, 0)),
        compiler_params=_params(("parallel",), 32 << 20),
        name="rmsnorm_rows",
    )(x, g.reshape(1, d))


def _residual_norm_kernel(x_ref, y_ref, gp_ref, gn_ref, xo_ref, ho_ref):
    xn = x_ref[...] + _rms(y_ref[...], gp_ref[...])
    xo_ref[...] = xn
    ho_ref[...] = _rms(xn, gn_ref[...]).astype(ho_ref.dtype)


def _residual_kernel(x_ref, y_ref, gp_ref, xo_ref):
    xo_ref[...] = x_ref[...] + _rms(y_ref[...], gp_ref[...])


def residual_norm(x, y, g_post, g_next):
    t, d = x.shape
    row = pl.BlockSpec((TM_NORM, d), lambda i: (i, 0))
    vec = pl.BlockSpec((1, d), lambda i: (0, 0))
    if g_next is None:
        return pl.pallas_call(
            _residual_kernel,
            out_shape=jax.ShapeDtypeStruct((t, d), F32),
            grid=(t // TM_NORM,),
            in_specs=[row, row, vec],
            out_specs=row,
            compiler_params=_params(("parallel",), 40 << 20),
            name="residual",
        )(x, y, g_post.reshape(1, d)), None
    return pl.pallas_call(
        _residual_norm_kernel,
        out_shape=(jax.ShapeDtypeStruct((t, d), F32), jax.ShapeDtypeStruct((t, d), BF16)),
        grid=(t // TM_NORM,),
        in_specs=[row, row, vec, vec],
        out_specs=(row, row),
        compiler_params=_params(("parallel",), 40 << 20),
        name="residual_norm",
    )(x, y, g_post.reshape(1, d), g_next.reshape(1, d))


def _rope_cols(x, cos, sin):
    lane = lax.broadcasted_iota(jnp.int32, x.shape, 1)
    first_half = (lane % MLA_ROPE) < (MLA_ROPE // 2)
    partner = jnp.where(first_half, pltpu.roll(x, LANES - MLA_ROPE // 2, 1), pltpu.roll(x, MLA_ROPE // 2, 1))
    return x * cos + partner * sin


def _in_proj_kernel(h_ref, w_ref, cos_ref, sin_ref, o_ref):
    j = pl.program_id(1)
    acc = jnp.dot(h_ref[...], w_ref[...], preferred_element_type=F32)

    @pl.when(j < ROPE_MIXED_TILE)
    def _():
        o_ref[...] = acc.astype(o_ref.dtype)

    @pl.when(j == ROPE_MIXED_TILE)
    def _():
        lo = COL_KR - COL_CKV
        o_ref[:, :lo] = acc[:, :lo].astype(o_ref.dtype)
        for c in range(lo // LANES, lo // LANES + 2):
            sl = slice(c * LANES, (c + 1) * LANES)
            o_ref[:, sl] = _rope_cols(acc[:, sl], cos_ref[...], sin_ref[...]).astype(o_ref.dtype)
        o_ref[:, lo + 2 * LANES:] = acc[:, lo + 2 * LANES:].astype(o_ref.dtype)

    @pl.when(j == ROPE_FULL_TILE)
    def _():
        q_cols = (RET_HEADS * RET_QK) // LANES
        for c in range(IN_TILE // LANES):
            sl = slice(c * LANES, (c + 1) * LANES)
            r = _rope_cols(acc[:, sl], cos_ref[...], sin_ref[...])
            if c < q_cols:
                r = r * (RET_QK ** -0.5)
            o_ref[:, sl] = r.astype(o_ref.dtype)


def in_proj(h, w, cos_t, sin_t, seq):
    t, d = h.shape
    n = w.shape[1]
    pos_blocks = seq // TM
    tab = pl.BlockSpec((TM, LANES), lambda i, j: (i % pos_blocks, 0))
    return pl.pallas_call(
        _in_proj_kernel,
        out_shape=jax.ShapeDtypeStruct((t, n), BF16),
        grid=(t // TM, n // IN_TILE),
        in_specs=[pl.BlockSpec((TM, d), lambda i, j: (i, 0)),
                  pl.BlockSpec((d, IN_TILE), lambda i, j: (0, j)),
                  tab, tab],
        out_specs=pl.BlockSpec((TM, IN_TILE), lambda i, j: (i, j)),
        compiler_params=_params(("parallel", "arbitrary"), 52 << 20),
        name="in_proj",
    )(h, w, cos_t, sin_t)


def _sgu_kernel(u_ref, v_ref, g_ref, b_ref, w_ref, bias_ref, o_ref):
    row = lax.broadcasted_iota(jnp.int32, (SGU_BLOCK, SGU_BLOCK), 0)
    col = lax.broadcasted_iota(jnp.int32, (SGU_BLOCK, SGU_BLOCK), 1)
    keep = (col // CHUNK) <= (row // CHUNK)
    for g in range(SGU_HEADS):
        sl = slice(g * LANES, (g + 1) * LANES)
        wg = jnp.where(keep, w_ref[g], 0.0).astype(BF16)
        vg = v_ref[:, sl].astype(F32)
        mu = jnp.mean(vg, axis=-1, keepdims=True)
        vc = vg - mu
        y = vc * lax.rsqrt(jnp.mean(vc * vc, axis=-1, keepdims=True) + EPS) * g_ref[:, sl] + b_ref[:, sl]
        yb = y.astype(BF16)
        for n in range(TB_SGU // SGU_BLOCK):
            rs = slice(n * SGU_BLOCK, (n + 1) * SGU_BLOCK)
            mixed = jnp.dot(wg, yb[rs, :], preferred_element_type=F32) + bias_ref[:, sl]
            o_ref[rs, sl] = (u_ref[rs, sl].astype(F32) * mixed).astype(o_ref.dtype)


def sgu_mixer(proj, ln_g, ln_b, w_s, bias_full):
    t = proj.shape[0]
    vec = pl.BlockSpec((1, SGU_WIDTH), lambda i: (0, 0))
    return pl.pallas_call(
        _sgu_kernel,
        out_shape=jax.ShapeDtypeStruct((t, SGU_WIDTH), BF16),
        grid=(t // TB_SGU,),
        in_specs=[pl.BlockSpec((TB_SGU, SGU_WIDTH), lambda i: (i, COL_U // SGU_WIDTH)),
                  pl.BlockSpec((TB_SGU, SGU_WIDTH), lambda i: (i, COL_V // SGU_WIDTH)),
                  vec, vec,
                  pl.BlockSpec((SGU_HEADS, SGU_BLOCK, SGU_BLOCK), lambda i: (0, 0, 0)),
                  pl.BlockSpec((SGU_BLOCK, SGU_WIDTH), lambda i: (0, 0))],
        out_specs=pl.BlockSpec((TB_SGU, SGU_WIDTH), lambda i: (i, 0)),
        compiler_params=_params(("parallel",), 32 << 20),
        name="sgu_mixer",
    )(proj, proj, ln_g.reshape(1, -1), ln_b.reshape(1, -1), w_s, bias_full)


Q_SCALE = (MLA_NOPE + MLA_ROPE) ** -0.5
Q_ROPE_TILE = (MLA_HEADS * MLA_NOPE) // IN_TILE


def _q_proj_kernel(c_ref, g_ref, w_ref, cos_ref, sin_ref, o_ref, cn_ref):
    j = pl.program_id(1)

    @pl.when(j == 0)
    def _():
        cn_ref[...] = _rms(c_ref[...].astype(F32), g_ref[...]).astype(cn_ref.dtype)

    acc = jnp.dot(cn_ref[...], w_ref[...], preferred_element_type=F32) * Q_SCALE

    @pl.when(j < Q_ROPE_TILE)
    def _():
        o_ref[...] = acc.astype(o_ref.dtype)

    @pl.when(j >= Q_ROPE_TILE)
    def _():
        for c in range(IN_TILE // LANES):
            sl = slice(c * LANES, (c + 1) * LANES)
            o_ref[:, sl] = _rope_cols(acc[:, sl], cos_ref[...], sin_ref[...]).astype(o_ref.dtype)


def q_proj(proj, g, w, cos_t, sin_t, seq):
    t = proj.shape[0]
    n = w.shape[1]
    pos_blocks = seq // TM
    tab = pl.BlockSpec((TM, LANES), lambda i, j: (i % pos_blocks, 0))
    return pl.pallas_call(
        _q_proj_kernel,
        out_shape=jax.ShapeDtypeStruct((t, n), BF16),
        grid=(t // TM, n // IN_TILE),
        in_specs=[pl.BlockSpec((TM, MLA_Q_RANK), lambda i, j: (i, COL_CQ // MLA_Q_RANK)),
                  pl.BlockSpec((1, MLA_Q_RANK), lambda i, j: (0, 0)),
                  pl.BlockSpec((MLA_Q_RANK, IN_TILE), lambda i, j: (0, j)),
                  tab, tab],
        out_specs=pl.BlockSpec((TM, IN_TILE), lambda i, j: (i, j)),
        scratch_shapes=[pltpu.VMEM((TM, MLA_Q_RANK), BF16)],
        compiler_params=_params(("parallel", "arbitrary"), 40 << 20),
        name="q_proj",
    )(proj, g.reshape(1, -1), w, cos_t, sin_t)


def _kv_proj_kernel(c_ref, g_ref, w_ref, o_ref, cn_ref):
    @pl.when(pl.program_id(1) == 0)
    def _():
        cn_ref[...] = _rms(c_ref[...].astype(F32), g_ref[...]).astype(cn_ref.dtype)

    o_ref[...] = jnp.dot(cn_ref[...], w_ref[...], preferred_element_type=F32).astype(o_ref.dtype)


def kv_proj(proj, g, w):
    t = proj.shape[0]
    n = w.shape[1]
    return pl.pallas_call(
        _kv_proj_kernel,
        out_shape=jax.ShapeDtypeStruct((t, n), BF16),
        grid=(t // TM, n // IN_TILE),
        in_specs=[pl.BlockSpec((TM, MLA_KV_RANK), lambda i, j: (i, COL_CKV // MLA_KV_RANK)),
                  pl.BlockSpec((1, MLA_KV_RANK), lambda i, j: (0, 0)),
                  pl.BlockSpec((MLA_KV_RANK, IN_TILE), lambda i, j: (0, j))],
        out_specs=pl.BlockSpec((TM, IN_TILE), lambda i, j: (i, j)),
        scratch_shapes=[pltpu.VMEM((TM, MLA_KV_RANK), BF16)],
        compiler_params=_params(("parallel", "arbitrary"), 32 << 20),
        name="kv_proj",
    )(proj, g.reshape(1, -1), w)


def _attn_kernel(qn_ref, qr_ref, kn_ref, kr_ref, v_ref, o_ref, m_ref, l_ref, acc_ref):
    qi = pl.program_id(2)
    q = jnp.concatenate([qn_ref[...], qr_ref[...]], axis=1)

    def scores(start):
        k = jnp.concatenate([kn_ref[pl.ds(start, TQ), :], kr_ref[pl.ds(start, TQ), :]], axis=1)
        return lax.dot_general(q, k, (((1,), (1,)), ((), ())), preferred_element_type=F32)

    start = pl.multiple_of(qi * TQ, TQ)
    s = scores(start)
    row = lax.broadcasted_iota(jnp.int32, (TQ, TQ), 0)
    col = lax.broadcasted_iota(jnp.int32, (TQ, TQ), 1)
    s = jnp.where((col // CHUNK) <= (row // CHUNK), s, -jnp.inf)
    m0 = jnp.max(s, axis=1, keepdims=True)
    p = jnp.exp(s - m0)
    m_ref[...] = jnp.broadcast_to(m0, m_ref.shape)
    l_ref[...] = jnp.broadcast_to(jnp.sum(p, axis=1, keepdims=True), l_ref.shape)
    acc_ref[...] = jnp.dot(p.astype(BF16), v_ref[pl.ds(start, TQ), :], preferred_element_type=F32)

    def body(kj, carry):
        st = pl.multiple_of(kj * TQ, TQ)
        sj = scores(st)
        m_prev = m_ref[...]
        m_next = jnp.maximum(m_prev, jnp.max(sj, axis=1, keepdims=True))
        alpha = jnp.exp(m_prev - m_next)
        pj = jnp.exp(sj - jnp.tile(m_next, (1, TQ // LANES)))
        l_ref[...] = alpha * l_ref[...] + jnp.sum(pj, axis=1, keepdims=True)
        acc_ref[...] = alpha * acc_ref[...] + jnp.dot(pj.astype(BF16), v_ref[pl.ds(st, TQ), :],
                                                     preferred_element_type=F32)
        m_ref[...] = m_next
        return carry

    lax.fori_loop(0, qi, body, 0)
    o_ref[...] = (acc_ref[...] / l_ref[...]).astype(o_ref.dtype)


def mla_attention(q, kv, proj, batch, seq):
    t = q.shape[0]
    nq = seq // TQ
    kr_block0 = COL_KR // LANES
    return pl.pallas_call(
        _attn_kernel,
        out_shape=jax.ShapeDtypeStruct((t, MLA_WIDTH), BF16),
        grid=(batch, MLA_HEADS, nq),
        in_specs=[pl.BlockSpec((TQ, LANES), lambda b, h, i: (b * nq + i, h)),
                  pl.BlockSpec((TQ, LANES), lambda b, h, i: (b * nq + i, MLA_HEADS + h // 2)),
                  pl.BlockSpec((seq, LANES), lambda b, h, i: (b, h)),
                  pl.BlockSpec((seq, LANES), lambda b, h, i: (b, kr_block0 + h % 2)),
                  pl.BlockSpec((seq, LANES), lambda b, h, i: (b, MLA_HEADS + h))],
        out_specs=pl.BlockSpec((TQ, LANES), lambda b, h, i: (b * nq + i, h)),
        scratch_shapes=[pltpu.VMEM((TQ, LANES), F32), pltpu.VMEM((TQ, LANES), F32),
                        pltpu.VMEM((TQ, LANES), F32)],
        compiler_params=_params(("parallel", "parallel", "arbitrary"), 32 << 20),
        name="mla_attention",
    )(q, q, kv, proj, kv)


def _ret_kernel(q_ref, k_ref, v_ref, g_ref, d_ref, qd_ref, kd_ref, cd_ref, o_ref, s_ref):
    @pl.when(pl.program_id(2) == 0)
    def _():
        s_ref[...] = jnp.zeros_like(s_ref)

    q = q_ref[...]
    k = k_ref[...]
    qf = q.astype(F32)
    kdec = (k.astype(F32) * kd_ref[...]).astype(BF16)
    lane = lax.broadcasted_iota(jnp.int32, q.shape, 1)
    for hh in range(2):
        own = (lane // RET_QK) == hh
        vs = slice(hh * RET_V, (hh + 1) * RET_V)
        v = v_ref[:, vs]
        qm = jnp.where(own, q, jnp.zeros_like(q))
        s = lax.dot_general(qm, k, (((1,), (1,)), ((), ())), preferred_element_type=F32)
        sd = (s * d_ref[hh]).astype(BF16)
        qx = jnp.where(own, qf * qd_ref[...], 0.0).astype(BF16)
        state = s_ref[hh]
        o = (jnp.dot(sd, v, preferred_element_type=F32)
             + jnp.dot(qx, state.astype(BF16), preferred_element_type=F32))
        s_ref[hh] = cd_ref[hh] * state + lax.dot_general(
            kdec, v, (((0,), (0,)), ((), ())), preferred_element_type=F32)
        o = o * lax.rsqrt(jnp.mean(o * o, axis=-1, keepdims=True) + EPS)
        gate = g_ref[:, vs].astype(F32)
        o_ref[:, vs] = (gate * jax.nn.sigmoid(gate) * o).astype(o_ref.dtype)


def retention_mixer(proj, d_tab, qd_tab, kd_tab, cd_tab, batch, seq):
    t = proj.shape[0]
    nsc = seq // RET_L
    pairs = RET_HEADS // 2
    pair_w = 2 * RET_V
    return pl.pallas_call(
        _ret_kernel,
        out_shape=jax.ShapeDtypeStruct((t, RET_WIDTH), BF16),
        grid=(batch, pairs, nsc),
        in_specs=[pl.BlockSpec((RET_L, LANES), lambda b, p, c: (b * nsc + c, COL_RQ // LANES + p)),
                  pl.BlockSpec((RET_L, LANES), lambda b, p, c: (b * nsc + c, COL_RK // LANES + p)),
                  pl.BlockSpec((RET_L, pair_w), lambda b, p, c: (b * nsc + c, COL_RV // pair_w + p)),
                  pl.BlockSpec((RET_L, pair_w), lambda b, p, c: (b * nsc + c, COL_RG // pair_w + p)),
                  pl.BlockSpec((2, RET_L, RET_L), lambda b, p, c: (p, 0, 0)),
                  pl.BlockSpec((None, RET_L, LANES), lambda b, p, c: (p, 0, 0)),
                  pl.BlockSpec((None, RET_L, LANES), lambda b, p, c: (p, 0, 0)),
                  pl.BlockSpec((2, LANES, RET_V), lambda b, p, c: (p, 0, 0))],
        out_specs=pl.BlockSpec((RET_L, pair_w), lambda b, p, c: (b * nsc + c, p)),
        scratch_shapes=[pltpu.VMEM((2, LANES, RET_V), F32)],
        compiler_params=_params(("parallel", "parallel", "arbitrary"), 32 << 20),
        name="retention_mixer",
    )(proj, proj, proj, proj, d_tab, qd_tab, kd_tab, cd_tab)


def retention_tables():
    log_gamma = jnp.log1p(-jnp.exp2(-5.0 - jnp.arange(RET_HEADS, dtype=F32)))
    idx = jnp.arange(RET_L, dtype=F32)
    dist = jnp.abs(idx[:, None] - idx[None, :])
    chunk = jnp.arange(RET_L) // CHUNK
    visible = chunk[None, :] <= chunk[:, None]
    d_tab = jnp.where(visible[None], jnp.exp(log_gamma[:, None, None] * dist[None]), 0.0)
    lane_head = jnp.arange(LANES) // RET_QK
    head_of_lane = 2 * jnp.arange(RET_HEADS // 2)[:, None] + lane_head[None, :]
    lg_lane = log_gamma[head_of_lane]
    qd_tab = jnp.exp(lg_lane[:, None, :] * (idx[None, :, None] + 1.0))
    kd_tab = jnp.exp(lg_lane[:, None, :] * (RET_L - 1.0 - idx[None, :, None]))
    cd = jnp.exp(log_gamma * RET_L)
    cd_tab = jnp.broadcast_to(cd[:, None, None], (RET_HEADS, LANES, RET_V))
    return d_tab, qd_tab, kd_tab, cd_tab


def _out_proj_kernel(a_ref, b_ref, c_ref, w_ref, o_ref):
    ka = a_ref.shape[1]
    kb = b_ref.shape[1]
    acc = jnp.dot(a_ref[...], w_ref[:ka, :], preferred_element_type=F32)
    acc = acc + jnp.dot(b_ref[...], w_ref[ka:ka + kb, :], preferred_element_type=F32)
    acc = acc + jnp.dot(c_ref[...], w_ref[ka + kb:, :], preferred_element_type=F32)
    o_ref[...] = acc


def out_proj(a, b, c, w):
    t = a.shape[0]
    k, n = w.shape
    return pl.pallas_call(
        _out_proj_kernel,
        out_shape=jax.ShapeDtypeStruct((t, n), F32),
        grid=(t // TM, n // IN_TILE),
        in_specs=[pl.BlockSpec((TM, a.shape[1]), lambda i, j: (i, 0)),
                  pl.BlockSpec((TM, b.shape[1]), lambda i, j: (i, 0)),
                  pl.BlockSpec((TM, c.shape[1]), lambda i, j: (i, 0)),
                  pl.BlockSpec((k, IN_TILE), lambda i, j: (0, j))],
        out_specs=pl.BlockSpec((TM, IN_TILE), lambda i, j: (i, j)),
        compiler_params=_params(("parallel", "arbitrary"), 56 << 20),
        name="out_proj",
    )(a, b, c, w)


def _ffn_kernel(h_ref, wu_ref, wd_ref, o_ref):
    j = pl.program_id(1)
    a = jnp.dot(h_ref[...], wu_ref[...], preferred_element_type=F32)
    a = jnp.square(jnp.maximum(a, 0.0)).astype(BF16)
    d = jnp.dot(a, wd_ref[...], preferred_element_type=F32)

    @pl.when(j == 0)
    def _():
        o_ref[...] = d

    @pl.when(j > 0)
    def _():
        o_ref[...] += d


def ffn(h, w_up, w_down):
    t, d = h.shape
    ff = w_up.shape[1]
    return pl.pallas_call(
        _ffn_kernel,
        out_shape=jax.ShapeDtypeStruct((t, d), F32),
        grid=(t // TM_FFN, ff // TF_FFN),
        in_specs=[pl.BlockSpec((TM_FFN, d), lambda i, j: (i, 0)),
                  pl.BlockSpec((d, TF_FFN), lambda i, j: (0, j)),
                  pl.BlockSpec((TF_FFN, d), lambda i, j: (j, 0))],
        out_specs=pl.BlockSpec((TM_FFN, d), lambda i, j: (i, 0)),
        compiler_params=_params(("parallel", "arbitrary"), 56 << 20),
        name="ffn",
    )(h, w_up, w_down)


def _prep_w_in(w):
    sizes = (SGU_WIDTH, SGU_WIDTH, MLA_Q_RANK, MLA_KV_RANK, MLA_ROPE,
             RET_HEADS * RET_QK, RET_HEADS * RET_QK, RET_WIDTH, RET_WIDTH)
    offs = np.concatenate([[0], np.cumsum(sizes)])
    u, v, c_q, c_kv, k_rope, r_q, r_k, r_v, r_g = [w[:, int(offs[i]):int(offs[i + 1])] for i in range(9)]
    z64 = jnp.zeros((w.shape[0], MLA_ROPE), w.dtype)
    z256 = jnp.zeros((w.shape[0], IN_TILE - MLA_KV_RANK - 2 * LANES), w.dtype)
    cols = [u, v, c_q, r_v, r_g, c_kv, k_rope, z64, z64, k_rope, z256, r_q, r_k]
    return jnp.concatenate(cols, axis=1).astype(BF16)


def _prep_wq(w):
    k = w.shape[0]
    w3 = w.reshape(k, MLA_HEADS, MLA_NOPE + MLA_ROPE)
    nope = w3[:, :, :MLA_NOPE].reshape(k, MLA_HEADS * MLA_NOPE)
    rope = w3[:, :, MLA_NOPE:].reshape(k, MLA_HEADS * MLA_ROPE)
    return jnp.concatenate([nope, rope], axis=1).astype(BF16)


def _prep_wkv(w):
    k = w.shape[0]
    w3 = w.reshape(k, MLA_HEADS, MLA_NOPE + MLA_V)
    kn = w3[:, :, :MLA_NOPE].reshape(k, MLA_HEADS * MLA_NOPE)
    vv = w3[:, :, MLA_NOPE:].reshape(k, MLA_HEADS * MLA_V)
    return jnp.concatenate([kn, vv], axis=1).astype(BF16)


def _rope_lane_tables(seq):
    inv_freq = 1.0 / (ROPE_THETA ** (jnp.arange(0, MLA_ROPE, 2, dtype=F32) / MLA_ROPE))
    ang = jnp.arange(seq, dtype=F32)[:, None] * inv_freq[None, :]
    cos, sin = jnp.cos(ang), jnp.sin(ang)
    cos_t = jnp.concatenate([cos, cos, cos, cos], axis=1)
    sin_t = jnp.concatenate([-sin, sin, -sin, sin], axis=1)
    return cos_t, sin_t


def kernel(x, norm_mix_pre, norm_mix_post, norm_ffn_pre, norm_ffn_post, w_in, sgu_ln_g, sgu_ln_b, sgu_w, sgu_b, mla_q_norm, mla_wq_b, mla_kv_norm, mla_wkv_b, w_out, w_up, w_down):
    batch, seq, d = x.shape
    depth = w_in.shape[0]
    cos_t, sin_t = _rope_lane_tables(seq)
    d_tab, qd_tab, kd_tab, cd_tab = retention_tables()

    xf = x.reshape(batch * seq, d)
    h = rmsnorm_rows(xf, norm_mix_pre[0])
    for l in range(depth):
        proj = in_proj(h, _prep_w_in(w_in[l]), cos_t, sin_t, seq)
        bias_full = jnp.repeat(sgu_b[l].T, SGU_WIDTH // SGU_HEADS, axis=1)
        out_a = sgu_mixer(proj, sgu_ln_g[l], sgu_ln_b[l], sgu_w[l], bias_full)
        q = q_proj(proj, mla_q_norm[l], _prep_wq(mla_wq_b[l]), cos_t, sin_t, seq)
        kv = kv_proj(proj, mla_kv_norm[l], _prep_wkv(mla_wkv_b[l]))
        out_b = mla_attention(q, kv, proj, batch, seq)
        out_c = retention_mixer(proj, d_tab, qd_tab, kd_tab, cd_tab, batch, seq)
        mixed = out_proj(out_a, out_b, out_c, w_out[l].astype(BF16))
        xf, h2 = residual_norm(xf, mixed, norm_mix_post[l], norm_ffn_pre[l])
        f = ffn(h2, w_up[l].astype(BF16), w_down[l].astype(BF16))
        g_next = norm_mix_pre[l + 1] if l + 1 < depth else None
        xf, h = residual_norm(xf, f, norm_ffn_post[l], g_next)
    return xf.reshape(batch, seq, d)
```

```python
import functools
import math

import numpy as np
import jax
import jax.numpy as jnp
from jax import lax
from jax.experimental import pallas as pl
from jax.experimental.pallas import tpu as pltpu

F32 = jnp.float32
BF16 = jnp.bfloat16

D_MODEL = 4096
CHUNK = 64
EPS = 1e-6
ROPE_THETA = 10000.0
SGU_BLOCK = 128
SGU_WIDTH = 1024
SGU_HEADS = 8
MLA_HEADS = 16
MLA_NOPE = 128
MLA_ROPE = 64
MLA_V = 128
MLA_Q_RANK = 1024
MLA_KV_RANK = 512
MLA_WIDTH = MLA_HEADS * MLA_V
RET_HEADS = 8
RET_QK = 64
RET_V = 128
RET_WIDTH = RET_HEADS * RET_V
FF_DIM = 4 * D_MODEL

LANES = 128
VMEM_CAP = 64 * 1024 * 1024

IN_TILE = 1024
COL_U = 0 * IN_TILE
COL_V = 1 * IN_TILE
COL_CQ = 2 * IN_TILE
COL_RV = 3 * IN_TILE
COL_RG = 4 * IN_TILE
COL_CKV = 5 * IN_TILE
COL_KR = COL_CKV + MLA_KV_RANK
COL_RQ = 6 * IN_TILE
COL_RK = COL_RQ + RET_HEADS * RET_QK
IN_WIDTH_PADDED = 7 * IN_TILE
ROPE_MIXED_TILE = 5
ROPE_FULL_TILE = 6

TM = 1024
TM_NORM = 256
TM_FFN = 512
TF_FFN = 512
TB_SGU = 512
TQ = 512
RET_L = 256


def _vmem_limit(nbytes):
    return int(min(VMEM_CAP - (2 << 20), nbytes))


def _params(sem, nbytes):
    return pltpu.CompilerParams(dimension_semantics=sem, vmem_limit_bytes=_vmem_limit(nbytes))


def _rms(x, g):
    return x * lax.rsqrt(jnp.mean(x * x, axis=-1, keepdims=True) + EPS) * g


def _rmsnorm_kernel(x_ref, g_ref, o_ref):
    o_ref[...] = _rms(x_ref[...], g_ref[...]).astype(o_ref.dtype)


def rmsnorm_rows(x, g):
    t, d = x.shape
    return pl.pallas_call(
        _rmsnorm_kernel,
        out_shape=jax.ShapeDtypeStruct((t, d), BF16),
        grid=(t // TM_NORM,),
        in_specs=[pl.BlockSpec((TM_NORM, d), lambda i: (i, 0)),
                  pl.BlockSpec((1, d), lambda i: (0, 0))],
        out_specs=pl.BlockSpec((TM_NORM, d), lambda i: (i, 0)),
        compiler_params=_params(("parallel",), 32 << 20),
        name="rmsnorm_rows",
    )(x, g.reshape(1, d))


def _residual_norm_kernel(x_ref, y_ref, gp_ref, gn_ref, xo_ref, ho_ref):
    xn = x_ref[...] + _rms(y_ref[...], gp_ref[...])
    xo_ref[...] = xn
    ho_ref[...] = _rms(xn, gn_ref[...]).astype(ho_ref.dtype)


def _residual_kernel(x_ref, y_ref, gp_ref, xo_ref):
    xo_ref[...] = x_ref[...] + _rms(y_ref[...], gp_ref[...])


def residual_norm(x, y, g_post, g_next):
    t, d = x.shape
    row = pl.BlockSpec((TM_NORM, d), lambda i: (i, 0))
    vec = pl.BlockSpec((1, d), lambda i: (0, 0))
    if g_next is None:
        return pl.pallas_call(
            _residual_kernel,
            out_shape=jax.ShapeDtypeStruct((t, d), F32),
            grid=(t // TM_NORM,),
            in_specs=[row, row, vec],
            out_specs=row,
            compiler_params=_params(("parallel",), 40 << 20),
            name="residual",
        )(x, y, g_post.reshape(1, d)), None
    return pl.pallas_call(
        _residual_norm_kernel,
        out_shape=(jax.ShapeDtypeStruct((t, d), F32), jax.ShapeDtypeStruct((t, d), BF16)),
        grid=(t // TM_NORM,),
        in_specs=[row, row, vec, vec],
        out_specs=(row, row),
        compiler_params=_params(("parallel",), 40 << 20),
        name="residual_norm",
    )(x, y, g_post.reshape(1, d), g_next.reshape(1, d))


def _rope_cols(x, cos, sin):
    lane = lax.broadcasted_iota(jnp.int32, x.shape, 1)
    first_half = (lane % MLA_ROPE) < (MLA_ROPE // 2)
    partner = jnp.where(first_half, pltpu.roll(x, LANES - MLA_ROPE // 2, 1), pltpu.roll(x, MLA_ROPE // 2, 1))
    return x * cos + partner * sin


def _in_proj_kernel(h_ref, w_ref, cos_ref, sin_ref, o_ref):
    j = pl.program_id(1)
    acc = jnp.dot(h_ref[...], w_ref[...], preferred_element_type=F32)

    @pl.when(j < ROPE_MIXED_TILE)
    def _():
        o_ref[...] = acc.astype(o_ref.dtype)

    @pl.when(j == ROPE_MIXED_TILE)
    def _():
        lo = COL_KR - COL_CKV
        o_ref[:, :lo] = acc[:, :lo].astype(o_ref.dtype)
        for c in range(lo // LANES, lo // LANES + 2):
            sl = slice(c * LANES, (c + 1) * LANES)
            o_ref[:, sl] = _rope_cols(acc[:, sl], cos_ref[...], sin_ref[...]).astype(o_ref.dtype)
        o_ref[:, lo + 2 * LANES:] = acc[:, lo + 2 * LANES:].astype(o_ref.dtype)

    @pl.when(j == ROPE_FULL_TILE)
    def _():
        q_cols = (RET_HEADS * RET_QK) // LANES
        for c in range(IN_TILE // LANES):
            sl = slice(c * LANES, (c + 1) * LANES)
            r = _rope_cols(acc[:, sl], cos_ref[...], sin_ref[...])
            if c < q_cols:
                r = r * (RET_QK ** -0.5)
            o_ref[:, sl] = r.astype(o_ref.dtype)


def in_proj(h, w, cos_t, sin_t, seq):
    t, d = h.shape
    n = w.shape[1]
    pos_blocks = seq // TM
    tab = pl.BlockSpec((TM, LANES), lambda i, j: (i % pos_blocks, 0))
    return pl.pallas_call(
        _in_proj_kernel,
        out_shape=jax.ShapeDtypeStruct((t, n), BF16),
        grid=(t // TM, n // IN_TILE),
        in_specs=[pl.BlockSpec((TM, d), lambda i, j: (i, 0)),
                  pl.BlockSpec((d, IN_TILE), lambda i, j: (0, j)),
                  tab, tab],
        out_specs=pl.BlockSpec((TM, IN_TILE), lambda i, j: (i, j)),
        compiler_params=_params(("parallel", "arbitrary"), 52 << 20),
        name="in_proj",
    )(h, w, cos_t, sin_t)


def _sgu_kernel(u_ref, v_ref, g_ref, b_ref, w_ref, bias_ref, o_ref):
    row = lax.broadcasted_iota(jnp.int32, (SGU_BLOCK, SGU_BLOCK), 0)
    col = lax.broadcasted_iota(jnp.int32, (SGU_BLOCK, SGU_BLOCK), 1)
    keep = (col // CHUNK) <= (row // CHUNK)
    for g in range(SGU_HEADS):
        sl = slice(g * LANES, (g + 1) * LANES)
        wg = jnp.where(keep, w_ref[g], 0.0).astype(BF16)
        vg = v_ref[:, sl].astype(F32)
        mu = jnp.mean(vg, axis=-1, keepdims=True)
        vc = vg - mu
        y = vc * lax.rsqrt(jnp.mean(vc * vc, axis=-1, keepdims=True) + EPS) * g_ref[:, sl] + b_ref[:, sl]
        yb = y.astype(BF16)
        for n in range(TB_SGU // SGU_BLOCK):
            rs = slice(n * SGU_BLOCK, (n + 1) * SGU_BLOCK)
            mixed = jnp.dot(wg, yb[rs, :], preferred_element_type=F32) + bias_ref[:, sl]
            o_ref[rs, sl] = (u_ref[rs, sl].astype(F32) * mixed).astype(o_ref.dtype)


def sgu_mixer(proj, ln_g, ln_b, w_s, bias_full):
    t = proj.shape[0]
    vec = pl.BlockSpec((1, SGU_WIDTH), lambda i: (0, 0))
    return pl.pallas_call(
        _sgu_kernel,
        out_shape=jax.ShapeDtypeStruct((t, SGU_WIDTH), BF16),
        grid=(t // TB_SGU,),
        in_specs=[pl.BlockSpec((TB_SGU, SGU_WIDTH), lambda i: (i, COL_U // SGU_WIDTH)),
                  pl.BlockSpec((TB_SGU, SGU_WIDTH), lambda i: (i, COL_V // SGU_WIDTH)),
                  vec, vec,
                  pl.BlockSpec((SGU_HEADS, SGU_BLOCK, SGU_BLOCK), lambda i: (0, 0, 0)),
                  pl.BlockSpec((SGU_BLOCK, SGU_WIDTH), lambda i: (0, 0))],
        out_specs=pl.BlockSpec((TB_SGU, SGU_WIDTH), lambda i: (i, 0)),
        compiler_params=_params(("parallel",), 32 << 20),
        name="sgu_mixer",
    )(proj, proj, ln_g.reshape(1, -1), ln_b.reshape(1, -1), w_s, bias_full)


Q_SCALE = (MLA_NOPE + MLA_ROPE) ** -0.5
Q_ROPE_TILE = (MLA_HEADS * MLA_NOPE) // IN_TILE


def _q_proj_kernel(c_ref, g_ref, w_ref, cos_ref, sin_ref, o_ref, cn_ref):
    j = pl.program_id(1)

    @pl.when(j == 0)
    def _():
        cn_ref[...] = _rms(c_ref[...].astype(F32), g_ref[...]).astype(cn_ref.dtype)

    acc = jnp.dot(cn_ref[...], w_ref[...], preferred_element_type=F32) * Q_SCALE

    @pl.when(j < Q_ROPE_TILE)
    def _():
        o_ref[...] = acc.astype(o_ref.dtype)

    @pl.when(j >= Q_ROPE_TILE)
    def _():
        for c in range(IN_TILE // LANES):
            sl = slice(c * LANES, (c + 1) * LANES)
            o_ref[:, sl] = _rope_cols(acc[:, sl], cos_ref[...], sin_ref[...]).astype(o_ref.dtype)


def q_proj(proj, g, w, cos_t, sin_t, seq):
    t = proj.shape[0]
    n = w.shape[1]
    pos_blocks = seq // TM
    tab = pl.BlockSpec((TM, LANES), lambda i, j: (i % pos_blocks, 0))
    return pl.pallas_call(
        _q_proj_kernel,
        out_shape=jax.ShapeDtypeStruct((t, n), BF16),
        grid=(t // TM, n // IN_TILE),
        in_specs=[pl.BlockSpec((TM, MLA_Q_RANK), lambda i, j: (i, COL_CQ // MLA_Q_RANK)),
                  pl.BlockSpec((1, MLA_Q_RANK), lambda i, j: (0, 0)),
                  pl.BlockSpec((MLA_Q_RANK, IN_TILE), lambda i, j: (0, j)),
                  tab, tab],
        out_specs=pl.BlockSpec((TM, IN_TILE), lambda i, j: (i, j)),
        scratch_shapes=[pltpu.VMEM((TM, MLA_Q_RANK), BF16)],
        compiler_params=_params(("parallel", "arbitrary"), 40 << 20),
        name="q_proj",
    )(proj, g.reshape(1, -1), w, cos_t, sin_t)


def _kv_proj_kernel(c_ref, g_ref, w_ref, o_ref, cn_ref):
    @pl.when(pl.program_id(1) == 0)
    def _():
        cn_ref[...] = _rms(c_ref[...].astype(F32), g_ref[...]).astype(cn_ref.dtype)

    o_ref[...] = jnp.dot(cn_ref[...], w_ref[...], preferred_element_type=F32).astype(o_ref.dtype)


def kv_proj(proj, g, w):
    t = proj.shape[0]
    n = w.shape[1]
    return pl.pallas_call(
        _kv_proj_kernel,
        out_shape=jax.ShapeDtypeStruct((t, n), BF16),
        grid=(t // TM, n // IN_TILE),
        in_specs=[pl.BlockSpec((TM, MLA_KV_RANK), lambda i, j: (i, COL_CKV // MLA_KV_RANK)),
                  pl.BlockSpec((1, MLA_KV_RANK), lambda i, j: (0, 0)),
                  pl.BlockSpec((MLA_KV_RANK, IN_TILE), lambda i, j: (0, j))],
        out_specs=pl.BlockSpec((TM, IN_TILE), lambda i, j: (i, j)),
        scratch_shapes=[pltpu.VMEM((TM, MLA_KV_RANK), BF16)],
        compiler_params=_params(("parallel", "arbitrary"), 32 << 20),
        name="kv_proj",
    )(proj, g.reshape(1, -1), w)


ATTN_GROUP = 4


def _attn_tiles(q, kn_ref, kr_ref, v_ref, base, ntiles, mask_last, state):
    m, l, acc = state
    for t in range(ntiles):
        st = pl.multiple_of(base + t * TQ, TQ)
        k = jnp.concatenate([kn_ref[pl.ds(st, TQ), :], kr_ref[pl.ds(st, TQ), :]], axis=1)
        s = lax.dot_general(q, k, (((1,), (1,)), ((), ())), preferred_element_type=F32)
        if mask_last and t == ntiles - 1:
            row = lax.broadcasted_iota(jnp.int32, (TQ, TQ), 0)
            col = lax.broadcasted_iota(jnp.int32, (TQ, TQ), 1)
            s = jnp.where((col // CHUNK) <= (row // CHUNK), s, -jnp.inf)
        m_next = jnp.maximum(m, jnp.max(s, axis=1, keepdims=True))
        alpha = jnp.exp(m - m_next)
        p = jnp.exp(s - jnp.tile(m_next, (1, TQ // LANES)))
        l = alpha * l + jnp.sum(p, axis=1, keepdims=True)
        acc = alpha * acc + jnp.dot(p.astype(BF16), v_ref[pl.ds(st, TQ), :], preferred_element_type=F32)
        m = m_next
    return m, l, acc


def _attn_kernel(qn_ref, qr_ref, kn_ref, kr_ref, v_ref, o_ref, m_ref, l_ref, acc_ref):
    qi = pl.program_id(2)
    q = jnp.concatenate([qn_ref[...], qr_ref[...]], axis=1)
    refs = (kn_ref, kr_ref, v_ref)

    m_ref[...] = jnp.full_like(m_ref, -jnp.inf)
    l_ref[...] = jnp.zeros_like(l_ref)
    acc_ref[...] = jnp.zeros_like(acc_ref)

    def body(g, carry):
        base = pl.multiple_of(g * (ATTN_GROUP * TQ), ATTN_GROUP * TQ)
        m, l, acc = _attn_tiles(q, *refs, base, ATTN_GROUP, False, (m_ref[...], l_ref[...], acc_ref[...]))
        m_ref[...] = m
        l_ref[...] = l
        acc_ref[...] = acc
        return carry

    lax.fori_loop(0, qi // ATTN_GROUP, body, 0)

    base = pl.multiple_of((qi // ATTN_GROUP) * (ATTN_GROUP * TQ), ATTN_GROUP * TQ)
    for r in range(ATTN_GROUP):
        @pl.when(qi % ATTN_GROUP == r)
        def _():
            _, l, acc = _attn_tiles(q, *refs, base, r + 1, True, (m_ref[...], l_ref[...], acc_ref[...]))
            o_ref[...] = (acc / l).astype(o_ref.dtype)


def mla_attention(q, kv, proj, batch, seq):
    t = q.shape[0]
    nq = seq // TQ
    kr_block0 = COL_KR // LANES
    return pl.pallas_call(
        _attn_kernel,
        out_shape=jax.ShapeDtypeStruct((t, MLA_WIDTH), BF16),
        grid=(batch, MLA_HEADS, nq),
        in_specs=[pl.BlockSpec((TQ, LANES), lambda b, h, i: (b * nq + i, h)),
                  pl.BlockSpec((TQ, LANES), lambda b, h, i: (b * nq + i, MLA_HEADS + h // 2)),
                  pl.BlockSpec((seq, LANES), lambda b, h, i: (b, h)),
                  pl.BlockSpec((seq, LANES), lambda b, h, i: (b, kr_block0 + h % 2)),
                  pl.BlockSpec((seq, LANES), lambda b, h, i: (b, MLA_HEADS + h))],
        out_specs=pl.BlockSpec((TQ, LANES), lambda b, h, i: (b * nq + i, h)),
        scratch_shapes=[pltpu.VMEM((TQ, LANES), F32), pltpu.VMEM((TQ, LANES), F32),
                        pltpu.VMEM((TQ, LANES), F32)],
        compiler_params=_params(("parallel", "parallel", "arbitrary"), 32 << 20),
        name="mla_attention",
    )(q, q, kv, proj, kv)


def _ret_kernel(q_ref, k_ref, v_ref, g_ref, d_ref, qd_ref, kd_ref, cd_ref, o_ref, s_ref):
    @pl.when(pl.program_id(2) == 0)
    def _():
        s_ref[...] = jnp.zeros_like(s_ref)

    q = q_ref[...]
    k = k_ref[...]
    qf = q.astype(F32)
    kdec = (k.astype(F32) * kd_ref[...]).astype(BF16)
    lane = lax.broadcasted_iota(jnp.int32, q.shape, 1)
    for hh in range(2):
        own = (lane // RET_QK) == hh
        vs = slice(hh * RET_V, (hh + 1) * RET_V)
        v = v_ref[:, vs]
        qm = jnp.where(own, q, jnp.zeros_like(q))
        s = lax.dot_general(qm, k, (((1,), (1,)), ((), ())), preferred_element_type=F32)
        sd = (s * d_ref[hh]).astype(BF16)
        qx = jnp.where(own, qf * qd_ref[...], 0.0).astype(BF16)
        state = s_ref[hh]
        o = (jnp.dot(sd, v, preferred_element_type=F32)
             + jnp.dot(qx, state.astype(BF16), preferred_element_type=F32))
        s_ref[hh] = cd_ref[hh] * state + lax.dot_general(
            kdec, v, (((0,), (0,)), ((), ())), preferred_element_type=F32)
        o = o * lax.rsqrt(jnp.mean(o * o, axis=-1, keepdims=True) + EPS)
        gate = g_ref[:, vs].astype(F32)
        o_ref[:, vs] = (gate * jax.nn.sigmoid(gate) * o).astype(o_ref.dtype)


def retention_mixer(proj, d_tab, qd_tab, kd_tab, cd_tab, batch, seq):
    t = proj.shape[0]
    nsc = seq // RET_L
    pairs = RET_HEADS // 2
    pair_w = 2 * RET_V
    return pl.pallas_call(
        _ret_kernel,
        out_shape=jax.ShapeDtypeStruct((t, RET_WIDTH), BF16),
        grid=(batch, pairs, nsc),
        in_specs=[pl.BlockSpec((RET_L, LANES), lambda b, p, c: (b * nsc + c, COL_RQ // LANES + p)),
                  pl.BlockSpec((RET_L, LANES), lambda b, p, c: (b * nsc + c, COL_RK // LANES + p)),
                  pl.BlockSpec((RET_L, pair_w), lambda b, p, c: (b * nsc + c, COL_RV // pair_w + p)),
                  pl.BlockSpec((RET_L, pair_w), lambda b, p, c: (b * nsc + c, COL_RG // pair_w + p)),
                  pl.BlockSpec((2, RET_L, RET_L), lambda b, p, c: (p, 0, 0)),
                  pl.BlockSpec((None, RET_L, LANES), lambda b, p, c: (p, 0, 0)),
                  pl.BlockSpec((None, RET_L, LANES), lambda b, p, c: (p, 0, 0)),
                  pl.BlockSpec((2, LANES, RET_V), lambda b, p, c: (p, 0, 0))],
        out_specs=pl.BlockSpec((RET_L, pair_w), lambda b, p, c: (b * nsc + c, p)),
        scratch_shapes=[pltpu.VMEM((2, LANES, RET_V), F32)],
        compiler_params=_params(("parallel", "parallel", "arbitrary"), 32 << 20),
        name="retention_mixer",
    )(proj, proj, proj, proj, d_tab, qd_tab, kd_tab, cd_tab)


def retention_tables():
    log_gamma = jnp.log1p(-jnp.exp2(-5.0 - jnp.arange(RET_HEADS, dtype=F32)))
    idx = jnp.arange(RET_L, dtype=F32)
    dist = jnp.abs(idx[:, None] - idx[None, :])
    chunk = jnp.arange(RET_L) // CHUNK
    visible = chunk[None, :] <= chunk[:, None]
    d_tab = jnp.where(visible[None], jnp.exp(log_gamma[:, None, None] * dist[None]), 0.0)
    lane_head = jnp.arange(LANES) // RET_QK
    head_of_lane = 2 * jnp.arange(RET_HEADS // 2)[:, None] + lane_head[None, :]
    lg_lane = log_gamma[head_of_lane]
    qd_tab = jnp.exp(lg_lane[:, None, :] * (idx[None, :, None] + 1.0))
    kd_tab = jnp.exp(lg_lane[:, None, :] * (RET_L - 1.0 - idx[None, :, None]))
    cd = jnp.exp(log_gamma * RET_L)
    cd_tab = jnp.broadcast_to(cd[:, None, None], (RET_HEADS, LANES, RET_V))
    return d_tab, qd_tab, kd_tab, cd_tab


def _out_proj_kernel(a_ref, b_ref, c_ref, w_ref, o_ref):
    ka = a_ref.shape[1]
    kb = b_ref.shape[1]
    acc = jnp.dot(a_ref[...], w_ref[:ka, :], preferred_element_type=F32)
    acc = acc + jnp.dot(b_ref[...], w_ref[ka:ka + kb, :], preferred_element_type=F32)
    acc = acc + jnp.dot(c_ref[...], w_ref[ka + kb:, :], preferred_element_type=F32)
    o_ref[...] = acc


def out_proj(a, b, c, w):
    t = a.shape[0]
    k, n = w.shape
    return pl.pallas_call(
        _out_proj_kernel,
        out_shape=jax.ShapeDtypeStruct((t, n), F32),
        grid=(t // TM, n // IN_TILE),
        in_specs=[pl.BlockSpec((TM, a.shape[1]), lambda i, j: (i, 0)),
                  pl.BlockSpec((TM, b.shape[1]), lambda i, j: (i, 0)),
                  pl.BlockSpec((TM, c.shape[1]), lambda i, j: (i, 0)),
                  pl.BlockSpec((k, IN_TILE), lambda i, j: (0, j))],
        out_specs=pl.BlockSpec((TM, IN_TILE), lambda i, j: (i, j)),
        compiler_params=_params(("parallel", "arbitrary"), 56 << 20),
        name="out_proj",
    )(a, b, c, w)


def _ffn_kernel(h_ref, wu_ref, wd_ref, o_ref):
    @pl.when(pl.program_id(1) == 0)
    def _():
        o_ref[...] = jnp.zeros_like(o_ref)

    a = jnp.dot(h_ref[...], wu_ref[...], preferred_element_type=F32)
    a = jnp.square(jnp.maximum(a, 0.0)).astype(BF16)
    o_ref[...] += jnp.dot(a, wd_ref[...], preferred_element_type=F32)


def ffn(h, w_up, w_down):
    t, d = h.shape
    ff = w_up.shape[1]
    return pl.pallas_call(
        _ffn_kernel,
        out_shape=jax.ShapeDtypeStruct((t, d), F32),
        grid=(t // TM_FFN, ff // TF_FFN),
        in_specs=[pl.BlockSpec((TM_FFN, d), lambda i, j: (i, 0)),
                  pl.BlockSpec((d, TF_FFN), lambda i, j: (0, j)),
                  pl.BlockSpec((TF_FFN, d), lambda i, j: (j, 0))],
        out_specs=pl.BlockSpec((TM_FFN, d), lambda i, j: (i, 0)),
        compiler_params=_params(("parallel", "arbitrary"), 56 << 20),
        name="ffn",
    )(h, w_up, w_down)


def _prep_w_in(w):
    sizes = (SGU_WIDTH, SGU_WIDTH, MLA_Q_RANK, MLA_KV_RANK, MLA_ROPE,
             RET_HEADS * RET_QK, RET_HEADS * RET_QK, RET_WIDTH, RET_WIDTH)
    offs = np.concatenate([[0], np.cumsum(sizes)])
    u, v, c_q, c_kv, k_rope, r_q, r_k, r_v, r_g = [w[:, int(offs[i]):int(offs[i + 1])] for i in range(9)]
    z64 = jnp.zeros((w.shape[0], MLA_ROPE), w.dtype)
    z256 = jnp.zeros((w.shape[0], IN_TILE - MLA_KV_RANK - 2 * LANES), w.dtype)
    cols = [u, v, c_q, r_v, r_g, c_kv, k_rope, z64, z64, k_rope, z256, r_q, r_k]
    return jnp.concatenate(cols, axis=1).astype(BF16)


def _prep_wq(w):
    k = w.shape[0]
    w3 = w.reshape(k, MLA_HEADS, MLA_NOPE + MLA_ROPE)
    nope = w3[:, :, :MLA_NOPE].reshape(k, MLA_HEADS * MLA_NOPE)
    rope = w3[:, :, MLA_NOPE:].reshape(k, MLA_HEADS * MLA_ROPE)
    return jnp.concatenate([nope, rope], axis=1).astype(BF16)


def _prep_wkv(w):
    k = w.shape[0]
    w3 = w.reshape(k, MLA_HEADS, MLA_NOPE + MLA_V)
    kn = w3[:, :, :MLA_NOPE].reshape(k, MLA_HEADS * MLA_NOPE)
    vv = w3[:, :, MLA_NOPE:].reshape(k, MLA_HEADS * MLA_V)
    return jnp.concatenate([kn, vv], axis=1).astype(BF16)


def _rope_lane_tables(seq):
    inv_freq = 1.0 / (ROPE_THETA ** (jnp.arange(0, MLA_ROPE, 2, dtype=F32) / MLA_ROPE))
    ang = jnp.arange(seq, dtype=F32)[:, None] * inv_freq[None, :]
    cos, sin = jnp.cos(ang), jnp.sin(ang)
    cos_t = jnp.concatenate([cos, cos, cos, cos], axis=1)
    sin_t = jnp.concatenate([-sin, sin, -sin, sin], axis=1)
    return cos_t, sin_t


def kernel(x, norm_mix_pre, norm_mix_post, norm_ffn_pre, norm_ffn_post, w_in, sgu_ln_g, sgu_ln_b, sgu_w, sgu_b, mla_q_norm, mla_wq_b, mla_kv_norm, mla_wkv_b, w_out, w_up, w_down):
    batch, seq, d = x.shape
    depth = w_in.shape[0]
    cos_t, sin_t = _rope_lane_tables(seq)
    d_tab, qd_tab, kd_tab, cd_tab = retention_tables()

    xf = x.reshape(batch * seq, d)
    h = rmsnorm_rows(xf, norm_mix_pre[0])
    for l in range(depth):
        proj = in_proj(h, _prep_w_in(w_in[l]), cos_t, sin_t, seq)
        bias_full = jnp.repeat(sgu_b[l].T, SGU_WIDTH // SGU_HEADS, axis=1)
        out_a = sgu_mixer(proj, sgu_ln_g[l], sgu_ln_b[l], sgu_w[l], bias_full)
        q = q_proj(proj, mla_q_norm[l], _prep_wq(mla_wq_b[l]), cos_t, sin_t, seq)
        kv = kv_proj(proj, mla_kv_norm[l], _prep_wkv(mla_wkv_b[l]))
        out_b = mla_attention(q, kv, proj, batch, seq)
        out_c = retention_mixer(proj, d_tab, qd_tab, kd_tab, cd_tab, batch, seq)
        mixed = out_proj(out_a, out_b, out_c, w_out[l].astype(BF16))
        xf, h2 = residual_norm(xf, mixed, norm_mix_post[l], norm_ffn_pre[l])
        f = ffn(h2, w_up[l].astype(BF16), w_down[l].astype(BF16))
        g_next = norm_mix_pre[l + 1] if l + 1 < depth else None
        xf, h = residual_norm(xf, f, norm_ffn_post[l], g_next)
    return xf.reshape(batch, seq, d)
```

```python
import functools
import math

import numpy as np
import jax
import jax.numpy as jnp
from jax import lax
from jax.experimental import pallas as pl
from jax.experimental.pallas import tpu as pltpu

F32 = jnp.float32
BF16 = jnp.bfloat16

D_MODEL = 4096
CHUNK = 64
EPS = 1e-6
ROPE_THETA = 10000.0
SGU_BLOCK = 128
SGU_WIDTH = 1024
SGU_HEADS = 8
MLA_HEADS = 16
MLA_NOPE = 128
MLA_ROPE = 64
MLA_V = 128
MLA_Q_RANK = 1024
MLA_KV_RANK = 512
MLA_WIDTH = MLA_HEADS * MLA_V
RET_HEADS = 8
RET_QK = 64
RET_V = 128
RET_WIDTH = RET_HEADS * RET_V
FF_DIM = 4 * D_MODEL

LANES = 128
VMEM_CAP = 64 * 1024 * 1024

IN_TILE = 1024
COL_U = 0 * IN_TILE
COL_V = 1 * IN_TILE
COL_CQ = 2 * IN_TILE
COL_RV = 3 * IN_TILE
COL_RG = 4 * IN_TILE
COL_CKV = 5 * IN_TILE
COL_KR = COL_CKV + MLA_KV_RANK
COL_RQ = 6 * IN_TILE
COL_RK = COL_RQ + RET_HEADS * RET_QK
IN_WIDTH_PADDED = 7 * IN_TILE
ROPE_MIXED_TILE = 5
ROPE_FULL_TILE = 6

TM = 1024
TM_NORM = 256
TM_FFN = 512
TF_FFN = 512
TF_FFN_CAST = 256
TN_OUT = 512
TR_PREP = 256
TB_SGU = 512
TQ = 512
RET_L = 256


def _vmem_limit(nbytes):
    return int(min(VMEM_CAP - (2 << 20), nbytes))


def _params(sem, nbytes):
    return pltpu.CompilerParams(dimension_semantics=sem, vmem_limit_bytes=_vmem_limit(nbytes))


def _rms(x, g):
    return x * lax.rsqrt(jnp.mean(x * x, axis=-1, keepdims=True) + EPS) * g


def _rmsnorm_kernel(x_ref, g_ref, o_ref):
    o_ref[...] = _rms(x_ref[...], g_ref[...]).astype(o_ref.dtype)


def rmsnorm_rows(x, g):
    t, d = x.shape
    return pl.pallas_call(
        _rmsnorm_kernel,
        out_shape=jax.ShapeDtypeStruct((t, d), BF16),
        grid=(t // TM_NORM,),
        in_specs=[pl.BlockSpec((TM_NORM, d), lambda i: (i, 0)),
                  pl.BlockSpec((1, d), lambda i: (0, 0))],
        out_specs=pl.BlockSpec((TM_NORM, d), lambda i: (i, 0)),
        compiler_params=_params(("parallel",), 32 << 20),
        name="rmsnorm_rows",
    )(x, g.reshape(1, d))


def _residual_norm_kernel(x_ref, y_ref, gp_ref, gn_ref, xo_ref, ho_ref):
    xn = x_ref[...] + _rms(y_ref[...], gp_ref[...])
    xo_ref[...] = xn
    ho_ref[...] = _rms(xn, gn_ref[...]).astype(ho_ref.dtype)


def _residual_kernel(x_ref, y_ref, gp_ref, xo_ref):
    xo_ref[...] = x_ref[...] + _rms(y_ref[...], gp_ref[...])


def residual_norm(x, y, g_post, g_next):
    t, d = x.shape
    row = pl.BlockSpec((TM_NORM, d), lambda i: (i, 0))
    vec = pl.BlockSpec((1, d), lambda i: (0, 0))
    if g_next is None:
        return pl.pallas_call(
            _residual_kernel,
            out_shape=jax.ShapeDtypeStruct((t, d), F32),
            grid=(t // TM_NORM,),
            in_specs=[row, row, vec],
            out_specs=row,
            compiler_params=_params(("parallel",), 40 << 20),
            name="residual",
        )(x, y, g_post.reshape(1, d)), None
    return pl.pallas_call(
        _residual_norm_kernel,
        out_shape=(jax.ShapeDtypeStruct((t, d), F32), jax.ShapeDtypeStruct((t, d), BF16)),
        grid=(t // TM_NORM,),
        in_specs=[row, row, vec, vec],
        out_specs=(row, row),
        compiler_params=_params(("parallel",), 40 << 20),
        name="residual_norm",
    )(x, y, g_post.reshape(1, d), g_next.reshape(1, d))


def _rope_cols(x, cos, sin):
    lane = lax.broadcasted_iota(jnp.int32, x.shape, 1)
    first_half = (lane % MLA_ROPE) < (MLA_ROPE // 2)
    partner = jnp.where(first_half, pltpu.roll(x, LANES - MLA_ROPE // 2, 1), pltpu.roll(x, MLA_ROPE // 2, 1))
    return x * cos + partner * sin


def _in_proj_kernel(h_ref, w_ref, cos_ref, sin_ref, o_ref):
    j = pl.program_id(1)
    acc = jnp.dot(h_ref[...], w_ref[...], preferred_element_type=F32)

    @pl.when(j < ROPE_MIXED_TILE)
    def _():
        o_ref[...] = acc.astype(o_ref.dtype)

    @pl.when(j == ROPE_MIXED_TILE)
    def _():
        lo = COL_KR - COL_CKV
        o_ref[:, :lo] = acc[:, :lo].astype(o_ref.dtype)
        for c in range(lo // LANES, lo // LANES + 2):
            sl = slice(c * LANES, (c + 1) * LANES)
            o_ref[:, sl] = _rope_cols(acc[:, sl], cos_ref[...], sin_ref[...]).astype(o_ref.dtype)
        o_ref[:, lo + 2 * LANES:] = acc[:, lo + 2 * LANES:].astype(o_ref.dtype)

    @pl.when(j == ROPE_FULL_TILE)
    def _():
        q_cols = (RET_HEADS * RET_QK) // LANES
        for c in range(IN_TILE // LANES):
            sl = slice(c * LANES, (c + 1) * LANES)
            r = _rope_cols(acc[:, sl], cos_ref[...], sin_ref[...])
            if c < q_cols:
                r = r * (RET_QK ** -0.5)
            o_ref[:, sl] = r.astype(o_ref.dtype)


def in_proj(h, w, cos_t, sin_t, seq):
    t, d = h.shape
    n = w.shape[1]
    pos_blocks = seq // TM
    tab = pl.BlockSpec((TM, LANES), lambda i, j: (i % pos_blocks, 0))
    return pl.pallas_call(
        _in_proj_kernel,
        out_shape=jax.ShapeDtypeStruct((t, n), BF16),
        grid=(t // TM, n // IN_TILE),
        in_specs=[pl.BlockSpec((TM, d), lambda i, j: (i, 0)),
                  pl.BlockSpec((d, IN_TILE), lambda i, j: (0, j)),
                  tab, tab],
        out_specs=pl.BlockSpec((TM, IN_TILE), lambda i, j: (i, j)),
        compiler_params=_params(("parallel", "arbitrary"), 52 << 20),
        name="in_proj",
    )(h, w, cos_t, sin_t)


def _sgu_kernel(u_ref, v_ref, g_ref, b_ref, w_ref, bias_ref, o_ref):
    row = lax.broadcasted_iota(jnp.int32, (SGU_BLOCK, SGU_BLOCK), 0)
    col = lax.broadcasted_iota(jnp.int32, (SGU_BLOCK, SGU_BLOCK), 1)
    keep = (col // CHUNK) <= (row // CHUNK)
    for g in range(SGU_HEADS):
        sl = slice(g * LANES, (g + 1) * LANES)
        wg = jnp.where(keep, w_ref[g], 0.0).astype(BF16)
        vg = v_ref[:, sl].astype(F32)
        mu = jnp.mean(vg, axis=-1, keepdims=True)
        vc = vg - mu
        y = vc * lax.rsqrt(jnp.mean(vc * vc, axis=-1, keepdims=True) + EPS) * g_ref[:, sl] + b_ref[:, sl]
        yb = y.astype(BF16)
        for n in range(TB_SGU // SGU_BLOCK):
            rs = slice(n * SGU_BLOCK, (n + 1) * SGU_BLOCK)
            mixed = jnp.dot(wg, yb[rs, :], preferred_element_type=F32) + bias_ref[:, sl]
            o_ref[rs, sl] = (u_ref[rs, sl].astype(F32) * mixed).astype(o_ref.dtype)


def sgu_mixer(proj, ln_g, ln_b, w_s, bias_full):
    t = proj.shape[0]
    vec = pl.BlockSpec((1, SGU_WIDTH), lambda i: (0, 0))
    return pl.pallas_call(
        _sgu_kernel,
        out_shape=jax.ShapeDtypeStruct((t, SGU_WIDTH), BF16),
        grid=(t // TB_SGU,),
        in_specs=[pl.BlockSpec((TB_SGU, SGU_WIDTH), lambda i: (i, COL_U // SGU_WIDTH)),
                  pl.BlockSpec((TB_SGU, SGU_WIDTH), lambda i: (i, COL_V // SGU_WIDTH)),
                  vec, vec,
                  pl.BlockSpec((SGU_HEADS, SGU_BLOCK, SGU_BLOCK), lambda i: (0, 0, 0)),
                  pl.BlockSpec((SGU_BLOCK, SGU_WIDTH), lambda i: (0, 0))],
        out_specs=pl.BlockSpec((TB_SGU, SGU_WIDTH), lambda i: (i, 0)),
        compiler_params=_params(("parallel",), 32 << 20),
        name="sgu_mixer",
    )(proj, proj, ln_g.reshape(1, -1), ln_b.reshape(1, -1), w_s, bias_full)


Q_SCALE = (MLA_NOPE + MLA_ROPE) ** -0.5
Q_ROPE_TILE = (MLA_HEADS * MLA_NOPE) // IN_TILE


def _q_proj_kernel(c_ref, g_ref, w_ref, cos_ref, sin_ref, o_ref, cn_ref):
    j = pl.program_id(1)

    @pl.when(j == 0)
    def _():
        cn_ref[...] = _rms(c_ref[...].astype(F32), g_ref[...]).astype(cn_ref.dtype)

    acc = jnp.dot(cn_ref[...], w_ref[...], preferred_element_type=F32) * Q_SCALE

    @pl.when(j < Q_ROPE_TILE)
    def _():
        o_ref[...] = acc.astype(o_ref.dtype)

    @pl.when(j >= Q_ROPE_TILE)
    def _():
        for c in range(IN_TILE // LANES):
            sl = slice(c * LANES, (c + 1) * LANES)
            o_ref[:, sl] = _rope_cols(acc[:, sl], cos_ref[...], sin_ref[...]).astype(o_ref.dtype)


def q_proj(proj, g, w, cos_t, sin_t, seq):
    t = proj.shape[0]
    n = w.shape[1]
    pos_blocks = seq // TM
    tab = pl.BlockSpec((TM, LANES), lambda i, j: (i % pos_blocks, 0))
    return pl.pallas_call(
        _q_proj_kernel,
        out_shape=jax.ShapeDtypeStruct((t, n), BF16),
        grid=(t // TM, n // IN_TILE),
        in_specs=[pl.BlockSpec((TM, MLA_Q_RANK), lambda i, j: (i, COL_CQ // MLA_Q_RANK)),
                  pl.BlockSpec((1, MLA_Q_RANK), lambda i, j: (0, 0)),
                  pl.BlockSpec((MLA_Q_RANK, IN_TILE), lambda i, j: (0, j)),
                  tab, tab],
        out_specs=pl.BlockSpec((TM, IN_TILE), lambda i, j: (i, j)),
        scratch_shapes=[pltpu.VMEM((TM, MLA_Q_RANK), BF16)],
        compiler_params=_params(("parallel", "arbitrary"), 40 << 20),
        name="q_proj",
    )(proj, g.reshape(1, -1), w, cos_t, sin_t)


def _kv_proj_kernel(c_ref, g_ref, w_ref, o_ref, cn_ref):
    @pl.when(pl.program_id(1) == 0)
    def _():
        cn_ref[...] = _rms(c_ref[...].astype(F32), g_ref[...]).astype(cn_ref.dtype)

    o_ref[...] = jnp.dot(cn_ref[...], w_ref[...].astype(BF16), preferred_element_type=F32).astype(o_ref.dtype)


def kv_proj(proj, g, w, layer):
    t = proj.shape[0]
    n = w.shape[2]
    return pl.pallas_call(
        _kv_proj_kernel,
        out_shape=jax.ShapeDtypeStruct((t, n), BF16),
        grid=(t // TM, n // IN_TILE),
        in_specs=[pl.BlockSpec((TM, MLA_KV_RANK), lambda i, j: (i, COL_CKV // MLA_KV_RANK)),
                  pl.BlockSpec((1, MLA_KV_RANK), lambda i, j: (0, 0)),
                  pl.BlockSpec((None, MLA_KV_RANK, IN_TILE), lambda i, j: (layer, 0, j))],
        out_specs=pl.BlockSpec((TM, IN_TILE), lambda i, j: (i, j)),
        scratch_shapes=[pltpu.VMEM((TM, MLA_KV_RANK), BF16)],
        compiler_params=_params(("parallel", "arbitrary"), 32 << 20),
        name="kv_proj",
    )(proj, g.reshape(1, -1), w)


ATTN_GROUP = 4


def _attn_tiles(q, kn_ref, kr_ref, v_ref, base, ntiles, mask_last, state):
    m, l, acc = state
    for t in range(ntiles):
        st = pl.multiple_of(base + t * TQ, TQ)
        k = jnp.concatenate([kn_ref[pl.ds(st, TQ), :], kr_ref[pl.ds(st, TQ), :]], axis=1)
        s = lax.dot_general(q, k, (((1,), (1,)), ((), ())), preferred_element_type=F32)
        if mask_last and t == ntiles - 1:
            row = lax.broadcasted_iota(jnp.int32, (TQ, TQ), 0)
            col = lax.broadcasted_iota(jnp.int32, (TQ, TQ), 1)
            s = jnp.where((col // CHUNK) <= (row // CHUNK), s, -jnp.inf)
        m_next = jnp.maximum(m, jnp.max(s, axis=1, keepdims=True))
        alpha = jnp.exp(m - m_next)
        p = jnp.exp(s - jnp.tile(m_next, (1, TQ // LANES)))
        l = alpha * l + jnp.sum(p, axis=1, keepdims=True)
        acc = alpha * acc + jnp.dot(p.astype(BF16), v_ref[pl.ds(st, TQ), :], preferred_element_type=F32)
        m = m_next
    return m, l, acc


def _attn_kernel(qn_ref, qr_ref, kn_ref, kr_ref, v_ref, o_ref, m_ref, l_ref, acc_ref):
    qi = pl.program_id(2)
    q = jnp.concatenate([qn_ref[...], qr_ref[...]], axis=1)
    refs = (kn_ref, kr_ref, v_ref)

    m_ref[...] = jnp.full_like(m_ref, -jnp.inf)
    l_ref[...] = jnp.zeros_like(l_ref)
    acc_ref[...] = jnp.zeros_like(acc_ref)

    def body(g, carry):
        base = pl.multiple_of(g * (ATTN_GROUP * TQ), ATTN_GROUP * TQ)
        m, l, acc = _attn_tiles(q, *refs, base, ATTN_GROUP, False, (m_ref[...], l_ref[...], acc_ref[...]))
        m_ref[...] = m
        l_ref[...] = l
        acc_ref[...] = acc
        return carry

    lax.fori_loop(0, qi // ATTN_GROUP, body, 0)

    base = pl.multiple_of((qi // ATTN_GROUP) * (ATTN_GROUP * TQ), ATTN_GROUP * TQ)
    for r in range(ATTN_GROUP):
        @pl.when(qi % ATTN_GROUP == r)
        def _():
            _, l, acc = _attn_tiles(q, *refs, base, r + 1, True, (m_ref[...], l_ref[...], acc_ref[...]))
            o_ref[...] = (acc / l).astype(o_ref.dtype)


def mla_attention(q, kv, proj, batch, seq):
    t = q.shape[0]
    nq = seq // TQ
    kr_block0 = COL_KR // LANES
    return pl.pallas_call(
        _attn_kernel,
        out_shape=jax.ShapeDtypeStruct((t, MLA_WIDTH), BF16),
        grid=(batch, MLA_HEADS, nq),
        in_specs=[pl.BlockSpec((TQ, LANES), lambda b, h, i: (b * nq + i, h)),
                  pl.BlockSpec((TQ, LANES), lambda b, h, i: (b * nq + i, MLA_HEADS + h // 2)),
                  pl.BlockSpec((seq, LANES), lambda b, h, i: (b, 2 * h)),
                  pl.BlockSpec((seq, LANES), lambda b, h, i: (b, kr_block0 + h % 2)),
                  pl.BlockSpec((seq, LANES), lambda b, h, i: (b, 2 * h + 1))],
        out_specs=pl.BlockSpec((TQ, LANES), lambda b, h, i: (b * nq + i, h)),
        scratch_shapes=[pltpu.VMEM((TQ, LANES), F32), pltpu.VMEM((TQ, LANES), F32),
                        pltpu.VMEM((TQ, LANES), F32)],
        compiler_params=_params(("parallel", "parallel", "arbitrary"), 32 << 20),
        name="mla_attention",
    )(q, q, kv, proj, kv)


def _ret_kernel(q_ref, k_ref, v_ref, g_ref, d_ref, qd_ref, kd_ref, cd_ref, o_ref, s_ref):
    @pl.when(pl.program_id(2) == 0)
    def _():
        s_ref[...] = jnp.zeros_like(s_ref)

    q = q_ref[...]
    k = k_ref[...]
    qf = q.astype(F32)
    kdec = (k.astype(F32) * kd_ref[...]).astype(BF16)
    lane = lax.broadcasted_iota(jnp.int32, q.shape, 1)
    for hh in range(2):
        own = (lane // RET_QK) == hh
        vs = slice(hh * RET_V, (hh + 1) * RET_V)
        v = v_ref[:, vs]
        qm = jnp.where(own, q, jnp.zeros_like(q))
        s = lax.dot_general(qm, k, (((1,), (1,)), ((), ())), preferred_element_type=F32)
        sd = (s * d_ref[hh]).astype(BF16)
        qx = jnp.where(own, qf * qd_ref[...], 0.0).astype(BF16)
        state = s_ref[hh]
        o = (jnp.dot(sd, v, preferred_element_type=F32)
             + jnp.dot(qx, state.astype(BF16), preferred_element_type=F32))
        s_ref[hh] = cd_ref[hh] * state + lax.dot_general(
            kdec, v, (((0,), (0,)), ((), ())), preferred_element_type=F32)
        o = o * lax.rsqrt(jnp.mean(o * o, axis=-1, keepdims=True) + EPS)
        gate = g_ref[:, vs].astype(F32)
        o_ref[:, vs] = (gate * jax.nn.sigmoid(gate) * o).astype(o_ref.dtype)


def retention_mixer(proj, d_tab, qd_tab, kd_tab, cd_tab, batch, seq):
    t = proj.shape[0]
    nsc = seq // RET_L
    pairs = RET_HEADS // 2
    pair_w = 2 * RET_V
    return pl.pallas_call(
        _ret_kernel,
        out_shape=jax.ShapeDtypeStruct((t, RET_WIDTH), BF16),
        grid=(batch, pairs, nsc),
        in_specs=[pl.BlockSpec((RET_L, LANES), lambda b, p, c: (b * nsc + c, COL_RQ // LANES + p)),
                  pl.BlockSpec((RET_L, LANES), lambda b, p, c: (b * nsc + c, COL_RK // LANES + p)),
                  pl.BlockSpec((RET_L, pair_w), lambda b, p, c: (b * nsc + c, COL_RV // pair_w + p)),
                  pl.BlockSpec((RET_L, pair_w), lambda b, p, c: (b * nsc + c, COL_RG // pair_w + p)),
                  pl.BlockSpec((2, RET_L, RET_L), lambda b, p, c: (p, 0, 0)),
                  pl.BlockSpec((None, RET_L, LANES), lambda b, p, c: (p, 0, 0)),
                  pl.BlockSpec((None, RET_L, LANES), lambda b, p, c: (p, 0, 0)),
                  pl.BlockSpec((2, LANES, RET_V), lambda b, p, c: (p, 0, 0))],
        out_specs=pl.BlockSpec((RET_L, pair_w), lambda b, p, c: (b * nsc + c, p)),
        scratch_shapes=[pltpu.VMEM((2, LANES, RET_V), F32)],
        compiler_params=_params(("parallel", "parallel", "arbitrary"), 32 << 20),
        name="retention_mixer",
    )(proj, proj, proj, proj, d_tab, qd_tab, kd_tab, cd_tab)


def retention_tables():
    log_gamma = jnp.log1p(-jnp.exp2(-5.0 - jnp.arange(RET_HEADS, dtype=F32)))
    idx = jnp.arange(RET_L, dtype=F32)
    dist = jnp.abs(idx[:, None] - idx[None, :])
    chunk = jnp.arange(RET_L) // CHUNK
    visible = chunk[None, :] <= chunk[:, None]
    d_tab = jnp.where(visible[None], jnp.exp(log_gamma[:, None, None] * dist[None]), 0.0)
    lane_head = jnp.arange(LANES) // RET_QK
    head_of_lane = 2 * jnp.arange(RET_HEADS // 2)[:, None] + lane_head[None, :]
    lg_lane = log_gamma[head_of_lane]
    qd_tab = jnp.exp(lg_lane[:, None, :] * (idx[None, :, None] + 1.0))
    kd_tab = jnp.exp(lg_lane[:, None, :] * (RET_L - 1.0 - idx[None, :, None]))
    cd = jnp.exp(log_gamma * RET_L)
    cd_tab = jnp.broadcast_to(cd[:, None, None], (RET_HEADS, LANES, RET_V))
    return d_tab, qd_tab, kd_tab, cd_tab


def _out_proj_kernel(a_ref, b_ref, c_ref, w_ref, o_ref):
    ka = a_ref.shape[1]
    kb = b_ref.shape[1]
    acc = jnp.dot(a_ref[...], w_ref[:ka, :].astype(BF16), preferred_element_type=F32)
    acc = acc + jnp.dot(b_ref[...], w_ref[ka:ka + kb, :].astype(BF16), preferred_element_type=F32)
    acc = acc + jnp.dot(c_ref[...], w_ref[ka + kb:, :].astype(BF16), preferred_element_type=F32)
    o_ref[...] = acc


def out_proj(a, b, c, w, layer):
    t = a.shape[0]
    _, k, n = w.shape
    return pl.pallas_call(
        _out_proj_kernel,
        out_shape=jax.ShapeDtypeStruct((t, n), F32),
        grid=(t // TM, n // TN_OUT),
        in_specs=[pl.BlockSpec((TM, a.shape[1]), lambda i, j: (i, 0)),
                  pl.BlockSpec((TM, b.shape[1]), lambda i, j: (i, 0)),
                  pl.BlockSpec((TM, c.shape[1]), lambda i, j: (i, 0)),
                  pl.BlockSpec((None, k, TN_OUT), lambda i, j: (layer, 0, j))],
        out_specs=pl.BlockSpec((TM, TN_OUT), lambda i, j: (i, j)),
        compiler_params=_params(("parallel", "arbitrary"), 56 << 20),
        name="out_proj",
    )(a, b, c, w)


def _ffn_accumulate(h_ref, wu, wd, o_ref):
    @pl.when(pl.program_id(1) == 0)
    def _():
        o_ref[...] = jnp.zeros_like(o_ref)

    a = jnp.dot(h_ref[...], wu, preferred_element_type=F32)
    a = jnp.square(jnp.maximum(a, 0.0)).astype(BF16)
    o_ref[...] += jnp.dot(a, wd, preferred_element_type=F32)


def _ffn_first_kernel(h_ref, wu_ref, wd_ref, o_ref, wub_ref, wdb_ref):
    wub_ref[...] = wu_ref[...].astype(BF16)
    wdb_ref[...] = wd_ref[...].astype(BF16)
    _ffn_accumulate(h_ref, wub_ref[...], wdb_ref[...], o_ref)


def _ffn_rest_kernel(h_ref, wu_ref, wd_ref, f_first_ref, o_ref):
    del f_first_ref
    _ffn_accumulate(h_ref, wu_ref[...], wd_ref[...], o_ref)


def ffn(h, w_up, w_down, layer):
    t, d = h.shape
    ff = w_up.shape[2]
    f_first, wu_b, wd_b = pl.pallas_call(
        _ffn_first_kernel,
        out_shape=(jax.ShapeDtypeStruct((t, d), F32),
                   jax.ShapeDtypeStruct((d, ff), BF16),
                   jax.ShapeDtypeStruct((ff, d), BF16)),
        grid=(1, ff // TF_FFN_CAST),
        in_specs=[pl.BlockSpec((TM_FFN, d), lambda i, j: (0, 0)),
                  pl.BlockSpec((None, d, TF_FFN_CAST), lambda i, j: (layer, 0, j)),
                  pl.BlockSpec((None, TF_FFN_CAST, d), lambda i, j: (layer, j, 0))],
        out_specs=(pl.BlockSpec((TM_FFN, d), lambda i, j: (0, 0)),
                   pl.BlockSpec((d, TF_FFN_CAST), lambda i, j: (0, j)),
                   pl.BlockSpec((TF_FFN_CAST, d), lambda i, j: (j, 0))),
        compiler_params=_params(("arbitrary", "arbitrary"), 60 << 20),
        name="ffn_first",
    )(h, w_up, w_down)
    return pl.pallas_call(
        _ffn_rest_kernel,
        out_shape=jax.ShapeDtypeStruct((t, d), F32),
        grid=(t // TM_FFN - 1, ff // TF_FFN),
        in_specs=[pl.BlockSpec((TM_FFN, d), lambda i, j: (i + 1, 0)),
                  pl.BlockSpec((d, TF_FFN), lambda i, j: (0, j)),
                  pl.BlockSpec((TF_FFN, d), lambda i, j: (j, 0)),
                  pl.BlockSpec(memory_space=pl.ANY)],
        out_specs=pl.BlockSpec((TM_FFN, d), lambda i, j: (i + 1, 0)),
        input_output_aliases={3: 0},
        compiler_params=_params(("parallel", "arbitrary"), 56 << 20),
        name="ffn_rest",
    )(h, wu_b, wd_b, f_first)


_SRC_SIZES = (SGU_WIDTH, SGU_WIDTH, MLA_Q_RANK, MLA_KV_RANK, MLA_ROPE,
              RET_HEADS * RET_QK, RET_HEADS * RET_QK, RET_WIDTH, RET_WIDTH)
(_SRC_U, _SRC_V, _SRC_CQ, _SRC_CKV, _SRC_KR, _SRC_RQ, _SRC_RK, _SRC_RV, _SRC_RG,
 IN_WIDTH) = [int(o) for o in np.concatenate([[0], np.cumsum(_SRC_SIZES)])]
_W_IN_MOVES = (
    (COL_U, _SRC_U, 3 * IN_TILE),
    (COL_RV, _SRC_RV, RET_WIDTH),
    (COL_RG, _SRC_RG, RET_WIDTH),
    (COL_CKV, _SRC_CKV, MLA_KV_RANK),
    (COL_KR, _SRC_KR, MLA_ROPE),
    (COL_KR + LANES + MLA_ROPE, _SRC_KR, MLA_ROPE),
    (COL_RQ, _SRC_RQ, RET_HEADS * RET_QK),
    (COL_RK, _SRC_RK, RET_HEADS * RET_QK),
)


def _w_in_prep_kernel(w_ref, o_ref):
    pad0 = COL_KR
    o_ref[:, pad0:COL_RQ] = jnp.zeros((o_ref.shape[0], COL_RQ - pad0), o_ref.dtype)
    for dst, src, width in _W_IN_MOVES:
        o_ref[:, dst:dst + width] = w_ref[:, src:src + width].astype(o_ref.dtype)


def prep_w_in(w, layer):
    _, d, n = w.shape
    return pl.pallas_call(
        _w_in_prep_kernel,
        out_shape=jax.ShapeDtypeStruct((d, IN_WIDTH_PADDED), BF16),
        grid=(d // TR_PREP,),
        in_specs=[pl.BlockSpec((None, TR_PREP, n), lambda i: (layer, i, 0))],
        out_specs=pl.BlockSpec((TR_PREP, IN_WIDTH_PADDED), lambda i: (i, 0)),
        compiler_params=_params(("parallel",), 40 << 20),
        name="prep_w_in",
    )(w)


def _prep_wq(w):
    k = w.shape[0]
    w3 = w.reshape(k, MLA_HEADS, MLA_NOPE + MLA_ROPE)
    nope = w3[:, :, :MLA_NOPE].reshape(k, MLA_HEADS * MLA_NOPE)
    rope = w3[:, :, MLA_NOPE:].reshape(k, MLA_HEADS * MLA_ROPE)
    return jnp.concatenate([nope, rope], axis=1).astype(BF16)


def _rope_lane_tables(seq):
    inv_freq = 1.0 / (ROPE_THETA ** (jnp.arange(0, MLA_ROPE, 2, dtype=F32) / MLA_ROPE))
    ang = jnp.arange(seq, dtype=F32)[:, None] * inv_freq[None, :]
    cos, sin = jnp.cos(ang), jnp.sin(ang)
    cos_t = jnp.concatenate([cos, cos, cos, cos], axis=1)
    sin_t = jnp.concatenate([-sin, sin, -sin, sin], axis=1)
    return cos_t, sin_t


def kernel(x, norm_mix_pre, norm_mix_post, norm_ffn_pre, norm_ffn_post, w_in, sgu_ln_g, sgu_ln_b, sgu_w, sgu_b, mla_q_norm, mla_wq_b, mla_kv_norm, mla_wkv_b, w_out, w_up, w_down):
    batch, seq, d = x.shape
    depth = w_in.shape[0]
    cos_t, sin_t = _rope_lane_tables(seq)
    d_tab, qd_tab, kd_tab, cd_tab = retention_tables()

    xf = x.reshape(batch * seq, d)
    h = rmsnorm_rows(xf, norm_mix_pre[0])
    for l in range(depth):
        proj = in_proj(h, prep_w_in(w_in, l), cos_t, sin_t, seq)
        bias_full = jnp.repeat(sgu_b[l].T, SGU_WIDTH // SGU_HEADS, axis=1)
        out_a = sgu_mixer(proj, sgu_ln_g[l], sgu_ln_b[l], sgu_w[l], bias_full)
        q = q_proj(proj, mla_q_norm[l], _prep_wq(mla_wq_b[l]), cos_t, sin_t, seq)
        kv = kv_proj(proj, mla_kv_norm[l], mla_wkv_b, l)
        out_b = mla_attention(q, kv, proj, batch, seq)
        out_c = retention_mixer(proj, d_tab, qd_tab, kd_tab, cd_tab, batch, seq)
        mixed = out_proj(out_a, out_b, out_c, w_out, l)
        xf, h2 = residual_norm(xf, mixed, norm_mix_post[l], norm_ffn_pre[l])
        f = ffn(h2, w_up, w_down, l)
        g_next = norm_mix_pre[l + 1] if l + 1 < depth else None
        xf, h = residual_norm(xf, f, norm_ffn_post[l], g_next)
    return xf.reshape(batch, seq, d)
```

```python
import functools
import math

import numpy as np
import jax
import jax.numpy as jnp
from jax import lax
from jax.experimental import pallas as pl
from jax.experimental.pallas import tpu as pltpu

F32 = jnp.float32
BF16 = jnp.bfloat16

D_MODEL = 4096
CHUNK = 64
EPS = 1e-6
ROPE_THETA = 10000.0
SGU_BLOCK = 128
SGU_WIDTH = 1024
SGU_HEADS = 8
MLA_HEADS = 16
MLA_NOPE = 128
MLA_ROPE = 64
MLA_V = 128
MLA_Q_RANK = 1024
MLA_KV_RANK = 512
MLA_WIDTH = MLA_HEADS * MLA_V
RET_HEADS = 8
RET_QK = 64
RET_V = 128
RET_WIDTH = RET_HEADS * RET_V
FF_DIM = 4 * D_MODEL

LANES = 128
VMEM_CAP = 64 * 1024 * 1024

IN_TILE = 1024
COL_U = 0 * IN_TILE
COL_V = 1 * IN_TILE
COL_CQ = 2 * IN_TILE
COL_RV = 3 * IN_TILE
COL_RG = 4 * IN_TILE
COL_CKV = 5 * IN_TILE
COL_KR = COL_CKV + MLA_KV_RANK
COL_RQ = 6 * IN_TILE
COL_RK = COL_RQ + RET_HEADS * RET_QK
IN_WIDTH_PADDED = 7 * IN_TILE
ROPE_MIXED_TILE = 5
ROPE_FULL_TILE = 6

TM = 1024
TM_NORM = 256
TM_FFN = 512
TF_FFN = 512
TF_FFN_CAST = 256
TN_OUT = 512
TK_PREP = 256
TB_SGU = 512
TQ = 512
RET_L = 256


def _vmem_limit(nbytes):
    return int(min(VMEM_CAP - (2 << 20), nbytes))


def _params(sem, nbytes):
    return pltpu.CompilerParams(dimension_semantics=sem, vmem_limit_bytes=_vmem_limit(nbytes))


def _rms(x, g):
    return x * lax.rsqrt(jnp.mean(x * x, axis=-1, keepdims=True) + EPS) * g


def _rmsnorm_kernel(x_ref, g_ref, o_ref):
    o_ref[...] = _rms(x_ref[...], g_ref[...]).astype(o_ref.dtype)


def rmsnorm_rows(x, g):
    t, d = x.shape
    return pl.pallas_call(
        _rmsnorm_kernel,
        out_shape=jax.ShapeDtypeStruct((t, d), BF16),
        grid=(t // TM_NORM,),
        in_specs=[pl.BlockSpec((TM_NORM, d), lambda i: (i, 0)),
                  pl.BlockSpec((1, d), lambda i: (0, 0))],
        out_specs=pl.BlockSpec((TM_NORM, d), lambda i: (i, 0)),
        compiler_params=_params(("parallel",), 32 << 20),
        name="rmsnorm_rows",
    )(x, g.reshape(1, d))


def _residual_norm_kernel(x_ref, y_ref, gp_ref, gn_ref, xo_ref, ho_ref):
    xn = x_ref[...] + _rms(y_ref[...], gp_ref[...])
    xo_ref[...] = xn
    ho_ref[...] = _rms(xn, gn_ref[...]).astype(ho_ref.dtype)


def _residual_kernel(x_ref, y_ref, gp_ref, xo_ref):
    xo_ref[...] = x_ref[...] + _rms(y_ref[...], gp_ref[...])


def residual_norm(x, y, g_post, g_next):
    t, d = x.shape
    row = pl.BlockSpec((TM_NORM, d), lambda i: (i, 0))
    vec = pl.BlockSpec((1, d), lambda i: (0, 0))
    if g_next is None:
        return pl.pallas_call(
            _residual_kernel,
            out_shape=jax.ShapeDtypeStruct((t, d), F32),
            grid=(t // TM_NORM,),
            in_specs=[row, row, vec],
            out_specs=row,
            compiler_params=_params(("parallel",), 40 << 20),
            name="residual",
        )(x, y, g_post.reshape(1, d)), None
    return pl.pallas_call(
        _residual_norm_kernel,
        out_shape=(jax.ShapeDtypeStruct((t, d), F32), jax.ShapeDtypeStruct((t, d), BF16)),
        grid=(t // TM_NORM,),
        in_specs=[row, row, vec, vec],
        out_specs=(row, row),
        compiler_params=_params(("parallel",), 40 << 20),
        name="residual_norm",
    )(x, y, g_post.reshape(1, d), g_next.reshape(1, d))


def _rope_cols(x, cos, sin):
    lane = lax.broadcasted_iota(jnp.int32, x.shape, 1)
    first_half = (lane % MLA_ROPE) < (MLA_ROPE // 2)
    partner = jnp.where(first_half, pltpu.roll(x, LANES - MLA_ROPE // 2, 1), pltpu.roll(x, MLA_ROPE // 2, 1))
    return x * cos + partner * sin


def _in_proj_kernel(h_ref, w_ref, cos_ref, sin_ref, o_ref):
    j = pl.program_id(1)
    acc = lax.dot_general(h_ref[...], w_ref[...], (((1,), (1,)), ((), ())), preferred_element_type=F32)

    @pl.when(j < ROPE_MIXED_TILE)
    def _():
        o_ref[...] = acc.astype(o_ref.dtype)

    @pl.when(j == ROPE_MIXED_TILE)
    def _():
        lo = COL_KR - COL_CKV
        o_ref[:, :lo] = acc[:, :lo].astype(o_ref.dtype)
        for c in range(lo // LANES, lo // LANES + 2):
            sl = slice(c * LANES, (c + 1) * LANES)
            o_ref[:, sl] = _rope_cols(acc[:, sl], cos_ref[...], sin_ref[...]).astype(o_ref.dtype)
        o_ref[:, lo + 2 * LANES:] = acc[:, lo + 2 * LANES:].astype(o_ref.dtype)

    @pl.when(j == ROPE_FULL_TILE)
    def _():
        q_cols = (RET_HEADS * RET_QK) // LANES
        for c in range(IN_TILE // LANES):
            sl = slice(c * LANES, (c + 1) * LANES)
            r = _rope_cols(acc[:, sl], cos_ref[...], sin_ref[...])
            if c < q_cols:
                r = r * (RET_QK ** -0.5)
            o_ref[:, sl] = r.astype(o_ref.dtype)


def in_proj(h, w_t, cos_t, sin_t, seq):
    t, d = h.shape
    n = w_t.shape[0]
    pos_blocks = seq // TM
    tab = pl.BlockSpec((TM, LANES), lambda i, j: (i % pos_blocks, 0))
    return pl.pallas_call(
        _in_proj_kernel,
        out_shape=jax.ShapeDtypeStruct((t, n), BF16),
        grid=(t // TM, n // IN_TILE),
        in_specs=[pl.BlockSpec((TM, d), lambda i, j: (i, 0)),
                  pl.BlockSpec((IN_TILE, d), lambda i, j: (j, 0)),
                  tab, tab],
        out_specs=pl.BlockSpec((TM, IN_TILE), lambda i, j: (i, j)),
        compiler_params=_params(("parallel", "arbitrary"), 52 << 20),
        name="in_proj",
    )(h, w_t, cos_t, sin_t)


def _sgu_kernel(u_ref, v_ref, g_ref, b_ref, w_ref, bias_ref, o_ref):
    row = lax.broadcasted_iota(jnp.int32, (SGU_BLOCK, SGU_BLOCK), 0)
    col = lax.broadcasted_iota(jnp.int32, (SGU_BLOCK, SGU_BLOCK), 1)
    keep = (col // CHUNK) <= (row // CHUNK)
    for g in range(SGU_HEADS):
        sl = slice(g * LANES, (g + 1) * LANES)
        wg = jnp.where(keep, w_ref[g], 0.0).astype(BF16)
        vg = v_ref[:, sl].astype(F32)
        mu = jnp.mean(vg, axis=-1, keepdims=True)
        vc = vg - mu
        y = vc * lax.rsqrt(jnp.mean(vc * vc, axis=-1, keepdims=True) + EPS) * g_ref[:, sl] + b_ref[:, sl]
        yb = y.astype(BF16)
        for n in range(TB_SGU // SGU_BLOCK):
            rs = slice(n * SGU_BLOCK, (n + 1) * SGU_BLOCK)
            mixed = jnp.dot(wg, yb[rs, :], preferred_element_type=F32) + bias_ref[:, sl]
            o_ref[rs, sl] = (u_ref[rs, sl].astype(F32) * mixed).astype(o_ref.dtype)


def sgu_mixer(proj, ln_g, ln_b, w_s, bias_full):
    t = proj.shape[0]
    vec = pl.BlockSpec((1, SGU_WIDTH), lambda i: (0, 0))
    return pl.pallas_call(
        _sgu_kernel,
        out_shape=jax.ShapeDtypeStruct((t, SGU_WIDTH), BF16),
        grid=(t // TB_SGU,),
        in_specs=[pl.BlockSpec((TB_SGU, SGU_WIDTH), lambda i: (i, COL_U // SGU_WIDTH)),
                  pl.BlockSpec((TB_SGU, SGU_WIDTH), lambda i: (i, COL_V // SGU_WIDTH)),
                  vec, vec,
                  pl.BlockSpec((SGU_HEADS, SGU_BLOCK, SGU_BLOCK), lambda i: (0, 0, 0)),
                  pl.BlockSpec((SGU_BLOCK, SGU_WIDTH), lambda i: (0, 0))],
        out_specs=pl.BlockSpec((TB_SGU, SGU_WIDTH), lambda i: (i, 0)),
        compiler_params=_params(("parallel",), 32 << 20),
        name="sgu_mixer",
    )(proj, proj, ln_g.reshape(1, -1), ln_b.reshape(1, -1), w_s, bias_full)


Q_SCALE = (MLA_NOPE + MLA_ROPE) ** -0.5
Q_ROPE_TILE = (MLA_HEADS * MLA_NOPE) // IN_TILE


def _q_proj_kernel(c_ref, g_ref, w_ref, cos_ref, sin_ref, o_ref, cn_ref):
    j = pl.program_id(1)

    @pl.when(j == 0)
    def _():
        cn_ref[...] = _rms(c_ref[...].astype(F32), g_ref[...]).astype(cn_ref.dtype)

    acc = jnp.dot(cn_ref[...], w_ref[...], preferred_element_type=F32) * Q_SCALE

    @pl.when(j < Q_ROPE_TILE)
    def _():
        o_ref[...] = acc.astype(o_ref.dtype)

    @pl.when(j >= Q_ROPE_TILE)
    def _():
        for c in range(IN_TILE // LANES):
            sl = slice(c * LANES, (c + 1) * LANES)
            o_ref[:, sl] = _rope_cols(acc[:, sl], cos_ref[...], sin_ref[...]).astype(o_ref.dtype)


def q_proj(proj, g, w, cos_t, sin_t, seq):
    t = proj.shape[0]
    n = w.shape[1]
    pos_blocks = seq // TM
    tab = pl.BlockSpec((TM, LANES), lambda i, j: (i % pos_blocks, 0))
    return pl.pallas_call(
        _q_proj_kernel,
        out_shape=jax.ShapeDtypeStruct((t, n), BF16),
        grid=(t // TM, n // IN_TILE),
        in_specs=[pl.BlockSpec((TM, MLA_Q_RANK), lambda i, j: (i, COL_CQ // MLA_Q_RANK)),
                  pl.BlockSpec((1, MLA_Q_RANK), lambda i, j: (0, 0)),
                  pl.BlockSpec((MLA_Q_RANK, IN_TILE), lambda i, j: (0, j)),
                  tab, tab],
        out_specs=pl.BlockSpec((TM, IN_TILE), lambda i, j: (i, j)),
        scratch_shapes=[pltpu.VMEM((TM, MLA_Q_RANK), BF16)],
        compiler_params=_params(("parallel", "arbitrary"), 40 << 20),
        name="q_proj",
    )(proj, g.reshape(1, -1), w, cos_t, sin_t)


def _kv_proj_kernel(c_ref, g_ref, w_ref, o_ref, cn_ref):
    @pl.when(pl.program_id(1) == 0)
    def _():
        cn_ref[...] = _rms(c_ref[...].astype(F32), g_ref[...]).astype(cn_ref.dtype)

    o_ref[...] = jnp.dot(cn_ref[...], w_ref[...].astype(BF16), preferred_element_type=F32).astype(o_ref.dtype)


def kv_proj(proj, g, w, layer):
    t = proj.shape[0]
    n = w.shape[2]
    return pl.pallas_call(
        _kv_proj_kernel,
        out_shape=jax.ShapeDtypeStruct((t, n), BF16),
        grid=(t // TM, n // IN_TILE),
        in_specs=[pl.BlockSpec((TM, MLA_KV_RANK), lambda i, j: (i, COL_CKV // MLA_KV_RANK)),
                  pl.BlockSpec((1, MLA_KV_RANK), lambda i, j: (0, 0)),
                  pl.BlockSpec((None, MLA_KV_RANK, IN_TILE), lambda i, j: (layer, 0, j))],
        out_specs=pl.BlockSpec((TM, IN_TILE), lambda i, j: (i, j)),
        scratch_shapes=[pltpu.VMEM((TM, MLA_KV_RANK), BF16)],
        compiler_params=_params(("parallel", "arbitrary"), 32 << 20),
        name="kv_proj",
    )(proj, g.reshape(1, -1), w)


ATTN_GROUP = 4


def _attn_tiles(q, kn_ref, kr_ref, v_ref, base, ntiles, mask_last, state):
    m, l, acc = state
    for t in range(ntiles):
        st = pl.multiple_of(base + t * TQ, TQ)
        k = jnp.concatenate([kn_ref[pl.ds(st, TQ), :], kr_ref[pl.ds(st, TQ), :]], axis=1)
        s = lax.dot_general(q, k, (((1,), (1,)), ((), ())), preferred_element_type=F32)
        if mask_last and t == ntiles - 1:
            row = lax.broadcasted_iota(jnp.int32, (TQ, TQ), 0)
            col = lax.broadcasted_iota(jnp.int32, (TQ, TQ), 1)
            s = jnp.where((col // CHUNK) <= (row // CHUNK), s, -jnp.inf)
        m_next = jnp.maximum(m, jnp.max(s, axis=1, keepdims=True))
        alpha = jnp.exp(m - m_next)
        p = jnp.exp(s - jnp.tile(m_next, (1, TQ // LANES)))
        l = alpha * l + jnp.sum(p, axis=1, keepdims=True)
        acc = alpha * acc + jnp.dot(p.astype(BF16), v_ref[pl.ds(st, TQ), :], preferred_element_type=F32)
        m = m_next
    return m, l, acc


def _attn_kernel(qn_ref, qr_ref, kn_ref, kr_ref, v_ref, o_ref, m_ref, l_ref, acc_ref):
    qi = pl.program_id(2)
    q = jnp.concatenate([qn_ref[...], qr_ref[...]], axis=1)
    refs = (kn_ref, kr_ref, v_ref)

    m_ref[...] = jnp.full_like(m_ref, -jnp.inf)
    l_ref[...] = jnp.zeros_like(l_ref)
    acc_ref[...] = jnp.zeros_like(acc_ref)

    def body(g, carry):
        base = pl.multiple_of(g * (ATTN_GROUP * TQ), ATTN_GROUP * TQ)
        m, l, acc = _attn_tiles(q, *refs, base, ATTN_GROUP, False, (m_ref[...], l_ref[...], acc_ref[...]))
        m_ref[...] = m
        l_ref[...] = l
        acc_ref[...] = acc
        return carry

    lax.fori_loop(0, qi // ATTN_GROUP, body, 0)

    base = pl.multiple_of((qi // ATTN_GROUP) * (ATTN_GROUP * TQ), ATTN_GROUP * TQ)
    for r in range(ATTN_GROUP):
        @pl.when(qi % ATTN_GROUP == r)
        def _():
            _, l, acc = _attn_tiles(q, *refs, base, r + 1, True, (m_ref[...], l_ref[...], acc_ref[...]))
            o_ref[...] = (acc / l).astype(o_ref.dtype)


def mla_attention(q, kv, proj, batch, seq):
    t = q.shape[0]
    nq = seq // TQ
    kr_block0 = COL_KR // LANES
    return pl.pallas_call(
        _attn_kernel,
        out_shape=jax.ShapeDtypeStruct((t, MLA_WIDTH), BF16),
        grid=(batch, MLA_HEADS, nq),
        in_specs=[pl.BlockSpec((TQ, LANES), lambda b, h, i: (b * nq + i, h)),
                  pl.BlockSpec((TQ, LANES), lambda b, h, i: (b * nq + i, MLA_HEADS + h // 2)),
                  pl.BlockSpec((seq, LANES), lambda b, h, i: (b, 2 * h)),
                  pl.BlockSpec((seq, LANES), lambda b, h, i: (b, kr_block0 + h % 2)),
                  pl.BlockSpec((seq, LANES), lambda b, h, i: (b, 2 * h + 1))],
        out_specs=pl.BlockSpec((TQ, LANES), lambda b, h, i: (b * nq + i, h)),
        scratch_shapes=[pltpu.VMEM((TQ, LANES), F32), pltpu.VMEM((TQ, LANES), F32),
                        pltpu.VMEM((TQ, LANES), F32)],
        compiler_params=_params(("parallel", "parallel", "arbitrary"), 32 << 20),
        name="mla_attention",
    )(q, q, kv, proj, kv)


def _ret_kernel(q_ref, k_ref, v_ref, g_ref, d_ref, qd_ref, kd_ref, cd_ref, o_ref, s_ref):
    @pl.when(pl.program_id(2) == 0)
    def _():
        s_ref[...] = jnp.zeros_like(s_ref)

    q = q_ref[...]
    k = k_ref[...]
    qf = q.astype(F32)
    kdec = (k.astype(F32) * kd_ref[...]).astype(BF16)
    lane = lax.broadcasted_iota(jnp.int32, q.shape, 1)
    for hh in range(2):
        own = (lane // RET_QK) == hh
        vs = slice(hh * RET_V, (hh + 1) * RET_V)
        v = v_ref[:, vs]
        qm = jnp.where(own, q, jnp.zeros_like(q))
        s = lax.dot_general(qm, k, (((1,), (1,)), ((), ())), preferred_element_type=F32)
        sd = (s * d_ref[hh]).astype(BF16)
        qx = jnp.where(own, qf * qd_ref[...], 0.0).astype(BF16)
        state = s_ref[hh]
        o = (jnp.dot(sd, v, preferred_element_type=F32)
             + jnp.dot(qx, state.astype(BF16), preferred_element_type=F32))
        s_ref[hh] = cd_ref[hh] * state + lax.dot_general(
            kdec, v, (((0,), (0,)), ((), ())), preferred_element_type=F32)
        o = o * lax.rsqrt(jnp.mean(o * o, axis=-1, keepdims=True) + EPS)
        gate = g_ref[:, vs].astype(F32)
        o_ref[:, vs] = (gate * jax.nn.sigmoid(gate) * o).astype(o_ref.dtype)


def retention_mixer(proj, d_tab, qd_tab, kd_tab, cd_tab, batch, seq):
    t = proj.shape[0]
    nsc = seq // RET_L
    pairs = RET_HEADS // 2
    pair_w = 2 * RET_V
    return pl.pallas_call(
        _ret_kernel,
        out_shape=jax.ShapeDtypeStruct((t, RET_WIDTH), BF16),
        grid=(batch, pairs, nsc),
        in_specs=[pl.BlockSpec((RET_L, LANES), lambda b, p, c: (b * nsc + c, COL_RQ // LANES + p)),
                  pl.BlockSpec((RET_L, LANES), lambda b, p, c: (b * nsc + c, COL_RK // LANES + p)),
                  pl.BlockSpec((RET_L, pair_w), lambda b, p, c: (b * nsc + c, COL_RV // pair_w + p)),
                  pl.BlockSpec((RET_L, pair_w), lambda b, p, c: (b * nsc + c, COL_RG // pair_w + p)),
                  pl.BlockSpec((2, RET_L, RET_L), lambda b, p, c: (p, 0, 0)),
                  pl.BlockSpec((None, RET_L, LANES), lambda b, p, c: (p, 0, 0)),
                  pl.BlockSpec((None, RET_L, LANES), lambda b, p, c: (p, 0, 0)),
                  pl.BlockSpec((2, LANES, RET_V), lambda b, p, c: (p, 0, 0))],
        out_specs=pl.BlockSpec((RET_L, pair_w), lambda b, p, c: (b * nsc + c, p)),
        scratch_shapes=[pltpu.VMEM((2, LANES, RET_V), F32)],
        compiler_params=_params(("parallel", "parallel", "arbitrary"), 32 << 20),
        name="retention_mixer",
    )(proj, proj, proj, proj, d_tab, qd_tab, kd_tab, cd_tab)


def retention_tables():
    log_gamma = jnp.log1p(-jnp.exp2(-5.0 - jnp.arange(RET_HEADS, dtype=F32)))
    idx = jnp.arange(RET_L, dtype=F32)
    dist = jnp.abs(idx[:, None] - idx[None, :])
    chunk = jnp.arange(RET_L) // CHUNK
    visible = chunk[None, :] <= chunk[:, None]
    d_tab = jnp.where(visible[None], jnp.exp(log_gamma[:, None, None] * dist[None]), 0.0)
    lane_head = jnp.arange(LANES) // RET_QK
    head_of_lane = 2 * jnp.arange(RET_HEADS // 2)[:, None] + lane_head[None, :]
    lg_lane = log_gamma[head_of_lane]
    qd_tab = jnp.exp(lg_lane[:, None, :] * (idx[None, :, None] + 1.0))
    kd_tab = jnp.exp(lg_lane[:, None, :] * (RET_L - 1.0 - idx[None, :, None]))
    cd = jnp.exp(log_gamma * RET_L)
    cd_tab = jnp.broadcast_to(cd[:, None, None], (RET_HEADS, LANES, RET_V))
    return d_tab, qd_tab, kd_tab, cd_tab


def _out_proj_kernel(a_ref, b_ref, c_ref, w_ref, o_ref):
    ka = a_ref.shape[1]
    kb = b_ref.shape[1]
    acc = jnp.dot(a_ref[...], w_ref[:ka, :].astype(BF16), preferred_element_type=F32)
    acc = acc + jnp.dot(b_ref[...], w_ref[ka:ka + kb, :].astype(BF16), preferred_element_type=F32)
    acc = acc + jnp.dot(c_ref[...], w_ref[ka + kb:, :].astype(BF16), preferred_element_type=F32)
    o_ref[...] = acc


def out_proj(a, b, c, w, layer):
    t = a.shape[0]
    _, k, n = w.shape
    return pl.pallas_call(
        _out_proj_kernel,
        out_shape=jax.ShapeDtypeStruct((t, n), F32),
        grid=(t // TM, n // TN_OUT),
        in_specs=[pl.BlockSpec((TM, a.shape[1]), lambda i, j: (i, 0)),
                  pl.BlockSpec((TM, b.shape[1]), lambda i, j: (i, 0)),
                  pl.BlockSpec((TM, c.shape[1]), lambda i, j: (i, 0)),
                  pl.BlockSpec((None, k, TN_OUT), lambda i, j: (layer, 0, j))],
        out_specs=pl.BlockSpec((TM, TN_OUT), lambda i, j: (i, j)),
        compiler_params=_params(("parallel", "arbitrary"), 56 << 20),
        name="out_proj",
    )(a, b, c, w)


def _ffn_accumulate(h_ref, wu_ref, wd_ref, o_ref):
    @pl.when(pl.program_id(1) == 0)
    def _():
        o_ref[...] = jnp.zeros_like(o_ref)

    a = jnp.dot(h_ref[...], wu_ref[...], preferred_element_type=F32)
    a = jnp.square(jnp.maximum(a, 0.0)).astype(BF16)
    o_ref[...] += jnp.dot(a, wd_ref[...], preferred_element_type=F32)


def _ffn_first_kernel(h_ref, wu_ref, wd_ref, o_ref, wub_ref, wdb_ref):
    wub_ref[...] = wu_ref[...].astype(BF16)
    wdb_ref[...] = wd_ref[...].astype(BF16)
    _ffn_accumulate(h_ref, wub_ref, wdb_ref, o_ref)


def _ffn_rest_kernel(h_ref, wu_ref, wd_ref, f_first_ref, o_ref):
    del f_first_ref
    _ffn_accumulate(h_ref, wu_ref, wd_ref, o_ref)


def ffn(h, w_up, w_down, layer):
    t, d = h.shape
    ff = w_up.shape[2]
    f_first, wu_b, wd_b = pl.pallas_call(
        _ffn_first_kernel,
        out_shape=(jax.ShapeDtypeStruct((t, d), F32),
                   jax.ShapeDtypeStruct((d, ff), BF16),
                   jax.ShapeDtypeStruct((ff, d), BF16)),
        grid=(1, ff // TF_FFN_CAST),
        in_specs=[pl.BlockSpec((TM_FFN, d), lambda i, j: (0, 0)),
                  pl.BlockSpec((None, d, TF_FFN_CAST), lambda i, j: (layer, 0, j)),
                  pl.BlockSpec((None, TF_FFN_CAST, d), lambda i, j: (layer, j, 0))],
        out_specs=(pl.BlockSpec((TM_FFN, d), lambda i, j: (0, 0)),
                   pl.BlockSpec((d, TF_FFN_CAST), lambda i, j: (0, j)),
                   pl.BlockSpec((TF_FFN_CAST, d), lambda i, j: (j, 0))),
        compiler_params=_params(("arbitrary", "arbitrary"), 60 << 20),
        name="ffn_first",
    )(h, w_up, w_down)
    return pl.pallas_call(
        _ffn_rest_kernel,
        out_shape=jax.ShapeDtypeStruct((t, d), F32),
        grid=(t // TM_FFN - 1, ff // TF_FFN),
        in_specs=[pl.BlockSpec((TM_FFN, d), lambda i, j: (i + 1, 0)),
                  pl.BlockSpec((d, TF_FFN), lambda i, j: (0, j)),
                  pl.BlockSpec((TF_FFN, d), lambda i, j: (j, 0)),
                  pl.BlockSpec(memory_space=pl.ANY)],
        out_specs=pl.BlockSpec((TM_FFN, d), lambda i, j: (i + 1, 0)),
        input_output_aliases={3: 0},
        compiler_params=_params(("parallel", "arbitrary"), 56 << 20),
        name="ffn_rest",
    )(h, wu_b, wd_b, f_first)


_SRC_SIZES = (SGU_WIDTH, SGU_WIDTH, MLA_Q_RANK, MLA_KV_RANK, MLA_ROPE,
              RET_HEADS * RET_QK, RET_HEADS * RET_QK, RET_WIDTH, RET_WIDTH)
(_SRC_U, _SRC_V, _SRC_CQ, _SRC_CKV, _SRC_KR, _SRC_RQ, _SRC_RK, _SRC_RV, _SRC_RG,
 IN_WIDTH) = [int(o) for o in np.concatenate([[0], np.cumsum(_SRC_SIZES)])]
_W_IN_MOVES = (
    (COL_U, _SRC_U, 3 * IN_TILE),
    (COL_RV, _SRC_RV, RET_WIDTH),
    (COL_RG, _SRC_RG, RET_WIDTH),
    (COL_CKV, _SRC_CKV, MLA_KV_RANK),
    (COL_KR, _SRC_KR, MLA_ROPE),
    (COL_KR + LANES + MLA_ROPE, _SRC_KR, MLA_ROPE),
    (COL_RQ, _SRC_RQ, RET_HEADS * RET_QK),
    (COL_RK, _SRC_RK, RET_HEADS * RET_QK),
)


def _w_in_prep_kernel(w_ref, o_ref):
    o_ref[COL_KR:COL_RQ, :] = jnp.zeros((COL_RQ - COL_KR, o_ref.shape[1]), o_ref.dtype)
    for dst, src, width in _W_IN_MOVES:
        o_ref[dst:dst + width, :] = w_ref[src:src + width, :].astype(o_ref.dtype)


def prep_w_in(w_t, layer):
    _, n, d = w_t.shape
    return pl.pallas_call(
        _w_in_prep_kernel,
        out_shape=jax.ShapeDtypeStruct((IN_WIDTH_PADDED, d), BF16),
        grid=(d // TK_PREP,),
        in_specs=[pl.BlockSpec((None, n, TK_PREP), lambda i: (layer, 0, i))],
        out_specs=pl.BlockSpec((IN_WIDTH_PADDED, TK_PREP), lambda i: (0, i)),
        compiler_params=_params(("parallel",), 40 << 20),
        name="prep_w_in",
    )(w_t)


def _prep_wq(w):
    k = w.shape[0]
    w3 = w.reshape(k, MLA_HEADS, MLA_NOPE + MLA_ROPE)
    nope = w3[:, :, :MLA_NOPE].reshape(k, MLA_HEADS * MLA_NOPE)
    rope = w3[:, :, MLA_NOPE:].reshape(k, MLA_HEADS * MLA_ROPE)
    return jnp.concatenate([nope, rope], axis=1).astype(BF16)


def _rope_lane_tables(seq):
    inv_freq = 1.0 / (ROPE_THETA ** (jnp.arange(0, MLA_ROPE, 2, dtype=F32) / MLA_ROPE))
    ang = jnp.arange(seq, dtype=F32)[:, None] * inv_freq[None, :]
    cos, sin = jnp.cos(ang), jnp.sin(ang)
    cos_t = jnp.concatenate([cos, cos, cos, cos], axis=1)
    sin_t = jnp.concatenate([-sin, sin, -sin, sin], axis=1)
    return cos_t, sin_t


def kernel(x, norm_mix_pre, norm_mix_post, norm_ffn_pre, norm_ffn_post, w_in, sgu_ln_g, sgu_ln_b, sgu_w, sgu_b, mla_q_norm, mla_wq_b, mla_kv_norm, mla_wkv_b, w_out, w_up, w_down):
    batch, seq, d = x.shape
    depth = w_in.shape[0]
    cos_t, sin_t = _rope_lane_tables(seq)
    d_tab, qd_tab, kd_tab, cd_tab = retention_tables()
    w_in_t = jnp.swapaxes(w_in, 1, 2)

    xf = x.reshape(batch * seq, d)
    h = rmsnorm_rows(xf, norm_mix_pre[0])
    for l in range(depth):
        proj = in_proj(h, prep_w_in(w_in_t, l), cos_t, sin_t, seq)
        bias_full = jnp.repeat(sgu_b[l].T, SGU_WIDTH // SGU_HEADS, axis=1)
        out_a = sgu_mixer(proj, sgu_ln_g[l], sgu_ln_b[l], sgu_w[l], bias_full)
        q = q_proj(proj, mla_q_norm[l], _prep_wq(mla_wq_b[l]), cos_t, sin_t, seq)
        kv = kv_proj(proj, mla_kv_norm[l], mla_wkv_b, l)
        out_b = mla_attention(q, kv, proj, batch, seq)
        out_c = retention_mixer(proj, d_tab, qd_tab, kd_tab, cd_tab, batch, seq)
        mixed = out_proj(out_a, out_b, out_c, w_out, l)
        xf, h2 = residual_norm(xf, mixed, norm_mix_post[l], norm_ffn_pre[l])
        f = ffn(h2, w_up, w_down, l)
        g_next = norm_mix_pre[l + 1] if l + 1 < depth else None
        xf, h = residual_norm(xf, f, norm_ffn_post[l], g_next)
    return xf.reshape(batch, seq, d)
```

```python
import functools
import math

import numpy as np
import jax
import jax.numpy as jnp
from jax import lax
from jax.experimental import pallas as pl
from jax.experimental.pallas import tpu as pltpu

F32 = jnp.float32
BF16 = jnp.bfloat16

D_MODEL = 4096
CHUNK = 64
EPS = 1e-6
ROPE_THETA = 10000.0
SGU_BLOCK = 128
SGU_WIDTH = 1024
SGU_HEADS = 8
MLA_HEADS = 16
MLA_NOPE = 128
MLA_ROPE = 64
MLA_V = 128
MLA_Q_RANK = 1024
MLA_KV_RANK = 512
MLA_WIDTH = MLA_HEADS * MLA_V
RET_HEADS = 8
RET_QK = 64
RET_V = 128
RET_WIDTH = RET_HEADS * RET_V
FF_DIM = 4 * D_MODEL

LANES = 128
VMEM_CAP = 64 * 1024 * 1024

IN_TILE = 1024
COL_U = 0 * IN_TILE
COL_V = 1 * IN_TILE
COL_CQ = 2 * IN_TILE
COL_RV = 3 * IN_TILE
COL_RG = 4 * IN_TILE
COL_CKV = 5 * IN_TILE
COL_KR = COL_CKV + MLA_KV_RANK
COL_RQ = 6 * IN_TILE
COL_RK = COL_RQ + RET_HEADS * RET_QK
IN_WIDTH_PADDED = 7 * IN_TILE
ROPE_BASE = COL_CKV

TM = 1024
TM_NORM = 256
TM_FFN = 512
TF_FFN = 512
TF_FFN_CAST = 256
TN_OUT = 512
TK_PREP = 256
TB_SGU = 512
TQ = 512
RET_L = 256


def _vmem_limit(nbytes):
    return int(min(VMEM_CAP - (2 << 20), nbytes))


def _params(sem, nbytes):
    return pltpu.CompilerParams(dimension_semantics=sem, vmem_limit_bytes=_vmem_limit(nbytes))


def _rms(x, g):
    return x * lax.rsqrt(jnp.mean(x * x, axis=-1, keepdims=True) + EPS) * g


def _rmsnorm_kernel(x_ref, g_ref, o_ref):
    o_ref[...] = _rms(x_ref[...], g_ref[...]).astype(o_ref.dtype)


def rmsnorm_rows(x, g):
    t, d = x.shape
    return pl.pallas_call(
        _rmsnorm_kernel,
        out_shape=jax.ShapeDtypeStruct((t, d), BF16),
        grid=(t // TM_NORM,),
        in_specs=[pl.BlockSpec((TM_NORM, d), lambda i: (i, 0)),
                  pl.BlockSpec((1, d), lambda i: (0, 0))],
        out_specs=pl.BlockSpec((TM_NORM, d), lambda i: (i, 0)),
        compiler_params=_params(("parallel",), 32 << 20),
        name="rmsnorm_rows",
    )(x, g.reshape(1, d))


def _residual_norm_kernel(x_ref, y_ref, gp_ref, gn_ref, xo_ref, ho_ref):
    xn = x_ref[...] + _rms(y_ref[...], gp_ref[...])
    xo_ref[...] = xn
    ho_ref[...] = _rms(xn, gn_ref[...]).astype(ho_ref.dtype)


def _residual_kernel(x_ref, y_ref, gp_ref, xo_ref):
    xo_ref[...] = x_ref[...] + _rms(y_ref[...], gp_ref[...])


def residual_norm(x, y, g_post, g_next):
    t, d = x.shape
    row = pl.BlockSpec((TM_NORM, d), lambda i: (i, 0))
    vec = pl.BlockSpec((1, d), lambda i: (0, 0))
    if g_next is None:
        return pl.pallas_call(
            _residual_kernel,
            out_shape=jax.ShapeDtypeStruct((t, d), F32),
            grid=(t // TM_NORM,),
            in_specs=[row, row, vec],
            out_specs=row,
            compiler_params=_params(("parallel",), 40 << 20),
            name="residual",
        )(x, y, g_post.reshape(1, d)), None
    return pl.pallas_call(
        _residual_norm_kernel,
        out_shape=(jax.ShapeDtypeStruct((t, d), F32), jax.ShapeDtypeStruct((t, d), BF16)),
        grid=(t // TM_NORM,),
        in_specs=[row, row, vec, vec],
        out_specs=(row, row),
        compiler_params=_params(("parallel",), 40 << 20),
        name="residual_norm",
    )(x, y, g_post.reshape(1, d), g_next.reshape(1, d))


def _rope_cols(x, cos, sin):
    lane = lax.broadcasted_iota(jnp.int32, x.shape, 1)
    first_half = (lane % MLA_ROPE) < (MLA_ROPE // 2)
    partner = jnp.where(first_half, pltpu.roll(x, LANES - MLA_ROPE // 2, 1), pltpu.roll(x, MLA_ROPE // 2, 1))
    return x * cos + partner * sin


def _in_proj_main_kernel(h_ref, w_ref, o_ref):
    o_ref[...] = lax.dot_general(h_ref[...], w_ref[...], (((1,), (1,)), ((), ())),
                                 preferred_element_type=F32).astype(o_ref.dtype)


def _in_proj_rope_kernel(h_ref, w_ref, cos_ref, sin_ref, o_ref):
    j = pl.program_id(1)
    acc = lax.dot_general(h_ref[...], w_ref[...], (((1,), (1,)), ((), ())), preferred_element_type=F32)

    @pl.when(j == 0)
    def _():
        lo = COL_KR - COL_CKV
        o_ref[:, :lo] = acc[:, :lo].astype(o_ref.dtype)
        for c in range(lo // LANES, lo // LANES + 2):
            sl = slice(c * LANES, (c + 1) * LANES)
            o_ref[:, sl] = _rope_cols(acc[:, sl], cos_ref[...], sin_ref[...]).astype(o_ref.dtype)
        o_ref[:, lo + 2 * LANES:] = acc[:, lo + 2 * LANES:].astype(o_ref.dtype)

    @pl.when(j == 1)
    def _():
        q_cols = (RET_HEADS * RET_QK) // LANES
        for c in range(IN_TILE // LANES):
            sl = slice(c * LANES, (c + 1) * LANES)
            r = _rope_cols(acc[:, sl], cos_ref[...], sin_ref[...])
            if c < q_cols:
                r = r * (RET_QK ** -0.5)
            o_ref[:, sl] = r.astype(o_ref.dtype)


def in_proj(h, w_t, cos_t, sin_t, seq):
    t, d = h.shape
    pos_blocks = seq // TM
    tab = pl.BlockSpec((TM, LANES), lambda i, j: (i % pos_blocks, 0))
    rope_tile0 = ROPE_BASE // IN_TILE
    main = pl.pallas_call(
        _in_proj_main_kernel,
        out_shape=jax.ShapeDtypeStruct((t, ROPE_BASE), BF16),
        grid=(t // TM, rope_tile0),
        in_specs=[pl.BlockSpec((TM, d), lambda i, j: (i, 0)),
                  pl.BlockSpec((IN_TILE, d), lambda i, j: (j, 0))],
        out_specs=pl.BlockSpec((TM, IN_TILE), lambda i, j: (i, j)),
        compiler_params=_params(("parallel", "arbitrary"), 52 << 20),
        name="in_proj_main",
    )(h, w_t)
    rope = pl.pallas_call(
        _in_proj_rope_kernel,
        out_shape=jax.ShapeDtypeStruct((t, IN_WIDTH_PADDED - ROPE_BASE), BF16),
        grid=(t // TM, (IN_WIDTH_PADDED - ROPE_BASE) // IN_TILE),
        in_specs=[pl.BlockSpec((TM, d), lambda i, j: (i, 0)),
                  pl.BlockSpec((IN_TILE, d), lambda i, j: (rope_tile0 + j, 0)),
                  tab, tab],
        out_specs=pl.BlockSpec((TM, IN_TILE), lambda i, j: (i, j)),
        compiler_params=_params(("parallel", "arbitrary"), 52 << 20),
        name="in_proj_rope",
    )(h, w_t, cos_t, sin_t)
    return main, rope


def _sgu_kernel(u_ref, v_ref, g_ref, b_ref, w_ref, bias_ref, o_ref):
    row = lax.broadcasted_iota(jnp.int32, (SGU_BLOCK, SGU_BLOCK), 0)
    col = lax.broadcasted_iota(jnp.int32, (SGU_BLOCK, SGU_BLOCK), 1)
    keep = (col // CHUNK) <= (row // CHUNK)
    for g in range(SGU_HEADS):
        sl = slice(g * LANES, (g + 1) * LANES)
        wg = jnp.where(keep, w_ref[g], 0.0).astype(BF16)
        vg = v_ref[:, sl].astype(F32)
        mu = jnp.mean(vg, axis=-1, keepdims=True)
        vc = vg - mu
        y = vc * lax.rsqrt(jnp.mean(vc * vc, axis=-1, keepdims=True) + EPS) * g_ref[:, sl] + b_ref[:, sl]
        yb = y.astype(BF16)
        for n in range(TB_SGU // SGU_BLOCK):
            rs = slice(n * SGU_BLOCK, (n + 1) * SGU_BLOCK)
            mixed = jnp.dot(wg, yb[rs, :], preferred_element_type=F32) + bias_ref[:, sl]
            o_ref[rs, sl] = (u_ref[rs, sl].astype(F32) * mixed).astype(o_ref.dtype)


def sgu_mixer(proj, ln_g, ln_b, w_s, bias_full):
    t = proj.shape[0]
    vec = pl.BlockSpec((1, SGU_WIDTH), lambda i: (0, 0))
    return pl.pallas_call(
        _sgu_kernel,
        out_shape=jax.ShapeDtypeStruct((t, SGU_WIDTH), BF16),
        grid=(t // TB_SGU,),
        in_specs=[pl.BlockSpec((TB_SGU, SGU_WIDTH), lambda i: (i, COL_U // SGU_WIDTH)),
                  pl.BlockSpec((TB_SGU, SGU_WIDTH), lambda i: (i, COL_V // SGU_WIDTH)),
                  vec, vec,
                  pl.BlockSpec((SGU_HEADS, SGU_BLOCK, SGU_BLOCK), lambda i: (0, 0, 0)),
                  pl.BlockSpec((SGU_BLOCK, SGU_WIDTH), lambda i: (0, 0))],
        out_specs=pl.BlockSpec((TB_SGU, SGU_WIDTH), lambda i: (i, 0)),
        compiler_params=_params(("parallel",), 32 << 20),
        name="sgu_mixer",
    )(proj, proj, ln_g.reshape(1, -1), ln_b.reshape(1, -1), w_s, bias_full)


Q_SCALE = (MLA_NOPE + MLA_ROPE) ** -0.5 * math.log2(math.e)
Q_ROPE_TILE = (MLA_HEADS * MLA_NOPE) // IN_TILE


def _q_proj_kernel(c_ref, g_ref, w_ref, cos_ref, sin_ref, o_ref, cn_ref):
    j = pl.program_id(1)

    @pl.when(j == 0)
    def _():
        cn_ref[...] = _rms(c_ref[...].astype(F32), g_ref[...]).astype(cn_ref.dtype)

    acc = jnp.dot(cn_ref[...], w_ref[...], preferred_element_type=F32) * Q_SCALE

    @pl.when(j < Q_ROPE_TILE)
    def _():
        o_ref[...] = acc.astype(o_ref.dtype)

    @pl.when(j >= Q_ROPE_TILE)
    def _():
        for c in range(IN_TILE // LANES):
            sl = slice(c * LANES, (c + 1) * LANES)
            o_ref[:, sl] = _rope_cols(acc[:, sl], cos_ref[...], sin_ref[...]).astype(o_ref.dtype)


def q_proj(proj, g, w, cos_t, sin_t, seq):
    t = proj.shape[0]
    n = w.shape[1]
    pos_blocks = seq // TM
    tab = pl.BlockSpec((TM, LANES), lambda i, j: (i % pos_blocks, 0))
    return pl.pallas_call(
        _q_proj_kernel,
        out_shape=jax.ShapeDtypeStruct((t, n), BF16),
        grid=(t // TM, n // IN_TILE),
        in_specs=[pl.BlockSpec((TM, MLA_Q_RANK), lambda i, j: (i, COL_CQ // MLA_Q_RANK)),
                  pl.BlockSpec((1, MLA_Q_RANK), lambda i, j: (0, 0)),
                  pl.BlockSpec((MLA_Q_RANK, IN_TILE), lambda i, j: (0, j)),
                  tab, tab],
        out_specs=pl.BlockSpec((TM, IN_TILE), lambda i, j: (i, j)),
        scratch_shapes=[pltpu.VMEM((TM, MLA_Q_RANK), BF16)],
        compiler_params=_params(("parallel", "arbitrary"), 40 << 20),
        name="q_proj",
    )(proj, g.reshape(1, -1), w, cos_t, sin_t)


def _kv_proj_kernel(c_ref, g_ref, w_ref, o_ref, cn_ref):
    @pl.when(pl.program_id(1) == 0)
    def _():
        cn_ref[...] = _rms(c_ref[...].astype(F32), g_ref[...]).astype(cn_ref.dtype)

    o_ref[...] = jnp.dot(cn_ref[...], w_ref[...].astype(BF16), preferred_element_type=F32).astype(o_ref.dtype)


def kv_proj(proj, g, w, layer):
    t = proj.shape[0]
    n = w.shape[2]
    return pl.pallas_call(
        _kv_proj_kernel,
        out_shape=jax.ShapeDtypeStruct((t, n), BF16),
        grid=(t // TM, n // IN_TILE),
        in_specs=[pl.BlockSpec((TM, MLA_KV_RANK), lambda i, j: (i, (COL_CKV - ROPE_BASE) // MLA_KV_RANK)),
                  pl.BlockSpec((1, MLA_KV_RANK), lambda i, j: (0, 0)),
                  pl.BlockSpec((None, MLA_KV_RANK, IN_TILE), lambda i, j: (layer, 0, j))],
        out_specs=pl.BlockSpec((TM, IN_TILE), lambda i, j: (i, j)),
        scratch_shapes=[pltpu.VMEM((TM, MLA_KV_RANK), BF16)],
        compiler_params=_params(("parallel", "arbitrary"), 32 << 20),
        name="kv_proj",
    )(proj, g.reshape(1, -1), w)


ATTN_GROUP = 4


HEAD_W = MLA_NOPE + MLA_V


def _attn_tiles(qs, kv_ref, kr_ref, base, ntiles, mask_last, states):
    states = list(states)
    ones = jnp.ones((TQ, LANES), BF16)
    for t in range(ntiles):
        st = pl.multiple_of(base + t * TQ, TQ)
        for hd in range(2):
            m, l, acc = states[hd]
            kn = kv_ref[pl.ds(st, TQ), hd * HEAD_W:hd * HEAD_W + MLA_NOPE]
            v = kv_ref[pl.ds(st, TQ), hd * HEAD_W + MLA_NOPE:(hd + 1) * HEAD_W]
            k = jnp.concatenate([kn, kr_ref[pl.ds(st, TQ), hd * LANES:(hd + 1) * LANES]], axis=1)
            s = lax.dot_general(qs[hd], k, (((1,), (1,)), ((), ())), preferred_element_type=F32)
            if mask_last and t == ntiles - 1:
                row = lax.broadcasted_iota(jnp.int32, (TQ, TQ), 0)
                col = lax.broadcasted_iota(jnp.int32, (TQ, TQ), 1)
                s = jnp.where((col // CHUNK) <= (row // CHUNK), s, -jnp.inf)
            m_next = jnp.maximum(m, jnp.max(s, axis=1, keepdims=True))
            alpha = jnp.exp2(m - m_next)
            p = jnp.exp2(s - jnp.tile(m_next, (1, TQ // LANES))).astype(BF16)
            pv = jnp.dot(p, jnp.concatenate([v, ones], axis=1), preferred_element_type=F32)
            states[hd] = (m_next, alpha * l + pv[:, LANES:], alpha * acc + pv[:, :LANES])
    return states


def _attn_kernel(qn_ref, qr_ref, kv_ref, kr_ref, o_ref, m_ref, l_ref, acc_ref):
    qi = pl.program_id(2)
    qr = qr_ref[...]
    qs = [jnp.concatenate([qn_ref[:, hd * LANES:(hd + 1) * LANES], qr], axis=1) for hd in range(2)]

    m_ref[...] = jnp.full_like(m_ref, -jnp.inf)
    l_ref[...] = jnp.zeros_like(l_ref)
    acc_ref[...] = jnp.zeros_like(acc_ref)

    def load_states():
        return [(m_ref[hd], l_ref[hd], acc_ref[hd]) for hd in range(2)]

    def body(g, carry):
        base = pl.multiple_of(g * (ATTN_GROUP * TQ), ATTN_GROUP * TQ)
        states = _attn_tiles(qs, kv_ref, kr_ref, base, ATTN_GROUP, False, load_states())
        for hd in range(2):
            m_ref[hd], l_ref[hd], acc_ref[hd] = states[hd]
        return carry

    lax.fori_loop(0, qi // ATTN_GROUP, body, 0)

    base = pl.multiple_of((qi // ATTN_GROUP) * (ATTN_GROUP * TQ), ATTN_GROUP * TQ)
    for r in range(ATTN_GROUP):
        @pl.when(qi % ATTN_GROUP == r)
        def _():
            states = _attn_tiles(qs, kv_ref, kr_ref, base, r + 1, True, load_states())
            for hd in range(2):
                _, l, acc = states[hd]
                o_ref[:, hd * LANES:(hd + 1) * LANES] = (acc / l).astype(o_ref.dtype)


def mla_attention(q, kv, proj_rope, batch, seq):
    t = q.shape[0]
    nq = seq // TQ
    pair_w = 2 * LANES
    kr_pair_block = (COL_KR - ROPE_BASE) // pair_w
    rope_block0 = (MLA_HEADS * MLA_NOPE) // LANES
    return pl.pallas_call(
        _attn_kernel,
        out_shape=jax.ShapeDtypeStruct((t, MLA_WIDTH), BF16),
        grid=(batch, MLA_HEADS // 2, nq),
        in_specs=[pl.BlockSpec((TQ, pair_w), lambda b, p, i: (b * nq + i, p)),
                  pl.BlockSpec((TQ, LANES), lambda b, p, i: (b * nq + i, rope_block0 + p)),
                  pl.BlockSpec((seq, 2 * HEAD_W), lambda b, p, i: (b, p)),
                  pl.BlockSpec((seq, pair_w), lambda b, p, i: (b, kr_pair_block))],
        out_specs=pl.BlockSpec((TQ, pair_w), lambda b, p, i: (b * nq + i, p)),
        scratch_shapes=[pltpu.VMEM((2, TQ, LANES), F32), pltpu.VMEM((2, TQ, LANES), F32),
                        pltpu.VMEM((2, TQ, LANES), F32)],
        compiler_params=_params(("parallel", "parallel", "arbitrary"), 40 << 20),
        name="mla_attention",
    )(q, q, kv, proj_rope)


def _ret_kernel(q_ref, k_ref, v_ref, g_ref, d_ref, qd_ref, kd_ref, cd_ref, o_ref, s_ref):
    @pl.when(pl.program_id(2) == 0)
    def _():
        s_ref[...] = jnp.zeros_like(s_ref)

    q = q_ref[...]
    k = k_ref[...]
    qf = q.astype(F32)
    kdec = (k.astype(F32) * kd_ref[...]).astype(BF16)
    lane = lax.broadcasted_iota(jnp.int32, q.shape, 1)
    for hh in range(2):
        own = (lane // RET_QK) == hh
        vs = slice(hh * RET_V, (hh + 1) * RET_V)
        v = v_ref[:, vs]
        qm = jnp.where(own, q, jnp.zeros_like(q))
        s = lax.dot_general(qm, k, (((1,), (1,)), ((), ())), preferred_element_type=F32)
        sd = (s * d_ref[hh]).astype(BF16)
        qx = jnp.where(own, qf * qd_ref[...], 0.0).astype(BF16)
        state = s_ref[hh]
        o = (jnp.dot(sd, v, preferred_element_type=F32)
             + jnp.dot(qx, state.astype(BF16), preferred_element_type=F32))
        s_ref[hh] = cd_ref[hh] * state + lax.dot_general(
            kdec, v, (((0,), (0,)), ((), ())), preferred_element_type=F32)
        o = o * lax.rsqrt(jnp.mean(o * o, axis=-1, keepdims=True) + EPS)
        gate = g_ref[:, vs].astype(F32)
        o_ref[:, vs] = (gate * jax.nn.sigmoid(gate) * o).astype(o_ref.dtype)


def retention_mixer(proj, proj_rope, d_tab, qd_tab, kd_tab, cd_tab, batch, seq):
    t = proj.shape[0]
    nsc = seq // RET_L
    pairs = RET_HEADS // 2
    pair_w = 2 * RET_V
    return pl.pallas_call(
        _ret_kernel,
        out_shape=jax.ShapeDtypeStruct((t, RET_WIDTH), BF16),
        grid=(batch, pairs, nsc),
        in_specs=[pl.BlockSpec((RET_L, LANES), lambda b, p, c: (b * nsc + c, (COL_RQ - ROPE_BASE) // LANES + p)),
                  pl.BlockSpec((RET_L, LANES), lambda b, p, c: (b * nsc + c, (COL_RK - ROPE_BASE) // LANES + p)),
                  pl.BlockSpec((RET_L, pair_w), lambda b, p, c: (b * nsc + c, COL_RV // pair_w + p)),
                  pl.BlockSpec((RET_L, pair_w), lambda b, p, c: (b * nsc + c, COL_RG // pair_w + p)),
                  pl.BlockSpec((2, RET_L, RET_L), lambda b, p, c: (p, 0, 0)),
                  pl.BlockSpec((None, RET_L, LANES), lambda b, p, c: (p, 0, 0)),
                  pl.BlockSpec((None, RET_L, LANES), lambda b, p, c: (p, 0, 0)),
                  pl.BlockSpec((2, LANES, RET_V), lambda b, p, c: (p, 0, 0))],
        out_specs=pl.BlockSpec((RET_L, pair_w), lambda b, p, c: (b * nsc + c, p)),
        scratch_shapes=[pltpu.VMEM((2, LANES, RET_V), F32)],
        compiler_params=_params(("parallel", "parallel", "arbitrary"), 32 << 20),
        name="retention_mixer",
    )(proj_rope, proj_rope, proj, proj, d_tab, qd_tab, kd_tab, cd_tab)


def retention_tables():
    log_gamma = jnp.log1p(-jnp.exp2(-5.0 - jnp.arange(RET_HEADS, dtype=F32)))
    idx = jnp.arange(RET_L, dtype=F32)
    dist = jnp.abs(idx[:, None] - idx[None, :])
    chunk = jnp.arange(RET_L) // CHUNK
    visible = chunk[None, :] <= chunk[:, None]
    d_tab = jnp.where(visible[None], jnp.exp(log_gamma[:, None, None] * dist[None]), 0.0)
    lane_head = jnp.arange(LANES) // RET_QK
    head_of_lane = 2 * jnp.arange(RET_HEADS // 2)[:, None] + lane_head[None, :]
    lg_lane = log_gamma[head_of_lane]
    qd_tab = jnp.exp(lg_lane[:, None, :] * (idx[None, :, None] + 1.0))
    kd_tab = jnp.exp(lg_lane[:, None, :] * (RET_L - 1.0 - idx[None, :, None]))
    cd = jnp.exp(log_gamma * RET_L)
    cd_tab = jnp.broadcast_to(cd[:, None, None], (RET_HEADS, LANES, RET_V))
    return d_tab, qd_tab, kd_tab, cd_tab


def _out_proj_kernel(a_ref, b_ref, c_ref, w_ref, o_ref):
    ka = a_ref.shape[1]
    kb = b_ref.shape[1]
    acc = jnp.dot(a_ref[...], w_ref[:ka, :].astype(BF16), preferred_element_type=F32)
    acc = acc + jnp.dot(b_ref[...], w_ref[ka:ka + kb, :].astype(BF16), preferred_element_type=F32)
    acc = acc + jnp.dot(c_ref[...], w_ref[ka + kb:, :].astype(BF16), preferred_element_type=F32)
    o_ref[...] = acc


def out_proj(a, b, c, w, layer):
    t = a.shape[0]
    _, k, n = w.shape
    return pl.pallas_call(
        _out_proj_kernel,
        out_shape=jax.ShapeDtypeStruct((t, n), F32),
        grid=(t // TM, n // TN_OUT),
        in_specs=[pl.BlockSpec((TM, a.shape[1]), lambda i, j: (i, 0)),
                  pl.BlockSpec((TM, b.shape[1]), lambda i, j: (i, 0)),
                  pl.BlockSpec((TM, c.shape[1]), lambda i, j: (i, 0)),
                  pl.BlockSpec((None, k, TN_OUT), lambda i, j: (layer, 0, j))],
        out_specs=pl.BlockSpec((TM, TN_OUT), lambda i, j: (i, j)),
        compiler_params=_params(("parallel", "arbitrary"), 56 << 20),
        name="out_proj",
    )(a, b, c, w)


def _ffn_accumulate(h_ref, wu_ref, wd_ref, o_ref):
    @pl.when(pl.program_id(1) == 0)
    def _():
        o_ref[...] = jnp.zeros_like(o_ref)

    a = jnp.dot(h_ref[...], wu_ref[...], preferred_element_type=F32)
    a = jnp.square(jnp.maximum(a, 0.0)).astype(BF16)
    o_ref[...] += jnp.dot(a, wd_ref[...], preferred_element_type=F32)


def _ffn_first_kernel(h_ref, wu_ref, wd_ref, o_ref, wub_ref, wdb_ref):
    wub_ref[...] = wu_ref[...].astype(BF16)
    wdb_ref[...] = wd_ref[...].astype(BF16)
    _ffn_accumulate(h_ref, wub_ref, wdb_ref, o_ref)


def _ffn_rest_kernel(h_ref, wu_ref, wd_ref, f_first_ref, o_ref):
    del f_first_ref
    _ffn_accumulate(h_ref, wu_ref, wd_ref, o_ref)


def ffn(h, w_up, w_down, layer):
    t, d = h.shape
    ff = w_up.shape[2]
    f_first, wu_b, wd_b = pl.pallas_call(
        _ffn_first_kernel,
        out_shape=(jax.ShapeDtypeStruct((t, d), F32),
                   jax.ShapeDtypeStruct((d, ff), BF16),
                   jax.ShapeDtypeStruct((ff, d), BF16)),
        grid=(1, ff // TF_FFN_CAST),
        in_specs=[pl.BlockSpec((TM_FFN, d), lambda i, j: (0, 0)),
                  pl.BlockSpec((None, d, TF_FFN_CAST), lambda i, j: (layer, 0, j)),
                  pl.BlockSpec((None, TF_FFN_CAST, d), lambda i, j: (layer, j, 0))],
        out_specs=(pl.BlockSpec((TM_FFN, d), lambda i, j: (0, 0)),
                   pl.BlockSpec((d, TF_FFN_CAST), lambda i, j: (0, j)),
                   pl.BlockSpec((TF_FFN_CAST, d), lambda i, j: (j, 0))),
        compiler_params=_params(("arbitrary", "arbitrary"), 60 << 20),
        name="ffn_first",
    )(h, w_up, w_down)
    return pl.pallas_call(
        _ffn_rest_kernel,
        out_shape=jax.ShapeDtypeStruct((t, d), F32),
        grid=(t // TM_FFN - 1, ff // TF_FFN),
        in_specs=[pl.BlockSpec((TM_FFN, d), lambda i, j: (i + 1, 0)),
                  pl.BlockSpec((d, TF_FFN), lambda i, j: (0, j)),
                  pl.BlockSpec((TF_FFN, d), lambda i, j: (j, 0)),
                  pl.BlockSpec(memory_space=pl.ANY)],
        out_specs=pl.BlockSpec((TM_FFN, d), lambda i, j: (i + 1, 0)),
        input_output_aliases={3: 0},
        compiler_params=_params(("parallel", "arbitrary"), 56 << 20),
        name="ffn_rest",
    )(h, wu_b, wd_b, f_first)


_SRC_SIZES = (SGU_WIDTH, SGU_WIDTH, MLA_Q_RANK, MLA_KV_RANK, MLA_ROPE,
              RET_HEADS * RET_QK, RET_HEADS * RET_QK, RET_WIDTH, RET_WIDTH)
(_SRC_U, _SRC_V, _SRC_CQ, _SRC_CKV, _SRC_KR, _SRC_RQ, _SRC_RK, _SRC_RV, _SRC_RG,
 IN_WIDTH) = [int(o) for o in np.concatenate([[0], np.cumsum(_SRC_SIZES)])]
_W_IN_MOVES = (
    (COL_U, _SRC_U, 3 * IN_TILE),
    (COL_RV, _SRC_RV, RET_WIDTH),
    (COL_RG, _SRC_RG, RET_WIDTH),
    (COL_CKV, _SRC_CKV, MLA_KV_RANK),
    (COL_KR, _SRC_KR, MLA_ROPE),
    (COL_KR + LANES + MLA_ROPE, _SRC_KR, MLA_ROPE),
    (COL_RQ, _SRC_RQ, RET_HEADS * RET_QK),
    (COL_RK, _SRC_RK, RET_HEADS * RET_QK),
)


def _w_in_prep_kernel(w_ref, o_ref):
    o_ref[COL_KR:COL_RQ, :] = jnp.zeros((COL_RQ - COL_KR, o_ref.shape[1]), o_ref.dtype)
    for dst, src, width in _W_IN_MOVES:
        o_ref[dst:dst + width, :] = w_ref[src:src + width, :].astype(o_ref.dtype)


def prep_w_in(w_t, layer):
    _, n, d = w_t.shape
    return pl.pallas_call(
        _w_in_prep_kernel,
        out_shape=jax.ShapeDtypeStruct((IN_WIDTH_PADDED, d), BF16),
        grid=(d // TK_PREP,),
        in_specs=[pl.BlockSpec((None, n, TK_PREP), lambda i: (layer, 0, i))],
        out_specs=pl.BlockSpec((IN_WIDTH_PADDED, TK_PREP), lambda i: (0, i)),
        compiler_params=_params(("parallel",), 40 << 20),
        name="prep_w_in",
    )(w_t)


def _prep_wq(w):
    k = w.shape[0]
    w3 = w.reshape(k, MLA_HEADS, MLA_NOPE + MLA_ROPE)
    nope = w3[:, :, :MLA_NOPE].reshape(k, MLA_HEADS * MLA_NOPE)
    rope = w3[:, :, MLA_NOPE:].reshape(k, MLA_HEADS * MLA_ROPE)
    return jnp.concatenate([nope, rope], axis=1).astype(BF16)


def _rope_lane_tables(seq):
    inv_freq = 1.0 / (ROPE_THETA ** (jnp.arange(0, MLA_ROPE, 2, dtype=F32) / MLA_ROPE))
    ang = jnp.arange(seq, dtype=F32)[:, None] * inv_freq[None, :]
    cos, sin = jnp.cos(ang), jnp.sin(ang)
    cos_t = jnp.concatenate([cos, cos, cos, cos], axis=1)
    sin_t = jnp.concatenate([-sin, sin, -sin, sin], axis=1)
    return cos_t, sin_t


def kernel(x, norm_mix_pre, norm_mix_post, norm_ffn_pre, norm_ffn_post, w_in, sgu_ln_g, sgu_ln_b, sgu_w, sgu_b, mla_q_norm, mla_wq_b, mla_kv_norm, mla_wkv_b, w_out, w_up, w_down):
    batch, seq, d = x.shape
    depth = w_in.shape[0]
    cos_t, sin_t = _rope_lane_tables(seq)
    d_tab, qd_tab, kd_tab, cd_tab = retention_tables()
    w_in_t = jnp.swapaxes(w_in, 1, 2)

    xf = x.reshape(batch * seq, d)
    h = rmsnorm_rows(xf, norm_mix_pre[0])
    for l in range(depth):
        proj, proj_rope = in_proj(h, prep_w_in(w_in_t, l), cos_t, sin_t, seq)
        bias_full = jnp.repeat(sgu_b[l].T, SGU_WIDTH // SGU_HEADS, axis=1)
        out_a = sgu_mixer(proj, sgu_ln_g[l], sgu_ln_b[l], sgu_w[l], bias_full)
        q = q_proj(proj, mla_q_norm[l], _prep_wq(mla_wq_b[l]), cos_t, sin_t, seq)
        kv = kv_proj(proj_rope, mla_kv_norm[l], mla_wkv_b, l)
        out_b = mla_attention(q, kv, proj_rope, batch, seq)
        out_c = retention_mixer(proj, proj_rope, d_tab, qd_tab, kd_tab, cd_tab, batch, seq)
        mixed = out_proj(out_a, out_b, out_c, w_out, l)
        xf, h2 = residual_norm(xf, mixed, norm_mix_post[l], norm_ffn_pre[l])
        f = ffn(h2, w_up, w_down, l)
        g_next = norm_mix_pre[l + 1] if l + 1 < depth else None
        xf, h = residual_norm(xf, f, norm_ffn_post[l], g_next)
    return xf.reshape(batch, seq, d)
```

```python
import functools
import math

import numpy as np
import jax
import jax.numpy as jnp
from jax import lax
from jax.experimental import pallas as pl
from jax.experimental.pallas import tpu as pltpu

F32 = jnp.float32
BF16 = jnp.bfloat16

D_MODEL = 4096
CHUNK = 64
EPS = 1e-6
ROPE_THETA = 10000.0
SGU_BLOCK = 128
SGU_WIDTH = 1024
SGU_HEADS = 8
MLA_HEADS = 16
MLA_NOPE = 128
MLA_ROPE = 64
MLA_V = 128
MLA_Q_RANK = 1024
MLA_KV_RANK = 512
MLA_WIDTH = MLA_HEADS * MLA_V
RET_HEADS = 8
RET_QK = 64
RET_V = 128
RET_WIDTH = RET_HEADS * RET_V
FF_DIM = 4 * D_MODEL

LANES = 128
VMEM_CAP = 64 * 1024 * 1024

IN_TILE = 1024
COL_U = 0 * IN_TILE
COL_V = 1 * IN_TILE
COL_CQ = 2 * IN_TILE
COL_RV = 3 * IN_TILE
COL_RG = 4 * IN_TILE
COL_CKV = 5 * IN_TILE
COL_KR = COL_CKV + MLA_KV_RANK
COL_RQ = 6 * IN_TILE
COL_RK = COL_RQ + RET_HEADS * RET_QK
IN_WIDTH_PADDED = 7 * IN_TILE
ROPE_BASE = COL_CKV

TM = 1024
TM_NORM = 256
TM_FFN = 512
TF_FFN = 1024
TN_OUT = 512
TK_PREP = 256
TB_SGU = 512
TQ = 512
RET_L = 256


def _vmem_limit(nbytes):
    return int(min(VMEM_CAP - (2 << 20), nbytes))


def _params(sem, nbytes):
    return pltpu.CompilerParams(dimension_semantics=sem, vmem_limit_bytes=_vmem_limit(nbytes))


def _rms(x, g):
    return x * lax.rsqrt(jnp.mean(x * x, axis=-1, keepdims=True) + EPS) * g


def _rmsnorm_kernel(x_ref, g_ref, o_ref):
    o_ref[...] = _rms(x_ref[...], g_ref[...]).astype(o_ref.dtype)


def rmsnorm_rows(x, g):
    t, d = x.shape
    return pl.pallas_call(
        _rmsnorm_kernel,
        out_shape=jax.ShapeDtypeStruct((t, d), BF16),
        grid=(t // TM_NORM,),
        in_specs=[pl.BlockSpec((TM_NORM, d), lambda i: (i, 0)),
                  pl.BlockSpec((1, d), lambda i: (0, 0))],
        out_specs=pl.BlockSpec((TM_NORM, d), lambda i: (i, 0)),
        compiler_params=_params(("parallel",), 32 << 20),
        name="rmsnorm_rows",
    )(x, g.reshape(1, d))


def _residual_norm_kernel(x_ref, y_ref, gp_ref, gn_ref, xo_ref, ho_ref):
    xn = x_ref[...] + _rms(y_ref[...], gp_ref[...])
    xo_ref[...] = xn
    ho_ref[...] = _rms(xn, gn_ref[...]).astype(ho_ref.dtype)


def _residual_kernel(x_ref, y_ref, gp_ref, xo_ref):
    xo_ref[...] = x_ref[...] + _rms(y_ref[...], gp_ref[...])


def residual_norm(x, y, g_post, g_next):
    t, d = x.shape
    row = pl.BlockSpec((TM_NORM, d), lambda i: (i, 0))
    vec = pl.BlockSpec((1, d), lambda i: (0, 0))
    if g_next is None:
        return pl.pallas_call(
            _residual_kernel,
            out_shape=jax.ShapeDtypeStruct((t, d), F32),
            grid=(t // TM_NORM,),
            in_specs=[row, row, vec],
            out_specs=row,
            compiler_params=_params(("parallel",), 40 << 20),
            name="residual",
        )(x, y, g_post.reshape(1, d)), None
    return pl.pallas_call(
        _residual_norm_kernel,
        out_shape=(jax.ShapeDtypeStruct((t, d), F32), jax.ShapeDtypeStruct((t, d), BF16)),
        grid=(t // TM_NORM,),
        in_specs=[row, row, vec, vec],
        out_specs=(row, row),
        compiler_params=_params(("parallel",), 40 << 20),
        name="residual_norm",
    )(x, y, g_post.reshape(1, d), g_next.reshape(1, d))


def _rope_cols(x, cos, sin):
    lane = lax.broadcasted_iota(jnp.int32, x.shape, 1)
    first_half = (lane % MLA_ROPE) < (MLA_ROPE // 2)
    partner = jnp.where(first_half, pltpu.roll(x, LANES - MLA_ROPE // 2, 1), pltpu.roll(x, MLA_ROPE // 2, 1))
    return x * cos + partner * sin


def _in_proj_main_kernel(h_ref, w_ref, o_ref):
    o_ref[...] = lax.dot_general(h_ref[...], w_ref[...], (((1,), (1,)), ((), ())),
                                 preferred_element_type=F32).astype(o_ref.dtype)


def _in_proj_rope_kernel(h_ref, w_ref, cos_ref, sin_ref, o_ref):
    j = pl.program_id(1)
    acc = lax.dot_general(h_ref[...], w_ref[...], (((1,), (1,)), ((), ())), preferred_element_type=F32)

    @pl.when(j == 0)
    def _():
        lo = COL_KR - COL_CKV
        o_ref[:, :lo] = acc[:, :lo].astype(o_ref.dtype)
        for c in range(lo // LANES, lo // LANES + 2):
            sl = slice(c * LANES, (c + 1) * LANES)
            o_ref[:, sl] = _rope_cols(acc[:, sl], cos_ref[...], sin_ref[...]).astype(o_ref.dtype)
        o_ref[:, lo + 2 * LANES:] = acc[:, lo + 2 * LANES:].astype(o_ref.dtype)

    @pl.when(j == 1)
    def _():
        q_cols = (RET_HEADS * RET_QK) // LANES
        for c in range(IN_TILE // LANES):
            sl = slice(c * LANES, (c + 1) * LANES)
            r = _rope_cols(acc[:, sl], cos_ref[...], sin_ref[...])
            if c < q_cols:
                r = r * (RET_QK ** -0.5)
            o_ref[:, sl] = r.astype(o_ref.dtype)


def in_proj(h, w_t, cos_t, sin_t, seq):
    t, d = h.shape
    pos_blocks = seq // TM
    tab = pl.BlockSpec((TM, LANES), lambda i, j: (i % pos_blocks, 0))
    rope_tile0 = ROPE_BASE // IN_TILE
    main = pl.pallas_call(
        _in_proj_main_kernel,
        out_shape=jax.ShapeDtypeStruct((t, ROPE_BASE), BF16),
        grid=(t // TM, rope_tile0),
        in_specs=[pl.BlockSpec((TM, d), lambda i, j: (i, 0)),
                  pl.BlockSpec((IN_TILE, d), lambda i, j: (j, 0))],
        out_specs=pl.BlockSpec((TM, IN_TILE), lambda i, j: (i, j)),
        compiler_params=_params(("parallel", "arbitrary"), 52 << 20),
        name="in_proj_main",
    )(h, w_t)
    rope = pl.pallas_call(
        _in_proj_rope_kernel,
        out_shape=jax.ShapeDtypeStruct((t, IN_WIDTH_PADDED - ROPE_BASE), BF16),
        grid=(t // TM, (IN_WIDTH_PADDED - ROPE_BASE) // IN_TILE),
        in_specs=[pl.BlockSpec((TM, d), lambda i, j: (i, 0)),
                  pl.BlockSpec((IN_TILE, d), lambda i, j: (rope_tile0 + j, 0)),
                  tab, tab],
        out_specs=pl.BlockSpec((TM, IN_TILE), lambda i, j: (i, j)),
        compiler_params=_params(("parallel", "arbitrary"), 52 << 20),
        name="in_proj_rope",
    )(h, w_t, cos_t, sin_t)
    return main, rope


def _sgu_kernel(u_ref, v_ref, g_ref, b_ref, w_ref, bias_ref, o_ref):
    row = lax.broadcasted_iota(jnp.int32, (SGU_BLOCK, SGU_BLOCK), 0)
    col = lax.broadcasted_iota(jnp.int32, (SGU_BLOCK, SGU_BLOCK), 1)
    keep = (col // CHUNK) <= (row // CHUNK)
    for g in range(SGU_HEADS):
        sl = slice(g * LANES, (g + 1) * LANES)
        wg = jnp.where(keep, w_ref[g], 0.0).astype(BF16)
        vg = v_ref[:, sl].astype(F32)
        mu = jnp.mean(vg, axis=-1, keepdims=True)
        vc = vg - mu
        y = vc * lax.rsqrt(jnp.mean(vc * vc, axis=-1, keepdims=True) + EPS) * g_ref[:, sl] + b_ref[:, sl]
        yb = y.astype(BF16)
        for n in range(TB_SGU // SGU_BLOCK):
            rs = slice(n * SGU_BLOCK, (n + 1) * SGU_BLOCK)
            mixed = jnp.dot(wg, yb[rs, :], preferred_element_type=F32) + bias_ref[:, sl]
            o_ref[rs, sl] = (u_ref[rs, sl].astype(F32) * mixed).astype(o_ref.dtype)


def sgu_mixer(proj, ln_g, ln_b, w_s, bias_full):
    t = proj.shape[0]
    vec = pl.BlockSpec((1, SGU_WIDTH), lambda i: (0, 0))
    return pl.pallas_call(
        _sgu_kernel,
        out_shape=jax.ShapeDtypeStruct((t, SGU_WIDTH), BF16),
        grid=(t // TB_SGU,),
        in_specs=[pl.BlockSpec((TB_SGU, SGU_WIDTH), lambda i: (i, COL_U // SGU_WIDTH)),
                  pl.BlockSpec((TB_SGU, SGU_WIDTH), lambda i: (i, COL_V // SGU_WIDTH)),
                  vec, vec,
                  pl.BlockSpec((SGU_HEADS, SGU_BLOCK, SGU_BLOCK), lambda i: (0, 0, 0)),
                  pl.BlockSpec((SGU_BLOCK, SGU_WIDTH), lambda i: (0, 0))],
        out_specs=pl.BlockSpec((TB_SGU, SGU_WIDTH), lambda i: (i, 0)),
        compiler_params=_params(("parallel",), 32 << 20),
        name="sgu_mixer",
    )(proj, proj, ln_g.reshape(1, -1), ln_b.reshape(1, -1), w_s, bias_full)


Q_SCALE = (MLA_NOPE + MLA_ROPE) ** -0.5 * math.log2(math.e)
Q_ROPE_TILE = (MLA_HEADS * MLA_NOPE) // IN_TILE


def _q_proj_kernel(c_ref, g_ref, w_ref, cos_ref, sin_ref, o_ref, cn_ref):
    j = pl.program_id(1)

    @pl.when(j == 0)
    def _():
        cn_ref[...] = _rms(c_ref[...].astype(F32), g_ref[...]).astype(cn_ref.dtype)

    acc = jnp.dot(cn_ref[...], w_ref[...], preferred_element_type=F32) * Q_SCALE

    @pl.when(j < Q_ROPE_TILE)
    def _():
        o_ref[...] = acc.astype(o_ref.dtype)

    @pl.when(j >= Q_ROPE_TILE)
    def _():
        for c in range(IN_TILE // LANES):
            sl = slice(c * LANES, (c + 1) * LANES)
            o_ref[:, sl] = _rope_cols(acc[:, sl], cos_ref[...], sin_ref[...]).astype(o_ref.dtype)


def q_proj(proj, g, w, cos_t, sin_t, seq):
    t = proj.shape[0]
    n = w.shape[1]
    pos_blocks = seq // TM
    tab = pl.BlockSpec((TM, LANES), lambda i, j: (i % pos_blocks, 0))
    return pl.pallas_call(
        _q_proj_kernel,
        out_shape=jax.ShapeDtypeStruct((t, n), BF16),
        grid=(t // TM, n // IN_TILE),
        in_specs=[pl.BlockSpec((TM, MLA_Q_RANK), lambda i, j: (i, COL_CQ // MLA_Q_RANK)),
                  pl.BlockSpec((1, MLA_Q_RANK), lambda i, j: (0, 0)),
                  pl.BlockSpec((MLA_Q_RANK, IN_TILE), lambda i, j: (0, j)),
                  tab, tab],
        out_specs=pl.BlockSpec((TM, IN_TILE), lambda i, j: (i, j)),
        scratch_shapes=[pltpu.VMEM((TM, MLA_Q_RANK), BF16)],
        compiler_params=_params(("parallel", "arbitrary"), 40 << 20),
        name="q_proj",
    )(proj, g.reshape(1, -1), w, cos_t, sin_t)


def _kv_proj_kernel(c_ref, g_ref, w_ref, o_ref, cn_ref):
    @pl.when(pl.program_id(1) == 0)
    def _():
        cn_ref[...] = _rms(c_ref[...].astype(F32), g_ref[...]).astype(cn_ref.dtype)

    o_ref[...] = jnp.dot(cn_ref[...], w_ref[...].astype(BF16), preferred_element_type=F32).astype(o_ref.dtype)


def kv_proj(proj, g, w, layer):
    t = proj.shape[0]
    n = w.shape[2]
    return pl.pallas_call(
        _kv_proj_kernel,
        out_shape=jax.ShapeDtypeStruct((t, n), BF16),
        grid=(t // TM, n // IN_TILE),
        in_specs=[pl.BlockSpec((TM, MLA_KV_RANK), lambda i, j: (i, (COL_CKV - ROPE_BASE) // MLA_KV_RANK)),
                  pl.BlockSpec((1, MLA_KV_RANK), lambda i, j: (0, 0)),
                  pl.BlockSpec((None, MLA_KV_RANK, IN_TILE), lambda i, j: (layer, 0, j))],
        out_specs=pl.BlockSpec((TM, IN_TILE), lambda i, j: (i, j)),
        scratch_shapes=[pltpu.VMEM((TM, MLA_KV_RANK), BF16)],
        compiler_params=_params(("parallel", "arbitrary"), 32 << 20),
        name="kv_proj",
    )(proj, g.reshape(1, -1), w)


ATTN_GROUP = 4


HEAD_W = MLA_NOPE + MLA_V


def _attn_tiles(qs, kv_ref, kr_ref, base, ntiles, mask_last, states):
    states = list(states)
    ones = jnp.ones((TQ, LANES), BF16)
    for t in range(ntiles):
        st = pl.multiple_of(base + t * TQ, TQ)
        for hd in range(2):
            m, l, acc = states[hd]
            kn = kv_ref[pl.ds(st, TQ), hd * HEAD_W:hd * HEAD_W + MLA_NOPE]
            v = kv_ref[pl.ds(st, TQ), hd * HEAD_W + MLA_NOPE:(hd + 1) * HEAD_W]
            k = jnp.concatenate([kn, kr_ref[pl.ds(st, TQ), hd * LANES:(hd + 1) * LANES]], axis=1)
            s = lax.dot_general(qs[hd], k, (((1,), (1,)), ((), ())), preferred_element_type=F32)
            if mask_last and t == ntiles - 1:
                row = lax.broadcasted_iota(jnp.int32, (TQ, TQ), 0)
                col = lax.broadcasted_iota(jnp.int32, (TQ, TQ), 1)
                s = jnp.where((col // CHUNK) <= (row // CHUNK), s, -jnp.inf)
            m_next = jnp.maximum(m, jnp.max(s, axis=1, keepdims=True))
            alpha = jnp.exp2(m - m_next)
            p = jnp.exp2(s - jnp.tile(m_next, (1, TQ // LANES))).astype(BF16)
            pv = jnp.dot(p, jnp.concatenate([v, ones], axis=1), preferred_element_type=F32)
            states[hd] = (m_next, alpha * l + pv[:, LANES:], alpha * acc + pv[:, :LANES])
    return states


def _attn_kernel(qn_ref, qr_ref, kv_ref, kr_ref, wu_ref, wd_ref, o_ref, wub_ref, wdb_ref,
                 m_ref, l_ref, acc_ref):
    wub_ref[...] = wu_ref[...].astype(wub_ref.dtype)
    wdb_ref[...] = wd_ref[...].astype(wdb_ref.dtype)

    qi = pl.program_id(2)
    qr = qr_ref[...]
    qs = [jnp.concatenate([qn_ref[:, hd * LANES:(hd + 1) * LANES], qr], axis=1) for hd in range(2)]

    m_ref[...] = jnp.full_like(m_ref, -jnp.inf)
    l_ref[...] = jnp.zeros_like(l_ref)
    acc_ref[...] = jnp.zeros_like(acc_ref)

    def load_states():
        return [(m_ref[hd], l_ref[hd], acc_ref[hd]) for hd in range(2)]

    def body(g, carry):
        base = pl.multiple_of(g * (ATTN_GROUP * TQ), ATTN_GROUP * TQ)
        states = _attn_tiles(qs, kv_ref, kr_ref, base, ATTN_GROUP, False, load_states())
        for hd in range(2):
            m_ref[hd], l_ref[hd], acc_ref[hd] = states[hd]
        return carry

    lax.fori_loop(0, qi // ATTN_GROUP, body, 0)

    base = pl.multiple_of((qi // ATTN_GROUP) * (ATTN_GROUP * TQ), ATTN_GROUP * TQ)
    for r in range(ATTN_GROUP):
        @pl.when(qi % ATTN_GROUP == r)
        def _():
            states = _attn_tiles(qs, kv_ref, kr_ref, base, r + 1, True, load_states())
            for hd in range(2):
                _, l, acc = states[hd]
                o_ref[:, hd * LANES:(hd + 1) * LANES] = (acc / l).astype(o_ref.dtype)


def mla_attention(q, kv, proj_rope, w_up, w_down, layer, batch, seq):
    t = q.shape[0]
    nq = seq // TQ
    pairs = MLA_HEADS // 2
    steps = batch * pairs * nq
    _, d, ff = w_up.shape
    up_rows, down_rows = d // steps, ff // steps
    assert up_rows * steps == d and down_rows * steps == ff and up_rows % 16 == 0

    def slab(b, p, i):
        return (b * pairs + p) * nq + i
    pair_w = 2 * LANES
    kr_pair_block = (COL_KR - ROPE_BASE) // pair_w
    rope_block0 = (MLA_HEADS * MLA_NOPE) // LANES
    return pl.pallas_call(
        _attn_kernel,
        out_shape=(jax.ShapeDtypeStruct((t, MLA_WIDTH), BF16),
                   jax.ShapeDtypeStruct((d, ff), BF16),
                   jax.ShapeDtypeStruct((ff, d), BF16)),
        grid=(batch, pairs, nq),
        in_specs=[pl.BlockSpec((TQ, pair_w), lambda b, p, i: (b * nq + i, p)),
                  pl.BlockSpec((TQ, LANES), lambda b, p, i: (b * nq + i, rope_block0 + p)),
                  pl.BlockSpec((seq, 2 * HEAD_W), lambda b, p, i: (b, p)),
                  pl.BlockSpec((seq, pair_w), lambda b, p, i: (b, kr_pair_block)),
                  pl.BlockSpec((None, up_rows, ff), lambda b, p, i: (layer, slab(b, p, i), 0)),
                  pl.BlockSpec((None, down_rows, d), lambda b, p, i: (layer, slab(b, p, i), 0))],
        out_specs=(pl.BlockSpec((TQ, pair_w), lambda b, p, i: (b * nq + i, p)),
                   pl.BlockSpec((up_rows, ff), lambda b, p, i: (slab(b, p, i), 0)),
                   pl.BlockSpec((down_rows, d), lambda b, p, i: (slab(b, p, i), 0))),
        scratch_shapes=[pltpu.VMEM((2, TQ, LANES), F32), pltpu.VMEM((2, TQ, LANES), F32),
                        pltpu.VMEM((2, TQ, LANES), F32)],
        compiler_params=_params(("parallel", "parallel", "arbitrary"), 48 << 20),
        name="mla_attention",
    )(q, q, kv, proj_rope, w_up, w_down)


def _ret_kernel(q_ref, k_ref, v_ref, g_ref, d_ref, qd_ref, kd_ref, cd_ref, o_ref, s_ref):
    @pl.when(pl.program_id(2) == 0)
    def _():
        s_ref[...] = jnp.zeros_like(s_ref)

    q = q_ref[...]
    k = k_ref[...]
    qf = q.astype(F32)
    kdec = (k.astype(F32) * kd_ref[...]).astype(BF16)
    lane = lax.broadcasted_iota(jnp.int32, q.shape, 1)
    for hh in range(2):
        own = (lane // RET_QK) == hh
        vs = slice(hh * RET_V, (hh + 1) * RET_V)
        v = v_ref[:, vs]
        qm = jnp.where(own, q, jnp.zeros_like(q))
        s = lax.dot_general(qm, k, (((1,), (1,)), ((), ())), preferred_element_type=F32)
        sd = (s * d_ref[hh]).astype(BF16)
        qx = jnp.where(own, qf * qd_ref[...], 0.0).astype(BF16)
        state = s_ref[hh]
        o = (jnp.dot(sd, v, preferred_element_type=F32)
             + jnp.dot(qx, state.astype(BF16), preferred_element_type=F32))
        s_ref[hh] = cd_ref[hh] * state + lax.dot_general(
            kdec, v, (((0,), (0,)), ((), ())), preferred_element_type=F32)
        o = o * lax.rsqrt(jnp.mean(o * o, axis=-1, keepdims=True) + EPS)
        gate = g_ref[:, vs].astype(F32)
        o_ref[:, vs] = (gate * jax.nn.sigmoid(gate) * o).astype(o_ref.dtype)


def retention_mixer(proj, proj_rope, d_tab, qd_tab, kd_tab, cd_tab, batch, seq):
    t = proj.shape[0]
    nsc = seq // RET_L
    pairs = RET_HEADS // 2
    pair_w = 2 * RET_V
    return pl.pallas_call(
        _ret_kernel,
        out_shape=jax.ShapeDtypeStruct((t, RET_WIDTH), BF16),
        grid=(batch, pairs, nsc),
        in_specs=[pl.BlockSpec((RET_L, LANES), lambda b, p, c: (b * nsc + c, (COL_RQ - ROPE_BASE) // LANES + p)),
                  pl.BlockSpec((RET_L, LANES), lambda b, p, c: (b * nsc + c, (COL_RK - ROPE_BASE) // LANES + p)),
                  pl.BlockSpec((RET_L, pair_w), lambda b, p, c: (b * nsc + c, COL_RV // pair_w + p)),
                  pl.BlockSpec((RET_L, pair_w), lambda b, p, c: (b * nsc + c, COL_RG // pair_w + p)),
                  pl.BlockSpec((2, RET_L, RET_L), lambda b, p, c: (p, 0, 0)),
                  pl.BlockSpec((None, RET_L, LANES), lambda b, p, c: (p, 0, 0)),
                  pl.BlockSpec((None, RET_L, LANES), lambda b, p, c: (p, 0, 0)),
                  pl.BlockSpec((2, LANES, RET_V), lambda b, p, c: (p, 0, 0))],
        out_specs=pl.BlockSpec((RET_L, pair_w), lambda b, p, c: (b * nsc + c, p)),
        scratch_shapes=[pltpu.VMEM((2, LANES, RET_V), F32)],
        compiler_params=_params(("parallel", "parallel", "arbitrary"), 32 << 20),
        name="retention_mixer",
    )(proj_rope, proj_rope, proj, proj, d_tab, qd_tab, kd_tab, cd_tab)


def retention_tables():
    log_gamma = jnp.log1p(-jnp.exp2(-5.0 - jnp.arange(RET_HEADS, dtype=F32)))
    idx = jnp.arange(RET_L, dtype=F32)
    dist = jnp.abs(idx[:, None] - idx[None, :])
    chunk = jnp.arange(RET_L) // CHUNK
    visible = chunk[None, :] <= chunk[:, None]
    d_tab = jnp.where(visible[None], jnp.exp(log_gamma[:, None, None] * dist[None]), 0.0)
    lane_head = jnp.arange(LANES) // RET_QK
    head_of_lane = 2 * jnp.arange(RET_HEADS // 2)[:, None] + lane_head[None, :]
    lg_lane = log_gamma[head_of_lane]
    qd_tab = jnp.exp(lg_lane[:, None, :] * (idx[None, :, None] + 1.0))
    kd_tab = jnp.exp(lg_lane[:, None, :] * (RET_L - 1.0 - idx[None, :, None]))
    cd = jnp.exp(log_gamma * RET_L)
    cd_tab = jnp.broadcast_to(cd[:, None, None], (RET_HEADS, LANES, RET_V))
    return d_tab, qd_tab, kd_tab, cd_tab


def _out_proj_kernel(a_ref, b_ref, c_ref, w_ref, o_ref):
    ka = a_ref.shape[1]
    kb = b_ref.shape[1]
    acc = jnp.dot(a_ref[...], w_ref[:ka, :].astype(BF16), preferred_element_type=F32)
    acc = acc + jnp.dot(b_ref[...], w_ref[ka:ka + kb, :].astype(BF16), preferred_element_type=F32)
    acc = acc + jnp.dot(c_ref[...], w_ref[ka + kb:, :].astype(BF16), preferred_element_type=F32)
    o_ref[...] = acc


def out_proj(a, b, c, w, layer):
    t = a.shape[0]
    _, k, n = w.shape
    return pl.pallas_call(
        _out_proj_kernel,
        out_shape=jax.ShapeDtypeStruct((t, n), F32),
        grid=(t // TM, n // TN_OUT),
        in_specs=[pl.BlockSpec((TM, a.shape[1]), lambda i, j: (i, 0)),
                  pl.BlockSpec((TM, b.shape[1]), lambda i, j: (i, 0)),
                  pl.BlockSpec((TM, c.shape[1]), lambda i, j: (i, 0)),
                  pl.BlockSpec((None, k, TN_OUT), lambda i, j: (layer, 0, j))],
        out_specs=pl.BlockSpec((TM, TN_OUT), lambda i, j: (i, j)),
        compiler_params=_params(("parallel", "arbitrary"), 56 << 20),
        name="out_proj",
    )(a, b, c, w)


def _ffn_accumulate(h_ref, wu_ref, wd_ref, o_ref):
    @pl.when(pl.program_id(1) == 0)
    def _():
        o_ref[...] = jnp.zeros_like(o_ref)

    a = jnp.dot(h_ref[...], wu_ref[...], preferred_element_type=F32)
    a = jnp.square(jnp.maximum(a, 0.0)).astype(BF16)
    o_ref[...] += jnp.dot(a, wd_ref[...], preferred_element_type=F32)


def _ffn_kernel(h_ref, wu_ref, wd_ref, o_ref):
    _ffn_accumulate(h_ref, wu_ref, wd_ref, o_ref)


def ffn(h, wu_b, wd_b):
    t, d = h.shape
    ff = wu_b.shape[1]
    return pl.pallas_call(
        _ffn_kernel,
        out_shape=jax.ShapeDtypeStruct((t, d), F32),
        grid=(t // TM_FFN, ff // TF_FFN),
        in_specs=[pl.BlockSpec((TM_FFN, d), lambda i, j: (i, 0), pipeline_mode=pl.Buffered(1)),
                  pl.BlockSpec((d, TF_FFN), lambda i, j: (0, j)),
                  pl.BlockSpec((TF_FFN, d), lambda i, j: (j, 0))],
        out_specs=pl.BlockSpec((TM_FFN, d), lambda i, j: (i, 0)),
        compiler_params=_params(("parallel", "arbitrary"), 60 << 20),
        name="ffn",
    )(h, wu_b, wd_b)


_SRC_SIZES = (SGU_WIDTH, SGU_WIDTH, MLA_Q_RANK, MLA_KV_RANK, MLA_ROPE,
              RET_HEADS * RET_QK, RET_HEADS * RET_QK, RET_WIDTH, RET_WIDTH)
(_SRC_U, _SRC_V, _SRC_CQ, _SRC_CKV, _SRC_KR, _SRC_RQ, _SRC_RK, _SRC_RV, _SRC_RG,
 IN_WIDTH) = [int(o) for o in np.concatenate([[0], np.cumsum(_SRC_SIZES)])]
_W_IN_MOVES = (
    (COL_U, _SRC_U, 3 * IN_TILE),
    (COL_RV, _SRC_RV, RET_WIDTH),
    (COL_RG, _SRC_RG, RET_WIDTH),
    (COL_CKV, _SRC_CKV, MLA_KV_RANK),
    (COL_KR, _SRC_KR, MLA_ROPE),
    (COL_KR + LANES + MLA_ROPE, _SRC_KR, MLA_ROPE),
    (COL_RQ, _SRC_RQ, RET_HEADS * RET_QK),
    (COL_RK, _SRC_RK, RET_HEADS * RET_QK),
)


def _w_in_prep_kernel(w_ref, o_ref):
    o_ref[COL_KR:COL_RQ, :] = jnp.zeros((COL_RQ - COL_KR, o_ref.shape[1]), o_ref.dtype)
    for dst, src, width in _W_IN_MOVES:
        o_ref[dst:dst + width, :] = w_ref[src:src + width, :].astype(o_ref.dtype)


def prep_w_in(w_t, layer):
    _, n, d = w_t.shape
    return pl.pallas_call(
        _w_in_prep_kernel,
        out_shape=jax.ShapeDtypeStruct((IN_WIDTH_PADDED, d), BF16),
        grid=(d // TK_PREP,),
        in_specs=[pl.BlockSpec((None, n, TK_PREP), lambda i: (layer, 0, i))],
        out_specs=pl.BlockSpec((IN_WIDTH_PADDED, TK_PREP), lambda i: (0, i)),
        compiler_params=_params(("parallel",), 40 << 20),
        name="prep_w_in",
    )(w_t)


def _prep_wq(w):
    k = w.shape[0]
    w3 = w.reshape(k, MLA_HEADS, MLA_NOPE + MLA_ROPE)
    nope = w3[:, :, :MLA_NOPE].reshape(k, MLA_HEADS * MLA_NOPE)
    rope = w3[:, :, MLA_NOPE:].reshape(k, MLA_HEADS * MLA_ROPE)
    return jnp.concatenate([nope, rope], axis=1).astype(BF16)


def _rope_lane_tables(seq):
    inv_freq = 1.0 / (ROPE_THETA ** (jnp.arange(0, MLA_ROPE, 2, dtype=F32) / MLA_ROPE))
    ang = jnp.arange(seq, dtype=F32)[:, None] * inv_freq[None, :]
    cos, sin = jnp.cos(ang), jnp.sin(ang)
    cos_t = jnp.concatenate([cos, cos, cos, cos], axis=1)
    sin_t = jnp.concatenate([-sin, sin, -sin, sin], axis=1)
    return cos_t, sin_t


def kernel(x, norm_mix_pre, norm_mix_post, norm_ffn_pre, norm_ffn_post, w_in, sgu_ln_g, sgu_ln_b, sgu_w, sgu_b, mla_q_norm, mla_wq_b, mla_kv_norm, mla_wkv_b, w_out, w_up, w_down):
    batch, seq, d = x.shape
    depth = w_in.shape[0]
    cos_t, sin_t = _rope_lane_tables(seq)
    d_tab, qd_tab, kd_tab, cd_tab = retention_tables()
    w_in_t = jnp.swapaxes(w_in, 1, 2)

    xf = x.reshape(batch * seq, d)
    h = rmsnorm_rows(xf, norm_mix_pre[0])
    for l in range(depth):
        proj, proj_rope = in_proj(h, prep_w_in(w_in_t, l), cos_t, sin_t, seq)
        bias_full = jnp.repeat(sgu_b[l].T, SGU_WIDTH // SGU_HEADS, axis=1)
        out_a = sgu_mixer(proj, sgu_ln_g[l], sgu_ln_b[l], sgu_w[l], bias_full)
        q = q_proj(proj, mla_q_norm[l], _prep_wq(mla_wq_b[l]), cos_t, sin_t, seq)
        kv = kv_proj(proj_rope, mla_kv_norm[l], mla_wkv_b, l)
        out_b, wu_b, wd_b = mla_attention(q, kv, proj_rope, w_up, w_down, l, batch, seq)
        out_c = retention_mixer(proj, proj_rope, d_tab, qd_tab, kd_tab, cd_tab, batch, seq)
        mixed = out_proj(out_a, out_b, out_c, w_out, l)
        xf, h2 = residual_norm(xf, mixed, norm_mix_post[l], norm_ffn_pre[l])
        f = ffn(h2, wu_b, wd_b)
        g_next = norm_mix_pre[l + 1] if l + 1 < depth else None
        xf, h = residual_norm(xf, f, norm_ffn_post[l], g_next)
    return xf.reshape(batch, seq, d)
```

```python
import functools
import math

import numpy as np
import jax
import jax.numpy as jnp
from jax import lax
from jax.experimental import pallas as pl
from jax.experimental.pallas import tpu as pltpu

F32 = jnp.float32
BF16 = jnp.bfloat16

D_MODEL = 4096
CHUNK = 64
EPS = 1e-6
ROPE_THETA = 10000.0
SGU_BLOCK = 128
SGU_WIDTH = 1024
SGU_HEADS = 8
MLA_HEADS = 16
MLA_NOPE = 128
MLA_ROPE = 64
MLA_V = 128
MLA_Q_RANK = 1024
MLA_KV_RANK = 512
MLA_WIDTH = MLA_HEADS * MLA_V
RET_HEADS = 8
RET_QK = 64
RET_V = 128
RET_WIDTH = RET_HEADS * RET_V
FF_DIM = 4 * D_MODEL

LANES = 128
VMEM_CAP = 64 * 1024 * 1024

IN_TILE = 1024
COL_U = 0 * IN_TILE
COL_V = 1 * IN_TILE
COL_CQ = 2 * IN_TILE
COL_RV = 3 * IN_TILE
COL_RG = 4 * IN_TILE
COL_CKV = 5 * IN_TILE
COL_KR = COL_CKV + MLA_KV_RANK
COL_RQ = 6 * IN_TILE
COL_RK = COL_RQ + RET_HEADS * RET_QK
IN_WIDTH_PADDED = 7 * IN_TILE
ROPE_BASE = COL_CKV

TM = 1024
TM_NORM = 256
TM_FFN = 512
TF_FFN = 1024
TN_OUT = 512
TK_PREP = 256
TB_SGU = 512
TQ = 512
RET_L = 256


def _vmem_limit(nbytes):
    return int(min(VMEM_CAP - (2 << 20), nbytes))


def _params(sem, nbytes):
    return pltpu.CompilerParams(dimension_semantics=sem, vmem_limit_bytes=_vmem_limit(nbytes))


def _rms(x, g):
    return x * lax.rsqrt(jnp.mean(x * x, axis=-1, keepdims=True) + EPS) * g


def _rmsnorm_kernel(x_ref, g_ref, o_ref):
    o_ref[...] = _rms(x_ref[...], g_ref[...]).astype(o_ref.dtype)


def rmsnorm_rows(x, g):
    t, d = x.shape
    return pl.pallas_call(
        _rmsnorm_kernel,
        out_shape=jax.ShapeDtypeStruct((t, d), BF16),
        grid=(t // TM_NORM,),
        in_specs=[pl.BlockSpec((TM_NORM, d), lambda i: (i, 0)),
                  pl.BlockSpec((1, d), lambda i: (0, 0))],
        out_specs=pl.BlockSpec((TM_NORM, d), lambda i: (i, 0)),
        compiler_params=_params(("parallel",), 32 << 20),
        name="rmsnorm_rows",
    )(x, g.reshape(1, d))


def _residual_norm_kernel(x_ref, y_ref, gp_ref, gn_ref, xo_ref, ho_ref):
    xn = x_ref[...] + _rms(y_ref[...].astype(F32), gp_ref[...])
    xo_ref[...] = xn
    ho_ref[...] = _rms(xn, gn_ref[...]).astype(ho_ref.dtype)


def _residual_kernel(x_ref, y_ref, gp_ref, xo_ref):
    xo_ref[...] = x_ref[...] + _rms(y_ref[...].astype(F32), gp_ref[...])


def residual_norm(x, y, g_post, g_next):
    t, d = x.shape
    row = pl.BlockSpec((TM_NORM, d), lambda i: (i, 0))
    vec = pl.BlockSpec((1, d), lambda i: (0, 0))
    if g_next is None:
        return pl.pallas_call(
            _residual_kernel,
            out_shape=jax.ShapeDtypeStruct((t, d), F32),
            grid=(t // TM_NORM,),
            in_specs=[row, row, vec],
            out_specs=row,
            compiler_params=_params(("parallel",), 40 << 20),
            name="residual",
        )(x, y, g_post.reshape(1, d)), None
    return pl.pallas_call(
        _residual_norm_kernel,
        out_shape=(jax.ShapeDtypeStruct((t, d), F32), jax.ShapeDtypeStruct((t, d), BF16)),
        grid=(t // TM_NORM,),
        in_specs=[row, row, vec, vec],
        out_specs=(row, row),
        compiler_params=_params(("parallel",), 40 << 20),
        name="residual_norm",
    )(x, y, g_post.reshape(1, d), g_next.reshape(1, d))


def _rope_cols(x, cos, sin):
    lane = lax.broadcasted_iota(jnp.int32, x.shape, 1)
    first_half = (lane % MLA_ROPE) < (MLA_ROPE // 2)
    partner = jnp.where(first_half, pltpu.roll(x, LANES - MLA_ROPE // 2, 1), pltpu.roll(x, MLA_ROPE // 2, 1))
    return x * cos + partner * sin


def _in_proj_main_kernel(h_ref, w_ref, o_ref):
    o_ref[...] = lax.dot_general(h_ref[...], w_ref[...], (((1,), (1,)), ((), ())),
                                 preferred_element_type=F32).astype(o_ref.dtype)


def _in_proj_rope_kernel(h_ref, w_ref, cos_ref, sin_ref, o_ref):
    j = pl.program_id(1)
    acc = lax.dot_general(h_ref[...], w_ref[...], (((1,), (1,)), ((), ())), preferred_element_type=F32)

    @pl.when(j == 0)
    def _():
        lo = COL_KR - COL_CKV
        o_ref[:, :lo] = acc[:, :lo].astype(o_ref.dtype)
        for c in range(lo // LANES, lo // LANES + 2):
            sl = slice(c * LANES, (c + 1) * LANES)
            o_ref[:, sl] = _rope_cols(acc[:, sl], cos_ref[...], sin_ref[...]).astype(o_ref.dtype)
        o_ref[:, lo + 2 * LANES:] = acc[:, lo + 2 * LANES:].astype(o_ref.dtype)

    @pl.when(j == 1)
    def _():
        q_cols = (RET_HEADS * RET_QK) // LANES
        for c in range(IN_TILE // LANES):
            sl = slice(c * LANES, (c + 1) * LANES)
            r = _rope_cols(acc[:, sl], cos_ref[...], sin_ref[...])
            if c < q_cols:
                r = r * (RET_QK ** -0.5)
            o_ref[:, sl] = r.astype(o_ref.dtype)


def in_proj(h, w_t, cos_t, sin_t, seq):
    t, d = h.shape
    pos_blocks = seq // TM
    tab = pl.BlockSpec((TM, LANES), lambda i, j: (i % pos_blocks, 0))
    rope_tile0 = ROPE_BASE // IN_TILE
    main = pl.pallas_call(
        _in_proj_main_kernel,
        out_shape=jax.ShapeDtypeStruct((t, ROPE_BASE), BF16),
        grid=(t // TM, rope_tile0),
        in_specs=[pl.BlockSpec((TM, d), lambda i, j: (i, 0)),
                  pl.BlockSpec((IN_TILE, d), lambda i, j: (j, 0))],
        out_specs=pl.BlockSpec((TM, IN_TILE), lambda i, j: (i, j)),
        compiler_params=_params(("parallel", "arbitrary"), 52 << 20),
        name="in_proj_main",
    )(h, w_t)
    rope = pl.pallas_call(
        _in_proj_rope_kernel,
        out_shape=jax.ShapeDtypeStruct((t, IN_WIDTH_PADDED - ROPE_BASE), BF16),
        grid=(t // TM, (IN_WIDTH_PADDED - ROPE_BASE) // IN_TILE),
        in_specs=[pl.BlockSpec((TM, d), lambda i, j: (i, 0)),
                  pl.BlockSpec((IN_TILE, d), lambda i, j: (rope_tile0 + j, 0)),
                  tab, tab],
        out_specs=pl.BlockSpec((TM, IN_TILE), lambda i, j: (i, j)),
        compiler_params=_params(("parallel", "arbitrary"), 52 << 20),
        name="in_proj_rope",
    )(h, w_t, cos_t, sin_t)
    return main, rope


def _sgu_kernel(u_ref, v_ref, g_ref, b_ref, w_ref, bias_ref, o_ref):
    row = lax.broadcasted_iota(jnp.int32, (SGU_BLOCK, SGU_BLOCK), 0)
    col = lax.broadcasted_iota(jnp.int32, (SGU_BLOCK, SGU_BLOCK), 1)
    keep = (col // CHUNK) <= (row // CHUNK)
    for g in range(SGU_HEADS):
        sl = slice(g * LANES, (g + 1) * LANES)
        wg = jnp.where(keep, w_ref[g], 0.0).astype(BF16)
        vg = v_ref[:, sl].astype(F32)
        mu = jnp.mean(vg, axis=-1, keepdims=True)
        vc = vg - mu
        y = vc * lax.rsqrt(jnp.mean(vc * vc, axis=-1, keepdims=True) + EPS) * g_ref[:, sl] + b_ref[:, sl]
        yb = y.astype(BF16)
        for n in range(TB_SGU // SGU_BLOCK):
            rs = slice(n * SGU_BLOCK, (n + 1) * SGU_BLOCK)
            mixed = jnp.dot(wg, yb[rs, :], preferred_element_type=F32) + bias_ref[:, sl]
            o_ref[rs, sl] = (u_ref[rs, sl].astype(F32) * mixed).astype(o_ref.dtype)


def sgu_mixer(proj, ln_g, ln_b, w_s, bias_full):
    t = proj.shape[0]
    vec = pl.BlockSpec((1, SGU_WIDTH), lambda i: (0, 0))
    return pl.pallas_call(
        _sgu_kernel,
        out_shape=jax.ShapeDtypeStruct((t, SGU_WIDTH), BF16),
        grid=(t // TB_SGU,),
        in_specs=[pl.BlockSpec((TB_SGU, SGU_WIDTH), lambda i: (i, COL_U // SGU_WIDTH)),
                  pl.BlockSpec((TB_SGU, SGU_WIDTH), lambda i: (i, COL_V // SGU_WIDTH)),
                  vec, vec,
                  pl.BlockSpec((SGU_HEADS, SGU_BLOCK, SGU_BLOCK), lambda i: (0, 0, 0)),
                  pl.BlockSpec((SGU_BLOCK, SGU_WIDTH), lambda i: (0, 0))],
        out_specs=pl.BlockSpec((TB_SGU, SGU_WIDTH), lambda i: (i, 0)),
        compiler_params=_params(("parallel",), 32 << 20),
        name="sgu_mixer",
    )(proj, proj, ln_g.reshape(1, -1), ln_b.reshape(1, -1), w_s, bias_full)


Q_SCALE = (MLA_NOPE + MLA_ROPE) ** -0.5 * math.log2(math.e)
Q_ROPE_TILE = (MLA_HEADS * MLA_NOPE) // IN_TILE


def _q_proj_kernel(c_ref, g_ref, w_ref, cos_ref, sin_ref, o_ref, cn_ref):
    j = pl.program_id(1)

    @pl.when(j == 0)
    def _():
        cn_ref[...] = _rms(c_ref[...].astype(F32), g_ref[...]).astype(cn_ref.dtype)

    acc = jnp.dot(cn_ref[...], w_ref[...], preferred_element_type=F32) * Q_SCALE

    @pl.when(j < Q_ROPE_TILE)
    def _():
        o_ref[...] = acc.astype(o_ref.dtype)

    @pl.when(j >= Q_ROPE_TILE)
    def _():
        for c in range(IN_TILE // LANES):
            sl = slice(c * LANES, (c + 1) * LANES)
            o_ref[:, sl] = _rope_cols(acc[:, sl], cos_ref[...], sin_ref[...]).astype(o_ref.dtype)


def q_proj(proj, g, w, cos_t, sin_t, seq):
    t = proj.shape[0]
    n = w.shape[1]
    pos_blocks = seq // TM
    tab = pl.BlockSpec((TM, LANES), lambda i, j: (i % pos_blocks, 0))
    return pl.pallas_call(
        _q_proj_kernel,
        out_shape=jax.ShapeDtypeStruct((t, n), BF16),
        grid=(t // TM, n // IN_TILE),
        in_specs=[pl.BlockSpec((TM, MLA_Q_RANK), lambda i, j: (i, COL_CQ // MLA_Q_RANK)),
                  pl.BlockSpec((1, MLA_Q_RANK), lambda i, j: (0, 0)),
                  pl.BlockSpec((MLA_Q_RANK, IN_TILE), lambda i, j: (0, j)),
                  tab, tab],
        out_specs=pl.BlockSpec((TM, IN_TILE), lambda i, j: (i, j)),
        scratch_shapes=[pltpu.VMEM((TM, MLA_Q_RANK), BF16)],
        compiler_params=_params(("parallel", "arbitrary"), 40 << 20),
        name="q_proj",
    )(proj, g.reshape(1, -1), w, cos_t, sin_t)


def _kv_proj_kernel(c_ref, g_ref, w_ref, o_ref, cn_ref):
    @pl.when(pl.program_id(1) == 0)
    def _():
        cn_ref[...] = _rms(c_ref[...].astype(F32), g_ref[...]).astype(cn_ref.dtype)

    o_ref[...] = jnp.dot(cn_ref[...], w_ref[...].astype(BF16), preferred_element_type=F32).astype(o_ref.dtype)


def kv_proj(proj, g, w, layer):
    t = proj.shape[0]
    n = w.shape[2]
    return pl.pallas_call(
        _kv_proj_kernel,
        out_shape=jax.ShapeDtypeStruct((t, n), BF16),
        grid=(t // TM, n // IN_TILE),
        in_specs=[pl.BlockSpec((TM, MLA_KV_RANK), lambda i, j: (i, (COL_CKV - ROPE_BASE) // MLA_KV_RANK)),
                  pl.BlockSpec((1, MLA_KV_RANK), lambda i, j: (0, 0)),
                  pl.BlockSpec((None, MLA_KV_RANK, IN_TILE), lambda i, j: (layer, 0, j))],
        out_specs=pl.BlockSpec((TM, IN_TILE), lambda i, j: (i, j)),
        scratch_shapes=[pltpu.VMEM((TM, MLA_KV_RANK), BF16)],
        compiler_params=_params(("parallel", "arbitrary"), 32 << 20),
        name="kv_proj",
    )(proj, g.reshape(1, -1), w)


ATTN_GROUP = 4


HEAD_W = MLA_NOPE + MLA_V


def _attn_tiles(qs, kv_ref, kr_ref, base, ntiles, mask_last, states):
    states = list(states)
    ones = jnp.ones((TQ, LANES), BF16)
    for t in range(ntiles):
        st = pl.multiple_of(base + t * TQ, TQ)
        for hd in range(2):
            m, l, acc = states[hd]
            kn = kv_ref[pl.ds(st, TQ), hd * HEAD_W:hd * HEAD_W + MLA_NOPE]
            v = kv_ref[pl.ds(st, TQ), hd * HEAD_W + MLA_NOPE:(hd + 1) * HEAD_W]
            k = jnp.concatenate([kn, kr_ref[pl.ds(st, TQ), hd * LANES:(hd + 1) * LANES]], axis=1)
            s = lax.dot_general(qs[hd], k, (((1,), (1,)), ((), ())), preferred_element_type=F32)
            if mask_last and t == ntiles - 1:
                row = lax.broadcasted_iota(jnp.int32, (TQ, TQ), 0)
                col = lax.broadcasted_iota(jnp.int32, (TQ, TQ), 1)
                s = jnp.where((col // CHUNK) <= (row // CHUNK), s, -jnp.inf)
            m_next = jnp.maximum(m, jnp.max(s, axis=1, keepdims=True))
            alpha = jnp.exp2(m - m_next)
            p = jnp.exp2(s - jnp.tile(m_next, (1, TQ // LANES))).astype(BF16)
            pv = jnp.dot(p, jnp.concatenate([v, ones], axis=1), preferred_element_type=F32)
            states[hd] = (m_next, alpha * l + pv[:, LANES:], alpha * acc + pv[:, :LANES])
    return states


def _attn_kernel(qn_ref, qr_ref, kv_ref, kr_ref, wu_ref, wd_ref, o_ref, wub_ref, wdb_ref,
                 m_ref, l_ref, acc_ref):
    wub_ref[...] = wu_ref[...].astype(wub_ref.dtype)
    wdb_ref[...] = wd_ref[...].astype(wdb_ref.dtype)

    qi = pl.program_id(2)
    qr = qr_ref[...]
    qs = [jnp.concatenate([qn_ref[:, hd * LANES:(hd + 1) * LANES], qr], axis=1) for hd in range(2)]

    m_ref[...] = jnp.full_like(m_ref, -jnp.inf)
    l_ref[...] = jnp.zeros_like(l_ref)
    acc_ref[...] = jnp.zeros_like(acc_ref)

    def load_states():
        return [(m_ref[hd], l_ref[hd], acc_ref[hd]) for hd in range(2)]

    def body(g, carry):
        base = pl.multiple_of(g * (ATTN_GROUP * TQ), ATTN_GROUP * TQ)
        states = _attn_tiles(qs, kv_ref, kr_ref, base, ATTN_GROUP, False, load_states())
        for hd in range(2):
            m_ref[hd], l_ref[hd], acc_ref[hd] = states[hd]
        return carry

    lax.fori_loop(0, qi // ATTN_GROUP, body, 0)

    base = pl.multiple_of((qi // ATTN_GROUP) * (ATTN_GROUP * TQ), ATTN_GROUP * TQ)
    for r in range(ATTN_GROUP):
        @pl.when(qi % ATTN_GROUP == r)
        def _():
            states = _attn_tiles(qs, kv_ref, kr_ref, base, r + 1, True, load_states())
            for hd in range(2):
                _, l, acc = states[hd]
                o_ref[:, hd * LANES:(hd + 1) * LANES] = (acc / l).astype(o_ref.dtype)


def mla_attention(q, kv, proj_rope, w_up, w_down, layer, batch, seq):
    t = q.shape[0]
    nq = seq // TQ
    pairs = MLA_HEADS // 2
    steps = batch * pairs * nq
    _, d, ff = w_up.shape
    up_rows, down_rows = d // steps, ff // steps
    assert up_rows * steps == d and down_rows * steps == ff and up_rows % 16 == 0

    def slab(b, p, i):
        return (b * pairs + p) * nq + i
    pair_w = 2 * LANES
    kr_pair_block = (COL_KR - ROPE_BASE) // pair_w
    rope_block0 = (MLA_HEADS * MLA_NOPE) // LANES
    return pl.pallas_call(
        _attn_kernel,
        out_shape=(jax.ShapeDtypeStruct((t, MLA_WIDTH), BF16),
                   jax.ShapeDtypeStruct((d, ff), BF16),
                   jax.ShapeDtypeStruct((ff, d), BF16)),
        grid=(batch, pairs, nq),
        in_specs=[pl.BlockSpec((TQ, pair_w), lambda b, p, i: (b * nq + i, p)),
                  pl.BlockSpec((TQ, LANES), lambda b, p, i: (b * nq + i, rope_block0 + p)),
                  pl.BlockSpec((seq, 2 * HEAD_W), lambda b, p, i: (b, p)),
                  pl.BlockSpec((seq, pair_w), lambda b, p, i: (b, kr_pair_block)),
                  pl.BlockSpec((None, up_rows, ff), lambda b, p, i: (layer, slab(b, p, i), 0)),
                  pl.BlockSpec((None, down_rows, d), lambda b, p, i: (layer, slab(b, p, i), 0))],
        out_specs=(pl.BlockSpec((TQ, pair_w), lambda b, p, i: (b * nq + i, p)),
                   pl.BlockSpec((up_rows, ff), lambda b, p, i: (slab(b, p, i), 0)),
                   pl.BlockSpec((down_rows, d), lambda b, p, i: (slab(b, p, i), 0))),
        scratch_shapes=[pltpu.VMEM((2, TQ, LANES), F32), pltpu.VMEM((2, TQ, LANES), F32),
                        pltpu.VMEM((2, TQ, LANES), F32)],
        compiler_params=_params(("parallel", "parallel", "arbitrary"), 48 << 20),
        name="mla_attention",
    )(q, q, kv, proj_rope, w_up, w_down)


def _ret_kernel(q_ref, k_ref, v_ref, g_ref, d_ref, qd_ref, kd_ref, cd_ref, o_ref, s_ref):
    @pl.when(pl.program_id(1) == 0)
    def _():
        s_ref[...] = jnp.zeros_like(s_ref)

    lane = lax.broadcasted_iota(jnp.int32, (RET_L, LANES), 1)
    for p in range(RET_HEADS // 2):
        ls = slice(p * LANES, (p + 1) * LANES)
        q = q_ref[:, ls]
        k = k_ref[:, ls]
        qf = q.astype(F32)
        kdec = (k.astype(F32) * kd_ref[:, ls]).astype(BF16)
        for hh in range(2):
            head = 2 * p + hh
            own = (lane // RET_QK) == hh
            vs = slice(head * RET_V, (head + 1) * RET_V)
            v = v_ref[:, vs]
            qm = jnp.where(own, q, jnp.zeros_like(q))
            s = lax.dot_general(qm, k, (((1,), (1,)), ((), ())), preferred_element_type=F32)
            sd = (s * d_ref[head]).astype(BF16)
            qx = jnp.where(own, qf * qd_ref[:, ls], 0.0).astype(BF16)
            state = s_ref[head]
            o = (jnp.dot(sd, v, preferred_element_type=F32)
                 + jnp.dot(qx, state.astype(BF16), preferred_element_type=F32))
            s_ref[head] = cd_ref[head] * state + lax.dot_general(
                kdec, v, (((0,), (0,)), ((), ())), preferred_element_type=F32)
            o = o * lax.rsqrt(jnp.mean(o * o, axis=-1, keepdims=True) + EPS)
            gate = g_ref[:, vs].astype(F32)
            o_ref[:, vs] = (gate * jax.nn.sigmoid(gate) * o).astype(o_ref.dtype)


def retention_mixer(proj, proj_rope, d_tab, qd_tab, kd_tab, cd_tab, batch, seq):
    t = proj.shape[0]
    nsc = seq // RET_L
    qk_w = RET_HEADS * RET_QK
    full = lambda b, c: (0, 0)
    full3 = lambda b, c: (0, 0, 0)
    return pl.pallas_call(
        _ret_kernel,
        out_shape=jax.ShapeDtypeStruct((t, RET_WIDTH), BF16),
        grid=(batch, nsc),
        in_specs=[pl.BlockSpec((RET_L, qk_w), lambda b, c: (b * nsc + c, (COL_RQ - ROPE_BASE) // qk_w)),
                  pl.BlockSpec((RET_L, qk_w), lambda b, c: (b * nsc + c, (COL_RK - ROPE_BASE) // qk_w)),
                  pl.BlockSpec((RET_L, RET_WIDTH), lambda b, c: (b * nsc + c, COL_RV // RET_WIDTH)),
                  pl.BlockSpec((RET_L, RET_WIDTH), lambda b, c: (b * nsc + c, COL_RG // RET_WIDTH)),
                  pl.BlockSpec((RET_HEADS, RET_L, RET_L), full3),
                  pl.BlockSpec((RET_L, qk_w), full),
                  pl.BlockSpec((RET_L, qk_w), full),
                  pl.BlockSpec((RET_HEADS, LANES, RET_V), full3)],
        out_specs=pl.BlockSpec((RET_L, RET_WIDTH), lambda b, c: (b * nsc + c, 0)),
        scratch_shapes=[pltpu.VMEM((RET_HEADS, LANES, RET_V), F32)],
        compiler_params=_params(("parallel", "arbitrary"), 32 << 20),
        name="retention_mixer",
    )(proj_rope, proj_rope, proj, proj, d_tab, qd_tab, kd_tab, cd_tab)


def retention_tables():
    log_gamma = jnp.log1p(-jnp.exp2(-5.0 - jnp.arange(RET_HEADS, dtype=F32)))
    idx = jnp.arange(RET_L, dtype=F32)
    dist = jnp.abs(idx[:, None] - idx[None, :])
    chunk = jnp.arange(RET_L) // CHUNK
    visible = chunk[None, :] <= chunk[:, None]
    d_tab = jnp.where(visible[None], jnp.exp(log_gamma[:, None, None] * dist[None]), 0.0)
    lg_lane = jnp.repeat(log_gamma, RET_QK)
    qd_tab = jnp.exp(lg_lane[None, :] * (idx[:, None] + 1.0))
    kd_tab = jnp.exp(lg_lane[None, :] * (RET_L - 1.0 - idx[:, None]))
    cd = jnp.exp(log_gamma * RET_L)
    cd_tab = jnp.broadcast_to(cd[:, None, None], (RET_HEADS, LANES, RET_V))
    return d_tab, qd_tab, kd_tab, cd_tab


def _out_proj_kernel(a_ref, b_ref, c_ref, w_ref, o_ref):
    ka = a_ref.shape[1]
    kb = b_ref.shape[1]
    acc = jnp.dot(a_ref[...], w_ref[:ka, :].astype(BF16), preferred_element_type=F32)
    acc = acc + jnp.dot(b_ref[...], w_ref[ka:ka + kb, :].astype(BF16), preferred_element_type=F32)
    acc = acc + jnp.dot(c_ref[...], w_ref[ka + kb:, :].astype(BF16), preferred_element_type=F32)
    o_ref[...] = acc.astype(o_ref.dtype)


def out_proj(a, b, c, w, layer):
    t = a.shape[0]
    _, k, n = w.shape
    return pl.pallas_call(
        _out_proj_kernel,
        out_shape=jax.ShapeDtypeStruct((t, n), BF16),
        grid=(t // TM, n // TN_OUT),
        in_specs=[pl.BlockSpec((TM, a.shape[1]), lambda i, j: (i, 0)),
                  pl.BlockSpec((TM, b.shape[1]), lambda i, j: (i, 0)),
                  pl.BlockSpec((TM, c.shape[1]), lambda i, j: (i, 0)),
                  pl.BlockSpec((None, k, TN_OUT), lambda i, j: (layer, 0, j))],
        out_specs=pl.BlockSpec((TM, TN_OUT), lambda i, j: (i, j)),
        compiler_params=_params(("parallel", "arbitrary"), 56 << 20),
        name="out_proj",
    )(a, b, c, w)


def _ffn_accumulate(h_ref, wu_ref, wd_ref, o_ref):
    @pl.when(pl.program_id(1) == 0)
    def _():
        o_ref[...] = jnp.zeros_like(o_ref)

    a = jnp.dot(h_ref[...], wu_ref[...], preferred_element_type=F32)
    a = jnp.square(jnp.maximum(a, 0.0)).astype(BF16)
    o_ref[...] += jnp.dot(a, wd_ref[...], preferred_element_type=F32)


def _ffn_kernel(h_ref, wu_ref, wd_ref, o_ref):
    _ffn_accumulate(h_ref, wu_ref, wd_ref, o_ref)


def ffn(h, wu_b, wd_b):
    t, d = h.shape
    ff = wu_b.shape[1]
    return pl.pallas_call(
        _ffn_kernel,
        out_shape=jax.ShapeDtypeStruct((t, d), F32),
        grid=(t // TM_FFN, ff // TF_FFN),
        in_specs=[pl.BlockSpec((TM_FFN, d), lambda i, j: (i, 0), pipeline_mode=pl.Buffered(1)),
                  pl.BlockSpec((d, TF_FFN), lambda i, j: (0, j)),
                  pl.BlockSpec((TF_FFN, d), lambda i, j: (j, 0))],
        out_specs=pl.BlockSpec((TM_FFN, d), lambda i, j: (i, 0)),
        compiler_params=_params(("parallel", "arbitrary"), 60 << 20),
        name="ffn",
    )(h, wu_b, wd_b)


_SRC_SIZES = (SGU_WIDTH, SGU_WIDTH, MLA_Q_RANK, MLA_KV_RANK, MLA_ROPE,
              RET_HEADS * RET_QK, RET_HEADS * RET_QK, RET_WIDTH, RET_WIDTH)
(_SRC_U, _SRC_V, _SRC_CQ, _SRC_CKV, _SRC_KR, _SRC_RQ, _SRC_RK, _SRC_RV, _SRC_RG,
 IN_WIDTH) = [int(o) for o in np.concatenate([[0], np.cumsum(_SRC_SIZES)])]
_W_IN_MOVES = (
    (COL_U, _SRC_U, 3 * IN_TILE),
    (COL_RV, _SRC_RV, RET_WIDTH),
    (COL_RG, _SRC_RG, RET_WIDTH),
    (COL_CKV, _SRC_CKV, MLA_KV_RANK),
    (COL_KR, _SRC_KR, MLA_ROPE),
    (COL_KR + LANES + MLA_ROPE, _SRC_KR, MLA_ROPE),
    (COL_RQ, _SRC_RQ, RET_HEADS * RET_QK),
    (COL_RK, _SRC_RK, RET_HEADS * RET_QK),
)


def _w_in_prep_kernel(w_ref, o_ref):
    o_ref[COL_KR:COL_RQ, :] = jnp.zeros((COL_RQ - COL_KR, o_ref.shape[1]), o_ref.dtype)
    for dst, src, width in _W_IN_MOVES:
        o_ref[dst:dst + width, :] = w_ref[src:src + width, :].astype(o_ref.dtype)


def prep_w_in(w_t, layer):
    _, n, d = w_t.shape
    return pl.pallas_call(
        _w_in_prep_kernel,
        out_shape=jax.ShapeDtypeStruct((IN_WIDTH_PADDED, d), BF16),
        grid=(d // TK_PREP,),
        in_specs=[pl.BlockSpec((None, n, TK_PREP), lambda i: (layer, 0, i))],
        out_specs=pl.BlockSpec((IN_WIDTH_PADDED, TK_PREP), lambda i: (0, i)),
        compiler_params=_params(("parallel",), 40 << 20),
        name="prep_w_in",
    )(w_t)


def _prep_wq(w):
    k = w.shape[0]
    w3 = w.reshape(k, MLA_HEADS, MLA_NOPE + MLA_ROPE)
    nope = w3[:, :, :MLA_NOPE].reshape(k, MLA_HEADS * MLA_NOPE)
    rope = w3[:, :, MLA_NOPE:].reshape(k, MLA_HEADS * MLA_ROPE)
    return jnp.concatenate([nope, rope], axis=1).astype(BF16)


def _rope_lane_tables(seq):
    inv_freq = 1.0 / (ROPE_THETA ** (jnp.arange(0, MLA_ROPE, 2, dtype=F32) / MLA_ROPE))
    ang = jnp.arange(seq, dtype=F32)[:, None] * inv_freq[None, :]
    cos, sin = jnp.cos(ang), jnp.sin(ang)
    cos_t = jnp.concatenate([cos, cos, cos, cos], axis=1)
    sin_t = jnp.concatenate([-sin, sin, -sin, sin], axis=1)
    return cos_t, sin_t


def kernel(x, norm_mix_pre, norm_mix_post, norm_ffn_pre, norm_ffn_post, w_in, sgu_ln_g, sgu_ln_b, sgu_w, sgu_b, mla_q_norm, mla_wq_b, mla_kv_norm, mla_wkv_b, w_out, w_up, w_down):
    batch, seq, d = x.shape
    depth = w_in.shape[0]
    cos_t, sin_t = _rope_lane_tables(seq)
    d_tab, qd_tab, kd_tab, cd_tab = retention_tables()
    w_in_t = jnp.swapaxes(w_in, 1, 2)

    xf = x.reshape(batch * seq, d)
    h = rmsnorm_rows(xf, norm_mix_pre[0])
    for l in range(depth):
        proj, proj_rope = in_proj(h, prep_w_in(w_in_t, l), cos_t, sin_t, seq)
        bias_full = jnp.repeat(sgu_b[l].T, SGU_WIDTH // SGU_HEADS, axis=1)
        out_a = sgu_mixer(proj, sgu_ln_g[l], sgu_ln_b[l], sgu_w[l], bias_full)
        q = q_proj(proj, mla_q_norm[l], _prep_wq(mla_wq_b[l]), cos_t, sin_t, seq)
        kv = kv_proj(proj_rope, mla_kv_norm[l], mla_wkv_b, l)
        out_b, wu_b, wd_b = mla_attention(q, kv, proj_rope, w_up, w_down, l, batch, seq)
        out_c = retention_mixer(proj, proj_rope, d_tab, qd_tab, kd_tab, cd_tab, batch, seq)
        mixed = out_proj(out_a, out_b, out_c, w_out, l)
        xf, h2 = residual_norm(xf, mixed, norm_mix_post[l], norm_ffn_pre[l])
        f = ffn(h2, wu_b, wd_b)
        g_next = norm_mix_pre[l + 1] if l + 1 < depth else None
        xf, h = residual_norm(xf, f, norm_ffn_post[l], g_next)
    return xf.reshape(batch, seq, d)
```

```python
import functools
import math

import numpy as np
import jax
import jax.numpy as jnp
from jax import lax
from jax.experimental import pallas as pl
from jax.experimental.pallas import tpu as pltpu

F32 = jnp.float32
BF16 = jnp.bfloat16

D_MODEL = 4096
CHUNK = 64
EPS = 1e-6
ROPE_THETA = 10000.0
SGU_BLOCK = 128
SGU_WIDTH = 1024
SGU_HEADS = 8
MLA_HEADS = 16
MLA_NOPE = 128
MLA_ROPE = 64
MLA_V = 128
MLA_Q_RANK = 1024
MLA_KV_RANK = 512
MLA_WIDTH = MLA_HEADS * MLA_V
RET_HEADS = 8
RET_QK = 64
RET_V = 128
RET_WIDTH = RET_HEADS * RET_V
FF_DIM = 4 * D_MODEL

LANES = 128
VMEM_CAP = 64 * 1024 * 1024

IN_TILE = 1024
COL_U = 0 * IN_TILE
COL_V = 1 * IN_TILE
COL_CQ = 2 * IN_TILE
COL_RV = 3 * IN_TILE
COL_RG = 4 * IN_TILE
COL_CKV = 5 * IN_TILE
COL_KR = COL_CKV + MLA_KV_RANK
COL_RQ = 6 * IN_TILE
COL_RK = COL_RQ + RET_HEADS * RET_QK
IN_WIDTH_PADDED = 7 * IN_TILE
ROPE_BASE = COL_CKV

TM = 1024
TM_NORM = 256
TM_FFN = 512
TF_FFN = 512
TN_OUT = 512
TK_PREP = 256
TB_SGU = 512
TQ = 512
RET_L = 256


def _vmem_limit(nbytes):
    return int(min(VMEM_CAP - (2 << 20), nbytes))


def _params(sem, nbytes):
    return pltpu.CompilerParams(dimension_semantics=sem, vmem_limit_bytes=_vmem_limit(nbytes))


def _rms(x, g):
    return x * lax.rsqrt(jnp.mean(x * x, axis=-1, keepdims=True) + EPS) * g


def _rmsnorm_kernel(x_ref, g_ref, o_ref):
    o_ref[...] = _rms(x_ref[...], g_ref[...]).astype(o_ref.dtype)


def rmsnorm_rows(x, g):
    t, d = x.shape
    return pl.pallas_call(
        _rmsnorm_kernel,
        out_shape=jax.ShapeDtypeStruct((t, d), BF16),
        grid=(t // TM_NORM,),
        in_specs=[pl.BlockSpec((TM_NORM, d), lambda i: (i, 0)),
                  pl.BlockSpec((1, d), lambda i: (0, 0))],
        out_specs=pl.BlockSpec((TM_NORM, d), lambda i: (i, 0)),
        compiler_params=_params(("parallel",), 32 << 20),
        name="rmsnorm_rows",
    )(x, g.reshape(1, d))


def _residual_norm_kernel(x_ref, y_ref, gp_ref, gn_ref, xo_ref, ho_ref):
    xn = x_ref[...] + _rms(y_ref[...].astype(F32), gp_ref[...])
    xo_ref[...] = xn
    ho_ref[...] = _rms(xn, gn_ref[...]).astype(ho_ref.dtype)


def _residual_kernel(x_ref, y_ref, gp_ref, xo_ref):
    xo_ref[...] = x_ref[...] + _rms(y_ref[...].astype(F32), gp_ref[...])


def residual_norm(x, y, g_post, g_next):
    t, d = x.shape
    row = pl.BlockSpec((TM_NORM, d), lambda i: (i, 0))
    vec = pl.BlockSpec((1, d), lambda i: (0, 0))
    if g_next is None:
        return pl.pallas_call(
            _residual_kernel,
            out_shape=jax.ShapeDtypeStruct((t, d), F32),
            grid=(t // TM_NORM,),
            in_specs=[row, row, vec],
            out_specs=row,
            compiler_params=_params(("parallel",), 40 << 20),
            name="residual",
        )(x, y, g_post.reshape(1, d)), None
    return pl.pallas_call(
        _residual_norm_kernel,
        out_shape=(jax.ShapeDtypeStruct((t, d), F32), jax.ShapeDtypeStruct((t, d), BF16)),
        grid=(t // TM_NORM,),
        in_specs=[row, row, vec, vec],
        out_specs=(row, row),
        compiler_params=_params(("parallel",), 40 << 20),
        name="residual_norm",
    )(x, y, g_post.reshape(1, d), g_next.reshape(1, d))


def _rope_cols(x, cos, sin):
    lane = lax.broadcasted_iota(jnp.int32, x.shape, 1)
    first_half = (lane % MLA_ROPE) < (MLA_ROPE // 2)
    partner = jnp.where(first_half, pltpu.roll(x, LANES - MLA_ROPE // 2, 1), pltpu.roll(x, MLA_ROPE // 2, 1))
    return x * cos + partner * sin


def _in_proj_main_kernel(h_ref, w_ref, o_ref):
    o_ref[...] = lax.dot_general(h_ref[...], w_ref[...], (((1,), (1,)), ((), ())),
                                 preferred_element_type=F32).astype(o_ref.dtype)


def _in_proj_rope_kernel(h_ref, w_ref, cos_ref, sin_ref, o_ref):
    j = pl.program_id(1)
    acc = lax.dot_general(h_ref[...], w_ref[...], (((1,), (1,)), ((), ())), preferred_element_type=F32)

    @pl.when(j == 0)
    def _():
        lo = COL_KR - COL_CKV
        o_ref[:, :lo] = acc[:, :lo].astype(o_ref.dtype)
        for c in range(lo // LANES, lo // LANES + 2):
            sl = slice(c * LANES, (c + 1) * LANES)
            o_ref[:, sl] = _rope_cols(acc[:, sl], cos_ref[...], sin_ref[...]).astype(o_ref.dtype)
        o_ref[:, lo + 2 * LANES:] = acc[:, lo + 2 * LANES:].astype(o_ref.dtype)

    @pl.when(j == 1)
    def _():
        q_cols = (RET_HEADS * RET_QK) // LANES
        for c in range(IN_TILE // LANES):
            sl = slice(c * LANES, (c + 1) * LANES)
            r = _rope_cols(acc[:, sl], cos_ref[...], sin_ref[...])
            if c < q_cols:
                r = r * (RET_QK ** -0.5)
            o_ref[:, sl] = r.astype(o_ref.dtype)


def in_proj(h, w_t, cos_t, sin_t, seq):
    t, d = h.shape
    pos_blocks = seq // TM
    tab = pl.BlockSpec((TM, LANES), lambda i, j: (i % pos_blocks, 0))
    rope_tile0 = ROPE_BASE // IN_TILE
    main = pl.pallas_call(
        _in_proj_main_kernel,
        out_shape=jax.ShapeDtypeStruct((t, ROPE_BASE), BF16),
        grid=(t // TM, rope_tile0),
        in_specs=[pl.BlockSpec((TM, d), lambda i, j: (i, 0)),
                  pl.BlockSpec((IN_TILE, d), lambda i, j: (j, 0))],
        out_specs=pl.BlockSpec((TM, IN_TILE), lambda i, j: (i, j)),
        compiler_params=_params(("parallel", "arbitrary"), 52 << 20),
        name="in_proj_main",
    )(h, w_t)
    rope = pl.pallas_call(
        _in_proj_rope_kernel,
        out_shape=jax.ShapeDtypeStruct((t, IN_WIDTH_PADDED - ROPE_BASE), BF16),
        grid=(t // TM, (IN_WIDTH_PADDED - ROPE_BASE) // IN_TILE),
        in_specs=[pl.BlockSpec((TM, d), lambda i, j: (i, 0)),
                  pl.BlockSpec((IN_TILE, d), lambda i, j: (rope_tile0 + j, 0)),
                  tab, tab],
        out_specs=pl.BlockSpec((TM, IN_TILE), lambda i, j: (i, j)),
        compiler_params=_params(("parallel", "arbitrary"), 52 << 20),
        name="in_proj_rope",
    )(h, w_t, cos_t, sin_t)
    return main, rope


def _sgu_kernel(u_ref, v_ref, g_ref, b_ref, w_ref, bias_ref, o_ref):
    row = lax.broadcasted_iota(jnp.int32, (SGU_BLOCK, SGU_BLOCK), 0)
    col = lax.broadcasted_iota(jnp.int32, (SGU_BLOCK, SGU_BLOCK), 1)
    keep = (col // CHUNK) <= (row // CHUNK)
    for g in range(SGU_HEADS):
        sl = slice(g * LANES, (g + 1) * LANES)
        wg = jnp.where(keep, w_ref[g], 0.0).astype(BF16)
        vg = v_ref[:, sl].astype(F32)
        mu = jnp.mean(vg, axis=-1, keepdims=True)
        vc = vg - mu
        y = vc * lax.rsqrt(jnp.mean(vc * vc, axis=-1, keepdims=True) + EPS) * g_ref[:, sl] + b_ref[:, sl]
        yb = y.astype(BF16)
        for n in range(TB_SGU // SGU_BLOCK):
            rs = slice(n * SGU_BLOCK, (n + 1) * SGU_BLOCK)
            mixed = jnp.dot(wg, yb[rs, :], preferred_element_type=F32) + bias_ref[:, sl]
            o_ref[rs, sl] = (u_ref[rs, sl].astype(F32) * mixed).astype(o_ref.dtype)


def sgu_mixer(proj, ln_g, ln_b, w_s, bias_full):
    t = proj.shape[0]
    vec = pl.BlockSpec((1, SGU_WIDTH), lambda i: (0, 0))
    return pl.pallas_call(
        _sgu_kernel,
        out_shape=jax.ShapeDtypeStruct((t, SGU_WIDTH), BF16),
        grid=(t // TB_SGU,),
        in_specs=[pl.BlockSpec((TB_SGU, SGU_WIDTH), lambda i: (i, COL_U // SGU_WIDTH)),
                  pl.BlockSpec((TB_SGU, SGU_WIDTH), lambda i: (i, COL_V // SGU_WIDTH)),
                  vec, vec,
                  pl.BlockSpec((SGU_HEADS, SGU_BLOCK, SGU_BLOCK), lambda i: (0, 0, 0)),
                  pl.BlockSpec((SGU_BLOCK, SGU_WIDTH), lambda i: (0, 0))],
        out_specs=pl.BlockSpec((TB_SGU, SGU_WIDTH), lambda i: (i, 0)),
        compiler_params=_params(("parallel",), 32 << 20),
        name="sgu_mixer",
    )(proj, proj, ln_g.reshape(1, -1), ln_b.reshape(1, -1), w_s, bias_full)


Q_SCALE = (MLA_NOPE + MLA_ROPE) ** -0.5 * math.log2(math.e)
Q_ROPE_TILE = (MLA_HEADS * MLA_NOPE) // IN_TILE


def _q_proj_kernel(c_ref, g_ref, w_ref, cos_ref, sin_ref, o_ref, cn_ref):
    j = pl.program_id(1)

    @pl.when(j == 0)
    def _():
        cn_ref[...] = _rms(c_ref[...].astype(F32), g_ref[...]).astype(cn_ref.dtype)

    acc = jnp.dot(cn_ref[...], w_ref[...], preferred_element_type=F32) * Q_SCALE

    @pl.when(j < Q_ROPE_TILE)
    def _():
        o_ref[...] = acc.astype(o_ref.dtype)

    @pl.when(j >= Q_ROPE_TILE)
    def _():
        for c in range(IN_TILE // LANES):
            sl = slice(c * LANES, (c + 1) * LANES)
            o_ref[:, sl] = _rope_cols(acc[:, sl], cos_ref[...], sin_ref[...]).astype(o_ref.dtype)


def q_proj(proj, g, w, cos_t, sin_t, seq):
    t = proj.shape[0]
    n = w.shape[1]
    pos_blocks = seq // TM
    tab = pl.BlockSpec((TM, LANES), lambda i, j: (i % pos_blocks, 0))
    return pl.pallas_call(
        _q_proj_kernel,
        out_shape=jax.ShapeDtypeStruct((t, n), BF16),
        grid=(t // TM, n // IN_TILE),
        in_specs=[pl.BlockSpec((TM, MLA_Q_RANK), lambda i, j: (i, COL_CQ // MLA_Q_RANK)),
                  pl.BlockSpec((1, MLA_Q_RANK), lambda i, j: (0, 0)),
                  pl.BlockSpec((MLA_Q_RANK, IN_TILE), lambda i, j: (0, j)),
                  tab, tab],
        out_specs=pl.BlockSpec((TM, IN_TILE), lambda i, j: (i, j)),
        scratch_shapes=[pltpu.VMEM((TM, MLA_Q_RANK), BF16)],
        compiler_params=_params(("parallel", "arbitrary"), 40 << 20),
        name="q_proj",
    )(proj, g.reshape(1, -1), w, cos_t, sin_t)


def _kv_proj_kernel(c_ref, g_ref, w_ref, o_ref, cn_ref):
    @pl.when(pl.program_id(1) == 0)
    def _():
        cn_ref[...] = _rms(c_ref[...].astype(F32), g_ref[...]).astype(cn_ref.dtype)

    o_ref[...] = jnp.dot(cn_ref[...], w_ref[...].astype(BF16), preferred_element_type=F32).astype(o_ref.dtype)


def kv_proj(proj, g, w, layer):
    t = proj.shape[0]
    n = w.shape[2]
    return pl.pallas_call(
        _kv_proj_kernel,
        out_shape=jax.ShapeDtypeStruct((t, n), BF16),
        grid=(t // TM, n // IN_TILE),
        in_specs=[pl.BlockSpec((TM, MLA_KV_RANK), lambda i, j: (i, (COL_CKV - ROPE_BASE) // MLA_KV_RANK)),
                  pl.BlockSpec((1, MLA_KV_RANK), lambda i, j: (0, 0)),
                  pl.BlockSpec((None, MLA_KV_RANK, IN_TILE), lambda i, j: (layer, 0, j))],
        out_specs=pl.BlockSpec((TM, IN_TILE), lambda i, j: (i, j)),
        scratch_shapes=[pltpu.VMEM((TM, MLA_KV_RANK), BF16)],
        compiler_params=_params(("parallel", "arbitrary"), 32 << 20),
        name="kv_proj",
    )(proj, g.reshape(1, -1), w)


ATTN_GROUP = 4


HEAD_W = MLA_NOPE + MLA_V


def _attn_tiles(qs, kv_ref, kr_ref, base, ntiles, mask_last, states):
    states = list(states)
    ones = jnp.ones((TQ, LANES), BF16)
    for t in range(ntiles):
        st = pl.multiple_of(base + t * TQ, TQ)
        for hd in range(2):
            m, l, acc = states[hd]
            kn = kv_ref[pl.ds(st, TQ), hd * HEAD_W:hd * HEAD_W + MLA_NOPE]
            v = kv_ref[pl.ds(st, TQ), hd * HEAD_W + MLA_NOPE:(hd + 1) * HEAD_W]
            k = jnp.concatenate([kn, kr_ref[pl.ds(st, TQ), hd * LANES:(hd + 1) * LANES]], axis=1)
            s = lax.dot_general(qs[hd], k, (((1,), (1,)), ((), ())), preferred_element_type=F32)
            if mask_last and t == ntiles - 1:
                row = lax.broadcasted_iota(jnp.int32, (TQ, TQ), 0)
                col = lax.broadcasted_iota(jnp.int32, (TQ, TQ), 1)
                s = jnp.where((col // CHUNK) <= (row // CHUNK), s, -jnp.inf)
            m_next = jnp.maximum(m, jnp.max(s, axis=1, keepdims=True))
            alpha = jnp.exp2(m - m_next)
            p = jnp.exp2(s - jnp.tile(m_next, (1, TQ // LANES))).astype(BF16)
            pv = jnp.dot(p, jnp.concatenate([v, ones], axis=1), preferred_element_type=F32)
            states[hd] = (m_next, alpha * l + pv[:, LANES:], alpha * acc + pv[:, :LANES])
    return states


def _attn_kernel(qn_ref, qr_ref, kv_ref, kr_ref, wu_ref, wd_ref, o_ref, wub_ref, wdb_ref,
                 m_ref, l_ref, acc_ref, *, cast_from):
    qi = pl.program_id(2)

    @pl.when(qi >= cast_from)
    def _():
        wub_ref[...] = wu_ref[...].astype(wub_ref.dtype)
        wdb_ref[...] = wd_ref[...].astype(wdb_ref.dtype)

    qr = qr_ref[...]
    qs = [jnp.concatenate([qn_ref[:, hd * LANES:(hd + 1) * LANES], qr], axis=1) for hd in range(2)]

    m_ref[...] = jnp.full_like(m_ref, -jnp.inf)
    l_ref[...] = jnp.zeros_like(l_ref)
    acc_ref[...] = jnp.zeros_like(acc_ref)

    def load_states():
        return [(m_ref[hd], l_ref[hd], acc_ref[hd]) for hd in range(2)]

    def body(g, carry):
        base = pl.multiple_of(g * (ATTN_GROUP * TQ), ATTN_GROUP * TQ)
        states = _attn_tiles(qs, kv_ref, kr_ref, base, ATTN_GROUP, False, load_states())
        for hd in range(2):
            m_ref[hd], l_ref[hd], acc_ref[hd] = states[hd]
        return carry

    lax.fori_loop(0, qi // ATTN_GROUP, body, 0)

    base = pl.multiple_of((qi // ATTN_GROUP) * (ATTN_GROUP * TQ), ATTN_GROUP * TQ)
    for r in range(ATTN_GROUP):
        @pl.when(qi % ATTN_GROUP == r)
        def _():
            states = _attn_tiles(qs, kv_ref, kr_ref, base, r + 1, True, load_states())
            for hd in range(2):
                _, l, acc = states[hd]
                o_ref[:, hd * LANES:(hd + 1) * LANES] = (acc / l).astype(o_ref.dtype)


def mla_attention(q, kv, proj_rope, w_up, w_down, layer, batch, seq):
    t = q.shape[0]
    nq = seq // TQ
    pairs = MLA_HEADS // 2
    cast_from = nq // 2
    casts = nq - cast_from
    steps = batch * pairs * casts
    _, d, ff = w_up.shape
    up_rows, down_rows = d // steps, ff // steps
    assert up_rows * steps == d and down_rows * steps == ff and up_rows % 16 == 0

    def slab(b, p, i):
        return (b * pairs + p) * casts + jnp.maximum(i - cast_from, 0)
    pair_w = 2 * LANES
    kr_pair_block = (COL_KR - ROPE_BASE) // pair_w
    rope_block0 = (MLA_HEADS * MLA_NOPE) // LANES
    return pl.pallas_call(
        functools.partial(_attn_kernel, cast_from=cast_from),
        out_shape=(jax.ShapeDtypeStruct((t, MLA_WIDTH), BF16),
                   jax.ShapeDtypeStruct((d, ff), BF16),
                   jax.ShapeDtypeStruct((ff, d), BF16)),
        grid=(batch, pairs, nq),
        in_specs=[pl.BlockSpec((TQ, pair_w), lambda b, p, i: (b * nq + i, p)),
                  pl.BlockSpec((TQ, LANES), lambda b, p, i: (b * nq + i, rope_block0 + p)),
                  pl.BlockSpec((seq, 2 * HEAD_W), lambda b, p, i: (b, p)),
                  pl.BlockSpec((seq, pair_w), lambda b, p, i: (b, kr_pair_block)),
                  pl.BlockSpec((None, up_rows, ff), lambda b, p, i: (layer, slab(b, p, i), 0)),
                  pl.BlockSpec((None, down_rows, d), lambda b, p, i: (layer, slab(b, p, i), 0))],
        out_specs=(pl.BlockSpec((TQ, pair_w), lambda b, p, i: (b * nq + i, p)),
                   pl.BlockSpec((up_rows, ff), lambda b, p, i: (slab(b, p, i), 0)),
                   pl.BlockSpec((down_rows, d), lambda b, p, i: (slab(b, p, i), 0))),
        scratch_shapes=[pltpu.VMEM((2, TQ, LANES), F32), pltpu.VMEM((2, TQ, LANES), F32),
                        pltpu.VMEM((2, TQ, LANES), F32)],
        compiler_params=_params(("parallel", "parallel", "arbitrary"), 48 << 20),
        name="mla_attention",
    )(q, q, kv, proj_rope, w_up, w_down)


def _ret_kernel(q_ref, k_ref, v_ref, g_ref, d_ref, qd_ref, kd_ref, cd_ref, o_ref, s_ref):
    @pl.when(pl.program_id(1) == 0)
    def _():
        s_ref[...] = jnp.zeros_like(s_ref)

    lane = lax.broadcasted_iota(jnp.int32, (RET_L, LANES), 1)
    for p in range(RET_HEADS // 2):
        ls = slice(p * LANES, (p + 1) * LANES)
        q = q_ref[:, ls]
        k = k_ref[:, ls]
        qf = q.astype(F32)
        kdec = (k.astype(F32) * kd_ref[:, ls]).astype(BF16)
        for hh in range(2):
            head = 2 * p + hh
            own = (lane // RET_QK) == hh
            vs = slice(head * RET_V, (head + 1) * RET_V)
            v = v_ref[:, vs]
            qm = jnp.where(own, q, jnp.zeros_like(q))
            s = lax.dot_general(qm, k, (((1,), (1,)), ((), ())), preferred_element_type=F32)
            sd = (s * d_ref[head]).astype(BF16)
            qx = jnp.where(own, qf * qd_ref[:, ls], 0.0).astype(BF16)
            state = s_ref[head]
            o = (jnp.dot(sd, v, preferred_element_type=F32)
                 + jnp.dot(qx, state.astype(BF16), preferred_element_type=F32))
            s_ref[head] = cd_ref[head] * state + lax.dot_general(
                kdec, v, (((0,), (0,)), ((), ())), preferred_element_type=F32)
            o = o * lax.rsqrt(jnp.mean(o * o, axis=-1, keepdims=True) + EPS)
            gate = g_ref[:, vs].astype(F32)
            o_ref[:, vs] = (gate * jax.nn.sigmoid(gate) * o).astype(o_ref.dtype)


def retention_mixer(proj, proj_rope, d_tab, qd_tab, kd_tab, cd_tab, batch, seq):
    t = proj.shape[0]
    nsc = seq // RET_L
    qk_w = RET_HEADS * RET_QK
    full = lambda b, c: (0, 0)
    full3 = lambda b, c: (0, 0, 0)
    return pl.pallas_call(
        _ret_kernel,
        out_shape=jax.ShapeDtypeStruct((t, RET_WIDTH), BF16),
        grid=(batch, nsc),
        in_specs=[pl.BlockSpec((RET_L, qk_w), lambda b, c: (b * nsc + c, (COL_RQ - ROPE_BASE) // qk_w)),
                  pl.BlockSpec((RET_L, qk_w), lambda b, c: (b * nsc + c, (COL_RK - ROPE_BASE) // qk_w)),
                  pl.BlockSpec((RET_L, RET_WIDTH), lambda b, c: (b * nsc + c, COL_RV // RET_WIDTH)),
                  pl.BlockSpec((RET_L, RET_WIDTH), lambda b, c: (b * nsc + c, COL_RG // RET_WIDTH)),
                  pl.BlockSpec((RET_HEADS, RET_L, RET_L), full3),
                  pl.BlockSpec((RET_L, qk_w), full),
                  pl.BlockSpec((RET_L, qk_w), full),
                  pl.BlockSpec((RET_HEADS, LANES, RET_V), full3)],
        out_specs=pl.BlockSpec((RET_L, RET_WIDTH), lambda b, c: (b * nsc + c, 0)),
        scratch_shapes=[pltpu.VMEM((RET_HEADS, LANES, RET_V), F32)],
        compiler_params=_params(("parallel", "arbitrary"), 32 << 20),
        name="retention_mixer",
    )(proj_rope, proj_rope, proj, proj, d_tab, qd_tab, kd_tab, cd_tab)


def retention_tables():
    log_gamma = jnp.log1p(-jnp.exp2(-5.0 - jnp.arange(RET_HEADS, dtype=F32)))
    idx = jnp.arange(RET_L, dtype=F32)
    dist = jnp.abs(idx[:, None] - idx[None, :])
    chunk = jnp.arange(RET_L) // CHUNK
    visible = chunk[None, :] <= chunk[:, None]
    d_tab = jnp.where(visible[None], jnp.exp(log_gamma[:, None, None] * dist[None]), 0.0)
    lg_lane = jnp.repeat(log_gamma, RET_QK)
    qd_tab = jnp.exp(lg_lane[None, :] * (idx[:, None] + 1.0))
    kd_tab = jnp.exp(lg_lane[None, :] * (RET_L - 1.0 - idx[:, None]))
    cd = jnp.exp(log_gamma * RET_L)
    cd_tab = jnp.broadcast_to(cd[:, None, None], (RET_HEADS, LANES, RET_V))
    return d_tab, qd_tab, kd_tab, cd_tab


def _out_proj_kernel(a_ref, b_ref, c_ref, w_ref, o_ref):
    ka = a_ref.shape[1]
    kb = b_ref.shape[1]
    acc = jnp.dot(a_ref[...], w_ref[:ka, :].astype(BF16), preferred_element_type=F32)
    acc = acc + jnp.dot(b_ref[...], w_ref[ka:ka + kb, :].astype(BF16), preferred_element_type=F32)
    acc = acc + jnp.dot(c_ref[...], w_ref[ka + kb:, :].astype(BF16), preferred_element_type=F32)
    o_ref[...] = acc.astype(o_ref.dtype)


def out_proj(a, b, c, w, layer):
    t = a.shape[0]
    _, k, n = w.shape
    return pl.pallas_call(
        _out_proj_kernel,
        out_shape=jax.ShapeDtypeStruct((t, n), BF16),
        grid=(t // TM, n // TN_OUT),
        in_specs=[pl.BlockSpec((TM, a.shape[1]), lambda i, j: (i, 0)),
                  pl.BlockSpec((TM, b.shape[1]), lambda i, j: (i, 0)),
                  pl.BlockSpec((TM, c.shape[1]), lambda i, j: (i, 0)),
                  pl.BlockSpec((None, k, TN_OUT), lambda i, j: (layer, 0, j))],
        out_specs=pl.BlockSpec((TM, TN_OUT), lambda i, j: (i, j)),
        compiler_params=_params(("parallel", "arbitrary"), 56 << 20),
        name="out_proj",
    )(a, b, c, w)


def _ffn_accumulate(h_ref, wu_ref, wd_ref, o_ref):
    @pl.when(pl.program_id(1) == 0)
    def _():
        o_ref[...] = jnp.zeros_like(o_ref)

    a = jnp.dot(h_ref[...], wu_ref[...], preferred_element_type=F32)
    a = jnp.square(jnp.maximum(a, 0.0)).astype(BF16)
    o_ref[...] += jnp.dot(a, wd_ref[...], preferred_element_type=F32)


def _ffn_kernel(h_ref, wu_ref, wd_ref, o_ref):
    _ffn_accumulate(h_ref, wu_ref, wd_ref, o_ref)


def ffn(h, wu_b, wd_b):
    t, d = h.shape
    ff = wu_b.shape[1]
    return pl.pallas_call(
        _ffn_kernel,
        out_shape=jax.ShapeDtypeStruct((t, d), F32),
        grid=(t // TM_FFN, ff // TF_FFN),
        in_specs=[pl.BlockSpec((TM_FFN, d), lambda i, j: (i, 0)),
                  pl.BlockSpec((d, TF_FFN), lambda i, j: (0, j)),
                  pl.BlockSpec((TF_FFN, d), lambda i, j: (j, 0))],
        out_specs=pl.BlockSpec((TM_FFN, d), lambda i, j: (i, 0)),
        compiler_params=_params(("parallel", "arbitrary"), 56 << 20),
        name="ffn",
    )(h, wu_b, wd_b)


_SRC_SIZES = (SGU_WIDTH, SGU_WIDTH, MLA_Q_RANK, MLA_KV_RANK, MLA_ROPE,
              RET_HEADS * RET_QK, RET_HEADS * RET_QK, RET_WIDTH, RET_WIDTH)
(_SRC_U, _SRC_V, _SRC_CQ, _SRC_CKV, _SRC_KR, _SRC_RQ, _SRC_RK, _SRC_RV, _SRC_RG,
 IN_WIDTH) = [int(o) for o in np.concatenate([[0], np.cumsum(_SRC_SIZES)])]
_W_IN_MOVES = (
    (COL_U, _SRC_U, 3 * IN_TILE),
    (COL_RV, _SRC_RV, RET_WIDTH),
    (COL_RG, _SRC_RG, RET_WIDTH),
    (COL_CKV, _SRC_CKV, MLA_KV_RANK),
    (COL_KR, _SRC_KR, MLA_ROPE),
    (COL_KR + LANES + MLA_ROPE, _SRC_KR, MLA_ROPE),
    (COL_RQ, _SRC_RQ, RET_HEADS * RET_QK),
    (COL_RK, _SRC_RK, RET_HEADS * RET_QK),
)


def _w_in_prep_kernel(w_ref, o_ref):
    o_ref[COL_KR:COL_RQ, :] = jnp.zeros((COL_RQ - COL_KR, o_ref.shape[1]), o_ref.dtype)
    for dst, src, width in _W_IN_MOVES:
        o_ref[dst:dst + width, :] = w_ref[src:src + width, :].astype(o_ref.dtype)


def prep_w_in(w_t, layer):
    _, n, d = w_t.shape
    return pl.pallas_call(
        _w_in_prep_kernel,
        out_shape=jax.ShapeDtypeStruct((IN_WIDTH_PADDED, d), BF16),
        grid=(d // TK_PREP,),
        in_specs=[pl.BlockSpec((None, n, TK_PREP), lambda i: (layer, 0, i))],
        out_specs=pl.BlockSpec((IN_WIDTH_PADDED, TK_PREP), lambda i: (0, i)),
        compiler_params=_params(("parallel",), 40 << 20),
        name="prep_w_in",
    )(w_t)


def _prep_wq(w):
    k = w.shape[0]
    w3 = w.reshape(k, MLA_HEADS, MLA_NOPE + MLA_ROPE)
    nope = w3[:, :, :MLA_NOPE].reshape(k, MLA_HEADS * MLA_NOPE)
    rope = w3[:, :, MLA_NOPE:].reshape(k, MLA_HEADS * MLA_ROPE)
    return jnp.concatenate([nope, rope], axis=1).astype(BF16)


def _rope_lane_tables(seq):
    inv_freq = 1.0 / (ROPE_THETA ** (jnp.arange(0, MLA_ROPE, 2, dtype=F32) / MLA_ROPE))
    ang = jnp.arange(seq, dtype=F32)[:, None] * inv_freq[None, :]
    cos, sin = jnp.cos(ang), jnp.sin(ang)
    cos_t = jnp.concatenate([cos, cos, cos, cos], axis=1)
    sin_t = jnp.concatenate([-sin, sin, -sin, sin], axis=1)
    return cos_t, sin_t


def kernel(x, norm_mix_pre, norm_mix_post, norm_ffn_pre, norm_ffn_post, w_in, sgu_ln_g, sgu_ln_b, sgu_w, sgu_b, mla_q_norm, mla_wq_b, mla_kv_norm, mla_wkv_b, w_out, w_up, w_down):
    batch, seq, d = x.shape
    depth = w_in.shape[0]
    cos_t, sin_t = _rope_lane_tables(seq)
    d_tab, qd_tab, kd_tab, cd_tab = retention_tables()
    w_in_t = jnp.swapaxes(w_in, 1, 2)

    xf = x.reshape(batch * seq, d)
    h = rmsnorm_rows(xf, norm_mix_pre[0])
    for l in range(depth):
        proj, proj_rope = in_proj(h, prep_w_in(w_in_t, l), cos_t, sin_t, seq)
        bias_full = jnp.repeat(sgu_b[l].T, SGU_WIDTH // SGU_HEADS, axis=1)
        out_a = sgu_mixer(proj, sgu_ln_g[l], sgu_ln_b[l], sgu_w[l], bias_full)
        q = q_proj(proj, mla_q_norm[l], _prep_wq(mla_wq_b[l]), cos_t, sin_t, seq)
        kv = kv_proj(proj_rope, mla_kv_norm[l], mla_wkv_b, l)
        out_b, wu_b, wd_b = mla_attention(q, kv, proj_rope, w_up, w_down, l, batch, seq)
        out_c = retention_mixer(proj, proj_rope, d_tab, qd_tab, kd_tab, cd_tab, batch, seq)
        mixed = out_proj(out_a, out_b, out_c, w_out, l)
        xf, h2 = residual_norm(xf, mixed, norm_mix_post[l], norm_ffn_pre[l])
        f = ffn(h2, wu_b, wd_b)
        g_next = norm_mix_pre[l + 1] if l + 1 < depth else None
        xf, h = residual_norm(xf, f, norm_ffn_post[l], g_next)
    return xf.reshape(batch, seq, d)
```

```python
import functools
import math

import numpy as np
import jax
import jax.numpy as jnp
from jax import lax
from jax.experimental import pallas as pl
from jax.experimental.pallas import tpu as pltpu

F32 = jnp.float32
BF16 = jnp.bfloat16

D_MODEL = 4096
CHUNK = 64
EPS = 1e-6
ROPE_THETA = 10000.0
SGU_BLOCK = 128
SGU_WIDTH = 1024
SGU_HEADS = 8
MLA_HEADS = 16
MLA_NOPE = 128
MLA_ROPE = 64
MLA_V = 128
MLA_Q_RANK = 1024
MLA_KV_RANK = 512
MLA_WIDTH = MLA_HEADS * MLA_V
RET_HEADS = 8
RET_QK = 64
RET_V = 128
RET_WIDTH = RET_HEADS * RET_V
FF_DIM = 4 * D_MODEL

LANES = 128
VMEM_CAP = 64 * 1024 * 1024

IN_TILE = 1024
COL_U = 0 * IN_TILE
COL_V = 1 * IN_TILE
COL_CQ = 2 * IN_TILE
COL_RV = 3 * IN_TILE
COL_RG = 4 * IN_TILE
COL_CKV = 5 * IN_TILE
COL_KR = COL_CKV + MLA_KV_RANK
COL_RQ = 6 * IN_TILE
COL_RK = COL_RQ + RET_HEADS * RET_QK
IN_WIDTH_PADDED = 7 * IN_TILE
ROPE_BASE = COL_CKV

TM = 1024
TM_NORM = 256
TM_FFN = 512
TF_FFN = 512
TN_OUT = 1024
TK_PREP = 256
TB_SGU = 512
TQ = 512
RET_L = 256


def _vmem_limit(nbytes):
    return int(min(VMEM_CAP - (2 << 20), nbytes))


def _params(sem, nbytes):
    return pltpu.CompilerParams(dimension_semantics=sem, vmem_limit_bytes=_vmem_limit(nbytes))


def _rms(x, g):
    return x * lax.rsqrt(jnp.mean(x * x, axis=-1, keepdims=True) + EPS) * g


def _rmsnorm_kernel(x_ref, g_ref, o_ref):
    o_ref[...] = _rms(x_ref[...], g_ref[...]).astype(o_ref.dtype)


def rmsnorm_rows(x, g):
    t, d = x.shape
    return pl.pallas_call(
        _rmsnorm_kernel,
        out_shape=jax.ShapeDtypeStruct((t, d), BF16),
        grid=(t // TM_NORM,),
        in_specs=[pl.BlockSpec((TM_NORM, d), lambda i: (i, 0)),
                  pl.BlockSpec((1, d), lambda i: (0, 0))],
        out_specs=pl.BlockSpec((TM_NORM, d), lambda i: (i, 0)),
        compiler_params=_params(("parallel",), 32 << 20),
        name="rmsnorm_rows",
    )(x, g.reshape(1, d))


def _residual_norm_kernel(x_ref, y_ref, gp_ref, gn_ref, xo_ref, ho_ref):
    xn = x_ref[...] + _rms(y_ref[...].astype(F32), gp_ref[...])
    xo_ref[...] = xn
    ho_ref[...] = _rms(xn, gn_ref[...]).astype(ho_ref.dtype)


def _residual_kernel(x_ref, y_ref, gp_ref, xo_ref):
    xo_ref[...] = x_ref[...] + _rms(y_ref[...].astype(F32), gp_ref[...])


def residual_norm(x, y, g_post, g_next):
    t, d = x.shape
    row = pl.BlockSpec((TM_NORM, d), lambda i: (i, 0))
    vec = pl.BlockSpec((1, d), lambda i: (0, 0))
    if g_next is None:
        return pl.pallas_call(
            _residual_kernel,
            out_shape=jax.ShapeDtypeStruct((t, d), F32),
            grid=(t // TM_NORM,),
            in_specs=[row, row, vec],
            out_specs=row,
            compiler_params=_params(("parallel",), 40 << 20),
            name="residual",
        )(x, y, g_post.reshape(1, d)), None
    return pl.pallas_call(
        _residual_norm_kernel,
        out_shape=(jax.ShapeDtypeStruct((t, d), F32), jax.ShapeDtypeStruct((t, d), BF16)),
        grid=(t // TM_NORM,),
        in_specs=[row, row, vec, vec],
        out_specs=(row, row),
        compiler_params=_params(("parallel",), 40 << 20),
        name="residual_norm",
    )(x, y, g_post.reshape(1, d), g_next.reshape(1, d))


def _rope_cols(x, cos, sin):
    lane = lax.broadcasted_iota(jnp.int32, x.shape, 1)
    first_half = (lane % MLA_ROPE) < (MLA_ROPE // 2)
    partner = jnp.where(first_half, pltpu.roll(x, LANES - MLA_ROPE // 2, 1), pltpu.roll(x, MLA_ROPE // 2, 1))
    return x * cos + partner * sin


def _in_proj_main_kernel(h_ref, w_ref, o_ref):
    o_ref[...] = lax.dot_general(h_ref[...], w_ref[...], (((1,), (1,)), ((), ())),
                                 preferred_element_type=F32).astype(o_ref.dtype)


def _in_proj_rope_kernel(h_ref, w_ref, cos_ref, sin_ref, o_ref):
    j = pl.program_id(1)
    acc = lax.dot_general(h_ref[...], w_ref[...], (((1,), (1,)), ((), ())), preferred_element_type=F32)

    @pl.when(j == 0)
    def _():
        lo = COL_KR - COL_CKV
        o_ref[:, :lo] = acc[:, :lo].astype(o_ref.dtype)
        for c in range(lo // LANES, lo // LANES + 2):
            sl = slice(c * LANES, (c + 1) * LANES)
            o_ref[:, sl] = _rope_cols(acc[:, sl], cos_ref[...], sin_ref[...]).astype(o_ref.dtype)
        o_ref[:, lo + 2 * LANES:] = acc[:, lo + 2 * LANES:].astype(o_ref.dtype)

    @pl.when(j == 1)
    def _():
        q_cols = (RET_HEADS * RET_QK) // LANES
        for c in range(IN_TILE // LANES):
            sl = slice(c * LANES, (c + 1) * LANES)
            r = _rope_cols(acc[:, sl], cos_ref[...], sin_ref[...])
            if c < q_cols:
                r = r * (RET_QK ** -0.5)
            o_ref[:, sl] = r.astype(o_ref.dtype)


def in_proj(h, w_t, cos_t, sin_t, seq):
    t, d = h.shape
    pos_blocks = seq // TM
    tab = pl.BlockSpec((TM, LANES), lambda i, j: (i % pos_blocks, 0))
    rope_tile0 = ROPE_BASE // IN_TILE
    main = pl.pallas_call(
        _in_proj_main_kernel,
        out_shape=jax.ShapeDtypeStruct((t, ROPE_BASE), BF16),
        grid=(t // TM, rope_tile0),
        in_specs=[pl.BlockSpec((TM, d), lambda i, j: (i, 0)),
                  pl.BlockSpec((IN_TILE, d), lambda i, j: (j, 0))],
        out_specs=pl.BlockSpec((TM, IN_TILE), lambda i, j: (i, j)),
        compiler_params=_params(("parallel", "arbitrary"), 52 << 20),
        name="in_proj_main",
    )(h, w_t)
    rope = pl.pallas_call(
        _in_proj_rope_kernel,
        out_shape=jax.ShapeDtypeStruct((t, IN_WIDTH_PADDED - ROPE_BASE), BF16),
        grid=(t // TM, (IN_WIDTH_PADDED - ROPE_BASE) // IN_TILE),
        in_specs=[pl.BlockSpec((TM, d), lambda i, j: (i, 0)),
                  pl.BlockSpec((IN_TILE, d), lambda i, j: (rope_tile0 + j, 0)),
                  tab, tab],
        out_specs=pl.BlockSpec((TM, IN_TILE), lambda i, j: (i, j)),
        compiler_params=_params(("parallel", "arbitrary"), 52 << 20),
        name="in_proj_rope",
    )(h, w_t, cos_t, sin_t)
    return main, rope


def _sgu_kernel(u_ref, v_ref, g_ref, b_ref, w_ref, bias_ref, o_ref):
    row = lax.broadcasted_iota(jnp.int32, (SGU_BLOCK, SGU_BLOCK), 0)
    col = lax.broadcasted_iota(jnp.int32, (SGU_BLOCK, SGU_BLOCK), 1)
    keep = (col // CHUNK) <= (row // CHUNK)
    for g in range(SGU_HEADS):
        sl = slice(g * LANES, (g + 1) * LANES)
        wg = jnp.where(keep, w_ref[g], 0.0).astype(BF16)
        vg = v_ref[:, sl].astype(F32)
        mu = jnp.mean(vg, axis=-1, keepdims=True)
        vc = vg - mu
        y = vc * lax.rsqrt(jnp.mean(vc * vc, axis=-1, keepdims=True) + EPS) * g_ref[:, sl] + b_ref[:, sl]
        yb = y.astype(BF16)
        for n in range(TB_SGU // SGU_BLOCK):
            rs = slice(n * SGU_BLOCK, (n + 1) * SGU_BLOCK)
            mixed = jnp.dot(wg, yb[rs, :], preferred_element_type=F32) + bias_ref[:, sl]
            o_ref[rs, sl] = (u_ref[rs, sl].astype(F32) * mixed).astype(o_ref.dtype)


def sgu_mixer(proj, ln_g, ln_b, w_s, bias_full):
    t = proj.shape[0]
    vec = pl.BlockSpec((1, SGU_WIDTH), lambda i: (0, 0))
    return pl.pallas_call(
        _sgu_kernel,
        out_shape=jax.ShapeDtypeStruct((t, SGU_WIDTH), BF16),
        grid=(t // TB_SGU,),
        in_specs=[pl.BlockSpec((TB_SGU, SGU_WIDTH), lambda i: (i, COL_U // SGU_WIDTH)),
                  pl.BlockSpec((TB_SGU, SGU_WIDTH), lambda i: (i, COL_V // SGU_WIDTH)),
                  vec, vec,
                  pl.BlockSpec((SGU_HEADS, SGU_BLOCK, SGU_BLOCK), lambda i: (0, 0, 0)),
                  pl.BlockSpec((SGU_BLOCK, SGU_WIDTH), lambda i: (0, 0))],
        out_specs=pl.BlockSpec((TB_SGU, SGU_WIDTH), lambda i: (i, 0)),
        compiler_params=_params(("parallel",), 32 << 20),
        name="sgu_mixer",
    )(proj, proj, ln_g.reshape(1, -1), ln_b.reshape(1, -1), w_s, bias_full)


Q_SCALE = (MLA_NOPE + MLA_ROPE) ** -0.5 * math.log2(math.e)
Q_ROPE_TILE = (MLA_HEADS * MLA_NOPE) // IN_TILE


def _q_proj_kernel(c_ref, g_ref, w_ref, cos_ref, sin_ref, o_ref, cn_ref):
    j = pl.program_id(1)

    @pl.when(j == 0)
    def _():
        cn_ref[...] = _rms(c_ref[...].astype(F32), g_ref[...]).astype(cn_ref.dtype)

    acc = jnp.dot(cn_ref[...], w_ref[...], preferred_element_type=F32) * Q_SCALE

    @pl.when(j < Q_ROPE_TILE)
    def _():
        o_ref[...] = acc.astype(o_ref.dtype)

    @pl.when(j >= Q_ROPE_TILE)
    def _():
        for c in range(IN_TILE // LANES):
            sl = slice(c * LANES, (c + 1) * LANES)
            o_ref[:, sl] = _rope_cols(acc[:, sl], cos_ref[...], sin_ref[...]).astype(o_ref.dtype)


def q_proj(proj, g, w, cos_t, sin_t, seq):
    t = proj.shape[0]
    n = w.shape[1]
    pos_blocks = seq // TM
    tab = pl.BlockSpec((TM, LANES), lambda i, j: (i % pos_blocks, 0))
    return pl.pallas_call(
        _q_proj_kernel,
        out_shape=jax.ShapeDtypeStruct((t, n), BF16),
        grid=(t // TM, n // IN_TILE),
        in_specs=[pl.BlockSpec((TM, MLA_Q_RANK), lambda i, j: (i, COL_CQ // MLA_Q_RANK)),
                  pl.BlockSpec((1, MLA_Q_RANK), lambda i, j: (0, 0)),
                  pl.BlockSpec((MLA_Q_RANK, IN_TILE), lambda i, j: (0, j)),
                  tab, tab],
        out_specs=pl.BlockSpec((TM, IN_TILE), lambda i, j: (i, j)),
        scratch_shapes=[pltpu.VMEM((TM, MLA_Q_RANK), BF16)],
        compiler_params=_params(("parallel", "arbitrary"), 40 << 20),
        name="q_proj",
    )(proj, g.reshape(1, -1), w, cos_t, sin_t)


def _kv_proj_kernel(c_ref, g_ref, w_ref, o_ref, cn_ref):
    @pl.when(pl.program_id(1) == 0)
    def _():
        cn_ref[...] = _rms(c_ref[...].astype(F32), g_ref[...]).astype(cn_ref.dtype)

    o_ref[...] = jnp.dot(cn_ref[...], w_ref[...].astype(BF16), preferred_element_type=F32).astype(o_ref.dtype)


def kv_proj(proj, g, w, layer):
    t = proj.shape[0]
    n = w.shape[2]
    return pl.pallas_call(
        _kv_proj_kernel,
        out_shape=jax.ShapeDtypeStruct((t, n), BF16),
        grid=(t // TM, n // IN_TILE),
        in_specs=[pl.BlockSpec((TM, MLA_KV_RANK), lambda i, j: (i, (COL_CKV - ROPE_BASE) // MLA_KV_RANK)),
                  pl.BlockSpec((1, MLA_KV_RANK), lambda i, j: (0, 0)),
                  pl.BlockSpec((None, MLA_KV_RANK, IN_TILE), lambda i, j: (layer, 0, j))],
        out_specs=pl.BlockSpec((TM, IN_TILE), lambda i, j: (i, j)),
        scratch_shapes=[pltpu.VMEM((TM, MLA_KV_RANK), BF16)],
        compiler_params=_params(("parallel", "arbitrary"), 32 << 20),
        name="kv_proj",
    )(proj, g.reshape(1, -1), w)


ATTN_GROUP = 4


HEAD_W = MLA_NOPE + MLA_V


def _attn_tiles(qs, kv_ref, kr_ref, base, ntiles, mask_last, states):
    states = list(states)
    ones = jnp.ones((TQ, LANES), BF16)
    for t in range(ntiles):
        st = pl.multiple_of(base + t * TQ, TQ)
        for hd in range(2):
            m, l, acc = states[hd]
            kn = kv_ref[pl.ds(st, TQ), hd * HEAD_W:hd * HEAD_W + MLA_NOPE]
            v = kv_ref[pl.ds(st, TQ), hd * HEAD_W + MLA_NOPE:(hd + 1) * HEAD_W]
            k = jnp.concatenate([kn, kr_ref[pl.ds(st, TQ), hd * LANES:(hd + 1) * LANES]], axis=1)
            s = lax.dot_general(qs[hd], k, (((1,), (1,)), ((), ())), preferred_element_type=F32)
            if mask_last and t == ntiles - 1:
                row = lax.broadcasted_iota(jnp.int32, (TQ, TQ), 0)
                col = lax.broadcasted_iota(jnp.int32, (TQ, TQ), 1)
                s = jnp.where((col // CHUNK) <= (row // CHUNK), s, -jnp.inf)
            m_next = jnp.maximum(m, jnp.max(s, axis=1, keepdims=True))
            alpha = jnp.exp2(m - m_next)
            p = jnp.exp2(s - jnp.tile(m_next, (1, TQ // LANES))).astype(BF16)
            pv = jnp.dot(p, jnp.concatenate([v, ones], axis=1), preferred_element_type=F32)
            states[hd] = (m_next, alpha * l + pv[:, LANES:], alpha * acc + pv[:, :LANES])
    return states


def _attn_kernel(qn_ref, qr_ref, kv_ref, kr_ref, *rest, cast_from, n_cast):
    w_refs, o_ref = rest[:n_cast], rest[n_cast]
    wb_refs = rest[n_cast + 1:2 * n_cast + 1]
    m_ref, l_ref, acc_ref = rest[2 * n_cast + 1:]
    qi = pl.program_id(2)

    @pl.when(qi >= cast_from)
    def _():
        for w_ref, wb_ref in zip(w_refs, wb_refs):
            wb_ref[...] = w_ref[...].astype(wb_ref.dtype)

    qr = qr_ref[...]
    qs = [jnp.concatenate([qn_ref[:, hd * LANES:(hd + 1) * LANES], qr], axis=1) for hd in range(2)]

    m_ref[...] = jnp.full_like(m_ref, -jnp.inf)
    l_ref[...] = jnp.zeros_like(l_ref)
    acc_ref[...] = jnp.zeros_like(acc_ref)

    def load_states():
        return [(m_ref[hd], l_ref[hd], acc_ref[hd]) for hd in range(2)]

    def body(g, carry):
        base = pl.multiple_of(g * (ATTN_GROUP * TQ), ATTN_GROUP * TQ)
        states = _attn_tiles(qs, kv_ref, kr_ref, base, ATTN_GROUP, False, load_states())
        for hd in range(2):
            m_ref[hd], l_ref[hd], acc_ref[hd] = states[hd]
        return carry

    lax.fori_loop(0, qi // ATTN_GROUP, body, 0)

    base = pl.multiple_of((qi // ATTN_GROUP) * (ATTN_GROUP * TQ), ATTN_GROUP * TQ)
    for r in range(ATTN_GROUP):
        @pl.when(qi % ATTN_GROUP == r)
        def _():
            states = _attn_tiles(qs, kv_ref, kr_ref, base, r + 1, True, load_states())
            for hd in range(2):
                _, l, acc = states[hd]
                o_ref[:, hd * LANES:(hd + 1) * LANES] = (acc / l).astype(o_ref.dtype)


def mla_attention(q, kv, proj_rope, weights, layer, batch, seq):
    t = q.shape[0]
    nq = seq // TQ
    pairs = MLA_HEADS // 2
    cast_from = nq // 2
    casts = nq - cast_from
    steps = batch * pairs * casts
    slab_rows = [w.shape[1] // steps for w in weights]
    assert all(r * steps == w.shape[1] and r % 16 == 0 for r, w in zip(slab_rows, weights))

    def slab(b, p, i):
        return (b * pairs + p) * casts + jnp.maximum(i - cast_from, 0)
    pair_w = 2 * LANES
    kr_pair_block = (COL_KR - ROPE_BASE) // pair_w
    rope_block0 = (MLA_HEADS * MLA_NOPE) // LANES
    return pl.pallas_call(
        functools.partial(_attn_kernel, cast_from=cast_from, n_cast=len(weights)),
        out_shape=(jax.ShapeDtypeStruct((t, MLA_WIDTH), BF16),
                   *[jax.ShapeDtypeStruct(w.shape[1:], BF16) for w in weights]),
        grid=(batch, pairs, nq),
        in_specs=[pl.BlockSpec((TQ, pair_w), lambda b, p, i: (b * nq + i, p)),
                  pl.BlockSpec((TQ, LANES), lambda b, p, i: (b * nq + i, rope_block0 + p)),
                  pl.BlockSpec((seq, 2 * HEAD_W), lambda b, p, i: (b, p)),
                  pl.BlockSpec((seq, pair_w), lambda b, p, i: (b, kr_pair_block)),
                  *[pl.BlockSpec((None, r, w.shape[2]), lambda b, p, i: (layer, slab(b, p, i), 0))
                    for r, w in zip(slab_rows, weights)]],
        out_specs=(pl.BlockSpec((TQ, pair_w), lambda b, p, i: (b * nq + i, p)),
                   *[pl.BlockSpec((r, w.shape[2]), lambda b, p, i: (slab(b, p, i), 0))
                     for r, w in zip(slab_rows, weights)]),
        scratch_shapes=[pltpu.VMEM((2, TQ, LANES), F32), pltpu.VMEM((2, TQ, LANES), F32),
                        pltpu.VMEM((2, TQ, LANES), F32)],
        compiler_params=_params(("parallel", "parallel", "arbitrary"), 48 << 20),
        name="mla_attention",
    )(q, q, kv, proj_rope, *weights)


def _ret_kernel(q_ref, k_ref, v_ref, g_ref, d_ref, qd_ref, kd_ref, cd_ref, o_ref, s_ref):
    @pl.when(pl.program_id(1) == 0)
    def _():
        s_ref[...] = jnp.zeros_like(s_ref)

    lane = lax.broadcasted_iota(jnp.int32, (RET_L, LANES), 1)
    for p in range(RET_HEADS // 2):
        ls = slice(p * LANES, (p + 1) * LANES)
        q = q_ref[:, ls]
        k = k_ref[:, ls]
        qf = q.astype(F32)
        kdec = (k.astype(F32) * kd_ref[:, ls]).astype(BF16)
        for hh in range(2):
            head = 2 * p + hh
            own = (lane // RET_QK) == hh
            vs = slice(head * RET_V, (head + 1) * RET_V)
            v = v_ref[:, vs]
            qm = jnp.where(own, q, jnp.zeros_like(q))
            s = lax.dot_general(qm, k, (((1,), (1,)), ((), ())), preferred_element_type=F32)
            sd = (s * d_ref[head]).astype(BF16)
            qx = jnp.where(own, qf * qd_ref[:, ls], 0.0).astype(BF16)
            state = s_ref[head]
            o = (jnp.dot(sd, v, preferred_element_type=F32)
                 + jnp.dot(qx, state.astype(BF16), preferred_element_type=F32))
            s_ref[head] = cd_ref[head] * state + lax.dot_general(
                kdec, v, (((0,), (0,)), ((), ())), preferred_element_type=F32)
            o = o * lax.rsqrt(jnp.mean(o * o, axis=-1, keepdims=True) + EPS)
            gate = g_ref[:, vs].astype(F32)
            o_ref[:, vs] = (gate * jax.nn.sigmoid(gate) * o).astype(o_ref.dtype)


def retention_mixer(proj, proj_rope, d_tab, qd_tab, kd_tab, cd_tab, batch, seq):
    t = proj.shape[0]
    nsc = seq // RET_L
    qk_w = RET_HEADS * RET_QK
    full = lambda b, c: (0, 0)
    full3 = lambda b, c: (0, 0, 0)
    return pl.pallas_call(
        _ret_kernel,
        out_shape=jax.ShapeDtypeStruct((t, RET_WIDTH), BF16),
        grid=(batch, nsc),
        in_specs=[pl.BlockSpec((RET_L, qk_w), lambda b, c: (b * nsc + c, (COL_RQ - ROPE_BASE) // qk_w)),
                  pl.BlockSpec((RET_L, qk_w), lambda b, c: (b * nsc + c, (COL_RK - ROPE_BASE) // qk_w)),
                  pl.BlockSpec((RET_L, RET_WIDTH), lambda b, c: (b * nsc + c, COL_RV // RET_WIDTH)),
                  pl.BlockSpec((RET_L, RET_WIDTH), lambda b, c: (b * nsc + c, COL_RG // RET_WIDTH)),
                  pl.BlockSpec((RET_HEADS, RET_L, RET_L), full3),
                  pl.BlockSpec((RET_L, qk_w), full),
                  pl.BlockSpec((RET_L, qk_w), full),
                  pl.BlockSpec((RET_HEADS, LANES, RET_V), full3)],
        out_specs=pl.BlockSpec((RET_L, RET_WIDTH), lambda b, c: (b * nsc + c, 0)),
        scratch_shapes=[pltpu.VMEM((RET_HEADS, LANES, RET_V), F32)],
        compiler_params=_params(("parallel", "arbitrary"), 32 << 20),
        name="retention_mixer",
    )(proj_rope, proj_rope, proj, proj, d_tab, qd_tab, kd_tab, cd_tab)


def retention_tables():
    log_gamma = jnp.log1p(-jnp.exp2(-5.0 - jnp.arange(RET_HEADS, dtype=F32)))
    idx = jnp.arange(RET_L, dtype=F32)
    dist = jnp.abs(idx[:, None] - idx[None, :])
    chunk = jnp.arange(RET_L) // CHUNK
    visible = chunk[None, :] <= chunk[:, None]
    d_tab = jnp.where(visible[None], jnp.exp(log_gamma[:, None, None] * dist[None]), 0.0)
    lg_lane = jnp.repeat(log_gamma, RET_QK)
    qd_tab = jnp.exp(lg_lane[None, :] * (idx[:, None] + 1.0))
    kd_tab = jnp.exp(lg_lane[None, :] * (RET_L - 1.0 - idx[:, None]))
    cd = jnp.exp(log_gamma * RET_L)
    cd_tab = jnp.broadcast_to(cd[:, None, None], (RET_HEADS, LANES, RET_V))
    return d_tab, qd_tab, kd_tab, cd_tab


def _out_proj_kernel(a_ref, b_ref, c_ref, w_ref, o_ref):
    ka = a_ref.shape[1]
    kb = b_ref.shape[1]
    acc = jnp.dot(a_ref[...], w_ref[:ka, :], preferred_element_type=F32)
    acc = acc + jnp.dot(b_ref[...], w_ref[ka:ka + kb, :], preferred_element_type=F32)
    acc = acc + jnp.dot(c_ref[...], w_ref[ka + kb:, :], preferred_element_type=F32)
    o_ref[...] = acc.astype(o_ref.dtype)


def out_proj(a, b, c, w):
    t = a.shape[0]
    k, n = w.shape
    return pl.pallas_call(
        _out_proj_kernel,
        out_shape=jax.ShapeDtypeStruct((t, n), BF16),
        grid=(t // TM, n // TN_OUT),
        in_specs=[pl.BlockSpec((TM, a.shape[1]), lambda i, j: (i, 0)),
                  pl.BlockSpec((TM, b.shape[1]), lambda i, j: (i, 0)),
                  pl.BlockSpec((TM, c.shape[1]), lambda i, j: (i, 0)),
                  pl.BlockSpec((k, TN_OUT), lambda i, j: (0, j))],
        out_specs=pl.BlockSpec((TM, TN_OUT), lambda i, j: (i, j)),
        compiler_params=_params(("parallel", "arbitrary"), 56 << 20),
        name="out_proj",
    )(a, b, c, w)


def _ffn_kernel(h_ref, wu_ref, wd_ref, o_ref):
    def contribution():
        a = jnp.dot(h_ref[...], wu_ref[...], preferred_element_type=F32)
        a = jnp.square(jnp.maximum(a, 0.0)).astype(BF16)
        return jnp.dot(a, wd_ref[...], preferred_element_type=F32)

    @pl.when(pl.program_id(1) == 0)
    def _():
        o_ref[...] = contribution()

    @pl.when(pl.program_id(1) > 0)
    def _():
        o_ref[...] += contribution()


def ffn(h, wu_b, wd_b):
    t, d = h.shape
    ff = wu_b.shape[1]
    return pl.pallas_call(
        _ffn_kernel,
        out_shape=jax.ShapeDtypeStruct((t, d), F32),
        grid=(t // TM_FFN, ff // TF_FFN),
        in_specs=[pl.BlockSpec((TM_FFN, d), lambda i, j: (i, 0)),
                  pl.BlockSpec((d, TF_FFN), lambda i, j: (0, j)),
                  pl.BlockSpec((TF_FFN, d), lambda i, j: (j, 0))],
        out_specs=pl.BlockSpec((TM_FFN, d), lambda i, j: (i, 0)),
        compiler_params=_params(("parallel", "arbitrary"), 56 << 20),
        name="ffn",
    )(h, wu_b, wd_b)


_SRC_SIZES = (SGU_WIDTH, SGU_WIDTH, MLA_Q_RANK, MLA_KV_RANK, MLA_ROPE,
              RET_HEADS * RET_QK, RET_HEADS * RET_QK, RET_WIDTH, RET_WIDTH)
(_SRC_U, _SRC_V, _SRC_CQ, _SRC_CKV, _SRC_KR, _SRC_RQ, _SRC_RK, _SRC_RV, _SRC_RG,
 IN_WIDTH) = [int(o) for o in np.concatenate([[0], np.cumsum(_SRC_SIZES)])]
_W_IN_MOVES = (
    (COL_U, _SRC_U, 3 * IN_TILE),
    (COL_RV, _SRC_RV, RET_WIDTH),
    (COL_RG, _SRC_RG, RET_WIDTH),
    (COL_CKV, _SRC_CKV, MLA_KV_RANK),
    (COL_KR, _SRC_KR, MLA_ROPE),
    (COL_KR + LANES + MLA_ROPE, _SRC_KR, MLA_ROPE),
    (COL_RQ, _SRC_RQ, RET_HEADS * RET_QK),
    (COL_RK, _SRC_RK, RET_HEADS * RET_QK),
)


def _w_in_prep_kernel(w_ref, o_ref):
    o_ref[COL_KR:COL_RQ, :] = jnp.zeros((COL_RQ - COL_KR, o_ref.shape[1]), o_ref.dtype)
    for dst, src, width in _W_IN_MOVES:
        o_ref[dst:dst + width, :] = w_ref[src:src + width, :].astype(o_ref.dtype)


def prep_w_in(w_t, layer):
    _, n, d = w_t.shape
    return pl.pallas_call(
        _w_in_prep_kernel,
        out_shape=jax.ShapeDtypeStruct((IN_WIDTH_PADDED, d), BF16),
        grid=(d // TK_PREP,),
        in_specs=[pl.BlockSpec((None, n, TK_PREP), lambda i: (layer, 0, i))],
        out_specs=pl.BlockSpec((IN_WIDTH_PADDED, TK_PREP), lambda i: (0, i)),
        compiler_params=_params(("parallel",), 40 << 20),
        name="prep_w_in",
    )(w_t)


def _prep_wq(w):
    k = w.shape[0]
    w3 = w.reshape(k, MLA_HEADS, MLA_NOPE + MLA_ROPE)
    nope = w3[:, :, :MLA_NOPE].reshape(k, MLA_HEADS * MLA_NOPE)
    rope = w3[:, :, MLA_NOPE:].reshape(k, MLA_HEADS * MLA_ROPE)
    return jnp.concatenate([nope, rope], axis=1).astype(BF16)


def _rope_lane_tables(seq):
    inv_freq = 1.0 / (ROPE_THETA ** (jnp.arange(0, MLA_ROPE, 2, dtype=F32) / MLA_ROPE))
    ang = jnp.arange(seq, dtype=F32)[:, None] * inv_freq[None, :]
    cos, sin = jnp.cos(ang), jnp.sin(ang)
    cos_t = jnp.concatenate([cos, cos, cos, cos], axis=1)
    sin_t = jnp.concatenate([-sin, sin, -sin, sin], axis=1)
    return cos_t, sin_t


def kernel(x, norm_mix_pre, norm_mix_post, norm_ffn_pre, norm_ffn_post, w_in, sgu_ln_g, sgu_ln_b, sgu_w, sgu_b, mla_q_norm, mla_wq_b, mla_kv_norm, mla_wkv_b, w_out, w_up, w_down):
    batch, seq, d = x.shape
    depth = w_in.shape[0]
    cos_t, sin_t = _rope_lane_tables(seq)
    d_tab, qd_tab, kd_tab, cd_tab = retention_tables()
    w_in_t = jnp.swapaxes(w_in, 1, 2)

    xf = x.reshape(batch * seq, d)
    h = rmsnorm_rows(xf, norm_mix_pre[0])
    for l in range(depth):
        proj, proj_rope = in_proj(h, prep_w_in(w_in_t, l), cos_t, sin_t, seq)
        bias_full = jnp.repeat(sgu_b[l].T, SGU_WIDTH // SGU_HEADS, axis=1)
        out_a = sgu_mixer(proj, sgu_ln_g[l], sgu_ln_b[l], sgu_w[l], bias_full)
        q = q_proj(proj, mla_q_norm[l], _prep_wq(mla_wq_b[l]), cos_t, sin_t, seq)
        kv = kv_proj(proj_rope, mla_kv_norm[l], mla_wkv_b, l)
        out_b, wu_b, wd_b, wo_b = mla_attention(q, kv, proj_rope, (w_up, w_down, w_out), l, batch, seq)
        out_c = retention_mixer(proj, proj_rope, d_tab, qd_tab, kd_tab, cd_tab, batch, seq)
        mixed = out_proj(out_a, out_b, out_c, wo_b)
        xf, h2 = residual_norm(xf, mixed, norm_mix_post[l], norm_ffn_pre[l])
        f = ffn(h2, wu_b, wd_b)
        g_next = norm_mix_pre[l + 1] if l + 1 < depth else None
        xf, h = residual_norm(xf, f, norm_ffn_post[l], g_next)
    return xf.reshape(batch, seq, d)
```

```python
import functools
import math

import numpy as np
import jax
import jax.numpy as jnp
from jax import lax
from jax.experimental import pallas as pl
from jax.experimental.pallas import tpu as pltpu

F32 = jnp.float32
BF16 = jnp.bfloat16

D_MODEL = 4096
CHUNK = 64
EPS = 1e-6
ROPE_THETA = 10000.0
SGU_BLOCK = 128
SGU_WIDTH = 1024
SGU_HEADS = 8
MLA_HEADS = 16
MLA_NOPE = 128
MLA_ROPE = 64
MLA_V = 128
MLA_Q_RANK = 1024
MLA_KV_RANK = 512
MLA_WIDTH = MLA_HEADS * MLA_V
RET_HEADS = 8
RET_QK = 64
RET_V = 128
RET_WIDTH = RET_HEADS * RET_V
FF_DIM = 4 * D_MODEL

LANES = 128
VMEM_CAP = 64 * 1024 * 1024

IN_TILE = 1024
COL_U = 0 * IN_TILE
COL_V = 1 * IN_TILE
COL_CQ = 2 * IN_TILE
COL_RV = 3 * IN_TILE
COL_RG = 4 * IN_TILE
COL_CKV = 5 * IN_TILE
COL_KR = COL_CKV + MLA_KV_RANK
COL_RQ = 6 * IN_TILE
COL_RK = COL_RQ + RET_HEADS * RET_QK
IN_WIDTH_PADDED = 7 * IN_TILE
ROPE_BASE = COL_CKV

TM = 1024
TM_NORM = 256
TM_FFN = 512
TF_FFN = 1024
TN_OUT = 1024
TK_PREP = 256
TB_SGU = 512
TQ = 512
RET_L = 256


def _vmem_limit(nbytes):
    return int(min(VMEM_CAP - (2 << 20), nbytes))


def _params(sem, nbytes):
    return pltpu.CompilerParams(dimension_semantics=sem, vmem_limit_bytes=_vmem_limit(nbytes))


def _rms(x, g):
    return x * lax.rsqrt(jnp.mean(x * x, axis=-1, keepdims=True) + EPS) * g


def _rmsnorm_kernel(x_ref, g_ref, o_ref):
    o_ref[...] = _rms(x_ref[...], g_ref[...]).astype(o_ref.dtype)


def rmsnorm_rows(x, g):
    t, d = x.shape
    return pl.pallas_call(
        _rmsnorm_kernel,
        out_shape=jax.ShapeDtypeStruct((t, d), BF16),
        grid=(t // TM_NORM,),
        in_specs=[pl.BlockSpec((TM_NORM, d), lambda i: (i, 0)),
                  pl.BlockSpec((1, d), lambda i: (0, 0))],
        out_specs=pl.BlockSpec((TM_NORM, d), lambda i: (i, 0)),
        compiler_params=_params(("parallel",), 32 << 20),
        name="rmsnorm_rows",
    )(x, g.reshape(1, d))


def _residual_norm_kernel(x_ref, y_ref, gp_ref, gn_ref, xo_ref, ho_ref):
    xn = x_ref[...] + _rms(y_ref[...].astype(F32), gp_ref[...])
    xo_ref[...] = xn
    ho_ref[...] = _rms(xn, gn_ref[...]).astype(ho_ref.dtype)


def _residual_kernel(x_ref, y_ref, gp_ref, xo_ref):
    xo_ref[...] = x_ref[...] + _rms(y_ref[...].astype(F32), gp_ref[...])


def residual_norm(x, y, g_post, g_next):
    t, d = x.shape
    row = pl.BlockSpec((TM_NORM, d), lambda i: (i, 0))
    vec = pl.BlockSpec((1, d), lambda i: (0, 0))
    if g_next is None:
        return pl.pallas_call(
            _residual_kernel,
            out_shape=jax.ShapeDtypeStruct((t, d), F32),
            grid=(t // TM_NORM,),
            in_specs=[row, row, vec],
            out_specs=row,
            compiler_params=_params(("parallel",), 40 << 20),
            name="residual",
        )(x, y, g_post.reshape(1, d)), None
    return pl.pallas_call(
        _residual_norm_kernel,
        out_shape=(jax.ShapeDtypeStruct((t, d), F32), jax.ShapeDtypeStruct((t, d), BF16)),
        grid=(t // TM_NORM,),
        in_specs=[row, row, vec, vec],
        out_specs=(row, row),
        compiler_params=_params(("parallel",), 40 << 20),
        name="residual_norm",
    )(x, y, g_post.reshape(1, d), g_next.reshape(1, d))


def _rope_cols(x, cos, sin):
    lane = lax.broadcasted_iota(jnp.int32, x.shape, 1)
    first_half = (lane % MLA_ROPE) < (MLA_ROPE // 2)
    partner = jnp.where(first_half, pltpu.roll(x, LANES - MLA_ROPE // 2, 1), pltpu.roll(x, MLA_ROPE // 2, 1))
    return x * cos + partner * sin


def _in_proj_main_kernel(h_ref, w_ref, o_ref):
    o_ref[...] = lax.dot_general(h_ref[...], w_ref[...], (((1,), (1,)), ((), ())),
                                 preferred_element_type=F32).astype(o_ref.dtype)


def _in_proj_rope_kernel(h_ref, w_ref, cos_ref, sin_ref, o_ref):
    j = pl.program_id(1)
    acc = lax.dot_general(h_ref[...], w_ref[...], (((1,), (1,)), ((), ())), preferred_element_type=F32)

    @pl.when(j == 0)
    def _():
        lo = COL_KR - COL_CKV
        o_ref[:, :lo] = acc[:, :lo].astype(o_ref.dtype)
        for c in range(lo // LANES, lo // LANES + 2):
            sl = slice(c * LANES, (c + 1) * LANES)
            o_ref[:, sl] = _rope_cols(acc[:, sl], cos_ref[...], sin_ref[...]).astype(o_ref.dtype)
        o_ref[:, lo + 2 * LANES:] = acc[:, lo + 2 * LANES:].astype(o_ref.dtype)

    @pl.when(j == 1)
    def _():
        q_cols = (RET_HEADS * RET_QK) // LANES
        for c in range(IN_TILE // LANES):
            sl = slice(c * LANES, (c + 1) * LANES)
            r = _rope_cols(acc[:, sl], cos_ref[...], sin_ref[...])
            if c < q_cols:
                r = r * (RET_QK ** -0.5)
            o_ref[:, sl] = r.astype(o_ref.dtype)


def in_proj(h, w_t, cos_t, sin_t, seq):
    t, d = h.shape
    pos_blocks = seq // TM
    tab = pl.BlockSpec((TM, LANES), lambda i, j: (i % pos_blocks, 0))
    rope_tile0 = ROPE_BASE // IN_TILE
    main = pl.pallas_call(
        _in_proj_main_kernel,
        out_shape=jax.ShapeDtypeStruct((t, ROPE_BASE), BF16),
        grid=(t // TM, rope_tile0),
        in_specs=[pl.BlockSpec((TM, d), lambda i, j: (i, 0)),
                  pl.BlockSpec((IN_TILE, d), lambda i, j: (j, 0))],
        out_specs=pl.BlockSpec((TM, IN_TILE), lambda i, j: (i, j)),
        compiler_params=_params(("parallel", "arbitrary"), 52 << 20),
        name="in_proj_main",
    )(h, w_t)
    rope = pl.pallas_call(
        _in_proj_rope_kernel,
        out_shape=jax.ShapeDtypeStruct((t, IN_WIDTH_PADDED - ROPE_BASE), BF16),
        grid=(t // TM, (IN_WIDTH_PADDED - ROPE_BASE) // IN_TILE),
        in_specs=[pl.BlockSpec((TM, d), lambda i, j: (i, 0)),
                  pl.BlockSpec((IN_TILE, d), lambda i, j: (rope_tile0 + j, 0)),
                  tab, tab],
        out_specs=pl.BlockSpec((TM, IN_TILE), lambda i, j: (i, j)),
        compiler_params=_params(("parallel", "arbitrary"), 52 << 20),
        name="in_proj_rope",
    )(h, w_t, cos_t, sin_t)
    return main, rope


def _sgu_kernel(u_ref, v_ref, g_ref, b_ref, w_ref, bias_ref, o_ref):
    row = lax.broadcasted_iota(jnp.int32, (SGU_BLOCK, SGU_BLOCK), 0)
    col = lax.broadcasted_iota(jnp.int32, (SGU_BLOCK, SGU_BLOCK), 1)
    keep = (col // CHUNK) <= (row // CHUNK)
    for g in range(SGU_HEADS):
        sl = slice(g * LANES, (g + 1) * LANES)
        wg = jnp.where(keep, w_ref[g], 0.0).astype(BF16)
        vg = v_ref[:, sl].astype(F32)
        mu = jnp.mean(vg, axis=-1, keepdims=True)
        vc = vg - mu
        y = vc * lax.rsqrt(jnp.mean(vc * vc, axis=-1, keepdims=True) + EPS) * g_ref[:, sl] + b_ref[:, sl]
        yb = y.astype(BF16)
        for n in range(TB_SGU // SGU_BLOCK):
            rs = slice(n * SGU_BLOCK, (n + 1) * SGU_BLOCK)
            mixed = jnp.dot(wg, yb[rs, :], preferred_element_type=F32) + bias_ref[:, sl]
            o_ref[rs, sl] = (u_ref[rs, sl].astype(F32) * mixed).astype(o_ref.dtype)


def sgu_mixer(proj, ln_g, ln_b, w_s, bias_full):
    t = proj.shape[0]
    vec = pl.BlockSpec((1, SGU_WIDTH), lambda i: (0, 0))
    return pl.pallas_call(
        _sgu_kernel,
        out_shape=jax.ShapeDtypeStruct((t, SGU_WIDTH), BF16),
        grid=(t // TB_SGU,),
        in_specs=[pl.BlockSpec((TB_SGU, SGU_WIDTH), lambda i: (i, COL_U // SGU_WIDTH)),
                  pl.BlockSpec((TB_SGU, SGU_WIDTH), lambda i: (i, COL_V // SGU_WIDTH)),
                  vec, vec,
                  pl.BlockSpec((SGU_HEADS, SGU_BLOCK, SGU_BLOCK), lambda i: (0, 0, 0)),
                  pl.BlockSpec((SGU_BLOCK, SGU_WIDTH), lambda i: (0, 0))],
        out_specs=pl.BlockSpec((TB_SGU, SGU_WIDTH), lambda i: (i, 0)),
        compiler_params=_params(("parallel",), 32 << 20),
        name="sgu_mixer",
    )(proj, proj, ln_g.reshape(1, -1), ln_b.reshape(1, -1), w_s, bias_full)


Q_SCALE = (MLA_NOPE + MLA_ROPE) ** -0.5 * math.log2(math.e)
Q_ROPE_TILE = (MLA_HEADS * MLA_NOPE) // IN_TILE


def _q_proj_kernel(c_ref, g_ref, w_ref, cos_ref, sin_ref, o_ref, cn_ref):
    j = pl.program_id(1)

    @pl.when(j == 0)
    def _():
        cn_ref[...] = _rms(c_ref[...].astype(F32), g_ref[...]).astype(cn_ref.dtype)

    acc = jnp.dot(cn_ref[...], w_ref[...], preferred_element_type=F32) * Q_SCALE

    @pl.when(j < Q_ROPE_TILE)
    def _():
        o_ref[...] = acc.astype(o_ref.dtype)

    @pl.when(j >= Q_ROPE_TILE)
    def _():
        for c in range(IN_TILE // LANES):
            sl = slice(c * LANES, (c + 1) * LANES)
            o_ref[:, sl] = _rope_cols(acc[:, sl], cos_ref[...], sin_ref[...]).astype(o_ref.dtype)


def q_proj(proj, g, w, cos_t, sin_t, seq):
    t = proj.shape[0]
    n = w.shape[1]
    pos_blocks = seq // TM
    tab = pl.BlockSpec((TM, LANES), lambda i, j: (i % pos_blocks, 0))
    return pl.pallas_call(
        _q_proj_kernel,
        out_shape=jax.ShapeDtypeStruct((t, n), BF16),
        grid=(t // TM, n // IN_TILE),
        in_specs=[pl.BlockSpec((TM, MLA_Q_RANK), lambda i, j: (i, COL_CQ // MLA_Q_RANK)),
                  pl.BlockSpec((1, MLA_Q_RANK), lambda i, j: (0, 0)),
                  pl.BlockSpec((MLA_Q_RANK, IN_TILE), lambda i, j: (0, j)),
                  tab, tab],
        out_specs=pl.BlockSpec((TM, IN_TILE), lambda i, j: (i, j)),
        scratch_shapes=[pltpu.VMEM((TM, MLA_Q_RANK), BF16)],
        compiler_params=_params(("parallel", "arbitrary"), 40 << 20),
        name="q_proj",
    )(proj, g.reshape(1, -1), w, cos_t, sin_t)


def _kv_proj_kernel(c_ref, g_ref, w_ref, o_ref, cn_ref):
    @pl.when(pl.program_id(1) == 0)
    def _():
        cn_ref[...] = _rms(c_ref[...].astype(F32), g_ref[...]).astype(cn_ref.dtype)

    o_ref[...] = jnp.dot(cn_ref[...], w_ref[...].astype(BF16), preferred_element_type=F32).astype(o_ref.dtype)


def kv_proj(proj, g, w, layer):
    t = proj.shape[0]
    n = w.shape[2]
    return pl.pallas_call(
        _kv_proj_kernel,
        out_shape=jax.ShapeDtypeStruct((t, n), BF16),
        grid=(t // TM, n // IN_TILE),
        in_specs=[pl.BlockSpec((TM, MLA_KV_RANK), lambda i, j: (i, (COL_CKV - ROPE_BASE) // MLA_KV_RANK)),
                  pl.BlockSpec((1, MLA_KV_RANK), lambda i, j: (0, 0)),
                  pl.BlockSpec((None, MLA_KV_RANK, IN_TILE), lambda i, j: (layer, 0, j))],
        out_specs=pl.BlockSpec((TM, IN_TILE), lambda i, j: (i, j)),
        scratch_shapes=[pltpu.VMEM((TM, MLA_KV_RANK), BF16)],
        compiler_params=_params(("parallel", "arbitrary"), 32 << 20),
        name="kv_proj",
    )(proj, g.reshape(1, -1), w)


ATTN_GROUP = 4


HEAD_W = MLA_NOPE + MLA_V


def _attn_tiles(qs, kv_ref, kr_ref, base, ntiles, mask_last, states):
    states = list(states)
    ones = jnp.ones((TQ, LANES), BF16)
    for t in range(ntiles):
        st = pl.multiple_of(base + t * TQ, TQ)
        for hd in range(2):
            m, l, acc = states[hd]
            kn = kv_ref[pl.ds(st, TQ), hd * HEAD_W:hd * HEAD_W + MLA_NOPE]
            v = kv_ref[pl.ds(st, TQ), hd * HEAD_W + MLA_NOPE:(hd + 1) * HEAD_W]
            k = jnp.concatenate([kn, kr_ref[pl.ds(st, TQ), hd * LANES:(hd + 1) * LANES]], axis=1)
            s = lax.dot_general(qs[hd], k, (((1,), (1,)), ((), ())), preferred_element_type=F32)
            if mask_last and t == ntiles - 1:
                row = lax.broadcasted_iota(jnp.int32, (TQ, TQ), 0)
                col = lax.broadcasted_iota(jnp.int32, (TQ, TQ), 1)
                s = jnp.where((col // CHUNK) <= (row // CHUNK), s, -jnp.inf)
            m_next = jnp.maximum(m, jnp.max(s, axis=1, keepdims=True))
            alpha = jnp.exp2(m - m_next)
            p = jnp.exp2(s - jnp.tile(m_next, (1, TQ // LANES))).astype(BF16)
            pv = jnp.dot(p, jnp.concatenate([v, ones], axis=1), preferred_element_type=F32)
            states[hd] = (m_next, alpha * l + pv[:, LANES:], alpha * acc + pv[:, :LANES])
    return states


def _attn_kernel(qn_ref, qr_ref, kv_ref, kr_ref, *rest, cast_from, n_cast):
    w_refs, o_ref = rest[:n_cast], rest[n_cast]
    wb_refs = rest[n_cast + 1:2 * n_cast + 1]
    m_ref, l_ref, acc_ref = rest[2 * n_cast + 1:]
    qi = pl.program_id(2)

    @pl.when(qi >= cast_from)
    def _():
        for w_ref, wb_ref in zip(w_refs, wb_refs):
            wb_ref[...] = w_ref[...].astype(wb_ref.dtype)

    qr = qr_ref[...]
    qs = [jnp.concatenate([qn_ref[:, hd * LANES:(hd + 1) * LANES], qr], axis=1) for hd in range(2)]

    m_ref[...] = jnp.full_like(m_ref, -jnp.inf)
    l_ref[...] = jnp.zeros_like(l_ref)
    acc_ref[...] = jnp.zeros_like(acc_ref)

    def load_states():
        return [(m_ref[hd], l_ref[hd], acc_ref[hd]) for hd in range(2)]

    def body(g, carry):
        base = pl.multiple_of(g * (ATTN_GROUP * TQ), ATTN_GROUP * TQ)
        states = _attn_tiles(qs, kv_ref, kr_ref, base, ATTN_GROUP, False, load_states())
        for hd in range(2):
            m_ref[hd], l_ref[hd], acc_ref[hd] = states[hd]
        return carry

    lax.fori_loop(0, qi // ATTN_GROUP, body, 0)

    base = pl.multiple_of((qi // ATTN_GROUP) * (ATTN_GROUP * TQ), ATTN_GROUP * TQ)
    for r in range(ATTN_GROUP):
        @pl.when(qi % ATTN_GROUP == r)
        def _():
            states = _attn_tiles(qs, kv_ref, kr_ref, base, r + 1, True, load_states())
            for hd in range(2):
                _, l, acc = states[hd]
                o_ref[:, hd * LANES:(hd + 1) * LANES] = (acc / l).astype(o_ref.dtype)


def mla_attention(q, kv, proj_rope, weights, layer, batch, seq):
    t = q.shape[0]
    nq = seq // TQ
    pairs = MLA_HEADS // 2
    cast_from = nq // 2
    casts = nq - cast_from
    steps = batch * pairs * casts
    slab_rows = [w.shape[1] // steps for w in weights]
    assert all(r * steps == w.shape[1] and r % 16 == 0 for r, w in zip(slab_rows, weights))

    def slab(b, p, i):
        return (b * pairs + p) * casts + jnp.maximum(i - cast_from, 0)
    pair_w = 2 * LANES
    kr_pair_block = (COL_KR - ROPE_BASE) // pair_w
    rope_block0 = (MLA_HEADS * MLA_NOPE) // LANES
    return pl.pallas_call(
        functools.partial(_attn_kernel, cast_from=cast_from, n_cast=len(weights)),
        out_shape=(jax.ShapeDtypeStruct((t, MLA_WIDTH), BF16),
                   *[jax.ShapeDtypeStruct(w.shape[1:], BF16) for w in weights]),
        grid=(batch, pairs, nq),
        in_specs=[pl.BlockSpec((TQ, pair_w), lambda b, p, i: (b * nq + i, p)),
                  pl.BlockSpec((TQ, LANES), lambda b, p, i: (b * nq + i, rope_block0 + p)),
                  pl.BlockSpec((seq, 2 * HEAD_W), lambda b, p, i: (b, p)),
                  pl.BlockSpec((seq, pair_w), lambda b, p, i: (b, kr_pair_block)),
                  *[pl.BlockSpec((None, r, w.shape[2]), lambda b, p, i: (layer, slab(b, p, i), 0))
                    for r, w in zip(slab_rows, weights)]],
        out_specs=(pl.BlockSpec((TQ, pair_w), lambda b, p, i: (b * nq + i, p)),
                   *[pl.BlockSpec((r, w.shape[2]), lambda b, p, i: (slab(b, p, i), 0))
                     for r, w in zip(slab_rows, weights)]),
        scratch_shapes=[pltpu.VMEM((2, TQ, LANES), F32), pltpu.VMEM((2, TQ, LANES), F32),
                        pltpu.VMEM((2, TQ, LANES), F32)],
        compiler_params=_params(("parallel", "parallel", "arbitrary"), 48 << 20),
        name="mla_attention",
    )(q, q, kv, proj_rope, *weights)


def _ret_kernel(q_ref, k_ref, v_ref, g_ref, d_ref, qd_ref, kd_ref, cd_ref, o_ref, s_ref):
    @pl.when(pl.program_id(1) == 0)
    def _():
        s_ref[...] = jnp.zeros_like(s_ref)

    lane = lax.broadcasted_iota(jnp.int32, (RET_L, LANES), 1)
    for p in range(RET_HEADS // 2):
        ls = slice(p * LANES, (p + 1) * LANES)
        q = q_ref[:, ls]
        k = k_ref[:, ls]
        qf = q.astype(F32)
        kdec = (k.astype(F32) * kd_ref[:, ls]).astype(BF16)
        for hh in range(2):
            head = 2 * p + hh
            own = (lane // RET_QK) == hh
            vs = slice(head * RET_V, (head + 1) * RET_V)
            v = v_ref[:, vs]
            qm = jnp.where(own, q, jnp.zeros_like(q))
            s = lax.dot_general(qm, k, (((1,), (1,)), ((), ())), preferred_element_type=F32)
            sd = (s * d_ref[head]).astype(BF16)
            qx = jnp.where(own, qf * qd_ref[:, ls], 0.0).astype(BF16)
            state = s_ref[head]
            o = (jnp.dot(sd, v, preferred_element_type=F32)
                 + jnp.dot(qx, state.astype(BF16), preferred_element_type=F32))
            s_ref[head] = cd_ref[head] * state + lax.dot_general(
                kdec, v, (((0,), (0,)), ((), ())), preferred_element_type=F32)
            o = o * lax.rsqrt(jnp.mean(o * o, axis=-1, keepdims=True) + EPS)
            gate = g_ref[:, vs].astype(F32)
            o_ref[:, vs] = (gate * jax.nn.sigmoid(gate) * o).astype(o_ref.dtype)


def retention_mixer(proj, proj_rope, d_tab, qd_tab, kd_tab, cd_tab, batch, seq):
    t = proj.shape[0]
    nsc = seq // RET_L
    qk_w = RET_HEADS * RET_QK
    full = lambda b, c: (0, 0)
    full3 = lambda b, c: (0, 0, 0)
    return pl.pallas_call(
        _ret_kernel,
        out_shape=jax.ShapeDtypeStruct((t, RET_WIDTH), BF16),
        grid=(batch, nsc),
        in_specs=[pl.BlockSpec((RET_L, qk_w), lambda b, c: (b * nsc + c, (COL_RQ - ROPE_BASE) // qk_w)),
                  pl.BlockSpec((RET_L, qk_w), lambda b, c: (b * nsc + c, (COL_RK - ROPE_BASE) // qk_w)),
                  pl.BlockSpec((RET_L, RET_WIDTH), lambda b, c: (b * nsc + c, COL_RV // RET_WIDTH)),
                  pl.BlockSpec((RET_L, RET_WIDTH), lambda b, c: (b * nsc + c, COL_RG // RET_WIDTH)),
                  pl.BlockSpec((RET_HEADS, RET_L, RET_L), full3),
                  pl.BlockSpec((RET_L, qk_w), full),
                  pl.BlockSpec((RET_L, qk_w), full),
                  pl.BlockSpec((RET_HEADS, LANES, RET_V), full3)],
        out_specs=pl.BlockSpec((RET_L, RET_WIDTH), lambda b, c: (b * nsc + c, 0)),
        scratch_shapes=[pltpu.VMEM((RET_HEADS, LANES, RET_V), F32)],
        compiler_params=_params(("parallel", "arbitrary"), 32 << 20),
        name="retention_mixer",
    )(proj_rope, proj_rope, proj, proj, d_tab, qd_tab, kd_tab, cd_tab)


def retention_tables():
    log_gamma = jnp.log1p(-jnp.exp2(-5.0 - jnp.arange(RET_HEADS, dtype=F32)))
    idx = jnp.arange(RET_L, dtype=F32)
    dist = jnp.abs(idx[:, None] - idx[None, :])
    chunk = jnp.arange(RET_L) // CHUNK
    visible = chunk[None, :] <= chunk[:, None]
    d_tab = jnp.where(visible[None], jnp.exp(log_gamma[:, None, None] * dist[None]), 0.0)
    lg_lane = jnp.repeat(log_gamma, RET_QK)
    qd_tab = jnp.exp(lg_lane[None, :] * (idx[:, None] + 1.0))
    kd_tab = jnp.exp(lg_lane[None, :] * (RET_L - 1.0 - idx[:, None]))
    cd = jnp.exp(log_gamma * RET_L)
    cd_tab = jnp.broadcast_to(cd[:, None, None], (RET_HEADS, LANES, RET_V))
    return d_tab, qd_tab, kd_tab, cd_tab


def _out_proj_kernel(a_ref, b_ref, c_ref, w_ref, o_ref):
    ka = a_ref.shape[1]
    kb = b_ref.shape[1]
    acc = jnp.dot(a_ref[...], w_ref[:ka, :], preferred_element_type=F32)
    acc = acc + jnp.dot(b_ref[...], w_ref[ka:ka + kb, :], preferred_element_type=F32)
    acc = acc + jnp.dot(c_ref[...], w_ref[ka + kb:, :], preferred_element_type=F32)
    o_ref[...] = acc.astype(o_ref.dtype)


def out_proj(a, b, c, w):
    t = a.shape[0]
    k, n = w.shape
    return pl.pallas_call(
        _out_proj_kernel,
        out_shape=jax.ShapeDtypeStruct((t, n), BF16),
        grid=(t // TM, n // TN_OUT),
        in_specs=[pl.BlockSpec((TM, a.shape[1]), lambda i, j: (i, 0)),
                  pl.BlockSpec((TM, b.shape[1]), lambda i, j: (i, 0)),
                  pl.BlockSpec((TM, c.shape[1]), lambda i, j: (i, 0)),
                  pl.BlockSpec((k, TN_OUT), lambda i, j: (0, j))],
        out_specs=pl.BlockSpec((TM, TN_OUT), lambda i, j: (i, j)),
        compiler_params=_params(("parallel", "arbitrary"), 56 << 20),
        name="out_proj",
    )(a, b, c, w)


def _ffn_kernel(h_ref, wu_ref, wd_ref, o_ref):
    @pl.when(pl.program_id(1) == 0)
    def _():
        o_ref[...] = jnp.zeros_like(o_ref)

    a = jnp.dot(h_ref[...], wu_ref[...], preferred_element_type=F32)
    a = jnp.square(jnp.maximum(a, 0.0)).astype(BF16)
    o_ref[...] += jnp.dot(a, wd_ref[...], preferred_element_type=F32)


def ffn(h, wu_b, wd_b):
    t, d = h.shape
    ff = wu_b.shape[1]
    return pl.pallas_call(
        _ffn_kernel,
        out_shape=jax.ShapeDtypeStruct((t, d), F32),
        grid=(t // TM_FFN, ff // TF_FFN),
        in_specs=[pl.BlockSpec((TM_FFN, d), lambda i, j: (i, 0)),
                  pl.BlockSpec((d, TF_FFN), lambda i, j: (0, j)),
                  pl.BlockSpec((TF_FFN, d), lambda i, j: (j, 0))],
        out_specs=pl.BlockSpec((TM_FFN, d), lambda i, j: (i, 0)),
        compiler_params=_params(("parallel", "arbitrary"), 62 << 20),
        name="ffn",
    )(h, wu_b, wd_b)


_SRC_SIZES = (SGU_WIDTH, SGU_WIDTH, MLA_Q_RANK, MLA_KV_RANK, MLA_ROPE,
              RET_HEADS * RET_QK, RET_HEADS * RET_QK, RET_WIDTH, RET_WIDTH)
(_SRC_U, _SRC_V, _SRC_CQ, _SRC_CKV, _SRC_KR, _SRC_RQ, _SRC_RK, _SRC_RV, _SRC_RG,
 IN_WIDTH) = [int(o) for o in np.concatenate([[0], np.cumsum(_SRC_SIZES)])]
_W_IN_MOVES = (
    (COL_U, _SRC_U, 3 * IN_TILE),
    (COL_RV, _SRC_RV, RET_WIDTH),
    (COL_RG, _SRC_RG, RET_WIDTH),
    (COL_CKV, _SRC_CKV, MLA_KV_RANK),
    (COL_KR, _SRC_KR, MLA_ROPE),
    (COL_KR + LANES + MLA_ROPE, _SRC_KR, MLA_ROPE),
    (COL_RQ, _SRC_RQ, RET_HEADS * RET_QK),
    (COL_RK, _SRC_RK, RET_HEADS * RET_QK),
)


def _w_in_prep_kernel(w_ref, o_ref):
    o_ref[COL_KR:COL_RQ, :] = jnp.zeros((COL_RQ - COL_KR, o_ref.shape[1]), o_ref.dtype)
    for dst, src, width in _W_IN_MOVES:
        o_ref[dst:dst + width, :] = w_ref[src:src + width, :].astype(o_ref.dtype)


def prep_w_in(w_t, layer):
    _, n, d = w_t.shape
    return pl.pallas_call(
        _w_in_prep_kernel,
        out_shape=jax.ShapeDtypeStruct((IN_WIDTH_PADDED, d), BF16),
        grid=(d // TK_PREP,),
        in_specs=[pl.BlockSpec((None, n, TK_PREP), lambda i: (layer, 0, i))],
        out_specs=pl.BlockSpec((IN_WIDTH_PADDED, TK_PREP), lambda i: (0, i)),
        compiler_params=_params(("parallel",), 40 << 20),
        name="prep_w_in",
    )(w_t)


def _prep_wq(w):
    k = w.shape[0]
    w3 = w.reshape(k, MLA_HEADS, MLA_NOPE + MLA_ROPE)
    nope = w3[:, :, :MLA_NOPE].reshape(k, MLA_HEADS * MLA_NOPE)
    rope = w3[:, :, MLA_NOPE:].reshape(k, MLA_HEADS * MLA_ROPE)
    return jnp.concatenate([nope, rope], axis=1).astype(BF16)


def _rope_lane_tables(seq):
    inv_freq = 1.0 / (ROPE_THETA ** (jnp.arange(0, MLA_ROPE, 2, dtype=F32) / MLA_ROPE))
    ang = jnp.arange(seq, dtype=F32)[:, None] * inv_freq[None, :]
    cos, sin = jnp.cos(ang), jnp.sin(ang)
    cos_t = jnp.concatenate([cos, cos, cos, cos], axis=1)
    sin_t = jnp.concatenate([-sin, sin, -sin, sin], axis=1)
    return cos_t, sin_t


def kernel(x, norm_mix_pre, norm_mix_post, norm_ffn_pre, norm_ffn_post, w_in, sgu_ln_g, sgu_ln_b, sgu_w, sgu_b, mla_q_norm, mla_wq_b, mla_kv_norm, mla_wkv_b, w_out, w_up, w_down):
    batch, seq, d = x.shape
    depth = w_in.shape[0]
    cos_t, sin_t = _rope_lane_tables(seq)
    d_tab, qd_tab, kd_tab, cd_tab = retention_tables()
    w_in_t = jnp.swapaxes(w_in, 1, 2)

    xf = x.reshape(batch * seq, d)
    h = rmsnorm_rows(xf, norm_mix_pre[0])
    for l in range(depth):
        proj, proj_rope = in_proj(h, prep_w_in(w_in_t, l), cos_t, sin_t, seq)
        bias_full = jnp.repeat(sgu_b[l].T, SGU_WIDTH // SGU_HEADS, axis=1)
        out_a = sgu_mixer(proj, sgu_ln_g[l], sgu_ln_b[l], sgu_w[l], bias_full)
        q = q_proj(proj, mla_q_norm[l], _prep_wq(mla_wq_b[l]), cos_t, sin_t, seq)
        kv = kv_proj(proj_rope, mla_kv_norm[l], mla_wkv_b, l)
        out_b, wu_b, wd_b, wo_b = mla_attention(q, kv, proj_rope, (w_up, w_down, w_out), l, batch, seq)
        out_c = retention_mixer(proj, proj_rope, d_tab, qd_tab, kd_tab, cd_tab, batch, seq)
        mixed = out_proj(out_a, out_b, out_c, wo_b)
        xf, h2 = residual_norm(xf, mixed, norm_mix_post[l], norm_ffn_pre[l])
        f = ffn(h2, wu_b, wd_b)
        g_next = norm_mix_pre[l + 1] if l + 1 < depth else None
        xf, h = residual_norm(xf, f, norm_ffn_post[l], g_next)
    return xf.reshape(batch, seq, d)
```

```python
import functools
import math

import numpy as np
import jax
import jax.numpy as jnp
from jax import lax
from jax.experimental import pallas as pl
from jax.experimental.pallas import tpu as pltpu

F32 = jnp.float32
BF16 = jnp.bfloat16

D_MODEL = 4096
CHUNK = 64
EPS = 1e-6
ROPE_THETA = 10000.0
SGU_BLOCK = 128
SGU_WIDTH = 1024
SGU_HEADS = 8
MLA_HEADS = 16
MLA_NOPE = 128
MLA_ROPE = 64
MLA_V = 128
MLA_Q_RANK = 1024
MLA_KV_RANK = 512
MLA_WIDTH = MLA_HEADS * MLA_V
RET_HEADS = 8
RET_QK = 64
RET_V = 128
RET_WIDTH = RET_HEADS * RET_V
FF_DIM = 4 * D_MODEL

LANES = 128
VMEM_CAP = 64 * 1024 * 1024

IN_TILE = 1024
COL_U = 0 * IN_TILE
COL_V = 1 * IN_TILE
COL_CQ = 2 * IN_TILE
COL_RV = 3 * IN_TILE
COL_RG = 4 * IN_TILE
COL_CKV = 5 * IN_TILE
COL_KR = COL_CKV + MLA_KV_RANK
COL_RQ = 6 * IN_TILE
COL_RK = COL_RQ + RET_HEADS * RET_QK
IN_WIDTH_PADDED = 7 * IN_TILE
ROPE_BASE = COL_CKV

TM = 1024
TM_NORM = 256
TM_FFN = 512
TF_FFN = 1024
TN_OUT = 1024
TK_PREP = 256
TB_SGU = 512
TQ = 512
RET_L = 256


def _vmem_limit(nbytes):
    return int(min(VMEM_CAP - (2 << 20), nbytes))


def _params(sem, nbytes):
    return pltpu.CompilerParams(dimension_semantics=sem, vmem_limit_bytes=_vmem_limit(nbytes))


def _rms(x, g):
    return x * lax.rsqrt(jnp.mean(x * x, axis=-1, keepdims=True) + EPS) * g


def _rmsnorm_kernel(x_ref, g_ref, o_ref):
    o_ref[...] = _rms(x_ref[...], g_ref[...]).astype(o_ref.dtype)


def rmsnorm_rows(x, g):
    t, d = x.shape
    return pl.pallas_call(
        _rmsnorm_kernel,
        out_shape=jax.ShapeDtypeStruct((t, d), BF16),
        grid=(t // TM_NORM,),
        in_specs=[pl.BlockSpec((TM_NORM, d), lambda i: (i, 0)),
                  pl.BlockSpec((1, d), lambda i: (0, 0))],
        out_specs=pl.BlockSpec((TM_NORM, d), lambda i: (i, 0)),
        compiler_params=_params(("parallel",), 32 << 20),
        name="rmsnorm_rows",
    )(x, g.reshape(1, d))


def _residual_norm_kernel(x_ref, y_ref, gp_ref, gn_ref, xo_ref, ho_ref):
    xn = x_ref[...] + _rms(y_ref[...].astype(F32), gp_ref[...])
    xo_ref[...] = xn
    ho_ref[...] = _rms(xn, gn_ref[...]).astype(ho_ref.dtype)


def _residual_kernel(x_ref, y_ref, gp_ref, xo_ref):
    xo_ref[...] = x_ref[...] + _rms(y_ref[...].astype(F32), gp_ref[...])


def residual_norm(x, y, g_post, g_next):
    t, d = x.shape
    row = pl.BlockSpec((TM_NORM, d), lambda i: (i, 0))
    vec = pl.BlockSpec((1, d), lambda i: (0, 0))
    if g_next is None:
        return pl.pallas_call(
            _residual_kernel,
            out_shape=jax.ShapeDtypeStruct((t, d), F32),
            grid=(t // TM_NORM,),
            in_specs=[row, row, vec],
            out_specs=row,
            compiler_params=_params(("parallel",), 40 << 20),
            name="residual",
        )(x, y, g_post.reshape(1, d)), None
    return pl.pallas_call(
        _residual_norm_kernel,
        out_shape=(jax.ShapeDtypeStruct((t, d), F32), jax.ShapeDtypeStruct((t, d), BF16)),
        grid=(t // TM_NORM,),
        in_specs=[row, row, vec, vec],
        out_specs=(row, row),
        compiler_params=_params(("parallel",), 40 << 20),
        name="residual_norm",
    )(x, y, g_post.reshape(1, d), g_next.reshape(1, d))


def _rope_cols(x, cos, sin):
    lane = lax.broadcasted_iota(jnp.int32, x.shape, 1)
    first_half = (lane % MLA_ROPE) < (MLA_ROPE // 2)
    partner = jnp.where(first_half, pltpu.roll(x, LANES - MLA_ROPE // 2, 1), pltpu.roll(x, MLA_ROPE // 2, 1))
    return x * cos + partner * sin


def _in_proj_main_kernel(h_ref, w_ref, o_ref):
    o_ref[...] = lax.dot_general(h_ref[...], w_ref[...], (((1,), (1,)), ((), ())),
                                 preferred_element_type=F32).astype(o_ref.dtype)


def _in_proj_rope_kernel(h_ref, w_ref, cos_ref, sin_ref, o_ref):
    j = pl.program_id(1)
    acc = lax.dot_general(h_ref[...], w_ref[...], (((1,), (1,)), ((), ())), preferred_element_type=F32)

    @pl.when(j == 0)
    def _():
        lo = COL_KR - COL_CKV
        o_ref[:, :lo] = acc[:, :lo].astype(o_ref.dtype)
        for c in range(lo // LANES, lo // LANES + 2):
            sl = slice(c * LANES, (c + 1) * LANES)
            o_ref[:, sl] = _rope_cols(acc[:, sl], cos_ref[...], sin_ref[...]).astype(o_ref.dtype)
        o_ref[:, lo + 2 * LANES:] = acc[:, lo + 2 * LANES:].astype(o_ref.dtype)

    @pl.when(j == 1)
    def _():
        q_cols = (RET_HEADS * RET_QK) // LANES
        for c in range(IN_TILE // LANES):
            sl = slice(c * LANES, (c + 1) * LANES)
            r = _rope_cols(acc[:, sl], cos_ref[...], sin_ref[...])
            if c < q_cols:
                r = r * (RET_QK ** -0.5)
            o_ref[:, sl] = r.astype(o_ref.dtype)


def in_proj(h, w_t, cos_t, sin_t, seq):
    t, d = h.shape
    pos_blocks = seq // TM
    tab = pl.BlockSpec((TM, LANES), lambda i, j: (i % pos_blocks, 0))
    rope_tile0 = ROPE_BASE // IN_TILE
    main = pl.pallas_call(
        _in_proj_main_kernel,
        out_shape=jax.ShapeDtypeStruct((t, ROPE_BASE), BF16),
        grid=(t // TM, rope_tile0),
        in_specs=[pl.BlockSpec((TM, d), lambda i, j: (i, 0)),
                  pl.BlockSpec((IN_TILE, d), lambda i, j: (j, 0))],
        out_specs=pl.BlockSpec((TM, IN_TILE), lambda i, j: (i, j)),
        compiler_params=_params(("parallel", "arbitrary"), 52 << 20),
        name="in_proj_main",
    )(h, w_t)
    rope = pl.pallas_call(
        _in_proj_rope_kernel,
        out_shape=jax.ShapeDtypeStruct((t, IN_WIDTH_PADDED - ROPE_BASE), BF16),
        grid=(t // TM, (IN_WIDTH_PADDED - ROPE_BASE) // IN_TILE),
        in_specs=[pl.BlockSpec((TM, d), lambda i, j: (i, 0)),
                  pl.BlockSpec((IN_TILE, d), lambda i, j: (rope_tile0 + j, 0)),
                  tab, tab],
        out_specs=pl.BlockSpec((TM, IN_TILE), lambda i, j: (i, j)),
        compiler_params=_params(("parallel", "arbitrary"), 52 << 20),
        name="in_proj_rope",
    )(h, w_t, cos_t, sin_t)
    return main, rope


def _sgu_kernel(u_ref, v_ref, g_ref, b_ref, w_ref, bias_ref, o_ref):
    row = lax.broadcasted_iota(jnp.int32, (SGU_BLOCK, SGU_BLOCK), 0)
    col = lax.broadcasted_iota(jnp.int32, (SGU_BLOCK, SGU_BLOCK), 1)
    keep = (col // CHUNK) <= (row // CHUNK)
    for g in range(SGU_HEADS):
        sl = slice(g * LANES, (g + 1) * LANES)
        wg = jnp.where(keep, w_ref[g], 0.0).astype(BF16)
        vg = v_ref[:, sl].astype(F32)
        mu = jnp.mean(vg, axis=-1, keepdims=True)
        vc = vg - mu
        y = vc * lax.rsqrt(jnp.mean(vc * vc, axis=-1, keepdims=True) + EPS) * g_ref[:, sl] + b_ref[:, sl]
        yb = y.astype(BF16)
        for n in range(TB_SGU // SGU_BLOCK):
            rs = slice(n * SGU_BLOCK, (n + 1) * SGU_BLOCK)
            mixed = jnp.dot(wg, yb[rs, :], preferred_element_type=F32) + bias_ref[:, sl]
            o_ref[rs, sl] = (u_ref[rs, sl].astype(F32) * mixed).astype(o_ref.dtype)


def sgu_mixer(proj, ln_g, ln_b, w_s, bias_full):
    t = proj.shape[0]
    vec = pl.BlockSpec((1, SGU_WIDTH), lambda i: (0, 0))
    return pl.pallas_call(
        _sgu_kernel,
        out_shape=jax.ShapeDtypeStruct((t, SGU_WIDTH), BF16),
        grid=(t // TB_SGU,),
        in_specs=[pl.BlockSpec((TB_SGU, SGU_WIDTH), lambda i: (i, COL_U // SGU_WIDTH)),
                  pl.BlockSpec((TB_SGU, SGU_WIDTH), lambda i: (i, COL_V // SGU_WIDTH)),
                  vec, vec,
                  pl.BlockSpec((SGU_HEADS, SGU_BLOCK, SGU_BLOCK), lambda i: (0, 0, 0)),
                  pl.BlockSpec((SGU_BLOCK, SGU_WIDTH), lambda i: (0, 0))],
        out_specs=pl.BlockSpec((TB_SGU, SGU_WIDTH), lambda i: (i, 0)),
        compiler_params=_params(("parallel",), 32 << 20),
        name="sgu_mixer",
    )(proj, proj, ln_g.reshape(1, -1), ln_b.reshape(1, -1), w_s, bias_full)


Q_SCALE = (MLA_NOPE + MLA_ROPE) ** -0.5 * math.log2(math.e)
Q_ROPE_TILE = (MLA_HEADS * MLA_NOPE) // IN_TILE


def _q_proj_kernel(c_ref, g_ref, w_ref, cos_ref, sin_ref, o_ref, cn_ref):
    j = pl.program_id(1)

    @pl.when(j == 0)
    def _():
        cn_ref[...] = _rms(c_ref[...].astype(F32), g_ref[...]).astype(cn_ref.dtype)

    acc = jnp.dot(cn_ref[...], w_ref[...], preferred_element_type=F32) * Q_SCALE

    @pl.when(j < Q_ROPE_TILE)
    def _():
        o_ref[...] = acc.astype(o_ref.dtype)

    @pl.when(j >= Q_ROPE_TILE)
    def _():
        for c in range(IN_TILE // LANES):
            sl = slice(c * LANES, (c + 1) * LANES)
            o_ref[:, sl] = _rope_cols(acc[:, sl], cos_ref[...], sin_ref[...]).astype(o_ref.dtype)


def q_proj(proj, g, w, cos_t, sin_t, seq):
    t = proj.shape[0]
    n = w.shape[1]
    pos_blocks = seq // TM
    tab = pl.BlockSpec((TM, LANES), lambda i, j: (i % pos_blocks, 0))
    return pl.pallas_call(
        _q_proj_kernel,
        out_shape=jax.ShapeDtypeStruct((t, n), BF16),
        grid=(t // TM, n // IN_TILE),
        in_specs=[pl.BlockSpec((TM, MLA_Q_RANK), lambda i, j: (i, COL_CQ // MLA_Q_RANK)),
                  pl.BlockSpec((1, MLA_Q_RANK), lambda i, j: (0, 0)),
                  pl.BlockSpec((MLA_Q_RANK, IN_TILE), lambda i, j: (0, j)),
                  tab, tab],
        out_specs=pl.BlockSpec((TM, IN_TILE), lambda i, j: (i, j)),
        scratch_shapes=[pltpu.VMEM((TM, MLA_Q_RANK), BF16)],
        compiler_params=_params(("parallel", "arbitrary"), 40 << 20),
        name="q_proj",
    )(proj, g.reshape(1, -1), w, cos_t, sin_t)


def _kv_proj_kernel(c_ref, g_ref, w_ref, o_ref, cn_ref):
    @pl.when(pl.program_id(1) == 0)
    def _():
        cn_ref[...] = _rms(c_ref[...].astype(F32), g_ref[...]).astype(cn_ref.dtype)

    o_ref[...] = jnp.dot(cn_ref[...], w_ref[...].astype(BF16), preferred_element_type=F32).astype(o_ref.dtype)


def kv_proj(proj, g, w, layer):
    t = proj.shape[0]
    n = w.shape[2]
    return pl.pallas_call(
        _kv_proj_kernel,
        out_shape=jax.ShapeDtypeStruct((t, n), BF16),
        grid=(t // TM, n // IN_TILE),
        in_specs=[pl.BlockSpec((TM, MLA_KV_RANK), lambda i, j: (i, (COL_CKV - ROPE_BASE) // MLA_KV_RANK)),
                  pl.BlockSpec((1, MLA_KV_RANK), lambda i, j: (0, 0)),
                  pl.BlockSpec((None, MLA_KV_RANK, IN_TILE), lambda i, j: (layer, 0, j))],
        out_specs=pl.BlockSpec((TM, IN_TILE), lambda i, j: (i, j)),
        scratch_shapes=[pltpu.VMEM((TM, MLA_KV_RANK), BF16)],
        compiler_params=_params(("parallel", "arbitrary"), 32 << 20),
        name="kv_proj",
    )(proj, g.reshape(1, -1), w)


ATTN_GROUP = 4
ATTN_HEADS = 4


HEAD_W = MLA_NOPE + MLA_V


def _attn_tiles(qs, kv_ref, kr_ref, base, ntiles, mask_last, states):
    states = list(states)
    ones = jnp.ones((TQ, LANES), BF16)
    for t in range(ntiles):
        st = pl.multiple_of(base + t * TQ, TQ)
        for hd in range(ATTN_HEADS):
            m, l, acc = states[hd]
            kn = kv_ref[pl.ds(st, TQ), hd * HEAD_W:hd * HEAD_W + MLA_NOPE]
            v = kv_ref[pl.ds(st, TQ), hd * HEAD_W + MLA_NOPE:(hd + 1) * HEAD_W]
            par = hd % 2
            k = jnp.concatenate([kn, kr_ref[pl.ds(st, TQ), par * LANES:(par + 1) * LANES]], axis=1)
            s = lax.dot_general(qs[hd], k, (((1,), (1,)), ((), ())), preferred_element_type=F32)
            if mask_last and t == ntiles - 1:
                row = lax.broadcasted_iota(jnp.int32, (TQ, TQ), 0)
                col = lax.broadcasted_iota(jnp.int32, (TQ, TQ), 1)
                s = jnp.where((col // CHUNK) <= (row // CHUNK), s, -jnp.inf)
            m_next = jnp.maximum(m, jnp.max(s, axis=1, keepdims=True))
            alpha = jnp.exp2(m - m_next)
            p = jnp.exp2(s - jnp.tile(m_next, (1, TQ // LANES))).astype(BF16)
            pv = jnp.dot(p, jnp.concatenate([v, ones], axis=1), preferred_element_type=F32)
            states[hd] = (m_next, alpha * l + pv[:, LANES:], alpha * acc + pv[:, :LANES])
    return states


def _attn_kernel(qn_ref, qr_ref, kv_ref, kr_ref, *rest, cast_from, n_cast):
    w_refs, o_ref = rest[:n_cast], rest[n_cast]
    wb_refs = rest[n_cast + 1:2 * n_cast + 1]
    m_ref, l_ref, acc_ref = rest[2 * n_cast + 1:]
    qi = pl.program_id(2)

    @pl.when(qi >= cast_from)
    def _():
        for w_ref, wb_ref in zip(w_refs, wb_refs):
            wb_ref[...] = w_ref[...].astype(wb_ref.dtype)

    qs = [jnp.concatenate([qn_ref[:, hd * LANES:(hd + 1) * LANES],
                           qr_ref[:, (hd // 2) * LANES:(hd // 2 + 1) * LANES]], axis=1)
          for hd in range(ATTN_HEADS)]

    m_ref[...] = jnp.full_like(m_ref, -jnp.inf)
    l_ref[...] = jnp.zeros_like(l_ref)
    acc_ref[...] = jnp.zeros_like(acc_ref)

    def load_states():
        return [(m_ref[hd], l_ref[hd], acc_ref[hd]) for hd in range(ATTN_HEADS)]

    def body(g, carry):
        base = pl.multiple_of(g * (ATTN_GROUP * TQ), ATTN_GROUP * TQ)
        states = _attn_tiles(qs, kv_ref, kr_ref, base, ATTN_GROUP, False, load_states())
        for hd in range(ATTN_HEADS):
            m_ref[hd], l_ref[hd], acc_ref[hd] = states[hd]
        return carry

    lax.fori_loop(0, qi // ATTN_GROUP, body, 0)

    base = pl.multiple_of((qi // ATTN_GROUP) * (ATTN_GROUP * TQ), ATTN_GROUP * TQ)
    for r in range(ATTN_GROUP):
        @pl.when(qi % ATTN_GROUP == r)
        def _():
            states = _attn_tiles(qs, kv_ref, kr_ref, base, r + 1, True, load_states())
            for hd in range(ATTN_HEADS):
                _, l, acc = states[hd]
                o_ref[:, hd * LANES:(hd + 1) * LANES] = (acc / l).astype(o_ref.dtype)


def mla_attention(q, kv, proj_rope, weights, layer, batch, seq):
    t = q.shape[0]
    nq = seq // TQ
    groups = MLA_HEADS // ATTN_HEADS
    cast_from = 0
    casts = nq - cast_from
    steps = batch * groups * casts
    slab_rows = [w.shape[1] // steps for w in weights]
    assert all(r * steps == w.shape[1] and r % 16 == 0 for r, w in zip(slab_rows, weights))

    def slab(b, g, i):
        return (b * groups + g) * casts + jnp.maximum(i - cast_from, 0)
    nope_w = ATTN_HEADS * MLA_NOPE
    rope_w = ATTN_HEADS * MLA_ROPE
    kr_w = 2 * LANES
    kr_block = (COL_KR - ROPE_BASE) // kr_w
    rope_block0 = (MLA_HEADS * MLA_NOPE) // rope_w
    return pl.pallas_call(
        functools.partial(_attn_kernel, cast_from=cast_from, n_cast=len(weights)),
        out_shape=(jax.ShapeDtypeStruct((t, MLA_WIDTH), BF16),
                   *[jax.ShapeDtypeStruct(w.shape[1:], BF16) for w in weights]),
        grid=(batch, groups, nq),
        in_specs=[pl.BlockSpec((TQ, nope_w), lambda b, p, i: (b * nq + i, p)),
                  pl.BlockSpec((TQ, rope_w), lambda b, p, i: (b * nq + i, rope_block0 + p)),
                  pl.BlockSpec((seq, ATTN_HEADS * HEAD_W), lambda b, p, i: (b, p)),
                  pl.BlockSpec((seq, kr_w), lambda b, p, i: (b, kr_block)),
                  *[pl.BlockSpec((None, r, w.shape[2]), lambda b, p, i: (layer, slab(b, p, i), 0))
                    for r, w in zip(slab_rows, weights)]],
        out_specs=(pl.BlockSpec((TQ, nope_w), lambda b, p, i: (b * nq + i, p)),
                   *[pl.BlockSpec((r, w.shape[2]), lambda b, p, i: (slab(b, p, i), 0))
                     for r, w in zip(slab_rows, weights)]),
        scratch_shapes=[pltpu.VMEM((ATTN_HEADS, TQ, LANES), F32)] * 3,
        compiler_params=_params(("parallel", "parallel", "arbitrary"), 62 << 20),
        name="mla_attention",
    )(q, q, kv, proj_rope, *weights)


def _ret_kernel(q_ref, k_ref, v_ref, g_ref, d_ref, qd_ref, kd_ref, cd_ref, o_ref, s_ref):
    @pl.when(pl.program_id(1) == 0)
    def _():
        s_ref[...] = jnp.zeros_like(s_ref)

    lane = lax.broadcasted_iota(jnp.int32, (RET_L, LANES), 1)
    for p in range(RET_HEADS // 2):
        ls = slice(p * LANES, (p + 1) * LANES)
        q = q_ref[:, ls]
        k = k_ref[:, ls]
        qf = q.astype(F32)
        kdec = (k.astype(F32) * kd_ref[:, ls]).astype(BF16)
        for hh in range(2):
            head = 2 * p + hh
            own = (lane // RET_QK) == hh
            vs = slice(head * RET_V, (head + 1) * RET_V)
            v = v_ref[:, vs]
            qm = jnp.where(own, q, jnp.zeros_like(q))
            s = lax.dot_general(qm, k, (((1,), (1,)), ((), ())), preferred_element_type=F32)
            sd = (s * d_ref[head]).astype(BF16)
            qx = jnp.where(own, qf * qd_ref[:, ls], 0.0).astype(BF16)
            state = s_ref[head]
            o = (jnp.dot(sd, v, preferred_element_type=F32)
                 + jnp.dot(qx, state.astype(BF16), preferred_element_type=F32))
            s_ref[head] = cd_ref[head] * state + lax.dot_general(
                kdec, v, (((0,), (0,)), ((), ())), preferred_element_type=F32)
            o = o * lax.rsqrt(jnp.mean(o * o, axis=-1, keepdims=True) + EPS)
            gate = g_ref[:, vs].astype(F32)
            o_ref[:, vs] = (gate * jax.nn.sigmoid(gate) * o).astype(o_ref.dtype)


def retention_mixer(proj, proj_rope, d_tab, qd_tab, kd_tab, cd_tab, batch, seq):
    t = proj.shape[0]
    nsc = seq // RET_L
    qk_w = RET_HEADS * RET_QK
    full = lambda b, c: (0, 0)
    full3 = lambda b, c: (0, 0, 0)
    return pl.pallas_call(
        _ret_kernel,
        out_shape=jax.ShapeDtypeStruct((t, RET_WIDTH), BF16),
        grid=(batch, nsc),
        in_specs=[pl.BlockSpec((RET_L, qk_w), lambda b, c: (b * nsc + c, (COL_RQ - ROPE_BASE) // qk_w)),
                  pl.BlockSpec((RET_L, qk_w), lambda b, c: (b * nsc + c, (COL_RK - ROPE_BASE) // qk_w)),
                  pl.BlockSpec((RET_L, RET_WIDTH), lambda b, c: (b * nsc + c, COL_RV // RET_WIDTH)),
                  pl.BlockSpec((RET_L, RET_WIDTH), lambda b, c: (b * nsc + c, COL_RG // RET_WIDTH)),
                  pl.BlockSpec((RET_HEADS, RET_L, RET_L), full3),
                  pl.BlockSpec((RET_L, qk_w), full),
                  pl.BlockSpec((RET_L, qk_w), full),
                  pl.BlockSpec((RET_HEADS, LANES, RET_V), full3)],
        out_specs=pl.BlockSpec((RET_L, RET_WIDTH), lambda b, c: (b * nsc + c, 0)),
        scratch_shapes=[pltpu.VMEM((RET_HEADS, LANES, RET_V), F32)],
        compiler_params=_params(("parallel", "arbitrary"), 32 << 20),
        name="retention_mixer",
    )(proj_rope, proj_rope, proj, proj, d_tab, qd_tab, kd_tab, cd_tab)


def retention_tables():
    log_gamma = jnp.log1p(-jnp.exp2(-5.0 - jnp.arange(RET_HEADS, dtype=F32)))
    idx = jnp.arange(RET_L, dtype=F32)
    dist = jnp.abs(idx[:, None] - idx[None, :])
    chunk = jnp.arange(RET_L) // CHUNK
    visible = chunk[None, :] <= chunk[:, None]
    d_tab = jnp.where(visible[None], jnp.exp(log_gamma[:, None, None] * dist[None]), 0.0)
    lg_lane = jnp.repeat(log_gamma, RET_QK)
    qd_tab = jnp.exp(lg_lane[None, :] * (idx[:, None] + 1.0))
    kd_tab = jnp.exp(lg_lane[None, :] * (RET_L - 1.0 - idx[:, None]))
    cd = jnp.exp(log_gamma * RET_L)
    cd_tab = jnp.broadcast_to(cd[:, None, None], (RET_HEADS, LANES, RET_V))
    return d_tab, qd_tab, kd_tab, cd_tab


def _out_proj_kernel(a_ref, b_ref, c_ref, w_ref, o_ref):
    ka = a_ref.shape[1]
    kb = b_ref.shape[1]
    acc = jnp.dot(a_ref[...], w_ref[:ka, :], preferred_element_type=F32)
    acc = acc + jnp.dot(b_ref[...], w_ref[ka:ka + kb, :], preferred_element_type=F32)
    acc = acc + jnp.dot(c_ref[...], w_ref[ka + kb:, :], preferred_element_type=F32)
    o_ref[...] = acc.astype(o_ref.dtype)


def out_proj(a, b, c, w):
    t = a.shape[0]
    k, n = w.shape
    return pl.pallas_call(
        _out_proj_kernel,
        out_shape=jax.ShapeDtypeStruct((t, n), BF16),
        grid=(t // TM, n // TN_OUT),
        in_specs=[pl.BlockSpec((TM, a.shape[1]), lambda i, j: (i, 0)),
                  pl.BlockSpec((TM, b.shape[1]), lambda i, j: (i, 0)),
                  pl.BlockSpec((TM, c.shape[1]), lambda i, j: (i, 0)),
                  pl.BlockSpec((k, TN_OUT), lambda i, j: (0, j))],
        out_specs=pl.BlockSpec((TM, TN_OUT), lambda i, j: (i, j)),
        compiler_params=_params(("parallel", "arbitrary"), 56 << 20),
        name="out_proj",
    )(a, b, c, w)


def _ffn_kernel(h_ref, wu_ref, wd_ref, o_ref):
    @pl.when(pl.program_id(1) == 0)
    def _():
        o_ref[...] = jnp.zeros_like(o_ref)

    a = jnp.dot(h_ref[...], wu_ref[...], preferred_element_type=F32)
    a = jnp.square(jnp.maximum(a, 0.0)).astype(BF16)
    o_ref[...] += jnp.dot(a, wd_ref[...], preferred_element_type=F32)


def ffn(h, wu_b, wd_b):
    t, d = h.shape
    ff = wu_b.shape[1]
    return pl.pallas_call(
        _ffn_kernel,
        out_shape=jax.ShapeDtypeStruct((t, d), F32),
        grid=(t // TM_FFN, ff // TF_FFN),
        in_specs=[pl.BlockSpec((TM_FFN, d), lambda i, j: (i, 0)),
                  pl.BlockSpec((d, TF_FFN), lambda i, j: (0, j)),
                  pl.BlockSpec((TF_FFN, d), lambda i, j: (j, 0))],
        out_specs=pl.BlockSpec((TM_FFN, d), lambda i, j: (i, 0)),
        compiler_params=_params(("parallel", "arbitrary"), 62 << 20),
        name="ffn",
    )(h, wu_b, wd_b)


_SRC_SIZES = (SGU_WIDTH, SGU_WIDTH, MLA_Q_RANK, MLA_KV_RANK, MLA_ROPE,
              RET_HEADS * RET_QK, RET_HEADS * RET_QK, RET_WIDTH, RET_WIDTH)
(_SRC_U, _SRC_V, _SRC_CQ, _SRC_CKV, _SRC_KR, _SRC_RQ, _SRC_RK, _SRC_RV, _SRC_RG,
 IN_WIDTH) = [int(o) for o in np.concatenate([[0], np.cumsum(_SRC_SIZES)])]
_W_IN_MOVES = (
    (COL_U, _SRC_U, 3 * IN_TILE),
    (COL_RV, _SRC_RV, RET_WIDTH),
    (COL_RG, _SRC_RG, RET_WIDTH),
    (COL_CKV, _SRC_CKV, MLA_KV_RANK),
    (COL_KR, _SRC_KR, MLA_ROPE),
    (COL_KR + LANES + MLA_ROPE, _SRC_KR, MLA_ROPE),
    (COL_RQ, _SRC_RQ, RET_HEADS * RET_QK),
    (COL_RK, _SRC_RK, RET_HEADS * RET_QK),
)


def _w_in_prep_kernel(w_ref, o_ref):
    o_ref[COL_KR:COL_RQ, :] = jnp.zeros((COL_RQ - COL_KR, o_ref.shape[1]), o_ref.dtype)
    for dst, src, width in _W_IN_MOVES:
        o_ref[dst:dst + width, :] = w_ref[src:src + width, :].astype(o_ref.dtype)


def prep_w_in(w_t, layer):
    _, n, d = w_t.shape
    return pl.pallas_call(
        _w_in_prep_kernel,
        out_shape=jax.ShapeDtypeStruct((IN_WIDTH_PADDED, d), BF16),
        grid=(d // TK_PREP,),
        in_specs=[pl.BlockSpec((None, n, TK_PREP), lambda i: (layer, 0, i))],
        out_specs=pl.BlockSpec((IN_WIDTH_PADDED, TK_PREP), lambda i: (0, i)),
        compiler_params=_params(("parallel",), 40 << 20),
        name="prep_w_in",
    )(w_t)


def _prep_wq(w):
    k = w.shape[0]
    w3 = w.reshape(k, MLA_HEADS, MLA_NOPE + MLA_ROPE)
    nope = w3[:, :, :MLA_NOPE].reshape(k, MLA_HEADS * MLA_NOPE)
    rope = w3[:, :, MLA_NOPE:].reshape(k, MLA_HEADS * MLA_ROPE)
    return jnp.concatenate([nope, rope], axis=1).astype(BF16)


def _rope_lane_tables(seq):
    inv_freq = 1.0 / (ROPE_THETA ** (jnp.arange(0, MLA_ROPE, 2, dtype=F32) / MLA_ROPE))
    ang = jnp.arange(seq, dtype=F32)[:, None] * inv_freq[None, :]
    cos, sin = jnp.cos(ang), jnp.sin(ang)
    cos_t = jnp.concatenate([cos, cos, cos, cos], axis=1)
    sin_t = jnp.concatenate([-sin, sin, -sin, sin], axis=1)
    return cos_t, sin_t


def kernel(x, norm_mix_pre, norm_mix_post, norm_ffn_pre, norm_ffn_post, w_in, sgu_ln_g, sgu_ln_b, sgu_w, sgu_b, mla_q_norm, mla_wq_b, mla_kv_norm, mla_wkv_b, w_out, w_up, w_down):
    batch, seq, d = x.shape
    depth = w_in.shape[0]
    cos_t, sin_t = _rope_lane_tables(seq)
    d_tab, qd_tab, kd_tab, cd_tab = retention_tables()
    w_in_t = jnp.swapaxes(w_in, 1, 2)

    xf = x.reshape(batch * seq, d)
    h = rmsnorm_rows(xf, norm_mix_pre[0])
    for l in range(depth):
        proj, proj_rope = in_proj(h, prep_w_in(w_in_t, l), cos_t, sin_t, seq)
        bias_full = jnp.repeat(sgu_b[l].T, SGU_WIDTH // SGU_HEADS, axis=1)
        out_a = sgu_mixer(proj, sgu_ln_g[l], sgu_ln_b[l], sgu_w[l], bias_full)
        q = q_proj(proj, mla_q_norm[l], _prep_wq(mla_wq_b[l]), cos_t, sin_t, seq)
        kv = kv_proj(proj_rope, mla_kv_norm[l], mla_wkv_b, l)
        out_b, wu_b, wd_b, wo_b = mla_attention(q, kv, proj_rope, (w_up, w_down, w_out), l, batch, seq)
        out_c = retention_mixer(proj, proj_rope, d_tab, qd_tab, kd_tab, cd_tab, batch, seq)
        mixed = out_proj(out_a, out_b, out_c, wo_b)
        xf, h2 = residual_norm(xf, mixed, norm_mix_post[l], norm_ffn_pre[l])
        f = ffn(h2, wu_b, wd_b)
        g_next = norm_mix_pre[l + 1] if l + 1 < depth else None
        xf, h = residual_norm(xf, f, norm_ffn_post[l], g_next)
    return xf.reshape(batch, seq, d)
```

```python
import functools
import math

import numpy as np
import jax
import jax.numpy as jnp
from jax import lax
from jax.experimental import pallas as pl
from jax.experimental.pallas import tpu as pltpu

F32 = jnp.float32
BF16 = jnp.bfloat16

CHUNK = 64
EPS = 1e-6
ROPE_THETA = 10000.0
SGU_BLOCK = 128
SGU_WIDTH = 1024
SGU_HEADS = 8
MLA_HEADS = 16
MLA_NOPE = 128
MLA_ROPE = 64
MLA_V = 128
MLA_Q_RANK = 1024
MLA_KV_RANK = 512
MLA_WIDTH = MLA_HEADS * MLA_V
RET_HEADS = 8
RET_QK = 64
RET_V = 128
RET_WIDTH = RET_HEADS * RET_V

LANES = 128
BF16_SUBLANES = 16
VMEM_CAP = 64 * 1024 * 1024
VMEM_RESERVED = 2 * 1024 * 1024
VMEM_INTERNAL = 6 * 1024 * 1024
PIPELINE_BUFFERS = 2

IN_TILE = 1024
COL_U = 0 * IN_TILE
COL_V = 1 * IN_TILE
COL_CQ = 2 * IN_TILE
COL_RV = 3 * IN_TILE
COL_RG = 4 * IN_TILE
COL_CKV = 5 * IN_TILE
COL_KR = COL_CKV + MLA_KV_RANK
COL_RQ = 6 * IN_TILE
COL_RK = COL_RQ + RET_HEADS * RET_QK
IN_WIDTH_PADDED = 7 * IN_TILE
ROPE_BASE = COL_CKV

TM = 1024
TM_NORM = 256
TM_FFN = 512
TF_FFN = 1024
TN_OUT = 1024
TK_PREP = 256
TB_SGU = 512
TQ = 512
ATTN_GROUP = 4
ATTN_HEADS = 4
RET_L = 256


def _nbytes(shape, dtype):
    return math.prod(1 if s is None else s for s in shape) * jnp.dtype(dtype).itemsize


def _pallas(kernel, *, name, grid, semantics, in_specs, operands, out_specs, out_shape,
            scratch_shapes=(), temp_bytes=0):
    outs = out_shape if isinstance(out_shape, (tuple, list)) else (out_shape,)
    ospecs = out_specs if isinstance(out_specs, (tuple, list)) else (out_specs,)
    pipelined = sum(_nbytes(s.block_shape, a.dtype) for s, a in zip(in_specs, operands))
    pipelined += sum(_nbytes(s.block_shape, o.dtype) for s, o in zip(ospecs, outs))
    scratch = sum(_nbytes(s.shape, s.dtype) for s in scratch_shapes)
    need = PIPELINE_BUFFERS * pipelined + scratch + temp_bytes + VMEM_INTERNAL
    params = pltpu.CompilerParams(dimension_semantics=semantics,
                                  vmem_limit_bytes=int(min(need, VMEM_CAP - VMEM_RESERVED)))
    return pl.pallas_call(kernel, out_shape=out_shape, grid=grid, in_specs=in_specs, out_specs=out_specs,
                          scratch_shapes=scratch_shapes, compiler_params=params, name=name)(*operands)


def _rms(x, g):
    return x * lax.rsqrt(jnp.mean(x * x, axis=-1, keepdims=True) + EPS) * g


def _rmsnorm_kernel(x_ref, g_ref, o_ref):
    o_ref[...] = _rms(x_ref[...], g_ref[...]).astype(o_ref.dtype)


def rmsnorm_rows(x, g):
    t, d = x.shape
    return _pallas(
        _rmsnorm_kernel, name="rmsnorm_rows", grid=(t // TM_NORM,), semantics=("parallel",),
        in_specs=[pl.BlockSpec((TM_NORM, d), lambda i: (i, 0)),
                  pl.BlockSpec((1, d), lambda i: (0, 0))],
        operands=(x, g.reshape(1, d)),
        out_specs=pl.BlockSpec((TM_NORM, d), lambda i: (i, 0)),
        out_shape=jax.ShapeDtypeStruct((t, d), BF16),
        temp_bytes=_nbytes((TM_NORM, d), F32))


def _residual_norm_kernel(x_ref, y_ref, gp_ref, gn_ref, xo_ref, ho_ref):
    xn = x_ref[...] + _rms(y_ref[...].astype(F32), gp_ref[...])
    xo_ref[...] = xn
    ho_ref[...] = _rms(xn, gn_ref[...]).astype(ho_ref.dtype)


def _residual_kernel(x_ref, y_ref, gp_ref, xo_ref):
    xo_ref[...] = x_ref[...] + _rms(y_ref[...].astype(F32), gp_ref[...])


def residual_norm(x, y, g_post, g_next):
    t, d = x.shape
    row = pl.BlockSpec((TM_NORM, d), lambda i: (i, 0))
    vec = pl.BlockSpec((1, d), lambda i: (0, 0))
    common = dict(grid=(t // TM_NORM,), semantics=("parallel",), temp_bytes=2 * _nbytes((TM_NORM, d), F32))
    if g_next is None:
        return _pallas(_residual_kernel, name="residual", in_specs=[row, row, vec],
                       operands=(x, y, g_post.reshape(1, d)), out_specs=row,
                       out_shape=jax.ShapeDtypeStruct((t, d), F32), **common), None
    return _pallas(_residual_norm_kernel, name="residual_norm", in_specs=[row, row, vec, vec],
                   operands=(x, y, g_post.reshape(1, d), g_next.reshape(1, d)), out_specs=(row, row),
                   out_shape=(jax.ShapeDtypeStruct((t, d), F32), jax.ShapeDtypeStruct((t, d), BF16)),
                   **common)


def _rope_cols(x, cos, sin):
    lane = lax.broadcasted_iota(jnp.int32, x.shape, 1)
    first_half = (lane % MLA_ROPE) < (MLA_ROPE // 2)
    partner = jnp.where(first_half, pltpu.roll(x, LANES - MLA_ROPE // 2, 1), pltpu.roll(x, MLA_ROPE // 2, 1))
    return x * cos + partner * sin


def _in_proj_main_kernel(h_ref, w_ref, o_ref):
    o_ref[...] = lax.dot_general(h_ref[...], w_ref[...], (((1,), (1,)), ((), ())),
                                 preferred_element_type=F32).astype(o_ref.dtype)


def _in_proj_rope_kernel(h_ref, w_ref, cos_ref, sin_ref, o_ref):
    j = pl.program_id(1)
    acc = lax.dot_general(h_ref[...], w_ref[...], (((1,), (1,)), ((), ())), preferred_element_type=F32)

    @pl.when(j == 0)
    def _():
        lo = COL_KR - COL_CKV
        o_ref[:, :lo] = acc[:, :lo].astype(o_ref.dtype)
        for c in range(lo // LANES, lo // LANES + 2):
            sl = slice(c * LANES, (c + 1) * LANES)
            o_ref[:, sl] = _rope_cols(acc[:, sl], cos_ref[...], sin_ref[...]).astype(o_ref.dtype)
        o_ref[:, lo + 2 * LANES:] = acc[:, lo + 2 * LANES:].astype(o_ref.dtype)

    @pl.when(j == 1)
    def _():
        q_cols = (RET_HEADS * RET_QK) // LANES
        for c in range(IN_TILE // LANES):
            sl = slice(c * LANES, (c + 1) * LANES)
            r = _rope_cols(acc[:, sl], cos_ref[...], sin_ref[...])
            if c < q_cols:
                r = r * (RET_QK ** -0.5)
            o_ref[:, sl] = r.astype(o_ref.dtype)


def in_proj(h, w_t, cos_t, sin_t, seq):
    t, d = h.shape
    pos_blocks = seq // TM
    tab = pl.BlockSpec((TM, LANES), lambda i, j: (i % pos_blocks, 0))
    rope_tile0 = ROPE_BASE // IN_TILE
    rope_w = IN_WIDTH_PADDED - ROPE_BASE
    common = dict(semantics=("parallel", "arbitrary"), temp_bytes=_nbytes((TM, IN_TILE), F32),
                  out_specs=pl.BlockSpec((TM, IN_TILE), lambda i, j: (i, j)))
    main = _pallas(
        _in_proj_main_kernel, name="in_proj_main", grid=(t // TM, rope_tile0),
        in_specs=[pl.BlockSpec((TM, d), lambda i, j: (i, 0)),
                  pl.BlockSpec((IN_TILE, d), lambda i, j: (j, 0))],
        operands=(h, w_t), out_shape=jax.ShapeDtypeStruct((t, ROPE_BASE), BF16), **common)
    rope = _pallas(
        _in_proj_rope_kernel, name="in_proj_rope", grid=(t // TM, rope_w // IN_TILE),
        in_specs=[pl.BlockSpec((TM, d), lambda i, j: (i, 0)),
                  pl.BlockSpec((IN_TILE, d), lambda i, j: (rope_tile0 + j, 0)),
                  tab, tab],
        operands=(h, w_t, cos_t, sin_t), out_shape=jax.ShapeDtypeStruct((t, rope_w), BF16), **common)
    return main, rope


def _sgu_kernel(u_ref, v_ref, g_ref, b_ref, w_ref, bias_ref, o_ref):
    row = lax.broadcasted_iota(jnp.int32, (SGU_BLOCK, SGU_BLOCK), 0)
    col = lax.broadcasted_iota(jnp.int32, (SGU_BLOCK, SGU_BLOCK), 1)
    keep = (col // CHUNK) <= (row // CHUNK)
    for g in range(SGU_HEADS):
        sl = slice(g * LANES, (g + 1) * LANES)
        wg = jnp.where(keep, w_ref[g], 0.0).astype(BF16)
        vg = v_ref[:, sl].astype(F32)
        mu = jnp.mean(vg, axis=-1, keepdims=True)
        vc = vg - mu
        y = vc * lax.rsqrt(jnp.mean(vc * vc, axis=-1, keepdims=True) + EPS) * g_ref[:, sl] + b_ref[:, sl]
        yb = y.astype(BF16)
        for n in range(TB_SGU // SGU_BLOCK):
            rs = slice(n * SGU_BLOCK, (n + 1) * SGU_BLOCK)
            mixed = jnp.dot(wg, yb[rs, :], preferred_element_type=F32) + bias_ref[:, sl]
            o_ref[rs, sl] = (u_ref[rs, sl].astype(F32) * mixed).astype(o_ref.dtype)


def sgu_mixer(proj, ln_g, ln_b, w_s, bias_full):
    t = proj.shape[0]
    vec = pl.BlockSpec((1, SGU_WIDTH), lambda i: (0, 0))
    return _pallas(
        _sgu_kernel, name="sgu_mixer", grid=(t // TB_SGU,), semantics=("parallel",),
        in_specs=[pl.BlockSpec((TB_SGU, SGU_WIDTH), lambda i: (i, COL_U // SGU_WIDTH)),
                  pl.BlockSpec((TB_SGU, SGU_WIDTH), lambda i: (i, COL_V // SGU_WIDTH)),
                  vec, vec,
                  pl.BlockSpec((SGU_HEADS, SGU_BLOCK, SGU_BLOCK), lambda i: (0, 0, 0)),
                  pl.BlockSpec((SGU_BLOCK, SGU_WIDTH), lambda i: (0, 0))],
        operands=(proj, proj, ln_g.reshape(1, -1), ln_b.reshape(1, -1), w_s, bias_full),
        out_specs=pl.BlockSpec((TB_SGU, SGU_WIDTH), lambda i: (i, 0)),
        out_shape=jax.ShapeDtypeStruct((t, SGU_WIDTH), BF16),
        temp_bytes=_nbytes((TB_SGU, SGU_WIDTH), F32))


Q_SCALE = (MLA_NOPE + MLA_ROPE) ** -0.5 * math.log2(math.e)
Q_ROPE_TILE = (MLA_HEADS * MLA_NOPE) // IN_TILE


def _q_proj_kernel(c_ref, g_ref, w_ref, cos_ref, sin_ref, o_ref, cn_ref):
    j = pl.program_id(1)

    @pl.when(j == 0)
    def _():
        cn_ref[...] = _rms(c_ref[...].astype(F32), g_ref[...]).astype(cn_ref.dtype)

    acc = jnp.dot(cn_ref[...], w_ref[...], preferred_element_type=F32) * Q_SCALE

    @pl.when(j < Q_ROPE_TILE)
    def _():
        o_ref[...] = acc.astype(o_ref.dtype)

    @pl.when(j >= Q_ROPE_TILE)
    def _():
        for c in range(IN_TILE // LANES):
            sl = slice(c * LANES, (c + 1) * LANES)
            o_ref[:, sl] = _rope_cols(acc[:, sl], cos_ref[...], sin_ref[...]).astype(o_ref.dtype)


def q_proj(proj, g, w, cos_t, sin_t, seq):
    t = proj.shape[0]
    n = w.shape[1]
    pos_blocks = seq // TM
    tab = pl.BlockSpec((TM, LANES), lambda i, j: (i % pos_blocks, 0))
    return _pallas(
        _q_proj_kernel, name="q_proj", grid=(t // TM, n // IN_TILE), semantics=("parallel", "arbitrary"),
        in_specs=[pl.BlockSpec((TM, MLA_Q_RANK), lambda i, j: (i, COL_CQ // MLA_Q_RANK)),
                  pl.BlockSpec((1, MLA_Q_RANK), lambda i, j: (0, 0)),
                  pl.BlockSpec((MLA_Q_RANK, IN_TILE), lambda i, j: (0, j)),
                  tab, tab],
        operands=(proj, g.reshape(1, -1), w, cos_t, sin_t),
        out_specs=pl.BlockSpec((TM, IN_TILE), lambda i, j: (i, j)),
        out_shape=jax.ShapeDtypeStruct((t, n), BF16),
        scratch_shapes=[pltpu.VMEM((TM, MLA_Q_RANK), BF16)],
        temp_bytes=_nbytes((TM, IN_TILE), F32))


def _kv_proj_kernel(c_ref, g_ref, w_ref, o_ref, cn_ref):
    @pl.when(pl.program_id(1) == 0)
    def _():
        cn_ref[...] = _rms(c_ref[...].astype(F32), g_ref[...]).astype(cn_ref.dtype)

    o_ref[...] = jnp.dot(cn_ref[...], w_ref[...].astype(BF16), preferred_element_type=F32).astype(o_ref.dtype)


def kv_proj(proj_rope, g, w, layer):
    t = proj_rope.shape[0]
    n = w.shape[2]
    return _pallas(
        _kv_proj_kernel, name="kv_proj", grid=(t // TM, n // IN_TILE), semantics=("parallel", "arbitrary"),
        in_specs=[pl.BlockSpec((TM, MLA_KV_RANK), lambda i, j: (i, (COL_CKV - ROPE_BASE) // MLA_KV_RANK)),
                  pl.BlockSpec((1, MLA_KV_RANK), lambda i, j: (0, 0)),
                  pl.BlockSpec((None, MLA_KV_RANK, IN_TILE), lambda i, j: (layer, 0, j))],
        operands=(proj_rope, g.reshape(1, -1), w),
        out_specs=pl.BlockSpec((TM, IN_TILE), lambda i, j: (i, j)),
        out_shape=jax.ShapeDtypeStruct((t, n), BF16),
        scratch_shapes=[pltpu.VMEM((TM, MLA_KV_RANK), BF16)],
        temp_bytes=_nbytes((TM, IN_TILE), F32) + _nbytes((MLA_KV_RANK, IN_TILE), BF16))


HEAD_W = MLA_NOPE + MLA_V


def _attn_tiles(qs, kv_ref, kr_ref, base, ntiles, mask_last, states):
    states = list(states)
    ones = jnp.ones((TQ, LANES), BF16)
    for t in range(ntiles):
        st = pl.multiple_of(base + t * TQ, TQ)
        for hd in range(ATTN_HEADS):
            m, l, acc = states[hd]
            kn = kv_ref[pl.ds(st, TQ), hd * HEAD_W:hd * HEAD_W + MLA_NOPE]
            v = kv_ref[pl.ds(st, TQ), hd * HEAD_W + MLA_NOPE:(hd + 1) * HEAD_W]
            par = hd % 2
            k = jnp.concatenate([kn, kr_ref[pl.ds(st, TQ), par * LANES:(par + 1) * LANES]], axis=1)
            s = lax.dot_general(qs[hd], k, (((1,), (1,)), ((), ())), preferred_element_type=F32)
            if mask_last and t == ntiles - 1:
                row = lax.broadcasted_iota(jnp.int32, (TQ, TQ), 0)
                col = lax.broadcasted_iota(jnp.int32, (TQ, TQ), 1)
                s = jnp.where((col // CHUNK) <= (row // CHUNK), s, -jnp.inf)
            m_next = jnp.maximum(m, jnp.max(s, axis=1, keepdims=True))
            alpha = jnp.exp2(m - m_next)
            p = jnp.exp2(s - jnp.tile(m_next, (1, TQ // LANES))).astype(BF16)
            pv = jnp.dot(p, jnp.concatenate([v, ones], axis=1), preferred_element_type=F32)
            states[hd] = (m_next, alpha * l + pv[:, LANES:], alpha * acc + pv[:, :LANES])
    return states


def _attn_kernel(qn_ref, qr_ref, kv_ref, kr_ref, *rest, cast_from, n_cast):
    w_refs, o_ref = rest[:n_cast], rest[n_cast]
    wb_refs = rest[n_cast + 1:2 * n_cast + 1]
    m_ref, l_ref, acc_ref = rest[2 * n_cast + 1:]
    qi = pl.program_id(2)

    @pl.when(qi >= cast_from)
    def _():
        for w_ref, wb_ref in zip(w_refs, wb_refs):
            wb_ref[...] = w_ref[...].astype(wb_ref.dtype)

    qs = [jnp.concatenate([qn_ref[:, hd * LANES:(hd + 1) * LANES],
                           qr_ref[:, (hd // 2) * LANES:(hd // 2 + 1) * LANES]], axis=1)
          for hd in range(ATTN_HEADS)]

    m_ref[...] = jnp.full_like(m_ref, -jnp.inf)
    l_ref[...] = jnp.zeros_like(l_ref)
    acc_ref[...] = jnp.zeros_like(acc_ref)

    def load_states():
        return [(m_ref[hd], l_ref[hd], acc_ref[hd]) for hd in range(ATTN_HEADS)]

    def body(g, carry):
        base = pl.multiple_of(g * (ATTN_GROUP * TQ), ATTN_GROUP * TQ)
        states = _attn_tiles(qs, kv_ref, kr_ref, base, ATTN_GROUP, False, load_states())
        for hd in range(ATTN_HEADS):
            m_ref[hd], l_ref[hd], acc_ref[hd] = states[hd]
        return carry

    lax.fori_loop(0, qi // ATTN_GROUP, body, 0)

    base = pl.multiple_of((qi // ATTN_GROUP) * (ATTN_GROUP * TQ), ATTN_GROUP * TQ)
    for r in range(ATTN_GROUP):
        @pl.when(qi % ATTN_GROUP == r)
        def _():
            states = _attn_tiles(qs, kv_ref, kr_ref, base, r + 1, True, load_states())
            for hd in range(ATTN_HEADS):
                _, l, acc = states[hd]
                o_ref[:, hd * LANES:(hd + 1) * LANES] = (acc / l).astype(o_ref.dtype)


def mla_attention(q, kv, proj_rope, weights, layer, batch, seq):
    t = q.shape[0]
    nq = seq // TQ
    groups = MLA_HEADS // ATTN_HEADS
    cast_from = 0
    casts = nq - cast_from
    steps = batch * groups * casts
    slab_rows = [w.shape[1] // steps for w in weights]
    assert all(r * steps == w.shape[1] and r % BF16_SUBLANES == 0 for r, w in zip(slab_rows, weights))

    def slab(b, g, i):
        return (b * groups + g) * casts + jnp.maximum(i - cast_from, 0)
    nope_w = ATTN_HEADS * MLA_NOPE
    rope_w = ATTN_HEADS * MLA_ROPE
    kr_w = 2 * LANES
    kr_block = (COL_KR - ROPE_BASE) // kr_w
    rope_block0 = (MLA_HEADS * MLA_NOPE) // rope_w
    return _pallas(
        functools.partial(_attn_kernel, cast_from=cast_from, n_cast=len(weights)),
        name="mla_attention", grid=(batch, groups, nq), semantics=("parallel", "parallel", "arbitrary"),
        in_specs=[pl.BlockSpec((TQ, nope_w), lambda b, p, i: (b * nq + i, p)),
                  pl.BlockSpec((TQ, rope_w), lambda b, p, i: (b * nq + i, rope_block0 + p)),
                  pl.BlockSpec((seq, ATTN_HEADS * HEAD_W), lambda b, p, i: (b, p)),
                  pl.BlockSpec((seq, kr_w), lambda b, p, i: (b, kr_block)),
                  *[pl.BlockSpec((None, r, w.shape[2]), lambda b, p, i: (layer, slab(b, p, i), 0))
                    for r, w in zip(slab_rows, weights)]],
        operands=(q, q, kv, proj_rope, *weights),
        out_specs=(pl.BlockSpec((TQ, nope_w), lambda b, p, i: (b * nq + i, p)),
                   *[pl.BlockSpec((r, w.shape[2]), lambda b, p, i: (slab(b, p, i), 0))
                     for r, w in zip(slab_rows, weights)]),
        out_shape=(jax.ShapeDtypeStruct((t, MLA_WIDTH), BF16),
                   *[jax.ShapeDtypeStruct(w.shape[1:], BF16) for w in weights]),
        scratch_shapes=[pltpu.VMEM((ATTN_HEADS, TQ, LANES), F32)] * 3,
        temp_bytes=ATTN_HEADS * (_nbytes((TQ, TQ), F32) + _nbytes((TQ, TQ), BF16)))


def _ret_kernel(q_ref, k_ref, v_ref, g_ref, d_ref, qd_ref, kd_ref, cd_ref, o_ref, s_ref):
    @pl.when(pl.program_id(1) == 0)
    def _():
        s_ref[...] = jnp.zeros_like(s_ref)

    lane = lax.broadcasted_iota(jnp.int32, (RET_L, LANES), 1)
    for p in range(RET_HEADS // 2):
        ls = slice(p * LANES, (p + 1) * LANES)
        q = q_ref[:, ls]
        k = k_ref[:, ls]
        qf = q.astype(F32)
        kdec = (k.astype(F32) * kd_ref[:, ls]).astype(BF16)
        for hh in range(2):
            head = 2 * p + hh
            own = (lane // RET_QK) == hh
            vs = slice(head * RET_V, (head + 1) * RET_V)
            v = v_ref[:, vs]
            qm = jnp.where(own, q, jnp.zeros_like(q))
            s = lax.dot_general(qm, k, (((1,), (1,)), ((), ())), preferred_element_type=F32)
            sd = (s * d_ref[head]).astype(BF16)
            qx = jnp.where(own, qf * qd_ref[:, ls], 0.0).astype(BF16)
            state = s_ref[head]
            o = (jnp.dot(sd, v, preferred_element_type=F32)
                 + jnp.dot(qx, state.astype(BF16), preferred_element_type=F32))
            s_ref[head] = cd_ref[head] * state + lax.dot_general(
                kdec, v, (((0,), (0,)), ((), ())), preferred_element_type=F32)
            o = o * lax.rsqrt(jnp.mean(o * o, axis=-1, keepdims=True) + EPS)
            gate = g_ref[:, vs].astype(F32)
            o_ref[:, vs] = (gate * jax.nn.sigmoid(gate) * o).astype(o_ref.dtype)


def retention_mixer(proj, proj_rope, d_tab, qd_tab, kd_tab, cd_tab, batch, seq):
    t = proj.shape[0]
    nsc = seq // RET_L
    qk_w = RET_HEADS * RET_QK
    full = lambda b, c: (0, 0)
    full3 = lambda b, c: (0, 0, 0)
    return _pallas(
        _ret_kernel, name="retention_mixer", grid=(batch, nsc), semantics=("parallel", "arbitrary"),
        in_specs=[pl.BlockSpec((RET_L, qk_w), lambda b, c: (b * nsc + c, (COL_RQ - ROPE_BASE) // qk_w)),
                  pl.BlockSpec((RET_L, qk_w), lambda b, c: (b * nsc + c, (COL_RK - ROPE_BASE) // qk_w)),
                  pl.BlockSpec((RET_L, RET_WIDTH), lambda b, c: (b * nsc + c, COL_RV // RET_WIDTH)),
                  pl.BlockSpec((RET_L, RET_WIDTH), lambda b, c: (b * nsc + c, COL_RG // RET_WIDTH)),
                  pl.BlockSpec((RET_HEADS, RET_L, RET_L), full3),
                  pl.BlockSpec((RET_L, qk_w), full),
                  pl.BlockSpec((RET_L, qk_w), full),
                  pl.BlockSpec((RET_HEADS, LANES, RET_V), full3)],
        operands=(proj_rope, proj_rope, proj, proj, d_tab, qd_tab, kd_tab, cd_tab),
        out_specs=pl.BlockSpec((RET_L, RET_WIDTH), lambda b, c: (b * nsc + c, 0)),
        out_shape=jax.ShapeDtypeStruct((t, RET_WIDTH), BF16),
        scratch_shapes=[pltpu.VMEM((RET_HEADS, LANES, RET_V), F32)],
        temp_bytes=RET_HEADS * _nbytes((RET_L, RET_L), F32))


def retention_tables():
    log_gamma = jnp.log1p(-jnp.exp2(-5.0 - jnp.arange(RET_HEADS, dtype=F32)))
    idx = jnp.arange(RET_L, dtype=F32)
    dist = jnp.abs(idx[:, None] - idx[None, :])
    chunk = jnp.arange(RET_L) // CHUNK
    visible = chunk[None, :] <= chunk[:, None]
    d_tab = jnp.where(visible[None], jnp.exp(log_gamma[:, None, None] * dist[None]), 0.0)
    lg_lane = jnp.repeat(log_gamma, RET_QK)
    qd_tab = jnp.exp(lg_lane[None, :] * (idx[:, None] + 1.0))
    kd_tab = jnp.exp(lg_lane[None, :] * (RET_L - 1.0 - idx[:, None]))
    cd = jnp.exp(log_gamma * RET_L)
    cd_tab = jnp.broadcast_to(cd[:, None, None], (RET_HEADS, LANES, RET_V))
    return d_tab, qd_tab, kd_tab, cd_tab


def _out_proj_kernel(a_ref, b_ref, c_ref, w_ref, o_ref):
    ka = a_ref.shape[1]
    kb = b_ref.shape[1]
    acc = jnp.dot(a_ref[...], w_ref[:ka, :], preferred_element_type=F32)
    acc = acc + jnp.dot(b_ref[...], w_ref[ka:ka + kb, :], preferred_element_type=F32)
    acc = acc + jnp.dot(c_ref[...], w_ref[ka + kb:, :], preferred_element_type=F32)
    o_ref[...] = acc.astype(o_ref.dtype)


def out_proj(a, b, c, w):
    t = a.shape[0]
    k, n = w.shape
    return _pallas(
        _out_proj_kernel, name="out_proj", grid=(t // TM, n // TN_OUT), semantics=("parallel", "arbitrary"),
        in_specs=[pl.BlockSpec((TM, a.shape[1]), lambda i, j: (i, 0)),
                  pl.BlockSpec((TM, b.shape[1]), lambda i, j: (i, 0)),
                  pl.BlockSpec((TM, c.shape[1]), lambda i, j: (i, 0)),
                  pl.BlockSpec((k, TN_OUT), lambda i, j: (0, j))],
        operands=(a, b, c, w),
        out_specs=pl.BlockSpec((TM, TN_OUT), lambda i, j: (i, j)),
        out_shape=jax.ShapeDtypeStruct((t, n), BF16),
        temp_bytes=_nbytes((TM, TN_OUT), F32))


def _ffn_kernel(h_ref, wu_ref, wd_ref, o_ref):
    @pl.when(pl.program_id(1) == 0)
    def _():
        o_ref[...] = jnp.zeros_like(o_ref)

    a = jnp.dot(h_ref[...], wu_ref[...], preferred_element_type=F32)
    a = jnp.square(jnp.maximum(a, 0.0)).astype(BF16)
    o_ref[...] += jnp.dot(a, wd_ref[...], preferred_element_type=F32)


def ffn(h, wu_b, wd_b):
    t, d = h.shape
    ff = wu_b.shape[1]
    return _pallas(
        _ffn_kernel, name="ffn", grid=(t // TM_FFN, ff // TF_FFN), semantics=("parallel", "arbitrary"),
        in_specs=[pl.BlockSpec((TM_FFN, d), lambda i, j: (i, 0)),
                  pl.BlockSpec((d, TF_FFN), lambda i, j: (0, j)),
                  pl.BlockSpec((TF_FFN, d), lambda i, j: (j, 0))],
        operands=(h, wu_b, wd_b),
        out_specs=pl.BlockSpec((TM_FFN, d), lambda i, j: (i, 0)),
        out_shape=jax.ShapeDtypeStruct((t, d), F32),
        temp_bytes=_nbytes((TM_FFN, TF_FFN), F32) + _nbytes((TM_FFN, TF_FFN), BF16))


_SRC_SIZES = (SGU_WIDTH, SGU_WIDTH, MLA_Q_RANK, MLA_KV_RANK, MLA_ROPE,
              RET_HEADS * RET_QK, RET_HEADS * RET_QK, RET_WIDTH, RET_WIDTH)
(_SRC_U, _SRC_V, _SRC_CQ, _SRC_CKV, _SRC_KR, _SRC_RQ, _SRC_RK, _SRC_RV, _SRC_RG,
 IN_WIDTH) = [int(o) for o in np.concatenate([[0], np.cumsum(_SRC_SIZES)])]
_W_IN_MOVES = (
    (COL_U, _SRC_U, 3 * IN_TILE),
    (COL_RV, _SRC_RV, RET_WIDTH),
    (COL_RG, _SRC_RG, RET_WIDTH),
    (COL_CKV, _SRC_CKV, MLA_KV_RANK),
    (COL_KR, _SRC_KR, MLA_ROPE),
    (COL_KR + LANES + MLA_ROPE, _SRC_KR, MLA_ROPE),
    (COL_RQ, _SRC_RQ, RET_HEADS * RET_QK),
    (COL_RK, _SRC_RK, RET_HEADS * RET_QK),
)


def _w_in_prep_kernel(w_ref, o_ref):
    o_ref[COL_KR:COL_RQ, :] = jnp.zeros((COL_RQ - COL_KR, o_ref.shape[1]), o_ref.dtype)
    for dst, src, width in _W_IN_MOVES:
        o_ref[dst:dst + width, :] = w_ref[src:src + width, :].astype(o_ref.dtype)


def prep_w_in(w_t, layer):
    _, n, d = w_t.shape
    return _pallas(
        _w_in_prep_kernel, name="prep_w_in", grid=(d // TK_PREP,), semantics=("parallel",),
        in_specs=[pl.BlockSpec((None, n, TK_PREP), lambda i: (layer, 0, i))],
        operands=(w_t,),
        out_specs=pl.BlockSpec((IN_WIDTH_PADDED, TK_PREP), lambda i: (0, i)),
        out_shape=jax.ShapeDtypeStruct((IN_WIDTH_PADDED, d), BF16))


def _prep_wq(w):
    k = w.shape[0]
    w3 = w.reshape(k, MLA_HEADS, MLA_NOPE + MLA_ROPE)
    nope = w3[:, :, :MLA_NOPE].reshape(k, MLA_HEADS * MLA_NOPE)
    rope = w3[:, :, MLA_NOPE:].reshape(k, MLA_HEADS * MLA_ROPE)
    return jnp.concatenate([nope, rope], axis=1).astype(BF16)


def _rope_lane_tables(seq):
    inv_freq = 1.0 / (ROPE_THETA ** (jnp.arange(0, MLA_ROPE, 2, dtype=F32) / MLA_ROPE))
    ang = jnp.arange(seq, dtype=F32)[:, None] * inv_freq[None, :]
    cos, sin = jnp.cos(ang), jnp.sin(ang)
    cos_t = jnp.concatenate([cos, cos, cos, cos], axis=1)
    sin_t = jnp.concatenate([-sin, sin, -sin, sin], axis=1)
    return cos_t, sin_t


def kernel(x, norm_mix_pre, norm_mix_post, norm_ffn_pre, norm_ffn_post, w_in, sgu_ln_g, sgu_ln_b, sgu_w, sgu_b, mla_q_norm, mla_wq_b, mla_kv_norm, mla_wkv_b, w_out, w_up, w_down):
    batch, seq, d = x.shape
    depth = w_in.shape[0]
    cos_t, sin_t = _rope_lane_tables(seq)
    d_tab, qd_tab, kd_tab, cd_tab = retention_tables()
    w_in_t = jnp.swapaxes(w_in, 1, 2)

    xf = x.reshape(batch * seq, d)
    h = rmsnorm_rows(xf, norm_mix_pre[0])
    for l in range(depth):
        proj, proj_rope = in_proj(h, prep_w_in(w_in_t, l), cos_t, sin_t, seq)
        bias_full = jnp.repeat(sgu_b[l].T, SGU_WIDTH // SGU_HEADS, axis=1)
        out_a = sgu_mixer(proj, sgu_ln_g[l], sgu_ln_b[l], sgu_w[l], bias_full)
        q = q_proj(proj, mla_q_norm[l], _prep_wq(mla_wq_b[l]), cos_t, sin_t, seq)
        kv = kv_proj(proj_rope, mla_kv_norm[l], mla_wkv_b, l)
        out_b, wu_b, wd_b, wo_b = mla_attention(q, kv, proj_rope, (w_up, w_down, w_out), l, batch, seq)
        out_c = retention_mixer(proj, proj_rope, d_tab, qd_tab, kd_tab, cd_tab, batch, seq)
        mixed = out_proj(out_a, out_b, out_c, wo_b)
        xf, h2 = residual_norm(xf, mixed, norm_mix_post[l], norm_ffn_pre[l])
        f = ffn(h2, wu_b, wd_b)
        g_next = norm_mix_pre[l + 1] if l + 1 < depth else None
        xf, h = residual_norm(xf, f, norm_ffn_post[l], g_next)
    return xf.reshape(batch, seq, d)
```

```python
import functools
import math

import numpy as np
import jax
import jax.numpy as jnp
from jax import lax
from jax.experimental import pallas as pl
from jax.experimental.pallas import tpu as pltpu

F32 = jnp.float32
BF16 = jnp.bfloat16

CHUNK = 64
EPS = 1e-6
ROPE_THETA = 10000.0
SGU_BLOCK = 128
SGU_WIDTH = 1024
SGU_HEADS = 8
MLA_HEADS = 16
MLA_NOPE = 128
MLA_ROPE = 64
MLA_V = 128
MLA_Q_RANK = 1024
MLA_KV_RANK = 512
MLA_WIDTH = MLA_HEADS * MLA_V
RET_HEADS = 8
RET_QK = 64
RET_V = 128
RET_WIDTH = RET_HEADS * RET_V

LANES = 128
BF16_SUBLANES = 16
VMEM_CAP = 64 * 1024 * 1024
VMEM_RESERVED = 2 * 1024 * 1024
VMEM_INTERNAL = 6 * 1024 * 1024
PIPELINE_BUFFERS = 2

IN_TILE = 1024
COL_U = 0 * IN_TILE
COL_V = 1 * IN_TILE
COL_CQ = 2 * IN_TILE
COL_RV = 3 * IN_TILE
COL_RG = 4 * IN_TILE
COL_CKV = 5 * IN_TILE
COL_KR = COL_CKV + MLA_KV_RANK
COL_RQ = 6 * IN_TILE
COL_RK = COL_RQ + RET_HEADS * RET_QK
IN_WIDTH_PADDED = 7 * IN_TILE
ROPE_BASE = COL_CKV

TM = 1024
TM_QKV = 2048
TM_NORM = 256
TM_FFN = 512
TF_FFN = 1024
TN_OUT = 1024
TK_PREP = 256
TB_SGU = 512
TQ = 512
ATTN_GROUP = 4
ATTN_HEADS = 4
RET_L = 256


def _nbytes(shape, dtype):
    return math.prod(1 if s is None else s for s in shape) * jnp.dtype(dtype).itemsize


def _pallas(kernel, *, name, grid, semantics, in_specs, operands, out_specs, out_shape,
            scratch_shapes=(), temp_bytes=0):
    outs = out_shape if isinstance(out_shape, (tuple, list)) else (out_shape,)
    ospecs = out_specs if isinstance(out_specs, (tuple, list)) else (out_specs,)
    pipelined = sum(_nbytes(s.block_shape, a.dtype) for s, a in zip(in_specs, operands))
    pipelined += sum(_nbytes(s.block_shape, o.dtype) for s, o in zip(ospecs, outs))
    scratch = sum(_nbytes(s.shape, s.dtype) for s in scratch_shapes)
    need = PIPELINE_BUFFERS * pipelined + scratch + temp_bytes + VMEM_INTERNAL
    params = pltpu.CompilerParams(dimension_semantics=semantics,
                                  vmem_limit_bytes=int(min(need, VMEM_CAP - VMEM_RESERVED)))
    return pl.pallas_call(kernel, out_shape=out_shape, grid=grid, in_specs=in_specs, out_specs=out_specs,
                          scratch_shapes=scratch_shapes, compiler_params=params, name=name)(*operands)


def _rms(x, g):
    return x * lax.rsqrt(jnp.mean(x * x, axis=-1, keepdims=True) + EPS) * g


def _rmsnorm_kernel(x_ref, g_ref, o_ref):
    o_ref[...] = _rms(x_ref[...], g_ref[...]).astype(o_ref.dtype)


def rmsnorm_rows(x, g):
    t, d = x.shape
    return _pallas(
        _rmsnorm_kernel, name="rmsnorm_rows", grid=(t // TM_NORM,), semantics=("parallel",),
        in_specs=[pl.BlockSpec((TM_NORM, d), lambda i: (i, 0)),
                  pl.BlockSpec((1, d), lambda i: (0, 0))],
        operands=(x, g.reshape(1, d)),
        out_specs=pl.BlockSpec((TM_NORM, d), lambda i: (i, 0)),
        out_shape=jax.ShapeDtypeStruct((t, d), BF16),
        temp_bytes=_nbytes((TM_NORM, d), F32))


def _residual_norm_kernel(x_ref, y_ref, gp_ref, gn_ref, xo_ref, ho_ref):
    xn = x_ref[...] + _rms(y_ref[...].astype(F32), gp_ref[...])
    xo_ref[...] = xn
    ho_ref[...] = _rms(xn, gn_ref[...]).astype(ho_ref.dtype)


def _residual_kernel(x_ref, y_ref, gp_ref, xo_ref):
    xo_ref[...] = x_ref[...] + _rms(y_ref[...].astype(F32), gp_ref[...])


def residual_norm(x, y, g_post, g_next):
    t, d = x.shape
    row = pl.BlockSpec((TM_NORM, d), lambda i: (i, 0))
    vec = pl.BlockSpec((1, d), lambda i: (0, 0))
    common = dict(grid=(t // TM_NORM,), semantics=("parallel",), temp_bytes=2 * _nbytes((TM_NORM, d), F32))
    if g_next is None:
        return _pallas(_residual_kernel, name="residual", in_specs=[row, row, vec],
                       operands=(x, y, g_post.reshape(1, d)), out_specs=row,
                       out_shape=jax.ShapeDtypeStruct((t, d), F32), **common), None
    return _pallas(_residual_norm_kernel, name="residual_norm", in_specs=[row, row, vec, vec],
                   operands=(x, y, g_post.reshape(1, d), g_next.reshape(1, d)), out_specs=(row, row),
                   out_shape=(jax.ShapeDtypeStruct((t, d), F32), jax.ShapeDtypeStruct((t, d), BF16)),
                   **common)


ROPE_HALF = MLA_ROPE // 2


def _rope_cols(x, cos, sin):
    return x * cos + pltpu.roll(x, LANES // 2, 1) * sin


def _pair_rope_order(n_heads):
    src = np.arange(n_heads * MLA_ROPE).reshape(n_heads // 2, 2, 2, ROPE_HALF)
    return src.transpose(0, 2, 1, 3).reshape(-1)


def _in_proj_main_kernel(h_ref, w_ref, o_ref):
    o_ref[...] = lax.dot_general(h_ref[...], w_ref[...], (((1,), (1,)), ((), ())),
                                 preferred_element_type=F32).astype(o_ref.dtype)


def _in_proj_rope_kernel(h_ref, w_ref, cos_ref, sin_ref, o_ref):
    j = pl.program_id(1)
    acc = lax.dot_general(h_ref[...], w_ref[...], (((1,), (1,)), ((), ())), preferred_element_type=F32)

    @pl.when(j == 0)
    def _():
        lo = COL_KR - COL_CKV
        o_ref[:, :lo] = acc[:, :lo].astype(o_ref.dtype)
        for c in range(lo // LANES, lo // LANES + 2):
            sl = slice(c * LANES, (c + 1) * LANES)
            o_ref[:, sl] = _rope_cols(acc[:, sl], cos_ref[...], sin_ref[...]).astype(o_ref.dtype)
        o_ref[:, lo + 2 * LANES:] = acc[:, lo + 2 * LANES:].astype(o_ref.dtype)

    @pl.when(j == 1)
    def _():
        q_cols = (RET_HEADS * RET_QK) // LANES
        for c in range(IN_TILE // LANES):
            sl = slice(c * LANES, (c + 1) * LANES)
            r = _rope_cols(acc[:, sl], cos_ref[...], sin_ref[...])
            if c < q_cols:
                r = r * (RET_QK ** -0.5)
            o_ref[:, sl] = r.astype(o_ref.dtype)


def in_proj(h, w_t, cos_t, sin_t, seq):
    t, d = h.shape
    pos_blocks = seq // TM
    tab = pl.BlockSpec((TM, LANES), lambda i, j: (i % pos_blocks, 0))
    rope_tile0 = ROPE_BASE // IN_TILE
    rope_w = IN_WIDTH_PADDED - ROPE_BASE
    common = dict(semantics=("parallel", "arbitrary"), temp_bytes=_nbytes((TM, IN_TILE), F32),
                  out_specs=pl.BlockSpec((TM, IN_TILE), lambda i, j: (i, j)))
    main = _pallas(
        _in_proj_main_kernel, name="in_proj_main", grid=(t // TM, rope_tile0),
        in_specs=[pl.BlockSpec((TM, d), lambda i, j: (i, 0)),
                  pl.BlockSpec((IN_TILE, d), lambda i, j: (j, 0))],
        operands=(h, w_t), out_shape=jax.ShapeDtypeStruct((t, ROPE_BASE), BF16), **common)
    rope = _pallas(
        _in_proj_rope_kernel, name="in_proj_rope", grid=(t // TM, rope_w // IN_TILE),
        in_specs=[pl.BlockSpec((TM, d), lambda i, j: (i, 0)),
                  pl.BlockSpec((IN_TILE, d), lambda i, j: (rope_tile0 + j, 0)),
                  tab, tab],
        operands=(h, w_t, cos_t, sin_t), out_shape=jax.ShapeDtypeStruct((t, rope_w), BF16), **common)
    return main, rope


def _sgu_kernel(u_ref, v_ref, g_ref, b_ref, w_ref, bias_ref, o_ref):
    row = lax.broadcasted_iota(jnp.int32, (SGU_BLOCK, SGU_BLOCK), 0)
    col = lax.broadcasted_iota(jnp.int32, (SGU_BLOCK, SGU_BLOCK), 1)
    keep = (col // CHUNK) <= (row // CHUNK)
    for g in range(SGU_HEADS):
        sl = slice(g * LANES, (g + 1) * LANES)
        wg = jnp.where(keep, w_ref[g], 0.0).astype(BF16)
        vg = v_ref[:, sl].astype(F32)
        mu = jnp.mean(vg, axis=-1, keepdims=True)
        vc = vg - mu
        y = vc * lax.rsqrt(jnp.mean(vc * vc, axis=-1, keepdims=True) + EPS) * g_ref[:, sl] + b_ref[:, sl]
        yb = y.astype(BF16)
        for n in range(TB_SGU // SGU_BLOCK):
            rs = slice(n * SGU_BLOCK, (n + 1) * SGU_BLOCK)
            mixed = jnp.dot(wg, yb[rs, :], preferred_element_type=F32) + bias_ref[:, sl]
            o_ref[rs, sl] = (u_ref[rs, sl].astype(F32) * mixed).astype(o_ref.dtype)


def sgu_mixer(proj, ln_g, ln_b, w_s, bias_full):
    t = proj.shape[0]
    vec = pl.BlockSpec((1, SGU_WIDTH), lambda i: (0, 0))
    return _pallas(
        _sgu_kernel, name="sgu_mixer", grid=(t // TB_SGU,), semantics=("parallel",),
        in_specs=[pl.BlockSpec((TB_SGU, SGU_WIDTH), lambda i: (i, COL_U // SGU_WIDTH)),
                  pl.BlockSpec((TB_SGU, SGU_WIDTH), lambda i: (i, COL_V // SGU_WIDTH)),
                  vec, vec,
                  pl.BlockSpec((SGU_HEADS, SGU_BLOCK, SGU_BLOCK), lambda i: (0, 0, 0)),
                  pl.BlockSpec((SGU_BLOCK, SGU_WIDTH), lambda i: (0, 0))],
        operands=(proj, proj, ln_g.reshape(1, -1), ln_b.reshape(1, -1), w_s, bias_full),
        out_specs=pl.BlockSpec((TB_SGU, SGU_WIDTH), lambda i: (i, 0)),
        out_shape=jax.ShapeDtypeStruct((t, SGU_WIDTH), BF16),
        temp_bytes=_nbytes((TB_SGU, SGU_WIDTH), F32))


Q_SCALE = (MLA_NOPE + MLA_ROPE) ** -0.5 * math.log2(math.e)
Q_ROPE_TILE = (MLA_HEADS * MLA_NOPE) // IN_TILE


def _q_proj_kernel(c_ref, g_ref, w_ref, cos_ref, sin_ref, o_ref, cn_ref):
    j = pl.program_id(1)

    @pl.when(j == 0)
    def _():
        cn_ref[...] = _rms(c_ref[...].astype(F32), g_ref[...]).astype(cn_ref.dtype)

    acc = jnp.dot(cn_ref[...], w_ref[...], preferred_element_type=F32) * Q_SCALE

    @pl.when(j < Q_ROPE_TILE)
    def _():
        o_ref[...] = acc.astype(o_ref.dtype)

    @pl.when(j >= Q_ROPE_TILE)
    def _():
        for c in range(IN_TILE // LANES):
            sl = slice(c * LANES, (c + 1) * LANES)
            o_ref[:, sl] = _rope_cols(acc[:, sl], cos_ref[...], sin_ref[...]).astype(o_ref.dtype)


def q_proj(proj, g, w, cos_t, sin_t, seq):
    t = proj.shape[0]
    n = w.shape[1]
    pos_blocks = seq // TM_QKV
    tab = pl.BlockSpec((TM_QKV, LANES), lambda i, j: (i % pos_blocks, 0))
    return _pallas(
        _q_proj_kernel, name="q_proj", grid=(t // TM_QKV, n // IN_TILE), semantics=("parallel", "arbitrary"),
        in_specs=[pl.BlockSpec((TM_QKV, MLA_Q_RANK), lambda i, j: (i, COL_CQ // MLA_Q_RANK)),
                  pl.BlockSpec((1, MLA_Q_RANK), lambda i, j: (0, 0)),
                  pl.BlockSpec((MLA_Q_RANK, IN_TILE), lambda i, j: (0, j)),
                  tab, tab],
        operands=(proj, g.reshape(1, -1), w, cos_t, sin_t),
        out_specs=pl.BlockSpec((TM_QKV, IN_TILE), lambda i, j: (i, j)),
        out_shape=jax.ShapeDtypeStruct((t, n), BF16),
        scratch_shapes=[pltpu.VMEM((TM_QKV, MLA_Q_RANK), BF16)],
        temp_bytes=_nbytes((TM_QKV, IN_TILE), F32))


def _kv_proj_kernel(c_ref, g_ref, w_ref, o_ref, cn_ref):
    @pl.when(pl.program_id(1) == 0)
    def _():
        cn_ref[...] = _rms(c_ref[...].astype(F32), g_ref[...]).astype(cn_ref.dtype)

    o_ref[...] = jnp.dot(cn_ref[...], w_ref[...].astype(BF16), preferred_element_type=F32).astype(o_ref.dtype)


def kv_proj(proj_rope, g, w, layer):
    t = proj_rope.shape[0]
    n = w.shape[2]
    return _pallas(
        _kv_proj_kernel, name="kv_proj", grid=(t // TM_QKV, n // IN_TILE), semantics=("parallel", "arbitrary"),
        in_specs=[pl.BlockSpec((TM_QKV, MLA_KV_RANK), lambda i, j: (i, (COL_CKV - ROPE_BASE) // MLA_KV_RANK)),
                  pl.BlockSpec((1, MLA_KV_RANK), lambda i, j: (0, 0)),
                  pl.BlockSpec((None, MLA_KV_RANK, IN_TILE), lambda i, j: (layer, 0, j))],
        operands=(proj_rope, g.reshape(1, -1), w),
        out_specs=pl.BlockSpec((TM_QKV, IN_TILE), lambda i, j: (i, j)),
        out_shape=jax.ShapeDtypeStruct((t, n), BF16),
        scratch_shapes=[pltpu.VMEM((TM_QKV, MLA_KV_RANK), BF16)],
        temp_bytes=_nbytes((TM_QKV, IN_TILE), F32) + _nbytes((MLA_KV_RANK, IN_TILE), BF16))


HEAD_W = MLA_NOPE + MLA_V


def _attn_tiles(qs, kv_ref, kr_ref, base, ntiles, mask_last, states):
    states = list(states)
    ones = jnp.ones((TQ, LANES), BF16)
    for t in range(ntiles):
        st = pl.multiple_of(base + t * TQ, TQ)
        for hd in range(ATTN_HEADS):
            m, l, acc = states[hd]
            kn = kv_ref[pl.ds(st, TQ), hd * HEAD_W:hd * HEAD_W + MLA_NOPE]
            v = kv_ref[pl.ds(st, TQ), hd * HEAD_W + MLA_NOPE:(hd + 1) * HEAD_W]
            par = hd % 2
            k = jnp.concatenate([kn, kr_ref[pl.ds(st, TQ), par * LANES:(par + 1) * LANES]], axis=1)
            s = lax.dot_general(qs[hd], k, (((1,), (1,)), ((), ())), preferred_element_type=F32)
            if mask_last and t == ntiles - 1:
                row = lax.broadcasted_iota(jnp.int32, (TQ, TQ), 0)
                col = lax.broadcasted_iota(jnp.int32, (TQ, TQ), 1)
                s = jnp.where((col // CHUNK) <= (row // CHUNK), s, -jnp.inf)
            m_next = jnp.maximum(m, jnp.max(s, axis=1, keepdims=True))
            alpha = jnp.exp2(m - m_next)
            p = jnp.exp2(s - jnp.tile(m_next, (1, TQ // LANES))).astype(BF16)
            pv = jnp.dot(p, jnp.concatenate([v, ones], axis=1), preferred_element_type=F32)
            states[hd] = (m_next, alpha * l + pv[:, LANES:], alpha * acc + pv[:, :LANES])
    return states


def _attn_kernel(qn_ref, qr_ref, kv_ref, kr_ref, *rest, cast_from, n_cast):
    w_refs, o_ref = rest[:n_cast], rest[n_cast]
    wb_refs = rest[n_cast + 1:2 * n_cast + 1]
    m_ref, l_ref, acc_ref = rest[2 * n_cast + 1:]
    qi = pl.program_id(2)

    @pl.when(qi >= cast_from)
    def _():
        for w_ref, wb_ref in zip(w_refs, wb_refs):
            wb_ref[...] = w_ref[...].astype(wb_ref.dtype)

    qs = [jnp.concatenate([qn_ref[:, hd * LANES:(hd + 1) * LANES],
                           qr_ref[:, (hd // 2) * LANES:(hd // 2 + 1) * LANES]], axis=1)
          for hd in range(ATTN_HEADS)]

    m_ref[...] = jnp.full_like(m_ref, -jnp.inf)
    l_ref[...] = jnp.zeros_like(l_ref)
    acc_ref[...] = jnp.zeros_like(acc_ref)

    def load_states():
        return [(m_ref[hd], l_ref[hd], acc_ref[hd]) for hd in range(ATTN_HEADS)]

    def body(g, carry):
        base = pl.multiple_of(g * (ATTN_GROUP * TQ), ATTN_GROUP * TQ)
        states = _attn_tiles(qs, kv_ref, kr_ref, base, ATTN_GROUP, False, load_states())
        for hd in range(ATTN_HEADS):
            m_ref[hd], l_ref[hd], acc_ref[hd] = states[hd]
        return carry

    lax.fori_loop(0, qi // ATTN_GROUP, body, 0)

    base = pl.multiple_of((qi // ATTN_GROUP) * (ATTN_GROUP * TQ), ATTN_GROUP * TQ)
    for r in range(ATTN_GROUP):
        @pl.when(qi % ATTN_GROUP == r)
        def _():
            states = _attn_tiles(qs, kv_ref, kr_ref, base, r + 1, True, load_states())
            for hd in range(ATTN_HEADS):
                _, l, acc = states[hd]
                o_ref[:, hd * LANES:(hd + 1) * LANES] = (acc / l).astype(o_ref.dtype)


def mla_attention(q, kv, proj_rope, weights, layer, batch, seq):
    t = q.shape[0]
    nq = seq // TQ
    groups = MLA_HEADS // ATTN_HEADS
    cast_from = 0
    casts = nq - cast_from
    steps = batch * groups * casts
    slab_rows = [w.shape[1] // steps for w in weights]
    assert all(r * steps == w.shape[1] and r % BF16_SUBLANES == 0 for r, w in zip(slab_rows, weights))

    def slab(b, g, i):
        return (b * groups + g) * casts + jnp.maximum(i - cast_from, 0)
    nope_w = ATTN_HEADS * MLA_NOPE
    rope_w = ATTN_HEADS * MLA_ROPE
    kr_w = 2 * LANES
    kr_block = (COL_KR - ROPE_BASE) // kr_w
    rope_block0 = (MLA_HEADS * MLA_NOPE) // rope_w
    return _pallas(
        functools.partial(_attn_kernel, cast_from=cast_from, n_cast=len(weights)),
        name="mla_attention", grid=(batch, groups, nq), semantics=("parallel", "parallel", "arbitrary"),
        in_specs=[pl.BlockSpec((TQ, nope_w), lambda b, p, i: (b * nq + i, p)),
                  pl.BlockSpec((TQ, rope_w), lambda b, p, i: (b * nq + i, rope_block0 + p)),
                  pl.BlockSpec((seq, ATTN_HEADS * HEAD_W), lambda b, p, i: (b, p)),
                  pl.BlockSpec((seq, kr_w), lambda b, p, i: (b, kr_block)),
                  *[pl.BlockSpec((None, r, w.shape[2]), lambda b, p, i: (layer, slab(b, p, i), 0))
                    for r, w in zip(slab_rows, weights)]],
        operands=(q, q, kv, proj_rope, *weights),
        out_specs=(pl.BlockSpec((TQ, nope_w), lambda b, p, i: (b * nq + i, p)),
                   *[pl.BlockSpec((r, w.shape[2]), lambda b, p, i: (slab(b, p, i), 0))
                     for r, w in zip(slab_rows, weights)]),
        out_shape=(jax.ShapeDtypeStruct((t, MLA_WIDTH), BF16),
                   *[jax.ShapeDtypeStruct(w.shape[1:], BF16) for w in weights]),
        scratch_shapes=[pltpu.VMEM((ATTN_HEADS, TQ, LANES), F32)] * 3,
        temp_bytes=ATTN_HEADS * (_nbytes((TQ, TQ), F32) + _nbytes((TQ, TQ), BF16)))


def _ret_kernel(q_ref, k_ref, v_ref, g_ref, d_ref, qd_ref, kd_ref, cd_ref, o_ref, s_ref):
    @pl.when(pl.program_id(1) == 0)
    def _():
        s_ref[...] = jnp.zeros_like(s_ref)

    lane = lax.broadcasted_iota(jnp.int32, (RET_L, LANES), 1)
    for p in range(RET_HEADS // 2):
        ls = slice(p * LANES, (p + 1) * LANES)
        q = q_ref[:, ls]
        k = k_ref[:, ls]
        qf = q.astype(F32)
        kdec = (k.astype(F32) * kd_ref[:, ls]).astype(BF16)
        for hh in range(2):
            head = 2 * p + hh
            own = (lane // ROPE_HALF) % 2 == hh
            vs = slice(head * RET_V, (head + 1) * RET_V)
            v = v_ref[:, vs]
            qm = jnp.where(own, q, jnp.zeros_like(q))
            s = lax.dot_general(qm, k, (((1,), (1,)), ((), ())), preferred_element_type=F32)
            sd = (s * d_ref[head]).astype(BF16)
            qx = jnp.where(own, qf * qd_ref[:, ls], 0.0).astype(BF16)
            state = s_ref[head]
            o = (jnp.dot(sd, v, preferred_element_type=F32)
                 + jnp.dot(qx, state.astype(BF16), preferred_element_type=F32))
            s_ref[head] = cd_ref[head] * state + lax.dot_general(
                kdec, v, (((0,), (0,)), ((), ())), preferred_element_type=F32)
            o = o * lax.rsqrt(jnp.mean(o * o, axis=-1, keepdims=True) + EPS)
            gate = g_ref[:, vs].astype(F32)
            o_ref[:, vs] = (gate * jax.nn.sigmoid(gate) * o).astype(o_ref.dtype)


def retention_mixer(proj, proj_rope, d_tab, qd_tab, kd_tab, cd_tab, batch, seq):
    t = proj.shape[0]
    nsc = seq // RET_L
    qk_w = RET_HEADS * RET_QK
    full = lambda b, c: (0, 0)
    full3 = lambda b, c: (0, 0, 0)
    return _pallas(
        _ret_kernel, name="retention_mixer", grid=(batch, nsc), semantics=("parallel", "arbitrary"),
        in_specs=[pl.BlockSpec((RET_L, qk_w), lambda b, c: (b * nsc + c, (COL_RQ - ROPE_BASE) // qk_w)),
                  pl.BlockSpec((RET_L, qk_w), lambda b, c: (b * nsc + c, (COL_RK - ROPE_BASE) // qk_w)),
                  pl.BlockSpec((RET_L, RET_WIDTH), lambda b, c: (b * nsc + c, COL_RV // RET_WIDTH)),
                  pl.BlockSpec((RET_L, RET_WIDTH), lambda b, c: (b * nsc + c, COL_RG // RET_WIDTH)),
                  pl.BlockSpec((RET_HEADS, RET_L, RET_L), full3),
                  pl.BlockSpec((RET_L, qk_w), full),
                  pl.BlockSpec((RET_L, qk_w), full),
                  pl.BlockSpec((RET_HEADS, LANES, RET_V), full3)],
        operands=(proj_rope, proj_rope, proj, proj, d_tab, qd_tab, kd_tab, cd_tab),
        out_specs=pl.BlockSpec((RET_L, RET_WIDTH), lambda b, c: (b * nsc + c, 0)),
        out_shape=jax.ShapeDtypeStruct((t, RET_WIDTH), BF16),
        scratch_shapes=[pltpu.VMEM((RET_HEADS, LANES, RET_V), F32)],
        temp_bytes=RET_HEADS * _nbytes((RET_L, RET_L), F32))


def retention_tables():
    log_gamma = jnp.log1p(-jnp.exp2(-5.0 - jnp.arange(RET_HEADS, dtype=F32)))
    idx = jnp.arange(RET_L, dtype=F32)
    dist = jnp.abs(idx[:, None] - idx[None, :])
    chunk = jnp.arange(RET_L) // CHUNK
    visible = chunk[None, :] <= chunk[:, None]
    d_tab = jnp.where(visible[None], jnp.exp(log_gamma[:, None, None] * dist[None]), 0.0)
    col_head = _pair_rope_order(RET_HEADS) // RET_QK
    lg_lane = log_gamma[col_head]
    qd_tab = jnp.exp(lg_lane[None, :] * (idx[:, None] + 1.0))
    kd_tab = jnp.exp(lg_lane[None, :] * (RET_L - 1.0 - idx[:, None]))
    cd = jnp.exp(log_gamma * RET_L)
    cd_tab = jnp.broadcast_to(cd[:, None, None], (RET_HEADS, LANES, RET_V))
    return d_tab, qd_tab, kd_tab, cd_tab


def _out_proj_kernel(a_ref, b_ref, c_ref, w_ref, o_ref):
    ka = a_ref.shape[1]
    kb = b_ref.shape[1]
    acc = jnp.dot(a_ref[...], w_ref[:ka, :], preferred_element_type=F32)
    acc = acc + jnp.dot(b_ref[...], w_ref[ka:ka + kb, :], preferred_element_type=F32)
    acc = acc + jnp.dot(c_ref[...], w_ref[ka + kb:, :], preferred_element_type=F32)
    o_ref[...] = acc.astype(o_ref.dtype)


def out_proj(a, b, c, w):
    t = a.shape[0]
    k, n = w.shape
    return _pallas(
        _out_proj_kernel, name="out_proj", grid=(t // TM, n // TN_OUT), semantics=("parallel", "arbitrary"),
        in_specs=[pl.BlockSpec((TM, a.shape[1]), lambda i, j: (i, 0)),
                  pl.BlockSpec((TM, b.shape[1]), lambda i, j: (i, 0)),
                  pl.BlockSpec((TM, c.shape[1]), lambda i, j: (i, 0)),
                  pl.BlockSpec((k, TN_OUT), lambda i, j: (0, j))],
        operands=(a, b, c, w),
        out_specs=pl.BlockSpec((TM, TN_OUT), lambda i, j: (i, j)),
        out_shape=jax.ShapeDtypeStruct((t, n), BF16),
        temp_bytes=_nbytes((TM, TN_OUT), F32))


def _ffn_kernel(h_ref, wu_ref, wd_ref, o_ref):
    @pl.when(pl.program_id(1) == 0)
    def _():
        o_ref[...] = jnp.zeros_like(o_ref)

    a = jnp.dot(h_ref[...], wu_ref[...], preferred_element_type=F32)
    a = jnp.square(jnp.maximum(a, 0.0)).astype(BF16)
    o_ref[...] += jnp.dot(a, wd_ref[...], preferred_element_type=F32)


def ffn(h, wu_b, wd_b):
    t, d = h.shape
    ff = wu_b.shape[1]
    return _pallas(
        _ffn_kernel, name="ffn", grid=(t // TM_FFN, ff // TF_FFN), semantics=("parallel", "arbitrary"),
        in_specs=[pl.BlockSpec((TM_FFN, d), lambda i, j: (i, 0)),
                  pl.BlockSpec((d, TF_FFN), lambda i, j: (0, j)),
                  pl.BlockSpec((TF_FFN, d), lambda i, j: (j, 0))],
        operands=(h, wu_b, wd_b),
        out_specs=pl.BlockSpec((TM_FFN, d), lambda i, j: (i, 0)),
        out_shape=jax.ShapeDtypeStruct((t, d), F32),
        temp_bytes=_nbytes((TM_FFN, TF_FFN), F32) + _nbytes((TM_FFN, TF_FFN), BF16))


_SRC_SIZES = (SGU_WIDTH, SGU_WIDTH, MLA_Q_RANK, MLA_KV_RANK, MLA_ROPE,
              RET_HEADS * RET_QK, RET_HEADS * RET_QK, RET_WIDTH, RET_WIDTH)
(_SRC_U, _SRC_V, _SRC_CQ, _SRC_CKV, _SRC_KR, _SRC_RQ, _SRC_RK, _SRC_RV, _SRC_RG,
 IN_WIDTH) = [int(o) for o in np.concatenate([[0], np.cumsum(_SRC_SIZES)])]
_W_IN_MOVES = (
    (COL_U, _SRC_U, 3 * IN_TILE),
    (COL_RV, _SRC_RV, RET_WIDTH),
    (COL_RG, _SRC_RG, RET_WIDTH),
    (COL_CKV, _SRC_CKV, MLA_KV_RANK),
    (COL_KR, _SRC_KR, ROPE_HALF),
    (COL_KR + LANES // 2, _SRC_KR + ROPE_HALF, ROPE_HALF),
    (COL_KR + LANES + ROPE_HALF, _SRC_KR, ROPE_HALF),
    (COL_KR + LANES + LANES // 2 + ROPE_HALF, _SRC_KR + ROPE_HALF, ROPE_HALF),
    *[(dst0 + i * ROPE_HALF, src0 + int(s0), ROPE_HALF)
      for dst0, src0 in ((COL_RQ, _SRC_RQ), (COL_RK, _SRC_RK))
      for i, s0 in enumerate(_pair_rope_order(RET_HEADS)[::ROPE_HALF])],
)


def _w_in_prep_kernel(w_ref, o_ref):
    o_ref[COL_KR:COL_RQ, :] = jnp.zeros((COL_RQ - COL_KR, o_ref.shape[1]), o_ref.dtype)
    for dst, src, width in _W_IN_MOVES:
        o_ref[dst:dst + width, :] = w_ref[src:src + width, :].astype(o_ref.dtype)


def prep_w_in(w_t, layer):
    _, n, d = w_t.shape
    return _pallas(
        _w_in_prep_kernel, name="prep_w_in", grid=(d // TK_PREP,), semantics=("parallel",),
        in_specs=[pl.BlockSpec((None, n, TK_PREP), lambda i: (layer, 0, i))],
        operands=(w_t,),
        out_specs=pl.BlockSpec((IN_WIDTH_PADDED, TK_PREP), lambda i: (0, i)),
        out_shape=jax.ShapeDtypeStruct((IN_WIDTH_PADDED, d), BF16))


def _prep_wq(w):
    k = w.shape[0]
    w3 = w.reshape(k, MLA_HEADS, MLA_NOPE + MLA_ROPE)
    nope = w3[:, :, :MLA_NOPE].reshape(k, MLA_HEADS * MLA_NOPE)
    rope = w3[:, :, MLA_NOPE:].reshape(k, MLA_HEADS * MLA_ROPE)[:, _pair_rope_order(MLA_HEADS)]
    return jnp.concatenate([nope, rope], axis=1).astype(BF16)


def _rope_lane_tables(seq):
    inv_freq = 1.0 / (ROPE_THETA ** (jnp.arange(0, MLA_ROPE, 2, dtype=F32) / MLA_ROPE))
    ang = jnp.arange(seq, dtype=F32)[:, None] * inv_freq[None, :]
    cos, sin = jnp.cos(ang), jnp.sin(ang)
    cos_t = jnp.concatenate([cos, cos, cos, cos], axis=1)
    sin_t = jnp.concatenate([-sin, -sin, sin, sin], axis=1)
    return cos_t, sin_t


def kernel(x, norm_mix_pre, norm_mix_post, norm_ffn_pre, norm_ffn_post, w_in, sgu_ln_g, sgu_ln_b, sgu_w, sgu_b, mla_q_norm, mla_wq_b, mla_kv_norm, mla_wkv_b, w_out, w_up, w_down):
    batch, seq, d = x.shape
    depth = w_in.shape[0]
    cos_t, sin_t = _rope_lane_tables(seq)
    d_tab, qd_tab, kd_tab, cd_tab = retention_tables()
    w_in_t = jnp.swapaxes(w_in, 1, 2)

    xf = x.reshape(batch * seq, d)
    h = rmsnorm_rows(xf, norm_mix_pre[0])
    for l in range(depth):
        proj, proj_rope = in_proj(h, prep_w_in(w_in_t, l), cos_t, sin_t, seq)
        bias_full = jnp.repeat(sgu_b[l].T, SGU_WIDTH // SGU_HEADS, axis=1)
        out_a = sgu_mixer(proj, sgu_ln_g[l], sgu_ln_b[l], sgu_w[l], bias_full)
        q = q_proj(proj, mla_q_norm[l], _prep_wq(mla_wq_b[l]), cos_t, sin_t, seq)
        kv = kv_proj(proj_rope, mla_kv_norm[l], mla_wkv_b, l)
        out_b, wu_b, wd_b, wo_b = mla_attention(q, kv, proj_rope, (w_up, w_down, w_out), l, batch, seq)
        out_c = retention_mixer(proj, proj_rope, d_tab, qd_tab, kd_tab, cd_tab, batch, seq)
        mixed = out_proj(out_a, out_b, out_c, wo_b)
        xf, h2 = residual_norm(xf, mixed, norm_mix_post[l], norm_ffn_pre[l])
        f = ffn(h2, wu_b, wd_b)
        g_next = norm_mix_pre[l + 1] if l + 1 < depth else None
        xf, h = residual_norm(xf, f, norm_ffn_post[l], g_next)
    return xf.reshape(batch, seq, d)
```

```python
import functools
import math

import numpy as np
import jax
import jax.numpy as jnp
from jax import lax
from jax.experimental import pallas as pl
from jax.experimental.pallas import tpu as pltpu

F32 = jnp.float32
BF16 = jnp.bfloat16

CHUNK = 64
EPS = 1e-6
ROPE_THETA = 10000.0
SGU_BLOCK = 128
SGU_WIDTH = 1024
SGU_HEADS = 8
MLA_HEADS = 16
MLA_NOPE = 128
MLA_ROPE = 64
MLA_V = 128
MLA_Q_RANK = 1024
MLA_KV_RANK = 512
MLA_WIDTH = MLA_HEADS * MLA_V
RET_HEADS = 8
RET_QK = 64
RET_V = 128
RET_WIDTH = RET_HEADS * RET_V

LANES = 128
BF16_SUBLANES = 16
VMEM_CAP = 64 * 1024 * 1024
VMEM_RESERVED = 2 * 1024 * 1024
VMEM_INTERNAL = 6 * 1024 * 1024
PIPELINE_BUFFERS = 2

IN_TILE = 1024
COL_U = 0 * IN_TILE
COL_V = 1 * IN_TILE
COL_CQ = 2 * IN_TILE
COL_RV = 3 * IN_TILE
COL_RG = 4 * IN_TILE
COL_CKV = 5 * IN_TILE
COL_KR = COL_CKV + MLA_KV_RANK
COL_RQ = 6 * IN_TILE
COL_RK = COL_RQ + RET_HEADS * RET_QK
IN_WIDTH_PADDED = 7 * IN_TILE
ROPE_BASE = COL_CKV

TM = 1024
TM_QKV = 2048
NORM_TILES = (512, 256)
TM_FFN = 512
TF_FFN = 1024
TN_OUT = 1024
TK_PREP = 256
TB_SGU = 512
TQ = 512
ATTN_GROUP = 4
ATTN_HEADS = 4
RET_L = 256


def _nbytes(shape, dtype):
    return math.prod(1 if s is None else s for s in shape) * jnp.dtype(dtype).itemsize


def _pallas(kernel, *, name, grid, semantics, in_specs, operands, out_specs, out_shape,
            scratch_shapes=(), temp_bytes=0):
    outs = out_shape if isinstance(out_shape, (tuple, list)) else (out_shape,)
    ospecs = out_specs if isinstance(out_specs, (tuple, list)) else (out_specs,)
    pipelined = sum(_nbytes(s.block_shape, a.dtype) for s, a in zip(in_specs, operands))
    pipelined += sum(_nbytes(s.block_shape, o.dtype) for s, o in zip(ospecs, outs))
    scratch = sum(_nbytes(s.shape, s.dtype) for s in scratch_shapes)
    need = PIPELINE_BUFFERS * pipelined + scratch + temp_bytes + VMEM_INTERNAL
    params = pltpu.CompilerParams(dimension_semantics=semantics,
                                  vmem_limit_bytes=int(min(need, VMEM_CAP - VMEM_RESERVED)))
    return pl.pallas_call(kernel, out_shape=out_shape, grid=grid, in_specs=in_specs, out_specs=out_specs,
                          scratch_shapes=scratch_shapes, compiler_params=params, name=name)(*operands)


def _rms(x, g):
    return x * lax.rsqrt(jnp.mean(x * x, axis=-1, keepdims=True) + EPS) * g


def _rmsnorm_kernel(x_ref, g_ref, o_ref):
    o_ref[...] = _rms(x_ref[...], g_ref[...]).astype(o_ref.dtype)


def _norm_tile(d, row_dtypes):
    for tile in NORM_TILES:
        need = PIPELINE_BUFFERS * sum(_nbytes((tile, d), dt) for dt in row_dtypes)
        if need + _nbytes((tile, d), F32) + VMEM_INTERNAL <= VMEM_CAP - VMEM_RESERVED:
            return tile
    raise ValueError("no row tile fits VMEM")


def rmsnorm_rows(x, g):
    t, d = x.shape
    tm = _norm_tile(d, (x.dtype, BF16))
    return _pallas(
        _rmsnorm_kernel, name="rmsnorm_rows", grid=(t // tm,), semantics=("parallel",),
        in_specs=[pl.BlockSpec((tm, d), lambda i: (i, 0)),
                  pl.BlockSpec((1, d), lambda i: (0, 0))],
        operands=(x, g.reshape(1, d)),
        out_specs=pl.BlockSpec((tm, d), lambda i: (i, 0)),
        out_shape=jax.ShapeDtypeStruct((t, d), BF16),
        temp_bytes=_nbytes((tm, d), F32))


def _residual_norm_kernel(x_ref, y_ref, gp_ref, gn_ref, xo_ref, ho_ref):
    xn = x_ref[...] + _rms(y_ref[...].astype(F32), gp_ref[...])
    xo_ref[...] = xn
    ho_ref[...] = _rms(xn, gn_ref[...]).astype(ho_ref.dtype)


def _residual_kernel(x_ref, y_ref, gp_ref, xo_ref):
    xo_ref[...] = x_ref[...] + _rms(y_ref[...].astype(F32), gp_ref[...])


def residual_norm(x, y, g_post, g_next):
    t, d = x.shape
    out_dtypes = (F32,) if g_next is None else (F32, BF16)
    tm = _norm_tile(d, (x.dtype, y.dtype) + out_dtypes)
    row = pl.BlockSpec((tm, d), lambda i: (i, 0))
    vec = pl.BlockSpec((1, d), lambda i: (0, 0))
    common = dict(grid=(t // tm,), semantics=("parallel",), temp_bytes=_nbytes((tm, d), F32))
    if g_next is None:
        return _pallas(_residual_kernel, name="residual", in_specs=[row, row, vec],
                       operands=(x, y, g_post.reshape(1, d)), out_specs=row,
                       out_shape=jax.ShapeDtypeStruct((t, d), F32), **common), None
    return _pallas(_residual_norm_kernel, name="residual_norm", in_specs=[row, row, vec, vec],
                   operands=(x, y, g_post.reshape(1, d), g_next.reshape(1, d)), out_specs=(row, row),
                   out_shape=(jax.ShapeDtypeStruct((t, d), F32), jax.ShapeDtypeStruct((t, d), BF16)),
                   **common)


ROPE_HALF = MLA_ROPE // 2


def _rope_cols(x, cos, sin):
    return x * cos + pltpu.roll(x, LANES // 2, 1) * sin


def _pair_rope_order(n_heads):
    src = np.arange(n_heads * MLA_ROPE).reshape(n_heads // 2, 2, 2, ROPE_HALF)
    return src.transpose(0, 2, 1, 3).reshape(-1)


def _in_proj_main_kernel(h_ref, w_ref, o_ref):
    o_ref[...] = lax.dot_general(h_ref[...], w_ref[...], (((1,), (1,)), ((), ())),
                                 preferred_element_type=F32).astype(o_ref.dtype)


def _in_proj_rope_kernel(h_ref, w_ref, cos_ref, sin_ref, o_ref):
    j = pl.program_id(1)
    acc = lax.dot_general(h_ref[...], w_ref[...], (((1,), (1,)), ((), ())), preferred_element_type=F32)

    @pl.when(j == 0)
    def _():
        lo = COL_KR - COL_CKV
        o_ref[:, :lo] = acc[:, :lo].astype(o_ref.dtype)
        for c in range(lo // LANES, lo // LANES + 2):
            sl = slice(c * LANES, (c + 1) * LANES)
            o_ref[:, sl] = _rope_cols(acc[:, sl], cos_ref[...], sin_ref[...]).astype(o_ref.dtype)
        o_ref[:, lo + 2 * LANES:] = acc[:, lo + 2 * LANES:].astype(o_ref.dtype)

    @pl.when(j == 1)
    def _():
        q_cols = (RET_HEADS * RET_QK) // LANES
        for c in range(IN_TILE // LANES):
            sl = slice(c * LANES, (c + 1) * LANES)
            r = _rope_cols(acc[:, sl], cos_ref[...], sin_ref[...])
            if c < q_cols:
                r = r * (RET_QK ** -0.5)
            o_ref[:, sl] = r.astype(o_ref.dtype)


def in_proj(h, w_t, cos_t, sin_t, seq):
    t, d = h.shape
    pos_blocks = seq // TM
    tab = pl.BlockSpec((TM, LANES), lambda i, j: (i % pos_blocks, 0))
    rope_tile0 = ROPE_BASE // IN_TILE
    rope_w = IN_WIDTH_PADDED - ROPE_BASE
    common = dict(semantics=("parallel", "arbitrary"), temp_bytes=_nbytes((TM, IN_TILE), F32),
                  out_specs=pl.BlockSpec((TM, IN_TILE), lambda i, j: (i, j)))
    main = _pallas(
        _in_proj_main_kernel, name="in_proj_main", grid=(t // TM, rope_tile0),
        in_specs=[pl.BlockSpec((TM, d), lambda i, j: (i, 0)),
                  pl.BlockSpec((IN_TILE, d), lambda i, j: (j, 0))],
        operands=(h, w_t), out_shape=jax.ShapeDtypeStruct((t, ROPE_BASE), BF16), **common)
    rope = _pallas(
        _in_proj_rope_kernel, name="in_proj_rope", grid=(t // TM, rope_w // IN_TILE),
        in_specs=[pl.BlockSpec((TM, d), lambda i, j: (i, 0)),
                  pl.BlockSpec((IN_TILE, d), lambda i, j: (rope_tile0 + j, 0)),
                  tab, tab],
        operands=(h, w_t, cos_t, sin_t), out_shape=jax.ShapeDtypeStruct((t, rope_w), BF16), **common)
    return main, rope


def _sgu_kernel(u_ref, v_ref, g_ref, b_ref, w_ref, bias_ref, o_ref):
    row = lax.broadcasted_iota(jnp.int32, (SGU_BLOCK, SGU_BLOCK), 0)
    col = lax.broadcasted_iota(jnp.int32, (SGU_BLOCK, SGU_BLOCK), 1)
    keep = (col // CHUNK) <= (row // CHUNK)
    for g in range(SGU_HEADS):
        sl = slice(g * LANES, (g + 1) * LANES)
        wg = jnp.where(keep, w_ref[g], 0.0).astype(BF16)
        vg = v_ref[:, sl].astype(F32)
        mu = jnp.mean(vg, axis=-1, keepdims=True)
        vc = vg - mu
        y = vc * lax.rsqrt(jnp.mean(vc * vc, axis=-1, keepdims=True) + EPS) * g_ref[:, sl] + b_ref[:, sl]
        yb = y.astype(BF16)
        for n in range(TB_SGU // SGU_BLOCK):
            rs = slice(n * SGU_BLOCK, (n + 1) * SGU_BLOCK)
            mixed = jnp.dot(wg, yb[rs, :], preferred_element_type=F32) + bias_ref[:, sl]
            o_ref[rs, sl] = (u_ref[rs, sl].astype(F32) * mixed).astype(o_ref.dtype)


def sgu_mixer(proj, ln_g, ln_b, w_s, bias_full):
    t = proj.shape[0]
    vec = pl.BlockSpec((1, SGU_WIDTH), lambda i: (0, 0))
    return _pallas(
        _sgu_kernel, name="sgu_mixer", grid=(t // TB_SGU,), semantics=("parallel",),
        in_specs=[pl.BlockSpec((TB_SGU, SGU_WIDTH), lambda i: (i, COL_U // SGU_WIDTH)),
                  pl.BlockSpec((TB_SGU, SGU_WIDTH), lambda i: (i, COL_V // SGU_WIDTH)),
                  vec, vec,
                  pl.BlockSpec((SGU_HEADS, SGU_BLOCK, SGU_BLOCK), lambda i: (0, 0, 0)),
                  pl.BlockSpec((SGU_BLOCK, SGU_WIDTH), lambda i: (0, 0))],
        operands=(proj, proj, ln_g.reshape(1, -1), ln_b.reshape(1, -1), w_s, bias_full),
        out_specs=pl.BlockSpec((TB_SGU, SGU_WIDTH), lambda i: (i, 0)),
        out_shape=jax.ShapeDtypeStruct((t, SGU_WIDTH), BF16),
        temp_bytes=_nbytes((TB_SGU, SGU_WIDTH), F32))


Q_SCALE = (MLA_NOPE + MLA_ROPE) ** -0.5 * math.log2(math.e)
Q_ROPE_TILE = (MLA_HEADS * MLA_NOPE) // IN_TILE


def _q_proj_kernel(c_ref, g_ref, w_ref, cos_ref, sin_ref, o_ref, cn_ref):
    j = pl.program_id(1)

    @pl.when(j == 0)
    def _():
        cn_ref[...] = _rms(c_ref[...].astype(F32), g_ref[...]).astype(cn_ref.dtype)

    acc = jnp.dot(cn_ref[...], w_ref[...], preferred_element_type=F32) * Q_SCALE

    @pl.when(j < Q_ROPE_TILE)
    def _():
        o_ref[...] = acc.astype(o_ref.dtype)

    @pl.when(j >= Q_ROPE_TILE)
    def _():
        for c in range(IN_TILE // LANES):
            sl = slice(c * LANES, (c + 1) * LANES)
            o_ref[:, sl] = _rope_cols(acc[:, sl], cos_ref[...], sin_ref[...]).astype(o_ref.dtype)


def q_proj(proj, g, w, cos_t, sin_t, seq):
    t = proj.shape[0]
    n = w.shape[1]
    pos_blocks = seq // TM_QKV
    tab = pl.BlockSpec((TM_QKV, LANES), lambda i, j: (i % pos_blocks, 0))
    return _pallas(
        _q_proj_kernel, name="q_proj", grid=(t // TM_QKV, n // IN_TILE), semantics=("parallel", "arbitrary"),
        in_specs=[pl.BlockSpec((TM_QKV, MLA_Q_RANK), lambda i, j: (i, COL_CQ // MLA_Q_RANK)),
                  pl.BlockSpec((1, MLA_Q_RANK), lambda i, j: (0, 0)),
                  pl.BlockSpec((MLA_Q_RANK, IN_TILE), lambda i, j: (0, j)),
                  tab, tab],
        operands=(proj, g.reshape(1, -1), w, cos_t, sin_t),
        out_specs=pl.BlockSpec((TM_QKV, IN_TILE), lambda i, j: (i, j)),
        out_shape=jax.ShapeDtypeStruct((t, n), BF16),
        scratch_shapes=[pltpu.VMEM((TM_QKV, MLA_Q_RANK), BF16)],
        temp_bytes=_nbytes((TM_QKV, IN_TILE), F32))


def _kv_proj_kernel(c_ref, g_ref, w_ref, o_ref, cn_ref):
    @pl.when(pl.program_id(1) == 0)
    def _():
        cn_ref[...] = _rms(c_ref[...].astype(F32), g_ref[...]).astype(cn_ref.dtype)

    o_ref[...] = jnp.dot(cn_ref[...], w_ref[...].astype(BF16), preferred_element_type=F32).astype(o_ref.dtype)


def kv_proj(proj_rope, g, w, layer):
    t = proj_rope.shape[0]
    n = w.shape[2]
    return _pallas(
        _kv_proj_kernel, name="kv_proj", grid=(t // TM_QKV, n // IN_TILE), semantics=("parallel", "arbitrary"),
        in_specs=[pl.BlockSpec((TM_QKV, MLA_KV_RANK), lambda i, j: (i, (COL_CKV - ROPE_BASE) // MLA_KV_RANK)),
                  pl.BlockSpec((1, MLA_KV_RANK), lambda i, j: (0, 0)),
                  pl.BlockSpec((None, MLA_KV_RANK, IN_TILE), lambda i, j: (layer, 0, j))],
        operands=(proj_rope, g.reshape(1, -1), w),
        out_specs=pl.BlockSpec((TM_QKV, IN_TILE), lambda i, j: (i, j)),
        out_shape=jax.ShapeDtypeStruct((t, n), BF16),
        scratch_shapes=[pltpu.VMEM((TM_QKV, MLA_KV_RANK), BF16)],
        temp_bytes=_nbytes((TM_QKV, IN_TILE), F32) + _nbytes((MLA_KV_RANK, IN_TILE), BF16))


HEAD_W = MLA_NOPE + MLA_V


def _attn_tiles(qs, kv_ref, kr_ref, base, ntiles, mask_last, states):
    states = list(states)
    ones = jnp.ones((TQ, LANES), BF16)
    for t in range(ntiles):
        st = pl.multiple_of(base + t * TQ, TQ)
        for hd in range(ATTN_HEADS):
            m, l, acc = states[hd]
            kn = kv_ref[pl.ds(st, TQ), hd * HEAD_W:hd * HEAD_W + MLA_NOPE]
            v = kv_ref[pl.ds(st, TQ), hd * HEAD_W + MLA_NOPE:(hd + 1) * HEAD_W]
            par = hd % 2
            k = jnp.concatenate([kn, kr_ref[pl.ds(st, TQ), par * LANES:(par + 1) * LANES]], axis=1)
            s = lax.dot_general(qs[hd], k, (((1,), (1,)), ((), ())), preferred_element_type=F32)
            if mask_last and t == ntiles - 1:
                row = lax.broadcasted_iota(jnp.int32, (TQ, TQ), 0)
                col = lax.broadcasted_iota(jnp.int32, (TQ, TQ), 1)
                s = jnp.where((col // CHUNK) <= (row // CHUNK), s, -jnp.inf)
            m_next = jnp.maximum(m, jnp.max(s, axis=1, keepdims=True))
            alpha = jnp.exp2(m - m_next)
            p = jnp.exp2(s - jnp.tile(m_next, (1, TQ // LANES))).astype(BF16)
            pv = jnp.dot(p, jnp.concatenate([v, ones], axis=1), preferred_element_type=F32)
            states[hd] = (m_next, alpha * l + pv[:, LANES:], alpha * acc + pv[:, :LANES])
    return states


def _attn_kernel(qn_ref, qr_ref, kv_ref, kr_ref, *rest, cast_from, n_cast):
    w_refs, o_ref = rest[:n_cast], rest[n_cast]
    wb_refs = rest[n_cast + 1:2 * n_cast + 1]
    m_ref, l_ref, acc_ref = rest[2 * n_cast + 1:]
    qi = pl.program_id(2)

    @pl.when(qi >= cast_from)
    def _():
        for w_ref, wb_ref in zip(w_refs, wb_refs):
            wb_ref[...] = w_ref[...].astype(wb_ref.dtype)

    qs = [jnp.concatenate([qn_ref[:, hd * LANES:(hd + 1) * LANES],
                           qr_ref[:, (hd // 2) * LANES:(hd // 2 + 1) * LANES]], axis=1)
          for hd in range(ATTN_HEADS)]

    m_ref[...] = jnp.full_like(m_ref, -jnp.inf)
    l_ref[...] = jnp.zeros_like(l_ref)
    acc_ref[...] = jnp.zeros_like(acc_ref)

    def load_states():
        return [(m_ref[hd], l_ref[hd], acc_ref[hd]) for hd in range(ATTN_HEADS)]

    def body(g, carry):
        base = pl.multiple_of(g * (ATTN_GROUP * TQ), ATTN_GROUP * TQ)
        states = _attn_tiles(qs, kv_ref, kr_ref, base, ATTN_GROUP, False, load_states())
        for hd in range(ATTN_HEADS):
            m_ref[hd], l_ref[hd], acc_ref[hd] = states[hd]
        return carry

    lax.fori_loop(0, qi // ATTN_GROUP, body, 0)

    base = pl.multiple_of((qi // ATTN_GROUP) * (ATTN_GROUP * TQ), ATTN_GROUP * TQ)
    for r in range(ATTN_GROUP):
        @pl.when(qi % ATTN_GROUP == r)
        def _():
            states = _attn_tiles(qs, kv_ref, kr_ref, base, r + 1, True, load_states())
            for hd in range(ATTN_HEADS):
                _, l, acc = states[hd]
                o_ref[:, hd * LANES:(hd + 1) * LANES] = (acc / l).astype(o_ref.dtype)


def mla_attention(q, kv, proj_rope, weights, layer, batch, seq):
    t = q.shape[0]
    nq = seq // TQ
    groups = MLA_HEADS // ATTN_HEADS
    cast_from = 0
    casts = nq - cast_from
    steps = batch * groups * casts
    slab_rows = [w.shape[1] // steps for w in weights]
    assert all(r * steps == w.shape[1] and r % BF16_SUBLANES == 0 for r, w in zip(slab_rows, weights))

    def slab(b, g, i):
        return (b * groups + g) * casts + jnp.maximum(i - cast_from, 0)
    nope_w = ATTN_HEADS * MLA_NOPE
    rope_w = ATTN_HEADS * MLA_ROPE
    kr_w = 2 * LANES
    kr_block = (COL_KR - ROPE_BASE) // kr_w
    rope_block0 = (MLA_HEADS * MLA_NOPE) // rope_w
    return _pallas(
        functools.partial(_attn_kernel, cast_from=cast_from, n_cast=len(weights)),
        name="mla_attention", grid=(batch, groups, nq), semantics=("parallel", "parallel", "arbitrary"),
        in_specs=[pl.BlockSpec((TQ, nope_w), lambda b, p, i: (b * nq + i, p)),
                  pl.BlockSpec((TQ, rope_w), lambda b, p, i: (b * nq + i, rope_block0 + p)),
                  pl.BlockSpec((seq, ATTN_HEADS * HEAD_W), lambda b, p, i: (b, p)),
                  pl.BlockSpec((seq, kr_w), lambda b, p, i: (b, kr_block)),
                  *[pl.BlockSpec((None, r, w.shape[2]), lambda b, p, i: (layer, slab(b, p, i), 0))
                    for r, w in zip(slab_rows, weights)]],
        operands=(q, q, kv, proj_rope, *weights),
        out_specs=(pl.BlockSpec((TQ, nope_w), lambda b, p, i: (b * nq + i, p)),
                   *[pl.BlockSpec((r, w.shape[2]), lambda b, p, i: (slab(b, p, i), 0))
                     for r, w in zip(slab_rows, weights)]),
        out_shape=(jax.ShapeDtypeStruct((t, MLA_WIDTH), BF16),
                   *[jax.ShapeDtypeStruct(w.shape[1:], BF16) for w in weights]),
        scratch_shapes=[pltpu.VMEM((ATTN_HEADS, TQ, LANES), F32)] * 3,
        temp_bytes=ATTN_HEADS * (_nbytes((TQ, TQ), F32) + _nbytes((TQ, TQ), BF16)))


def _ret_kernel(q_ref, k_ref, v_ref, g_ref, d_ref, qd_ref, kd_ref, cd_ref, o_ref, s_ref):
    @pl.when(pl.program_id(1) == 0)
    def _():
        s_ref[...] = jnp.zeros_like(s_ref)

    lane = lax.broadcasted_iota(jnp.int32, (RET_L, LANES), 1)
    for p in range(RET_HEADS // 2):
        ls = slice(p * LANES, (p + 1) * LANES)
        q = q_ref[:, ls]
        k = k_ref[:, ls]
        qf = q.astype(F32)
        kdec = (k.astype(F32) * kd_ref[:, ls]).astype(BF16)
        for hh in range(2):
            head = 2 * p + hh
            own = (lane // ROPE_HALF) % 2 == hh
            vs = slice(head * RET_V, (head + 1) * RET_V)
            v = v_ref[:, vs]
            qm = jnp.where(own, q, jnp.zeros_like(q))
            s = lax.dot_general(qm, k, (((1,), (1,)), ((), ())), preferred_element_type=F32)
            sd = (s * d_ref[head]).astype(BF16)
            qx = jnp.where(own, qf * qd_ref[:, ls], 0.0).astype(BF16)
            state = s_ref[head]
            o = (jnp.dot(sd, v, preferred_element_type=F32)
                 + jnp.dot(qx, state.astype(BF16), preferred_element_type=F32))
            s_ref[head] = cd_ref[head] * state + lax.dot_general(
                kdec, v, (((0,), (0,)), ((), ())), preferred_element_type=F32)
            o = o * lax.rsqrt(jnp.mean(o * o, axis=-1, keepdims=True) + EPS)
            gate = g_ref[:, vs].astype(F32)
            o_ref[:, vs] = (gate * jax.nn.sigmoid(gate) * o).astype(o_ref.dtype)


def retention_mixer(proj, proj_rope, d_tab, qd_tab, kd_tab, cd_tab, batch, seq):
    t = proj.shape[0]
    nsc = seq // RET_L
    qk_w = RET_HEADS * RET_QK
    full = lambda b, c: (0, 0)
    full3 = lambda b, c: (0, 0, 0)
    return _pallas(
        _ret_kernel, name="retention_mixer", grid=(batch, nsc), semantics=("parallel", "arbitrary"),
        in_specs=[pl.BlockSpec((RET_L, qk_w), lambda b, c: (b * nsc + c, (COL_RQ - ROPE_BASE) // qk_w)),
                  pl.BlockSpec((RET_L, qk_w), lambda b, c: (b * nsc + c, (COL_RK - ROPE_BASE) // qk_w)),
                  pl.BlockSpec((RET_L, RET_WIDTH), lambda b, c: (b * nsc + c, COL_RV // RET_WIDTH)),
                  pl.BlockSpec((RET_L, RET_WIDTH), lambda b, c: (b * nsc + c, COL_RG // RET_WIDTH)),
                  pl.BlockSpec((RET_HEADS, RET_L, RET_L), full3),
                  pl.BlockSpec((RET_L, qk_w), full),
                  pl.BlockSpec((RET_L, qk_w), full),
                  pl.BlockSpec((RET_HEADS, LANES, RET_V), full3)],
        operands=(proj_rope, proj_rope, proj, proj, d_tab, qd_tab, kd_tab, cd_tab),
        out_specs=pl.BlockSpec((RET_L, RET_WIDTH), lambda b, c: (b * nsc + c, 0)),
        out_shape=jax.ShapeDtypeStruct((t, RET_WIDTH), BF16),
        scratch_shapes=[pltpu.VMEM((RET_HEADS, LANES, RET_V), F32)],
        temp_bytes=RET_HEADS * _nbytes((RET_L, RET_L), F32))


def retention_tables():
    log_gamma = jnp.log1p(-jnp.exp2(-5.0 - jnp.arange(RET_HEADS, dtype=F32)))
    idx = jnp.arange(RET_L, dtype=F32)
    dist = jnp.abs(idx[:, None] - idx[None, :])
    chunk = jnp.arange(RET_L) // CHUNK
    visible = chunk[None, :] <= chunk[:, None]
    d_tab = jnp.where(visible[None], jnp.exp(log_gamma[:, None, None] * dist[None]), 0.0)
    col_head = _pair_rope_order(RET_HEADS) // RET_QK
    lg_lane = log_gamma[col_head]
    qd_tab = jnp.exp(lg_lane[None, :] * (idx[:, None] + 1.0))
    kd_tab = jnp.exp(lg_lane[None, :] * (RET_L - 1.0 - idx[:, None]))
    cd = jnp.exp(log_gamma * RET_L)
    cd_tab = jnp.broadcast_to(cd[:, None, None], (RET_HEADS, LANES, RET_V))
    return d_tab, qd_tab, kd_tab, cd_tab


def _out_proj_kernel(a_ref, b_ref, c_ref, w_ref, o_ref):
    ka = a_ref.shape[1]
    kb = b_ref.shape[1]
    acc = jnp.dot(a_ref[...], w_ref[:ka, :], preferred_element_type=F32)
    acc = acc + jnp.dot(b_ref[...], w_ref[ka:ka + kb, :], preferred_element_type=F32)
    acc = acc + jnp.dot(c_ref[...], w_ref[ka + kb:, :], preferred_element_type=F32)
    o_ref[...] = acc.astype(o_ref.dtype)


def out_proj(a, b, c, w):
    t = a.shape[0]
    k, n = w.shape
    return _pallas(
        _out_proj_kernel, name="out_proj", grid=(t // TM, n // TN_OUT), semantics=("parallel", "arbitrary"),
        in_specs=[pl.BlockSpec((TM, a.shape[1]), lambda i, j: (i, 0)),
                  pl.BlockSpec((TM, b.shape[1]), lambda i, j: (i, 0)),
                  pl.BlockSpec((TM, c.shape[1]), lambda i, j: (i, 0)),
                  pl.BlockSpec((k, TN_OUT), lambda i, j: (0, j))],
        operands=(a, b, c, w),
        out_specs=pl.BlockSpec((TM, TN_OUT), lambda i, j: (i, j)),
        out_shape=jax.ShapeDtypeStruct((t, n), BF16),
        temp_bytes=_nbytes((TM, TN_OUT), F32))


def _ffn_kernel(h_ref, wu_ref, wd_ref, o_ref):
    @pl.when(pl.program_id(1) == 0)
    def _():
        o_ref[...] = jnp.zeros_like(o_ref)

    a = jnp.dot(h_ref[...], wu_ref[...], preferred_element_type=F32)
    a = jnp.square(jnp.maximum(a, 0.0)).astype(BF16)
    o_ref[...] += jnp.dot(a, wd_ref[...], preferred_element_type=F32)


def ffn(h, wu_b, wd_b):
    t, d = h.shape
    ff = wu_b.shape[1]
    return _pallas(
        _ffn_kernel, name="ffn", grid=(t // TM_FFN, ff // TF_FFN), semantics=("parallel", "arbitrary"),
        in_specs=[pl.BlockSpec((TM_FFN, d), lambda i, j: (i, 0)),
                  pl.BlockSpec((d, TF_FFN), lambda i, j: (0, j)),
                  pl.BlockSpec((TF_FFN, d), lambda i, j: (j, 0))],
        operands=(h, wu_b, wd_b),
        out_specs=pl.BlockSpec((TM_FFN, d), lambda i, j: (i, 0)),
        out_shape=jax.ShapeDtypeStruct((t, d), F32),
        temp_bytes=_nbytes((TM_FFN, TF_FFN), F32) + _nbytes((TM_FFN, TF_FFN), BF16))


_SRC_SIZES = (SGU_WIDTH, SGU_WIDTH, MLA_Q_RANK, MLA_KV_RANK, MLA_ROPE,
              RET_HEADS * RET_QK, RET_HEADS * RET_QK, RET_WIDTH, RET_WIDTH)
(_SRC_U, _SRC_V, _SRC_CQ, _SRC_CKV, _SRC_KR, _SRC_RQ, _SRC_RK, _SRC_RV, _SRC_RG,
 IN_WIDTH) = [int(o) for o in np.concatenate([[0], np.cumsum(_SRC_SIZES)])]
_W_IN_MOVES = (
    (COL_U, _SRC_U, 3 * IN_TILE),
    (COL_RV, _SRC_RV, RET_WIDTH),
    (COL_RG, _SRC_RG, RET_WIDTH),
    (COL_CKV, _SRC_CKV, MLA_KV_RANK),
    (COL_KR, _SRC_KR, ROPE_HALF),
    (COL_KR + LANES // 2, _SRC_KR + ROPE_HALF, ROPE_HALF),
    (COL_KR + LANES + ROPE_HALF, _SRC_KR, ROPE_HALF),
    (COL_KR + LANES + LANES // 2 + ROPE_HALF, _SRC_KR + ROPE_HALF, ROPE_HALF),
    *[(dst0 + i * ROPE_HALF, src0 + int(s0), ROPE_HALF)
      for dst0, src0 in ((COL_RQ, _SRC_RQ), (COL_RK, _SRC_RK))
      for i, s0 in enumerate(_pair_rope_order(RET_HEADS)[::ROPE_HALF])],
)


def _w_in_prep_kernel(w_ref, o_ref):
    o_ref[COL_KR:COL_RQ, :] = jnp.zeros((COL_RQ - COL_KR, o_ref.shape[1]), o_ref.dtype)
    for dst, src, width in _W_IN_MOVES:
        o_ref[dst:dst + width, :] = w_ref[src:src + width, :].astype(o_ref.dtype)


def prep_w_in(w_t, layer):
    _, n, d = w_t.shape
    return _pallas(
        _w_in_prep_kernel, name="prep_w_in", grid=(d // TK_PREP,), semantics=("parallel",),
        in_specs=[pl.BlockSpec((None, n, TK_PREP), lambda i: (layer, 0, i))],
        operands=(w_t,),
        out_specs=pl.BlockSpec((IN_WIDTH_PADDED, TK_PREP), lambda i: (0, i)),
        out_shape=jax.ShapeDtypeStruct((IN_WIDTH_PADDED, d), BF16))


def _prep_wq(w):
    k = w.shape[0]
    w3 = w.reshape(k, MLA_HEADS, MLA_NOPE + MLA_ROPE)
    nope = w3[:, :, :MLA_NOPE].reshape(k, MLA_HEADS * MLA_NOPE)
    rope = w3[:, :, MLA_NOPE:].reshape(k, MLA_HEADS * MLA_ROPE)[:, _pair_rope_order(MLA_HEADS)]
    return jnp.concatenate([nope, rope], axis=1).astype(BF16)


def _rope_lane_tables(seq):
    inv_freq = 1.0 / (ROPE_THETA ** (jnp.arange(0, MLA_ROPE, 2, dtype=F32) / MLA_ROPE))
    ang = jnp.arange(seq, dtype=F32)[:, None] * inv_freq[None, :]
    cos, sin = jnp.cos(ang), jnp.sin(ang)
    cos_t = jnp.concatenate([cos, cos, cos, cos], axis=1)
    sin_t = jnp.concatenate([-sin, -sin, sin, sin], axis=1)
    return cos_t, sin_t


def kernel(x, norm_mix_pre, norm_mix_post, norm_ffn_pre, norm_ffn_post, w_in, sgu_ln_g, sgu_ln_b, sgu_w, sgu_b, mla_q_norm, mla_wq_b, mla_kv_norm, mla_wkv_b, w_out, w_up, w_down):
    batch, seq, d = x.shape
    depth = w_in.shape[0]
    cos_t, sin_t = _rope_lane_tables(seq)
    d_tab, qd_tab, kd_tab, cd_tab = retention_tables()
    w_in_t = jnp.swapaxes(w_in, 1, 2)

    xf = x.reshape(batch * seq, d)
    h = rmsnorm_rows(xf, norm_mix_pre[0])
    for l in range(depth):
        proj, proj_rope = in_proj(h, prep_w_in(w_in_t, l), cos_t, sin_t, seq)
        bias_full = jnp.repeat(sgu_b[l].T, SGU_WIDTH // SGU_HEADS, axis=1)
        out_a = sgu_mixer(proj, sgu_ln_g[l], sgu_ln_b[l], sgu_w[l], bias_full)
        q = q_proj(proj, mla_q_norm[l], _prep_wq(mla_wq_b[l]), cos_t, sin_t, seq)
        kv = kv_proj(proj_rope, mla_kv_norm[l], mla_wkv_b, l)
        out_b, wu_b, wd_b, wo_b = mla_attention(q, kv, proj_rope, (w_up, w_down, w_out), l, batch, seq)
        out_c = retention_mixer(proj, proj_rope, d_tab, qd_tab, kd_tab, cd_tab, batch, seq)
        mixed = out_proj(out_a, out_b, out_c, wo_b)
        xf, h2 = residual_norm(xf, mixed, norm_mix_post[l], norm_ffn_pre[l])
        f = ffn(h2, wu_b, wd_b)
        g_next = norm_mix_pre[l + 1] if l + 1 < depth else None
        xf, h = residual_norm(xf, f, norm_ffn_post[l], g_next)
    return xf.reshape(batch, seq, d)
```

```python
import functools
import math

import numpy as np
import jax
import jax.numpy as jnp
from jax import lax
from jax.experimental import pallas as pl
from jax.experimental.pallas import tpu as pltpu

F32 = jnp.float32
BF16 = jnp.bfloat16

CHUNK = 64
EPS = 1e-6
ROPE_THETA = 10000.0
SGU_BLOCK = 128
SGU_WIDTH = 1024
SGU_HEADS = 8
MLA_HEADS = 16
MLA_NOPE = 128
MLA_ROPE = 64
MLA_V = 128
MLA_Q_RANK = 1024
MLA_KV_RANK = 512
MLA_WIDTH = MLA_HEADS * MLA_V
RET_HEADS = 8
RET_QK = 64
RET_V = 128
RET_WIDTH = RET_HEADS * RET_V

LANES = 128
BF16_SUBLANES = 16
VMEM_CAP = 64 * 1024 * 1024
VMEM_RESERVED = 2 * 1024 * 1024
VMEM_INTERNAL = 6 * 1024 * 1024
PIPELINE_BUFFERS = 2

IN_TILE = 1024
COL_U = 0 * IN_TILE
COL_V = 1 * IN_TILE
COL_CQ = 2 * IN_TILE
COL_RV = 3 * IN_TILE
COL_RG = 4 * IN_TILE
COL_CKV = 5 * IN_TILE
COL_KR = COL_CKV + MLA_KV_RANK
COL_RQ = 6 * IN_TILE
COL_RK = COL_RQ + RET_HEADS * RET_QK
IN_WIDTH_PADDED = 7 * IN_TILE
ROPE_BASE = COL_CKV

TM = 1024
TM_QKV = 2048
NORM_TILES = (512, 256)
TM_FFN = 512
TF_FFN = 1024
TN_OUT = 1024
TK_PREP = 256
TB_SGU = 512
TQ = 512
ATTN_GROUP = 4
ATTN_HEADS = 4
RET_L = 256


def _nbytes(shape, dtype):
    return math.prod(1 if s is None else s for s in shape) * jnp.dtype(dtype).itemsize


def _pallas(kernel, *, name, grid, semantics, in_specs, operands, out_specs, out_shape,
            scratch_shapes=(), temp_bytes=0):
    outs = out_shape if isinstance(out_shape, (tuple, list)) else (out_shape,)
    ospecs = out_specs if isinstance(out_specs, (tuple, list)) else (out_specs,)
    pipelined = sum(_nbytes(s.block_shape, a.dtype) for s, a in zip(in_specs, operands))
    pipelined += sum(_nbytes(s.block_shape, o.dtype) for s, o in zip(ospecs, outs))
    scratch = sum(_nbytes(s.shape, s.dtype) for s in scratch_shapes)
    need = PIPELINE_BUFFERS * pipelined + scratch + temp_bytes + VMEM_INTERNAL
    params = pltpu.CompilerParams(dimension_semantics=semantics,
                                  vmem_limit_bytes=int(min(need, VMEM_CAP - VMEM_RESERVED)))
    return pl.pallas_call(kernel, out_shape=out_shape, grid=grid, in_specs=in_specs, out_specs=out_specs,
                          scratch_shapes=scratch_shapes, compiler_params=params, name=name)(*operands)


def _rms(x, g):
    return x * lax.rsqrt(jnp.mean(x * x, axis=-1, keepdims=True) + EPS) * g


def _rmsnorm_kernel(x_ref, g_ref, o_ref):
    o_ref[...] = _rms(x_ref[...], g_ref[...]).astype(o_ref.dtype)


def _norm_tile(d, row_dtypes):
    for tile in NORM_TILES:
        need = PIPELINE_BUFFERS * sum(_nbytes((tile, d), dt) for dt in row_dtypes)
        if need + _nbytes((tile, d), F32) + VMEM_INTERNAL <= VMEM_CAP - VMEM_RESERVED:
            return tile
    raise ValueError("no row tile fits VMEM")


def rmsnorm_rows(x, g):
    t, d = x.shape
    tm = _norm_tile(d, (x.dtype, BF16))
    return _pallas(
        _rmsnorm_kernel, name="rmsnorm_rows", grid=(t // tm,), semantics=("parallel",),
        in_specs=[pl.BlockSpec((tm, d), lambda i: (i, 0)),
                  pl.BlockSpec((1, d), lambda i: (0, 0))],
        operands=(x, g.reshape(1, d)),
        out_specs=pl.BlockSpec((tm, d), lambda i: (i, 0)),
        out_shape=jax.ShapeDtypeStruct((t, d), BF16),
        temp_bytes=_nbytes((tm, d), F32))


def _mix_norm_kernel(x_ref, y_ref, gp_ref, gn_ref, ho_ref):
    xm = x_ref[...] + _rms(y_ref[...].astype(F32), gp_ref[...])
    ho_ref[...] = _rms(xm, gn_ref[...]).astype(ho_ref.dtype)


def mix_norm(x, y, g_post, g_next):
    t, d = x.shape
    tm = _norm_tile(d, (x.dtype, y.dtype, BF16))
    row = pl.BlockSpec((tm, d), lambda i: (i, 0))
    vec = pl.BlockSpec((1, d), lambda i: (0, 0))
    return _pallas(_mix_norm_kernel, name="mix_norm", grid=(t // tm,), semantics=("parallel",),
                   in_specs=[row, row, vec, vec],
                   operands=(x, y, g_post.reshape(1, d), g_next.reshape(1, d)),
                   out_specs=row, out_shape=jax.ShapeDtypeStruct((t, d), BF16),
                   temp_bytes=_nbytes((tm, d), F32))


def _layer_residual(x_ref, y_ref, gy_ref, f_ref, gf_ref):
    xm = x_ref[...] + _rms(y_ref[...].astype(F32), gy_ref[...])
    return xm + _rms(f_ref[...].astype(F32), gf_ref[...])


def _layer_residual_norm_kernel(x_ref, y_ref, gy_ref, f_ref, gf_ref, gn_ref, xo_ref, ho_ref):
    xn = _layer_residual(x_ref, y_ref, gy_ref, f_ref, gf_ref)
    xo_ref[...] = xn
    ho_ref[...] = _rms(xn, gn_ref[...]).astype(ho_ref.dtype)


def _layer_residual_kernel(x_ref, y_ref, gy_ref, f_ref, gf_ref, xo_ref):
    xo_ref[...] = _layer_residual(x_ref, y_ref, gy_ref, f_ref, gf_ref)


def layer_residual(x, y, g_y, f, g_f, g_next):
    t, d = x.shape
    out_dtypes = (F32,) if g_next is None else (F32, BF16)
    tm = _norm_tile(d, (x.dtype, y.dtype, f.dtype) + out_dtypes)
    row = pl.BlockSpec((tm, d), lambda i: (i, 0))
    vec = pl.BlockSpec((1, d), lambda i: (0, 0))
    common = dict(grid=(t // tm,), semantics=("parallel",), temp_bytes=_nbytes((tm, d), F32))
    if g_next is None:
        return _pallas(_layer_residual_kernel, name="layer_residual", in_specs=[row, row, vec, row, vec],
                       operands=(x, y, g_y.reshape(1, d), f, g_f.reshape(1, d)), out_specs=row,
                       out_shape=jax.ShapeDtypeStruct((t, d), F32), **common), None
    return _pallas(_layer_residual_norm_kernel, name="layer_residual_norm",
                   in_specs=[row, row, vec, row, vec, vec],
                   operands=(x, y, g_y.reshape(1, d), f, g_f.reshape(1, d), g_next.reshape(1, d)),
                   out_specs=(row, row),
                   out_shape=(jax.ShapeDtypeStruct((t, d), F32), jax.ShapeDtypeStruct((t, d), BF16)),
                   **common)


ROPE_HALF = MLA_ROPE // 2


def _rope_cols(x, cos, sin):
    return x * cos + pltpu.roll(x, LANES // 2, 1) * sin


def _pair_rope_order(n_heads):
    src = np.arange(n_heads * MLA_ROPE).reshape(n_heads // 2, 2, 2, ROPE_HALF)
    return src.transpose(0, 2, 1, 3).reshape(-1)


def _in_proj_main_kernel(h_ref, w_ref, o_ref):
    o_ref[...] = lax.dot_general(h_ref[...], w_ref[...], (((1,), (1,)), ((), ())),
                                 preferred_element_type=F32).astype(o_ref.dtype)


def _in_proj_rope_kernel(h_ref, w_ref, cos_ref, sin_ref, o_ref):
    j = pl.program_id(1)
    acc = lax.dot_general(h_ref[...], w_ref[...], (((1,), (1,)), ((), ())), preferred_element_type=F32)

    @pl.when(j == 0)
    def _():
        lo = COL_KR - COL_CKV
        o_ref[:, :lo] = acc[:, :lo].astype(o_ref.dtype)
        for c in range(lo // LANES, lo // LANES + 2):
            sl = slice(c * LANES, (c + 1) * LANES)
            o_ref[:, sl] = _rope_cols(acc[:, sl], cos_ref[...], sin_ref[...]).astype(o_ref.dtype)
        o_ref[:, lo + 2 * LANES:] = acc[:, lo + 2 * LANES:].astype(o_ref.dtype)

    @pl.when(j == 1)
    def _():
        q_cols = (RET_HEADS * RET_QK) // LANES
        for c in range(IN_TILE // LANES):
            sl = slice(c * LANES, (c + 1) * LANES)
            r = _rope_cols(acc[:, sl], cos_ref[...], sin_ref[...])
            if c < q_cols:
                r = r * (RET_QK ** -0.5)
            o_ref[:, sl] = r.astype(o_ref.dtype)


def in_proj(h, w_t, cos_t, sin_t, seq):
    t, d = h.shape
    pos_blocks = seq // TM
    tab = pl.BlockSpec((TM, LANES), lambda i, j: (i % pos_blocks, 0))
    rope_tile0 = ROPE_BASE // IN_TILE
    rope_w = IN_WIDTH_PADDED - ROPE_BASE
    common = dict(semantics=("parallel", "arbitrary"), temp_bytes=_nbytes((TM, IN_TILE), F32),
                  out_specs=pl.BlockSpec((TM, IN_TILE), lambda i, j: (i, j)))
    main = _pallas(
        _in_proj_main_kernel, name="in_proj_main", grid=(t // TM, rope_tile0),
        in_specs=[pl.BlockSpec((TM, d), lambda i, j: (i, 0)),
                  pl.BlockSpec((IN_TILE, d), lambda i, j: (j, 0))],
        operands=(h, w_t), out_shape=jax.ShapeDtypeStruct((t, ROPE_BASE), BF16), **common)
    rope = _pallas(
        _in_proj_rope_kernel, name="in_proj_rope", grid=(t // TM, rope_w // IN_TILE),
        in_specs=[pl.BlockSpec((TM, d), lambda i, j: (i, 0)),
                  pl.BlockSpec((IN_TILE, d), lambda i, j: (rope_tile0 + j, 0)),
                  tab, tab],
        operands=(h, w_t, cos_t, sin_t), out_shape=jax.ShapeDtypeStruct((t, rope_w), BF16), **common)
    return main, rope


def _sgu_kernel(u_ref, v_ref, g_ref, b_ref, w_ref, bias_ref, o_ref):
    row = lax.broadcasted_iota(jnp.int32, (SGU_BLOCK, SGU_BLOCK), 0)
    col = lax.broadcasted_iota(jnp.int32, (SGU_BLOCK, SGU_BLOCK), 1)
    keep = (col // CHUNK) <= (row // CHUNK)
    for g in range(SGU_HEADS):
        sl = slice(g * LANES, (g + 1) * LANES)
        wg = jnp.where(keep, w_ref[g], 0.0).astype(BF16)
        vg = v_ref[:, sl].astype(F32)
        mu = jnp.mean(vg, axis=-1, keepdims=True)
        vc = vg - mu
        y = vc * lax.rsqrt(jnp.mean(vc * vc, axis=-1, keepdims=True) + EPS) * g_ref[:, sl] + b_ref[:, sl]
        yb = y.astype(BF16)
        for n in range(TB_SGU // SGU_BLOCK):
            rs = slice(n * SGU_BLOCK, (n + 1) * SGU_BLOCK)
            mixed = jnp.dot(wg, yb[rs, :], preferred_element_type=F32) + bias_ref[:, sl]
            o_ref[rs, sl] = (u_ref[rs, sl].astype(F32) * mixed).astype(o_ref.dtype)


def sgu_mixer(proj, ln_g, ln_b, w_s, bias_full):
    t = proj.shape[0]
    vec = pl.BlockSpec((1, SGU_WIDTH), lambda i: (0, 0))
    return _pallas(
        _sgu_kernel, name="sgu_mixer", grid=(t // TB_SGU,), semantics=("parallel",),
        in_specs=[pl.BlockSpec((TB_SGU, SGU_WIDTH), lambda i: (i, COL_U // SGU_WIDTH)),
                  pl.BlockSpec((TB_SGU, SGU_WIDTH), lambda i: (i, COL_V // SGU_WIDTH)),
                  vec, vec,
                  pl.BlockSpec((SGU_HEADS, SGU_BLOCK, SGU_BLOCK), lambda i: (0, 0, 0)),
                  pl.BlockSpec((SGU_BLOCK, SGU_WIDTH), lambda i: (0, 0))],
        operands=(proj, proj, ln_g.reshape(1, -1), ln_b.reshape(1, -1), w_s, bias_full),
        out_specs=pl.BlockSpec((TB_SGU, SGU_WIDTH), lambda i: (i, 0)),
        out_shape=jax.ShapeDtypeStruct((t, SGU_WIDTH), BF16),
        temp_bytes=_nbytes((TB_SGU, SGU_WIDTH), F32))


Q_SCALE = (MLA_NOPE + MLA_ROPE) ** -0.5 * math.log2(math.e)
Q_ROPE_TILE = (MLA_HEADS * MLA_NOPE) // IN_TILE


def _q_proj_kernel(c_ref, g_ref, w_ref, cos_ref, sin_ref, o_ref, cn_ref):
    j = pl.program_id(1)

    @pl.when(j == 0)
    def _():
        cn_ref[...] = _rms(c_ref[...].astype(F32), g_ref[...]).astype(cn_ref.dtype)

    acc = jnp.dot(cn_ref[...], w_ref[...], preferred_element_type=F32) * Q_SCALE

    @pl.when(j < Q_ROPE_TILE)
    def _():
        o_ref[...] = acc.astype(o_ref.dtype)

    @pl.when(j >= Q_ROPE_TILE)
    def _():
        for c in range(IN_TILE // LANES):
            sl = slice(c * LANES, (c + 1) * LANES)
            o_ref[:, sl] = _rope_cols(acc[:, sl], cos_ref[...], sin_ref[...]).astype(o_ref.dtype)


def q_proj(proj, g, w, cos_t, sin_t, seq):
    t = proj.shape[0]
    n = w.shape[1]
    pos_blocks = seq // TM_QKV
    tab = pl.BlockSpec((TM_QKV, LANES), lambda i, j: (i % pos_blocks, 0))
    return _pallas(
        _q_proj_kernel, name="q_proj", grid=(t // TM_QKV, n // IN_TILE), semantics=("parallel", "arbitrary"),
        in_specs=[pl.BlockSpec((TM_QKV, MLA_Q_RANK), lambda i, j: (i, COL_CQ // MLA_Q_RANK)),
                  pl.BlockSpec((1, MLA_Q_RANK), lambda i, j: (0, 0)),
                  pl.BlockSpec((MLA_Q_RANK, IN_TILE), lambda i, j: (0, j)),
                  tab, tab],
        operands=(proj, g.reshape(1, -1), w, cos_t, sin_t),
        out_specs=pl.BlockSpec((TM_QKV, IN_TILE), lambda i, j: (i, j)),
        out_shape=jax.ShapeDtypeStruct((t, n), BF16),
        scratch_shapes=[pltpu.VMEM((TM_QKV, MLA_Q_RANK), BF16)],
        temp_bytes=_nbytes((TM_QKV, IN_TILE), F32))


def _kv_proj_kernel(c_ref, g_ref, w_ref, o_ref, cn_ref):
    @pl.when(pl.program_id(1) == 0)
    def _():
        cn_ref[...] = _rms(c_ref[...].astype(F32), g_ref[...]).astype(cn_ref.dtype)

    o_ref[...] = jnp.dot(cn_ref[...], w_ref[...].astype(BF16), preferred_element_type=F32).astype(o_ref.dtype)


def kv_proj(proj_rope, g, w, layer):
    t = proj_rope.shape[0]
    n = w.shape[2]
    return _pallas(
        _kv_proj_kernel, name="kv_proj", grid=(t // TM_QKV, n // IN_TILE), semantics=("parallel", "arbitrary"),
        in_specs=[pl.BlockSpec((TM_QKV, MLA_KV_RANK), lambda i, j: (i, (COL_CKV - ROPE_BASE) // MLA_KV_RANK)),
                  pl.BlockSpec((1, MLA_KV_RANK), lambda i, j: (0, 0)),
                  pl.BlockSpec((None, MLA_KV_RANK, IN_TILE), lambda i, j: (layer, 0, j))],
        operands=(proj_rope, g.reshape(1, -1), w),
        out_specs=pl.BlockSpec((TM_QKV, IN_TILE), lambda i, j: (i, j)),
        out_shape=jax.ShapeDtypeStruct((t, n), BF16),
        scratch_shapes=[pltpu.VMEM((TM_QKV, MLA_KV_RANK), BF16)],
        temp_bytes=_nbytes((TM_QKV, IN_TILE), F32) + _nbytes((MLA_KV_RANK, IN_TILE), BF16))


HEAD_W = MLA_NOPE + MLA_V


def _attn_tiles(qs, kv_ref, kr_ref, base, ntiles, mask_last, states):
    states = list(states)
    ones = jnp.ones((TQ, LANES), BF16)
    for t in range(ntiles):
        st = pl.multiple_of(base + t * TQ, TQ)
        for hd in range(ATTN_HEADS):
            m, l, acc = states[hd]
            kn = kv_ref[pl.ds(st, TQ), hd * HEAD_W:hd * HEAD_W + MLA_NOPE]
            v = kv_ref[pl.ds(st, TQ), hd * HEAD_W + MLA_NOPE:(hd + 1) * HEAD_W]
            par = hd % 2
            k = jnp.concatenate([kn, kr_ref[pl.ds(st, TQ), par * LANES:(par + 1) * LANES]], axis=1)
            s = lax.dot_general(qs[hd], k, (((1,), (1,)), ((), ())), preferred_element_type=F32)
            if mask_last and t == ntiles - 1:
                row = lax.broadcasted_iota(jnp.int32, (TQ, TQ), 0)
                col = lax.broadcasted_iota(jnp.int32, (TQ, TQ), 1)
                s = jnp.where((col // CHUNK) <= (row // CHUNK), s, -jnp.inf)
            m_next = jnp.maximum(m, jnp.max(s, axis=1, keepdims=True))
            alpha = jnp.exp2(m - m_next)
            p = jnp.exp2(s - jnp.tile(m_next, (1, TQ // LANES))).astype(BF16)
            pv = jnp.dot(p, jnp.concatenate([v, ones], axis=1), preferred_element_type=F32)
            states[hd] = (m_next, alpha * l + pv[:, LANES:], alpha * acc + pv[:, :LANES])
    return states


def _attn_kernel(qn_ref, qr_ref, kv_ref, kr_ref, *rest, cast_from, n_cast):
    w_refs, o_ref = rest[:n_cast], rest[n_cast]
    wb_refs = rest[n_cast + 1:2 * n_cast + 1]
    m_ref, l_ref, acc_ref = rest[2 * n_cast + 1:]
    qi = pl.program_id(2)

    @pl.when(qi >= cast_from)
    def _():
        for w_ref, wb_ref in zip(w_refs, wb_refs):
            wb_ref[...] = w_ref[...].astype(wb_ref.dtype)

    qs = [jnp.concatenate([qn_ref[:, hd * LANES:(hd + 1) * LANES],
                           qr_ref[:, (hd // 2) * LANES:(hd // 2 + 1) * LANES]], axis=1)
          for hd in range(ATTN_HEADS)]

    m_ref[...] = jnp.full_like(m_ref, -jnp.inf)
    l_ref[...] = jnp.zeros_like(l_ref)
    acc_ref[...] = jnp.zeros_like(acc_ref)

    def load_states():
        return [(m_ref[hd], l_ref[hd], acc_ref[hd]) for hd in range(ATTN_HEADS)]

    def body(g, carry):
        base = pl.multiple_of(g * (ATTN_GROUP * TQ), ATTN_GROUP * TQ)
        states = _attn_tiles(qs, kv_ref, kr_ref, base, ATTN_GROUP, False, load_states())
        for hd in range(ATTN_HEADS):
            m_ref[hd], l_ref[hd], acc_ref[hd] = states[hd]
        return carry

    lax.fori_loop(0, qi // ATTN_GROUP, body, 0)

    base = pl.multiple_of((qi // ATTN_GROUP) * (ATTN_GROUP * TQ), ATTN_GROUP * TQ)
    for r in range(ATTN_GROUP):
        @pl.when(qi % ATTN_GROUP == r)
        def _():
            states = _attn_tiles(qs, kv_ref, kr_ref, base, r + 1, True, load_states())
            for hd in range(ATTN_HEADS):
                _, l, acc = states[hd]
                o_ref[:, hd * LANES:(hd + 1) * LANES] = (acc / l).astype(o_ref.dtype)


def mla_attention(q, kv, proj_rope, weights, layer, batch, seq):
    t = q.shape[0]
    nq = seq // TQ
    groups = MLA_HEADS // ATTN_HEADS
    cast_from = 0
    casts = nq - cast_from
    steps = batch * groups * casts
    slab_rows = [w.shape[1] // steps for w in weights]
    assert all(r * steps == w.shape[1] and r % BF16_SUBLANES == 0 for r, w in zip(slab_rows, weights))

    def slab(b, g, i):
        return (b * groups + g) * casts + jnp.maximum(i - cast_from, 0)
    nope_w = ATTN_HEADS * MLA_NOPE
    rope_w = ATTN_HEADS * MLA_ROPE
    kr_w = 2 * LANES
    kr_block = (COL_KR - ROPE_BASE) // kr_w
    rope_block0 = (MLA_HEADS * MLA_NOPE) // rope_w
    return _pallas(
        functools.partial(_attn_kernel, cast_from=cast_from, n_cast=len(weights)),
        name="mla_attention", grid=(batch, groups, nq), semantics=("parallel", "parallel", "arbitrary"),
        in_specs=[pl.BlockSpec((TQ, nope_w), lambda b, p, i: (b * nq + i, p)),
                  pl.BlockSpec((TQ, rope_w), lambda b, p, i: (b * nq + i, rope_block0 + p)),
                  pl.BlockSpec((seq, ATTN_HEADS * HEAD_W), lambda b, p, i: (b, p)),
                  pl.BlockSpec((seq, kr_w), lambda b, p, i: (b, kr_block)),
                  *[pl.BlockSpec((None, r, w.shape[2]), lambda b, p, i: (layer, slab(b, p, i), 0))
                    for r, w in zip(slab_rows, weights)]],
        operands=(q, q, kv, proj_rope, *weights),
        out_specs=(pl.BlockSpec((TQ, nope_w), lambda b, p, i: (b * nq + i, p)),
                   *[pl.BlockSpec((r, w.shape[2]), lambda b, p, i: (slab(b, p, i), 0))
                     for r, w in zip(slab_rows, weights)]),
        out_shape=(jax.ShapeDtypeStruct((t, MLA_WIDTH), BF16),
                   *[jax.ShapeDtypeStruct(w.shape[1:], BF16) for w in weights]),
        scratch_shapes=[pltpu.VMEM((ATTN_HEADS, TQ, LANES), F32)] * 3,
        temp_bytes=ATTN_HEADS * (_nbytes((TQ, TQ), F32) + _nbytes((TQ, TQ), BF16)))


def _ret_kernel(q_ref, k_ref, v_ref, g_ref, d_ref, qd_ref, kd_ref, cd_ref, o_ref, s_ref):
    @pl.when(pl.program_id(1) == 0)
    def _():
        s_ref[...] = jnp.zeros_like(s_ref)

    lane = lax.broadcasted_iota(jnp.int32, (RET_L, LANES), 1)
    for p in range(RET_HEADS // 2):
        ls = slice(p * LANES, (p + 1) * LANES)
        q = q_ref[:, ls]
        k = k_ref[:, ls]
        qf = q.astype(F32)
        kdec = (k.astype(F32) * kd_ref[:, ls]).astype(BF16)
        for hh in range(2):
            head = 2 * p + hh
            own = (lane // ROPE_HALF) % 2 == hh
            vs = slice(head * RET_V, (head + 1) * RET_V)
            v = v_ref[:, vs]
            qm = jnp.where(own, q, jnp.zeros_like(q))
            s = lax.dot_general(qm, k, (((1,), (1,)), ((), ())), preferred_element_type=F32)
            sd = (s * d_ref[head]).astype(BF16)
            qx = jnp.where(own, qf * qd_ref[:, ls], 0.0).astype(BF16)
            state = s_ref[head]
            o = (jnp.dot(sd, v, preferred_element_type=F32)
                 + jnp.dot(qx, state.astype(BF16), preferred_element_type=F32))
            s_ref[head] = cd_ref[head] * state + lax.dot_general(
                kdec, v, (((0,), (0,)), ((), ())), preferred_element_type=F32)
            o = o * lax.rsqrt(jnp.mean(o * o, axis=-1, keepdims=True) + EPS)
            gate = g_ref[:, vs].astype(F32)
            o_ref[:, vs] = (gate * jax.nn.sigmoid(gate) * o).astype(o_ref.dtype)


def retention_mixer(proj, proj_rope, d_tab, qd_tab, kd_tab, cd_tab, batch, seq):
    t = proj.shape[0]
    nsc = seq // RET_L
    qk_w = RET_HEADS * RET_QK
    full = lambda b, c: (0, 0)
    full3 = lambda b, c: (0, 0, 0)
    return _pallas(
        _ret_kernel, name="retention_mixer", grid=(batch, nsc), semantics=("parallel", "arbitrary"),
        in_specs=[pl.BlockSpec((RET_L, qk_w), lambda b, c: (b * nsc + c, (COL_RQ - ROPE_BASE) // qk_w)),
                  pl.BlockSpec((RET_L, qk_w), lambda b, c: (b * nsc + c, (COL_RK - ROPE_BASE) // qk_w)),
                  pl.BlockSpec((RET_L, RET_WIDTH), lambda b, c: (b * nsc + c, COL_RV // RET_WIDTH)),
                  pl.BlockSpec((RET_L, RET_WIDTH), lambda b, c: (b * nsc + c, COL_RG // RET_WIDTH)),
                  pl.BlockSpec((RET_HEADS, RET_L, RET_L), full3),
                  pl.BlockSpec((RET_L, qk_w), full),
                  pl.BlockSpec((RET_L, qk_w), full),
                  pl.BlockSpec((RET_HEADS, LANES, RET_V), full3)],
        operands=(proj_rope, proj_rope, proj, proj, d_tab, qd_tab, kd_tab, cd_tab),
        out_specs=pl.BlockSpec((RET_L, RET_WIDTH), lambda b, c: (b * nsc + c, 0)),
        out_shape=jax.ShapeDtypeStruct((t, RET_WIDTH), BF16),
        scratch_shapes=[pltpu.VMEM((RET_HEADS, LANES, RET_V), F32)],
        temp_bytes=RET_HEADS * _nbytes((RET_L, RET_L), F32))


def retention_tables():
    log_gamma = jnp.log1p(-jnp.exp2(-5.0 - jnp.arange(RET_HEADS, dtype=F32)))
    idx = jnp.arange(RET_L, dtype=F32)
    dist = jnp.abs(idx[:, None] - idx[None, :])
    chunk = jnp.arange(RET_L) // CHUNK
    visible = chunk[None, :] <= chunk[:, None]
    d_tab = jnp.where(visible[None], jnp.exp(log_gamma[:, None, None] * dist[None]), 0.0)
    col_head = _pair_rope_order(RET_HEADS) // RET_QK
    lg_lane = log_gamma[col_head]
    qd_tab = jnp.exp(lg_lane[None, :] * (idx[:, None] + 1.0))
    kd_tab = jnp.exp(lg_lane[None, :] * (RET_L - 1.0 - idx[:, None]))
    cd = jnp.exp(log_gamma * RET_L)
    cd_tab = jnp.broadcast_to(cd[:, None, None], (RET_HEADS, LANES, RET_V))
    return d_tab, qd_tab, kd_tab, cd_tab


def _out_proj_kernel(a_ref, b_ref, c_ref, w_ref, o_ref):
    ka = a_ref.shape[1]
    kb = b_ref.shape[1]
    acc = jnp.dot(a_ref[...], w_ref[:ka, :], preferred_element_type=F32)
    acc = acc + jnp.dot(b_ref[...], w_ref[ka:ka + kb, :], preferred_element_type=F32)
    acc = acc + jnp.dot(c_ref[...], w_ref[ka + kb:, :], preferred_element_type=F32)
    o_ref[...] = acc.astype(o_ref.dtype)


def out_proj(a, b, c, w):
    t = a.shape[0]
    k, n = w.shape
    return _pallas(
        _out_proj_kernel, name="out_proj", grid=(t // TM, n // TN_OUT), semantics=("parallel", "arbitrary"),
        in_specs=[pl.BlockSpec((TM, a.shape[1]), lambda i, j: (i, 0)),
                  pl.BlockSpec((TM, b.shape[1]), lambda i, j: (i, 0)),
                  pl.BlockSpec((TM, c.shape[1]), lambda i, j: (i, 0)),
                  pl.BlockSpec((k, TN_OUT), lambda i, j: (0, j))],
        operands=(a, b, c, w),
        out_specs=pl.BlockSpec((TM, TN_OUT), lambda i, j: (i, j)),
        out_shape=jax.ShapeDtypeStruct((t, n), BF16),
        temp_bytes=_nbytes((TM, TN_OUT), F32))


def _ffn_kernel(h_ref, wu_ref, wd_ref, o_ref):
    @pl.when(pl.program_id(1) == 0)
    def _():
        o_ref[...] = jnp.zeros_like(o_ref)

    a = jnp.dot(h_ref[...], wu_ref[...], preferred_element_type=F32)
    a = jnp.square(jnp.maximum(a, 0.0)).astype(BF16)
    o_ref[...] += jnp.dot(a, wd_ref[...], preferred_element_type=F32)


def ffn(h, wu_b, wd_b):
    t, d = h.shape
    ff = wu_b.shape[1]
    return _pallas(
        _ffn_kernel, name="ffn", grid=(t // TM_FFN, ff // TF_FFN), semantics=("parallel", "arbitrary"),
        in_specs=[pl.BlockSpec((TM_FFN, d), lambda i, j: (i, 0)),
                  pl.BlockSpec((d, TF_FFN), lambda i, j: (0, j)),
                  pl.BlockSpec((TF_FFN, d), lambda i, j: (j, 0))],
        operands=(h, wu_b, wd_b),
        out_specs=pl.BlockSpec((TM_FFN, d), lambda i, j: (i, 0)),
        out_shape=jax.ShapeDtypeStruct((t, d), F32),
        temp_bytes=_nbytes((TM_FFN, TF_FFN), F32) + _nbytes((TM_FFN, TF_FFN), BF16))


_SRC_SIZES = (SGU_WIDTH, SGU_WIDTH, MLA_Q_RANK, MLA_KV_RANK, MLA_ROPE,
              RET_HEADS * RET_QK, RET_HEADS * RET_QK, RET_WIDTH, RET_WIDTH)
(_SRC_U, _SRC_V, _SRC_CQ, _SRC_CKV, _SRC_KR, _SRC_RQ, _SRC_RK, _SRC_RV, _SRC_RG,
 IN_WIDTH) = [int(o) for o in np.concatenate([[0], np.cumsum(_SRC_SIZES)])]
_W_IN_MOVES = (
    (COL_U, _SRC_U, 3 * IN_TILE),
    (COL_RV, _SRC_RV, RET_WIDTH),
    (COL_RG, _SRC_RG, RET_WIDTH),
    (COL_CKV, _SRC_CKV, MLA_KV_RANK),
    (COL_KR, _SRC_KR, ROPE_HALF),
    (COL_KR + LANES // 2, _SRC_KR + ROPE_HALF, ROPE_HALF),
    (COL_KR + LANES + ROPE_HALF, _SRC_KR, ROPE_HALF),
    (COL_KR + LANES + LANES // 2 + ROPE_HALF, _SRC_KR + ROPE_HALF, ROPE_HALF),
    *[(dst0 + i * ROPE_HALF, src0 + int(s0), ROPE_HALF)
      for dst0, src0 in ((COL_RQ, _SRC_RQ), (COL_RK, _SRC_RK))
      for i, s0 in enumerate(_pair_rope_order(RET_HEADS)[::ROPE_HALF])],
)


def _w_in_prep_kernel(w_ref, o_ref):
    o_ref[COL_KR:COL_RQ, :] = jnp.zeros((COL_RQ - COL_KR, o_ref.shape[1]), o_ref.dtype)
    for dst, src, width in _W_IN_MOVES:
        o_ref[dst:dst + width, :] = w_ref[src:src + width, :].astype(o_ref.dtype)


def prep_w_in(w_t, layer):
    _, n, d = w_t.shape
    return _pallas(
        _w_in_prep_kernel, name="prep_w_in", grid=(d // TK_PREP,), semantics=("parallel",),
        in_specs=[pl.BlockSpec((None, n, TK_PREP), lambda i: (layer, 0, i))],
        operands=(w_t,),
        out_specs=pl.BlockSpec((IN_WIDTH_PADDED, TK_PREP), lambda i: (0, i)),
        out_shape=jax.ShapeDtypeStruct((IN_WIDTH_PADDED, d), BF16))


def _prep_wq(w):
    k = w.shape[0]
    w3 = w.reshape(k, MLA_HEADS, MLA_NOPE + MLA_ROPE)
    nope = w3[:, :, :MLA_NOPE].reshape(k, MLA_HEADS * MLA_NOPE)
    rope = w3[:, :, MLA_NOPE:].reshape(k, MLA_HEADS * MLA_ROPE)[:, _pair_rope_order(MLA_HEADS)]
    return jnp.concatenate([nope, rope], axis=1).astype(BF16)


def _rope_lane_tables(seq):
    inv_freq = 1.0 / (ROPE_THETA ** (jnp.arange(0, MLA_ROPE, 2, dtype=F32) / MLA_ROPE))
    ang = jnp.arange(seq, dtype=F32)[:, None] * inv_freq[None, :]
    cos, sin = jnp.cos(ang), jnp.sin(ang)
    cos_t = jnp.concatenate([cos, cos, cos, cos], axis=1)
    sin_t = jnp.concatenate([-sin, -sin, sin, sin], axis=1)
    return cos_t, sin_t


def kernel(x, norm_mix_pre, norm_mix_post, norm_ffn_pre, norm_ffn_post, w_in, sgu_ln_g, sgu_ln_b, sgu_w, sgu_b, mla_q_norm, mla_wq_b, mla_kv_norm, mla_wkv_b, w_out, w_up, w_down):
    batch, seq, d = x.shape
    depth = w_in.shape[0]
    cos_t, sin_t = _rope_lane_tables(seq)
    d_tab, qd_tab, kd_tab, cd_tab = retention_tables()
    w_in_t = jnp.swapaxes(w_in, 1, 2)

    xf = x.reshape(batch * seq, d)
    h = rmsnorm_rows(xf, norm_mix_pre[0])
    for l in range(depth):
        proj, proj_rope = in_proj(h, prep_w_in(w_in_t, l), cos_t, sin_t, seq)
        bias_full = jnp.repeat(sgu_b[l].T, SGU_WIDTH // SGU_HEADS, axis=1)
        out_a = sgu_mixer(proj, sgu_ln_g[l], sgu_ln_b[l], sgu_w[l], bias_full)
        q = q_proj(proj, mla_q_norm[l], _prep_wq(mla_wq_b[l]), cos_t, sin_t, seq)
        kv = kv_proj(proj_rope, mla_kv_norm[l], mla_wkv_b, l)
        out_b, wu_b, wd_b, wo_b = mla_attention(q, kv, proj_rope, (w_up, w_down, w_out), l, batch, seq)
        out_c = retention_mixer(proj, proj_rope, d_tab, qd_tab, kd_tab, cd_tab, batch, seq)
        mixed = out_proj(out_a, out_b, out_c, wo_b)
        h2 = mix_norm(xf, mixed, norm_mix_post[l], norm_ffn_pre[l])
        f = ffn(h2, wu_b, wd_b)
        g_next = norm_mix_pre[l + 1] if l + 1 < depth else None
        xf, h = layer_residual(xf, mixed, norm_mix_post[l], f, norm_ffn_post[l], g_next)
    return xf.reshape(batch, seq, d)
```

```python
import functools
import math

import numpy as np
import jax
import jax.numpy as jnp
from jax import lax
from jax.experimental import pallas as pl
from jax.experimental.pallas import tpu as pltpu

F32 = jnp.float32
BF16 = jnp.bfloat16

CHUNK = 64
EPS = 1e-6
ROPE_THETA = 10000.0
SGU_BLOCK = 128
SGU_WIDTH = 1024
SGU_HEADS = 8
MLA_HEADS = 16
MLA_NOPE = 128
MLA_ROPE = 64
MLA_V = 128
MLA_Q_RANK = 1024
MLA_KV_RANK = 512
MLA_WIDTH = MLA_HEADS * MLA_V
RET_HEADS = 8
RET_QK = 64
RET_V = 128
RET_WIDTH = RET_HEADS * RET_V

LANES = 128
BF16_SUBLANES = 16
VMEM_CAP = 64 * 1024 * 1024
VMEM_RESERVED = 2 * 1024 * 1024
VMEM_INTERNAL = 6 * 1024 * 1024
PIPELINE_BUFFERS = 2

IN_TILE = 1024
COL_U = 0 * IN_TILE
COL_V = 1 * IN_TILE
COL_CQ = 2 * IN_TILE
COL_RV = 3 * IN_TILE
COL_RG = 4 * IN_TILE
COL_CKV = 5 * IN_TILE
COL_KR = COL_CKV + MLA_KV_RANK
COL_RQ = 6 * IN_TILE
COL_RK = COL_RQ + RET_HEADS * RET_QK
IN_WIDTH_PADDED = 7 * IN_TILE
ROPE_BASE = COL_CKV

TM = 1024
TM_QKV = 2048
NORM_TILES = (512, 256)
TM_FFN = 512
TF_FFN = 1024
TN_OUT = 1024
TK_PREP = 256
TB_SGU = 512
TQ = 512
ATTN_GROUP = 4
ATTN_HEADS = 4
RET_L = 256


def _nbytes(shape, dtype):
    return math.prod(1 if s is None else s for s in shape) * jnp.dtype(dtype).itemsize


def _pallas(kernel, *, name, grid, semantics, in_specs, operands, out_specs, out_shape,
            scratch_shapes=(), temp_bytes=0):
    outs = out_shape if isinstance(out_shape, (tuple, list)) else (out_shape,)
    ospecs = out_specs if isinstance(out_specs, (tuple, list)) else (out_specs,)
    pipelined = sum(_nbytes(s.block_shape, a.dtype) for s, a in zip(in_specs, operands))
    pipelined += sum(_nbytes(s.block_shape, o.dtype) for s, o in zip(ospecs, outs))
    scratch = sum(_nbytes(s.shape, s.dtype) for s in scratch_shapes)
    need = PIPELINE_BUFFERS * pipelined + scratch + temp_bytes + VMEM_INTERNAL
    params = pltpu.CompilerParams(dimension_semantics=semantics,
                                  vmem_limit_bytes=int(min(need, VMEM_CAP - VMEM_RESERVED)))
    return pl.pallas_call(kernel, out_shape=out_shape, grid=grid, in_specs=in_specs, out_specs=out_specs,
                          scratch_shapes=scratch_shapes, compiler_params=params, name=name)(*operands)


def _rms(x, g):
    return x * lax.rsqrt(jnp.mean(x * x, axis=-1, keepdims=True) + EPS) * g


def _rmsnorm_kernel(x_ref, g_ref, o_ref):
    o_ref[...] = _rms(x_ref[...], g_ref[...]).astype(o_ref.dtype)


def _norm_tile(d, row_dtypes):
    for tile in NORM_TILES:
        need = PIPELINE_BUFFERS * sum(_nbytes((tile, d), dt) for dt in row_dtypes)
        if need + _nbytes((tile, d), F32) + VMEM_INTERNAL <= VMEM_CAP - VMEM_RESERVED:
            return tile
    raise ValueError("no row tile fits VMEM")


def rmsnorm_rows(x, g):
    t, d = x.shape
    tm = _norm_tile(d, (x.dtype, BF16))
    return _pallas(
        _rmsnorm_kernel, name="rmsnorm_rows", grid=(t // tm,), semantics=("parallel",),
        in_specs=[pl.BlockSpec((tm, d), lambda i: (i, 0)),
                  pl.BlockSpec((1, d), lambda i: (0, 0))],
        operands=(x, g.reshape(1, d)),
        out_specs=pl.BlockSpec((tm, d), lambda i: (i, 0)),
        out_shape=jax.ShapeDtypeStruct((t, d), BF16),
        temp_bytes=_nbytes((tm, d), F32))


def _mix_norm_kernel(x_ref, y_ref, gp_ref, gn_ref, ho_ref):
    xm = x_ref[...] + _rms(y_ref[...].astype(F32), gp_ref[...])
    ho_ref[...] = _rms(xm, gn_ref[...]).astype(ho_ref.dtype)


def mix_norm(x, y, g_post, g_next):
    t, d = x.shape
    tm = _norm_tile(d, (x.dtype, y.dtype, BF16))
    row = pl.BlockSpec((tm, d), lambda i: (i, 0))
    vec = pl.BlockSpec((1, d), lambda i: (0, 0))
    return _pallas(_mix_norm_kernel, name="mix_norm", grid=(t // tm,), semantics=("parallel",),
                   in_specs=[row, row, vec, vec],
                   operands=(x, y, g_post.reshape(1, d), g_next.reshape(1, d)),
                   out_specs=row, out_shape=jax.ShapeDtypeStruct((t, d), BF16),
                   temp_bytes=_nbytes((tm, d), F32))


def _layer_residual(x_ref, y_ref, gy_ref, f_ref, gf_ref):
    xm = x_ref[...] + _rms(y_ref[...].astype(F32), gy_ref[...])
    return xm + _rms(f_ref[...].astype(F32), gf_ref[...])


def _layer_residual_norm_kernel(x_ref, y_ref, gy_ref, f_ref, gf_ref, gn_ref, xo_ref, ho_ref):
    xn = _layer_residual(x_ref, y_ref, gy_ref, f_ref, gf_ref)
    xo_ref[...] = xn
    ho_ref[...] = _rms(xn, gn_ref[...]).astype(ho_ref.dtype)


def _layer_residual_kernel(x_ref, y_ref, gy_ref, f_ref, gf_ref, xo_ref):
    xo_ref[...] = _layer_residual(x_ref, y_ref, gy_ref, f_ref, gf_ref)


def layer_residual(x, y, g_y, f, g_f, g_next):
    t, d = x.shape
    out_dtypes = (F32,) if g_next is None else (F32, BF16)
    tm = _norm_tile(d, (x.dtype, y.dtype, f.dtype) + out_dtypes)
    row = pl.BlockSpec((tm, d), lambda i: (i, 0))
    vec = pl.BlockSpec((1, d), lambda i: (0, 0))
    common = dict(grid=(t // tm,), semantics=("parallel",), temp_bytes=_nbytes((tm, d), F32))
    if g_next is None:
        return _pallas(_layer_residual_kernel, name="layer_residual", in_specs=[row, row, vec, row, vec],
                       operands=(x, y, g_y.reshape(1, d), f, g_f.reshape(1, d)), out_specs=row,
                       out_shape=jax.ShapeDtypeStruct((t, d), F32), **common), None
    return _pallas(_layer_residual_norm_kernel, name="layer_residual_norm",
                   in_specs=[row, row, vec, row, vec, vec],
                   operands=(x, y, g_y.reshape(1, d), f, g_f.reshape(1, d), g_next.reshape(1, d)),
                   out_specs=(row, row),
                   out_shape=(jax.ShapeDtypeStruct((t, d), F32), jax.ShapeDtypeStruct((t, d), BF16)),
                   **common)


ROPE_HALF = MLA_ROPE // 2


def _rope_cols(x, cos, sin):
    return x * cos + pltpu.roll(x, LANES // 2, 1) * sin


def _pair_rope_order(n_heads):
    src = np.arange(n_heads * MLA_ROPE).reshape(n_heads // 2, 2, 2, ROPE_HALF)
    return src.transpose(0, 2, 1, 3).reshape(-1)


def _in_proj_main_kernel(h_ref, w_ref, o_ref):
    o_ref[...] = lax.dot_general(h_ref[...], w_ref[...], (((1,), (1,)), ((), ())),
                                 preferred_element_type=F32).astype(o_ref.dtype)


def _in_proj_rope_kernel(h_ref, w_ref, cos_ref, sin_ref, o_ref):
    j = pl.program_id(1)
    acc = lax.dot_general(h_ref[...], w_ref[...], (((1,), (1,)), ((), ())), preferred_element_type=F32)

    @pl.when(j == 0)
    def _():
        lo = COL_KR - COL_CKV
        o_ref[:, :lo] = acc[:, :lo].astype(o_ref.dtype)
        for c in range(lo // LANES, lo // LANES + 2):
            sl = slice(c * LANES, (c + 1) * LANES)
            o_ref[:, sl] = _rope_cols(acc[:, sl], cos_ref[...], sin_ref[...]).astype(o_ref.dtype)
        o_ref[:, lo + 2 * LANES:] = acc[:, lo + 2 * LANES:].astype(o_ref.dtype)

    @pl.when(j == 1)
    def _():
        q_cols = (RET_HEADS * RET_QK) // LANES
        for c in range(IN_TILE // LANES):
            sl = slice(c * LANES, (c + 1) * LANES)
            r = _rope_cols(acc[:, sl], cos_ref[...], sin_ref[...])
            if c < q_cols:
                r = r * (RET_QK ** -0.5)
            o_ref[:, sl] = r.astype(o_ref.dtype)


def in_proj(h, w_t, cos_t, sin_t, seq):
    t, d = h.shape
    pos_blocks = seq // TM
    tab = pl.BlockSpec((TM, LANES), lambda i, j: (i % pos_blocks, 0))
    rope_tile0 = ROPE_BASE // IN_TILE
    rope_w = IN_WIDTH_PADDED - ROPE_BASE
    common = dict(semantics=("parallel", "arbitrary"), temp_bytes=_nbytes((TM, IN_TILE), F32),
                  out_specs=pl.BlockSpec((TM, IN_TILE), lambda i, j: (i, j)))
    main = _pallas(
        _in_proj_main_kernel, name="in_proj_main", grid=(t // TM, rope_tile0),
        in_specs=[pl.BlockSpec((TM, d), lambda i, j: (i, 0)),
                  pl.BlockSpec((IN_TILE, d), lambda i, j: (j, 0))],
        operands=(h, w_t), out_shape=jax.ShapeDtypeStruct((t, ROPE_BASE), BF16), **common)
    rope = _pallas(
        _in_proj_rope_kernel, name="in_proj_rope", grid=(t // TM, rope_w // IN_TILE),
        in_specs=[pl.BlockSpec((TM, d), lambda i, j: (i, 0)),
                  pl.BlockSpec((IN_TILE, d), lambda i, j: (rope_tile0 + j, 0)),
                  tab, tab],
        operands=(h, w_t, cos_t, sin_t), out_shape=jax.ShapeDtypeStruct((t, rope_w), BF16), **common)
    return main, rope


def _sgu_kernel(u_ref, v_ref, g_ref, b_ref, w_ref, bias_ref, o_ref):
    row = lax.broadcasted_iota(jnp.int32, (SGU_BLOCK, SGU_BLOCK), 0)
    col = lax.broadcasted_iota(jnp.int32, (SGU_BLOCK, SGU_BLOCK), 1)
    keep = (col // CHUNK) <= (row // CHUNK)
    for g in range(SGU_HEADS):
        sl = slice(g * LANES, (g + 1) * LANES)
        wg = jnp.where(keep, w_ref[g], 0.0).astype(BF16)
        vg = v_ref[:, sl].astype(F32)
        mu = jnp.mean(vg, axis=-1, keepdims=True)
        vc = vg - mu
        y = vc * lax.rsqrt(jnp.mean(vc * vc, axis=-1, keepdims=True) + EPS) * g_ref[:, sl] + b_ref[:, sl]
        yb = y.astype(BF16)
        for n in range(TB_SGU // SGU_BLOCK):
            rs = slice(n * SGU_BLOCK, (n + 1) * SGU_BLOCK)
            mixed = jnp.dot(wg, yb[rs, :], preferred_element_type=F32) + bias_ref[:, sl]
            o_ref[rs, sl] = (u_ref[rs, sl].astype(F32) * mixed).astype(o_ref.dtype)


def sgu_mixer(proj, ln_g, ln_b, w_s, bias_full):
    t = proj.shape[0]
    vec = pl.BlockSpec((1, SGU_WIDTH), lambda i: (0, 0))
    return _pallas(
        _sgu_kernel, name="sgu_mixer", grid=(t // TB_SGU,), semantics=("parallel",),
        in_specs=[pl.BlockSpec((TB_SGU, SGU_WIDTH), lambda i: (i, COL_U // SGU_WIDTH)),
                  pl.BlockSpec((TB_SGU, SGU_WIDTH), lambda i: (i, COL_V // SGU_WIDTH)),
                  vec, vec,
                  pl.BlockSpec((SGU_HEADS, SGU_BLOCK, SGU_BLOCK), lambda i: (0, 0, 0)),
                  pl.BlockSpec((SGU_BLOCK, SGU_WIDTH), lambda i: (0, 0))],
        operands=(proj, proj, ln_g.reshape(1, -1), ln_b.reshape(1, -1), w_s, bias_full),
        out_specs=pl.BlockSpec((TB_SGU, SGU_WIDTH), lambda i: (i, 0)),
        out_shape=jax.ShapeDtypeStruct((t, SGU_WIDTH), BF16),
        temp_bytes=_nbytes((TB_SGU, SGU_WIDTH), F32))


Q_SCALE = (MLA_NOPE + MLA_ROPE) ** -0.5 * math.log2(math.e)
Q_ROPE_TILE = (MLA_HEADS * MLA_NOPE) // IN_TILE


def _q_proj_kernel(c_ref, g_ref, w_ref, cos_ref, sin_ref, o_ref, cn_ref):
    j = pl.program_id(1)

    @pl.when(j == 0)
    def _():
        cn_ref[...] = _rms(c_ref[...].astype(F32), g_ref[...]).astype(cn_ref.dtype)

    acc = jnp.dot(cn_ref[...], w_ref[...], preferred_element_type=F32) * Q_SCALE

    @pl.when(j < Q_ROPE_TILE)
    def _():
        o_ref[...] = acc.astype(o_ref.dtype)

    @pl.when(j >= Q_ROPE_TILE)
    def _():
        for c in range(IN_TILE // LANES):
            sl = slice(c * LANES, (c + 1) * LANES)
            o_ref[:, sl] = _rope_cols(acc[:, sl], cos_ref[...], sin_ref[...]).astype(o_ref.dtype)


def q_proj(proj, g, w, cos_t, sin_t, seq):
    t = proj.shape[0]
    n = w.shape[1]
    pos_blocks = seq // TM_QKV
    tab = pl.BlockSpec((TM_QKV, LANES), lambda i, j: (i % pos_blocks, 0))
    return _pallas(
        _q_proj_kernel, name="q_proj", grid=(t // TM_QKV, n // IN_TILE), semantics=("parallel", "arbitrary"),
        in_specs=[pl.BlockSpec((TM_QKV, MLA_Q_RANK), lambda i, j: (i, COL_CQ // MLA_Q_RANK)),
                  pl.BlockSpec((1, MLA_Q_RANK), lambda i, j: (0, 0)),
                  pl.BlockSpec((MLA_Q_RANK, IN_TILE), lambda i, j: (0, j)),
                  tab, tab],
        operands=(proj, g.reshape(1, -1), w, cos_t, sin_t),
        out_specs=pl.BlockSpec((TM_QKV, IN_TILE), lambda i, j: (i, j)),
        out_shape=jax.ShapeDtypeStruct((t, n), BF16),
        scratch_shapes=[pltpu.VMEM((TM_QKV, MLA_Q_RANK), BF16)],
        temp_bytes=_nbytes((TM_QKV, IN_TILE), F32))


def _kv_proj_kernel(c_ref, g_ref, w_ref, o_ref, cn_ref):
    @pl.when(pl.program_id(1) == 0)
    def _():
        cn_ref[...] = _rms(c_ref[...].astype(F32), g_ref[...]).astype(cn_ref.dtype)

    o_ref[...] = jnp.dot(cn_ref[...], w_ref[...].astype(BF16), preferred_element_type=F32).astype(o_ref.dtype)


def kv_proj(proj_rope, g, w, layer):
    t = proj_rope.shape[0]
    n = w.shape[2]
    return _pallas(
        _kv_proj_kernel, name="kv_proj", grid=(t // TM_QKV, n // IN_TILE), semantics=("parallel", "arbitrary"),
        in_specs=[pl.BlockSpec((TM_QKV, MLA_KV_RANK), lambda i, j: (i, (COL_CKV - ROPE_BASE) // MLA_KV_RANK)),
                  pl.BlockSpec((1, MLA_KV_RANK), lambda i, j: (0, 0)),
                  pl.BlockSpec((None, MLA_KV_RANK, IN_TILE), lambda i, j: (layer, 0, j))],
        operands=(proj_rope, g.reshape(1, -1), w),
        out_specs=pl.BlockSpec((TM_QKV, IN_TILE), lambda i, j: (i, j)),
        out_shape=jax.ShapeDtypeStruct((t, n), BF16),
        scratch_shapes=[pltpu.VMEM((TM_QKV, MLA_KV_RANK), BF16)],
        temp_bytes=_nbytes((TM_QKV, IN_TILE), F32) + _nbytes((MLA_KV_RANK, IN_TILE), BF16))


HEAD_W = MLA_NOPE + MLA_V


def _attn_tiles(qs, kv_ref, kr_ref, base, ntiles, mask_last, states):
    states = list(states)
    ones = jnp.ones((TQ, LANES), BF16)
    for t in range(ntiles):
        st = pl.multiple_of(base + t * TQ, TQ)
        for hd in range(ATTN_HEADS):
            m, l, acc = states[hd]
            kn = kv_ref[pl.ds(st, TQ), hd * HEAD_W:hd * HEAD_W + MLA_NOPE]
            v = kv_ref[pl.ds(st, TQ), hd * HEAD_W + MLA_NOPE:(hd + 1) * HEAD_W]
            par = hd % 2
            k = jnp.concatenate([kn, kr_ref[pl.ds(st, TQ), par * LANES:(par + 1) * LANES]], axis=1)
            s = lax.dot_general(qs[hd], k, (((1,), (1,)), ((), ())), preferred_element_type=F32)
            if mask_last and t == ntiles - 1:
                row = lax.broadcasted_iota(jnp.int32, (TQ, TQ), 0)
                col = lax.broadcasted_iota(jnp.int32, (TQ, TQ), 1)
                s = jnp.where((col // CHUNK) <= (row // CHUNK), s, -jnp.inf)
            m_next = jnp.maximum(m, jnp.max(s, axis=1, keepdims=True))
            alpha = jnp.exp2(m - m_next)
            p = jnp.exp2(s - jnp.tile(m_next, (1, TQ // LANES))).astype(BF16)
            pv = jnp.dot(p, jnp.concatenate([v, ones], axis=1), preferred_element_type=F32)
            states[hd] = (m_next, alpha * l + pv[:, LANES:], alpha * acc + pv[:, :LANES])
    return states


def _attn_kernel(qn_ref, qr_ref, kv_ref, kr_ref, *rest, cast_from, n_cast):
    w_refs, o_ref = rest[:n_cast], rest[n_cast]
    wb_refs = rest[n_cast + 1:2 * n_cast + 1]
    m_ref, l_ref, acc_ref = rest[2 * n_cast + 1:]
    qi = pl.program_id(2)

    @pl.when(qi >= cast_from)
    def _():
        for w_ref, wb_ref in zip(w_refs, wb_refs):
            wb_ref[...] = w_ref[...].astype(wb_ref.dtype)

    qs = [jnp.concatenate([qn_ref[:, hd * LANES:(hd + 1) * LANES],
                           qr_ref[:, (hd // 2) * LANES:(hd // 2 + 1) * LANES]], axis=1)
          for hd in range(ATTN_HEADS)]

    m_ref[...] = jnp.full_like(m_ref, -jnp.inf)
    l_ref[...] = jnp.zeros_like(l_ref)
    acc_ref[...] = jnp.zeros_like(acc_ref)

    def load_states():
        return [(m_ref[hd], l_ref[hd], acc_ref[hd]) for hd in range(ATTN_HEADS)]

    def body(g, carry):
        base = pl.multiple_of(g * (ATTN_GROUP * TQ), ATTN_GROUP * TQ)
        states = _attn_tiles(qs, kv_ref, kr_ref, base, ATTN_GROUP, False, load_states())
        for hd in range(ATTN_HEADS):
            m_ref[hd], l_ref[hd], acc_ref[hd] = states[hd]
        return carry

    lax.fori_loop(0, qi // ATTN_GROUP, body, 0)

    base = pl.multiple_of((qi // ATTN_GROUP) * (ATTN_GROUP * TQ), ATTN_GROUP * TQ)
    for r in range(ATTN_GROUP):
        @pl.when(qi % ATTN_GROUP == r)
        def _():
            states = _attn_tiles(qs, kv_ref, kr_ref, base, r + 1, True, load_states())
            for hd in range(ATTN_HEADS):
                _, l, acc = states[hd]
                o_ref[:, hd * LANES:(hd + 1) * LANES] = (acc / l).astype(o_ref.dtype)


def mla_attention(q, kv, proj_rope, weights, layer, batch, seq):
    t = q.shape[0]
    nq = seq // TQ
    groups = MLA_HEADS // ATTN_HEADS
    cast_from = 0
    casts = nq - cast_from
    steps = batch * groups * casts
    slab_rows = [w.shape[1] // steps for w in weights]
    assert all(r * steps == w.shape[1] and r % BF16_SUBLANES == 0 for r, w in zip(slab_rows, weights))

    def slab(b, g, i):
        return (b * groups + g) * casts + jnp.maximum(i - cast_from, 0)
    nope_w = ATTN_HEADS * MLA_NOPE
    rope_w = ATTN_HEADS * MLA_ROPE
    kr_w = 2 * LANES
    kr_block = (COL_KR - ROPE_BASE) // kr_w
    rope_block0 = (MLA_HEADS * MLA_NOPE) // rope_w
    return _pallas(
        functools.partial(_attn_kernel, cast_from=cast_from, n_cast=len(weights)),
        name="mla_attention", grid=(batch, groups, nq), semantics=("parallel", "parallel", "arbitrary"),
        in_specs=[pl.BlockSpec((TQ, nope_w), lambda b, p, i: (b * nq + i, p)),
                  pl.BlockSpec((TQ, rope_w), lambda b, p, i: (b * nq + i, rope_block0 + p)),
                  pl.BlockSpec((seq, ATTN_HEADS * HEAD_W), lambda b, p, i: (b, p)),
                  pl.BlockSpec((seq, kr_w), lambda b, p, i: (b, kr_block)),
                  *[pl.BlockSpec((None, r, w.shape[2]), lambda b, p, i: (layer, slab(b, p, i), 0))
                    for r, w in zip(slab_rows, weights)]],
        operands=(q, q, kv, proj_rope, *weights),
        out_specs=(pl.BlockSpec((TQ, nope_w), lambda b, p, i: (b * nq + i, p)),
                   *[pl.BlockSpec((r, w.shape[2]), lambda b, p, i: (slab(b, p, i), 0))
                     for r, w in zip(slab_rows, weights)]),
        out_shape=(jax.ShapeDtypeStruct((t, MLA_WIDTH), BF16),
                   *[jax.ShapeDtypeStruct(w.shape[1:], BF16) for w in weights]),
        scratch_shapes=[pltpu.VMEM((ATTN_HEADS, TQ, LANES), F32)] * 3,
        temp_bytes=ATTN_HEADS * (_nbytes((TQ, TQ), F32) + _nbytes((TQ, TQ), BF16)))


def _ret_kernel(q_ref, k_ref, v_ref, g_ref, d_ref, qd_ref, kd_ref, cd_ref, o_ref, s_ref):
    @pl.when(pl.program_id(1) == 0)
    def _():
        s_ref[...] = jnp.zeros_like(s_ref)

    lane = lax.broadcasted_iota(jnp.int32, (RET_L, LANES), 1)
    for p in range(RET_HEADS // 2):
        ls = slice(p * LANES, (p + 1) * LANES)
        q = q_ref[:, ls]
        k = k_ref[:, ls]
        qf = q.astype(F32)
        kdec = (k.astype(F32) * kd_ref[:, ls]).astype(BF16)
        for hh in range(2):
            head = 2 * p + hh
            own = (lane // ROPE_HALF) % 2 == hh
            vs = slice(head * RET_V, (head + 1) * RET_V)
            v = v_ref[:, vs]
            qm = jnp.where(own, q, jnp.zeros_like(q))
            s = lax.dot_general(qm, k, (((1,), (1,)), ((), ())), preferred_element_type=F32)
            sd = (s * d_ref[head]).astype(BF16)
            qx = jnp.where(own, qf * qd_ref[:, ls], 0.0).astype(BF16)
            state = s_ref[head]
            o = (jnp.dot(sd, v, preferred_element_type=F32)
                 + jnp.dot(qx, state.astype(BF16), preferred_element_type=F32))
            s_ref[head] = cd_ref[head] * state + lax.dot_general(
                kdec, v, (((0,), (0,)), ((), ())), preferred_element_type=F32)
            o = o * lax.rsqrt(jnp.mean(o * o, axis=-1, keepdims=True) + EPS)
            gate = g_ref[:, vs].astype(F32)
            o_ref[:, vs] = (gate * jax.nn.sigmoid(gate) * o).astype(o_ref.dtype)


def retention_mixer(proj, proj_rope, d_tab, qd_tab, kd_tab, cd_tab, batch, seq):
    t = proj.shape[0]
    nsc = seq // RET_L
    qk_w = RET_HEADS * RET_QK
    full = lambda b, c: (0, 0)
    full3 = lambda b, c: (0, 0, 0)
    return _pallas(
        _ret_kernel, name="retention_mixer", grid=(batch, nsc), semantics=("parallel", "arbitrary"),
        in_specs=[pl.BlockSpec((RET_L, qk_w), lambda b, c: (b * nsc + c, (COL_RQ - ROPE_BASE) // qk_w)),
                  pl.BlockSpec((RET_L, qk_w), lambda b, c: (b * nsc + c, (COL_RK - ROPE_BASE) // qk_w)),
                  pl.BlockSpec((RET_L, RET_WIDTH), lambda b, c: (b * nsc + c, COL_RV // RET_WIDTH)),
                  pl.BlockSpec((RET_L, RET_WIDTH), lambda b, c: (b * nsc + c, COL_RG // RET_WIDTH)),
                  pl.BlockSpec((RET_HEADS, RET_L, RET_L), full3),
                  pl.BlockSpec((RET_L, qk_w), full),
                  pl.BlockSpec((RET_L, qk_w), full),
                  pl.BlockSpec((RET_HEADS, LANES, RET_V), full3)],
        operands=(proj_rope, proj_rope, proj, proj, d_tab, qd_tab, kd_tab, cd_tab),
        out_specs=pl.BlockSpec((RET_L, RET_WIDTH), lambda b, c: (b * nsc + c, 0)),
        out_shape=jax.ShapeDtypeStruct((t, RET_WIDTH), BF16),
        scratch_shapes=[pltpu.VMEM((RET_HEADS, LANES, RET_V), F32)],
        temp_bytes=RET_HEADS * _nbytes((RET_L, RET_L), F32))


def retention_tables():
    log_gamma = jnp.log1p(-jnp.exp2(-5.0 - jnp.arange(RET_HEADS, dtype=F32)))
    idx = jnp.arange(RET_L, dtype=F32)
    dist = jnp.abs(idx[:, None] - idx[None, :])
    chunk = jnp.arange(RET_L) // CHUNK
    visible = chunk[None, :] <= chunk[:, None]
    d_tab = jnp.where(visible[None], jnp.exp(log_gamma[:, None, None] * dist[None]), 0.0)
    col_head = _pair_rope_order(RET_HEADS) // RET_QK
    lg_lane = log_gamma[col_head]
    qd_tab = jnp.exp(lg_lane[None, :] * (idx[:, None] + 1.0))
    kd_tab = jnp.exp(lg_lane[None, :] * (RET_L - 1.0 - idx[:, None]))
    cd = jnp.exp(log_gamma * RET_L)
    cd_tab = jnp.broadcast_to(cd[:, None, None], (RET_HEADS, LANES, RET_V))
    return d_tab, qd_tab, kd_tab, cd_tab


def _out_proj_kernel(a_ref, b_ref, c_ref, w_ref, *rest):
    if len(rest) == 3:
        wt_ref, o_ref, wtb_ref = rest
        _w_in_prep_kernel(wt_ref, wtb_ref)
    else:
        o_ref, = rest
    ka = a_ref.shape[1]
    kb = b_ref.shape[1]
    acc = jnp.dot(a_ref[...], w_ref[:ka, :], preferred_element_type=F32)
    acc = acc + jnp.dot(b_ref[...], w_ref[ka:ka + kb, :], preferred_element_type=F32)
    acc = acc + jnp.dot(c_ref[...], w_ref[ka + kb:, :], preferred_element_type=F32)
    o_ref[...] = acc.astype(o_ref.dtype)


def out_proj(a, b, c, w, w_in_t=None, next_layer=None):
    t = a.shape[0]
    k, n = w.shape
    nj = n // TN_OUT
    steps = (t // TM) * nj
    in_specs = [pl.BlockSpec((TM, a.shape[1]), lambda i, j: (i, 0)),
                pl.BlockSpec((TM, b.shape[1]), lambda i, j: (i, 0)),
                pl.BlockSpec((TM, c.shape[1]), lambda i, j: (i, 0)),
                pl.BlockSpec((k, TN_OUT), lambda i, j: (0, j))]
    operands = (a, b, c, w)
    out_specs = pl.BlockSpec((TM, TN_OUT), lambda i, j: (i, j))
    out_shape = jax.ShapeDtypeStruct((t, n), BF16)
    if w_in_t is not None:
        _, rows, d = w_in_t.shape
        slab = d // steps
        assert slab * steps == d and slab % LANES == 0
        in_specs.append(pl.BlockSpec((None, rows, slab), lambda i, j: (next_layer, 0, i * nj + j)))
        operands += (w_in_t,)
        out_specs = (out_specs, pl.BlockSpec((IN_WIDTH_PADDED, slab), lambda i, j: (0, i * nj + j)))
        out_shape = (out_shape, jax.ShapeDtypeStruct((IN_WIDTH_PADDED, d), BF16))
    return _pallas(
        _out_proj_kernel, name="out_proj", grid=(t // TM, nj), semantics=("parallel", "arbitrary"),
        in_specs=in_specs, operands=operands, out_specs=out_specs, out_shape=out_shape,
        temp_bytes=_nbytes((TM, TN_OUT), F32))


def _ffn_kernel(h_ref, wu_ref, wd_ref, o_ref):
    @pl.when(pl.program_id(1) == 0)
    def _():
        o_ref[...] = jnp.zeros_like(o_ref)

    a = jnp.dot(h_ref[...], wu_ref[...], preferred_element_type=F32)
    a = jnp.square(jnp.maximum(a, 0.0)).astype(BF16)
    o_ref[...] += jnp.dot(a, wd_ref[...], preferred_element_type=F32)


def ffn(h, wu_b, wd_b):
    t, d = h.shape
    ff = wu_b.shape[1]
    return _pallas(
        _ffn_kernel, name="ffn", grid=(t // TM_FFN, ff // TF_FFN), semantics=("parallel", "arbitrary"),
        in_specs=[pl.BlockSpec((TM_FFN, d), lambda i, j: (i, 0)),
                  pl.BlockSpec((d, TF_FFN), lambda i, j: (0, j)),
                  pl.BlockSpec((TF_FFN, d), lambda i, j: (j, 0))],
        operands=(h, wu_b, wd_b),
        out_specs=pl.BlockSpec((TM_FFN, d), lambda i, j: (i, 0)),
        out_shape=jax.ShapeDtypeStruct((t, d), F32),
        temp_bytes=_nbytes((TM_FFN, TF_FFN), F32) + _nbytes((TM_FFN, TF_FFN), BF16))


_SRC_SIZES = (SGU_WIDTH, SGU_WIDTH, MLA_Q_RANK, MLA_KV_RANK, MLA_ROPE,
              RET_HEADS * RET_QK, RET_HEADS * RET_QK, RET_WIDTH, RET_WIDTH)
(_SRC_U, _SRC_V, _SRC_CQ, _SRC_CKV, _SRC_KR, _SRC_RQ, _SRC_RK, _SRC_RV, _SRC_RG,
 IN_WIDTH) = [int(o) for o in np.concatenate([[0], np.cumsum(_SRC_SIZES)])]
_W_IN_MOVES = (
    (COL_U, _SRC_U, 3 * IN_TILE),
    (COL_RV, _SRC_RV, RET_WIDTH),
    (COL_RG, _SRC_RG, RET_WIDTH),
    (COL_CKV, _SRC_CKV, MLA_KV_RANK),
    (COL_KR, _SRC_KR, ROPE_HALF),
    (COL_KR + LANES // 2, _SRC_KR + ROPE_HALF, ROPE_HALF),
    (COL_KR + LANES + ROPE_HALF, _SRC_KR, ROPE_HALF),
    (COL_KR + LANES + LANES // 2 + ROPE_HALF, _SRC_KR + ROPE_HALF, ROPE_HALF),
    *[(dst0 + i * ROPE_HALF, src0 + int(s0), ROPE_HALF)
      for dst0, src0 in ((COL_RQ, _SRC_RQ), (COL_RK, _SRC_RK))
      for i, s0 in enumerate(_pair_rope_order(RET_HEADS)[::ROPE_HALF])],
)


def _w_in_prep_kernel(w_ref, o_ref):
    o_ref[COL_KR:COL_RQ, :] = jnp.zeros((COL_RQ - COL_KR, o_ref.shape[1]), o_ref.dtype)
    for dst, src, width in _W_IN_MOVES:
        o_ref[dst:dst + width, :] = w_ref[src:src + width, :].astype(o_ref.dtype)


def prep_w_in(w_t, layer):
    _, n, d = w_t.shape
    return _pallas(
        _w_in_prep_kernel, name="prep_w_in", grid=(d // TK_PREP,), semantics=("parallel",),
        in_specs=[pl.BlockSpec((None, n, TK_PREP), lambda i: (layer, 0, i))],
        operands=(w_t,),
        out_specs=pl.BlockSpec((IN_WIDTH_PADDED, TK_PREP), lambda i: (0, i)),
        out_shape=jax.ShapeDtypeStruct((IN_WIDTH_PADDED, d), BF16))


def _prep_wq(w):
    k = w.shape[0]
    w3 = w.reshape(k, MLA_HEADS, MLA_NOPE + MLA_ROPE)
    nope = w3[:, :, :MLA_NOPE].reshape(k, MLA_HEADS * MLA_NOPE)
    rope = w3[:, :, MLA_NOPE:].reshape(k, MLA_HEADS * MLA_ROPE)[:, _pair_rope_order(MLA_HEADS)]
    return jnp.concatenate([nope, rope], axis=1).astype(BF16)


def _rope_lane_tables(seq):
    inv_freq = 1.0 / (ROPE_THETA ** (jnp.arange(0, MLA_ROPE, 2, dtype=F32) / MLA_ROPE))
    ang = jnp.arange(seq, dtype=F32)[:, None] * inv_freq[None, :]
    cos, sin = jnp.cos(ang), jnp.sin(ang)
    cos_t = jnp.concatenate([cos, cos, cos, cos], axis=1)
    sin_t = jnp.concatenate([-sin, -sin, sin, sin], axis=1)
    return cos_t, sin_t


def kernel(x, norm_mix_pre, norm_mix_post, norm_ffn_pre, norm_ffn_post, w_in, sgu_ln_g, sgu_ln_b, sgu_w, sgu_b, mla_q_norm, mla_wq_b, mla_kv_norm, mla_wkv_b, w_out, w_up, w_down):
    batch, seq, d = x.shape
    depth = w_in.shape[0]
    cos_t, sin_t = _rope_lane_tables(seq)
    d_tab, qd_tab, kd_tab, cd_tab = retention_tables()
    w_in_t = jnp.swapaxes(w_in, 1, 2)

    xf = x.reshape(batch * seq, d)
    h = rmsnorm_rows(xf, norm_mix_pre[0])
    w_in_prepped = prep_w_in(w_in_t, 0)
    for l in range(depth):
        proj, proj_rope = in_proj(h, w_in_prepped, cos_t, sin_t, seq)
        bias_full = jnp.repeat(sgu_b[l].T, SGU_WIDTH // SGU_HEADS, axis=1)
        out_a = sgu_mixer(proj, sgu_ln_g[l], sgu_ln_b[l], sgu_w[l], bias_full)
        q = q_proj(proj, mla_q_norm[l], _prep_wq(mla_wq_b[l]), cos_t, sin_t, seq)
        kv = kv_proj(proj_rope, mla_kv_norm[l], mla_wkv_b, l)
        out_b, wu_b, wd_b, wo_b = mla_attention(q, kv, proj_rope, (w_up, w_down, w_out), l, batch, seq)
        out_c = retention_mixer(proj, proj_rope, d_tab, qd_tab, kd_tab, cd_tab, batch, seq)
        if l + 1 < depth:
            mixed, w_in_prepped = out_proj(out_a, out_b, out_c, wo_b, w_in_t, l + 1)
        else:
            mixed = out_proj(out_a, out_b, out_c, wo_b)
        h2 = mix_norm(xf, mixed, norm_mix_post[l], norm_ffn_pre[l])
        f = ffn(h2, wu_b, wd_b)
        g_next = norm_mix_pre[l + 1] if l + 1 < depth else None
        xf, h = layer_residual(xf, mixed, norm_mix_post[l], f, norm_ffn_post[l], g_next)
    return xf.reshape(batch, seq, d)
```

```python
import functools
import math

import numpy as np
import jax
import jax.numpy as jnp
from jax import lax
from jax.experimental import pallas as pl
from jax.experimental.pallas import tpu as pltpu

F32 = jnp.float32
BF16 = jnp.bfloat16

CHUNK = 64
EPS = 1e-6
ROPE_THETA = 10000.0
SGU_BLOCK = 128
SGU_WIDTH = 1024
SGU_HEADS = 8
MLA_HEADS = 16
MLA_NOPE = 128
MLA_ROPE = 64
MLA_V = 128
MLA_Q_RANK = 1024
MLA_KV_RANK = 512
MLA_WIDTH = MLA_HEADS * MLA_V
RET_HEADS = 8
RET_QK = 64
RET_V = 128
RET_WIDTH = RET_HEADS * RET_V

LANES = 128
BF16_SUBLANES = 16
VMEM_CAP = 64 * 1024 * 1024
VMEM_RESERVED = 2 * 1024 * 1024
VMEM_INTERNAL = 6 * 1024 * 1024
PIPELINE_BUFFERS = 2

IN_TILE = 1024
COL_U = 0 * IN_TILE
COL_V = 1 * IN_TILE
COL_CQ = 2 * IN_TILE
COL_RV = 3 * IN_TILE
COL_RG = 4 * IN_TILE
COL_CKV = 5 * IN_TILE
COL_KR = COL_CKV + MLA_KV_RANK
COL_RQ = 6 * IN_TILE
COL_RK = COL_RQ + RET_HEADS * RET_QK
IN_WIDTH_PADDED = 7 * IN_TILE
ROPE_BASE = COL_CKV

TM = 1024
TM_QKV = 2048
NORM_TILES = (512, 256)
TM_FFN = 512
TF_FFN = 1024
TN_OUT = 1024
TK_PREP = 256
TB_SGU = 512
TQ = 512
ATTN_GROUP = 4
ATTN_HEADS = 4
RET_L = 256


def _nbytes(shape, dtype):
    return math.prod(1 if s is None else s for s in shape) * jnp.dtype(dtype).itemsize


def _pallas(kernel, *, name, grid, semantics, in_specs, operands, out_specs, out_shape,
            scratch_shapes=(), temp_bytes=0):
    outs = out_shape if isinstance(out_shape, (tuple, list)) else (out_shape,)
    ospecs = out_specs if isinstance(out_specs, (tuple, list)) else (out_specs,)
    pipelined = sum(_nbytes(s.block_shape, a.dtype) for s, a in zip(in_specs, operands))
    pipelined += sum(_nbytes(s.block_shape, o.dtype) for s, o in zip(ospecs, outs))
    scratch = sum(_nbytes(s.shape, s.dtype) for s in scratch_shapes)
    need = PIPELINE_BUFFERS * pipelined + scratch + temp_bytes + VMEM_INTERNAL
    params = pltpu.CompilerParams(dimension_semantics=semantics,
                                  vmem_limit_bytes=int(min(need, VMEM_CAP - VMEM_RESERVED)))
    return pl.pallas_call(kernel, out_shape=out_shape, grid=grid, in_specs=in_specs, out_specs=out_specs,
                          scratch_shapes=scratch_shapes, compiler_params=params, name=name)(*operands)


def _rms(x, g):
    return x * lax.rsqrt(jnp.mean(x * x, axis=-1, keepdims=True) + EPS) * g


def _rmsnorm_kernel(x_ref, g_ref, o_ref):
    o_ref[...] = _rms(x_ref[...], g_ref[...]).astype(o_ref.dtype)


def _norm_tile(d, row_dtypes):
    for tile in NORM_TILES:
        need = PIPELINE_BUFFERS * sum(_nbytes((tile, d), dt) for dt in row_dtypes)
        if need + _nbytes((tile, d), F32) + VMEM_INTERNAL <= VMEM_CAP - VMEM_RESERVED:
            return tile
    raise ValueError("no row tile fits VMEM")


def rmsnorm_rows(x, g):
    t, d = x.shape
    tm = _norm_tile(d, (x.dtype, BF16))
    return _pallas(
        _rmsnorm_kernel, name="rmsnorm_rows", grid=(t // tm,), semantics=("parallel",),
        in_specs=[pl.BlockSpec((tm, d), lambda i: (i, 0)),
                  pl.BlockSpec((1, d), lambda i: (0, 0))],
        operands=(x, g.reshape(1, d)),
        out_specs=pl.BlockSpec((tm, d), lambda i: (i, 0)),
        out_shape=jax.ShapeDtypeStruct((t, d), BF16),
        temp_bytes=_nbytes((tm, d), F32))


def _mix_norm_kernel(x_ref, y_ref, gp_ref, gn_ref, ho_ref):
    xm = x_ref[...] + _rms(y_ref[...].astype(F32), gp_ref[...])
    ho_ref[...] = _rms(xm, gn_ref[...]).astype(ho_ref.dtype)


def mix_norm(x, y, g_post, g_next):
    t, d = x.shape
    tm = _norm_tile(d, (x.dtype, y.dtype, BF16))
    row = pl.BlockSpec((tm, d), lambda i: (i, 0))
    vec = pl.BlockSpec((1, d), lambda i: (0, 0))
    return _pallas(_mix_norm_kernel, name="mix_norm", grid=(t // tm,), semantics=("parallel",),
                   in_specs=[row, row, vec, vec],
                   operands=(x, y, g_post.reshape(1, d), g_next.reshape(1, d)),
                   out_specs=row, out_shape=jax.ShapeDtypeStruct((t, d), BF16),
                   temp_bytes=_nbytes((tm, d), F32))


def _layer_residual(x_ref, y_ref, gy_ref, f_ref, gf_ref):
    xm = x_ref[...] + _rms(y_ref[...].astype(F32), gy_ref[...])
    return xm + _rms(f_ref[...].astype(F32), gf_ref[...])


def _layer_residual_norm_kernel(x_ref, y_ref, gy_ref, f_ref, gf_ref, gn_ref, xo_ref, ho_ref):
    xn = _layer_residual(x_ref, y_ref, gy_ref, f_ref, gf_ref)
    xo_ref[...] = xn
    ho_ref[...] = _rms(xn, gn_ref[...]).astype(ho_ref.dtype)


def _layer_residual_kernel(x_ref, y_ref, gy_ref, f_ref, gf_ref, xo_ref):
    xo_ref[...] = _layer_residual(x_ref, y_ref, gy_ref, f_ref, gf_ref)


def layer_residual(x, y, g_y, f, g_f, g_next):
    t, d = x.shape
    out_dtypes = (F32,) if g_next is None else (F32, BF16)
    tm = _norm_tile(d, (x.dtype, y.dtype, f.dtype) + out_dtypes)
    row = pl.BlockSpec((tm, d), lambda i: (i, 0))
    vec = pl.BlockSpec((1, d), lambda i: (0, 0))
    common = dict(grid=(t // tm,), semantics=("parallel",), temp_bytes=_nbytes((tm, d), F32))
    if g_next is None:
        return _pallas(_layer_residual_kernel, name="layer_residual", in_specs=[row, row, vec, row, vec],
                       operands=(x, y, g_y.reshape(1, d), f, g_f.reshape(1, d)), out_specs=row,
                       out_shape=jax.ShapeDtypeStruct((t, d), F32), **common), None
    return _pallas(_layer_residual_norm_kernel, name="layer_residual_norm",
                   in_specs=[row, row, vec, row, vec, vec],
                   operands=(x, y, g_y.reshape(1, d), f, g_f.reshape(1, d), g_next.reshape(1, d)),
                   out_specs=(row, row),
                   out_shape=(jax.ShapeDtypeStruct((t, d), F32), jax.ShapeDtypeStruct((t, d), BF16)),
                   **common)


ROPE_HALF = MLA_ROPE // 2


def _rope_cols(x, cos, sin):
    return x * cos + pltpu.roll(x, LANES // 2, 1) * sin


def _pair_rope_order(n_heads):
    src = np.arange(n_heads * MLA_ROPE).reshape(n_heads // 2, 2, 2, ROPE_HALF)
    return src.transpose(0, 2, 1, 3).reshape(-1)


def _in_proj_main_kernel(h_ref, w_ref, o_ref):
    o_ref[...] = lax.dot_general(h_ref[...], w_ref[...], (((1,), (1,)), ((), ())),
                                 preferred_element_type=F32).astype(o_ref.dtype)


def _in_proj_rope_kernel(h_ref, w_ref, cos_ref, sin_ref, o_ref):
    j = pl.program_id(1)
    acc = lax.dot_general(h_ref[...], w_ref[...], (((1,), (1,)), ((), ())), preferred_element_type=F32)

    @pl.when(j == 0)
    def _():
        lo = COL_KR - COL_CKV
        o_ref[:, :lo] = acc[:, :lo].astype(o_ref.dtype)
        for c in range(lo // LANES, lo // LANES + 2):
            sl = slice(c * LANES, (c + 1) * LANES)
            o_ref[:, sl] = _rope_cols(acc[:, sl], cos_ref[...], sin_ref[...]).astype(o_ref.dtype)
        o_ref[:, lo + 2 * LANES:] = acc[:, lo + 2 * LANES:].astype(o_ref.dtype)

    @pl.when(j == 1)
    def _():
        q_cols = (RET_HEADS * RET_QK) // LANES
        for c in range(IN_TILE // LANES):
            sl = slice(c * LANES, (c + 1) * LANES)
            r = _rope_cols(acc[:, sl], cos_ref[...], sin_ref[...])
            if c < q_cols:
                r = r * (RET_QK ** -0.5)
            o_ref[:, sl] = r.astype(o_ref.dtype)


def in_proj(h, w_t, cos_t, sin_t, seq):
    t, d = h.shape
    pos_blocks = seq // TM
    tab = pl.BlockSpec((TM, LANES), lambda i, j: (i % pos_blocks, 0))
    rope_tile0 = ROPE_BASE // IN_TILE
    rope_w = IN_WIDTH_PADDED - ROPE_BASE
    common = dict(semantics=("parallel", "arbitrary"), temp_bytes=_nbytes((TM, IN_TILE), F32),
                  out_specs=pl.BlockSpec((TM, IN_TILE), lambda i, j: (i, j)))
    main = _pallas(
        _in_proj_main_kernel, name="in_proj_main", grid=(t // TM, rope_tile0),
        in_specs=[pl.BlockSpec((TM, d), lambda i, j: (i, 0)),
                  pl.BlockSpec((IN_TILE, d), lambda i, j: (j, 0))],
        operands=(h, w_t), out_shape=jax.ShapeDtypeStruct((t, ROPE_BASE), BF16), **common)
    rope = _pallas(
        _in_proj_rope_kernel, name="in_proj_rope", grid=(t // TM, rope_w // IN_TILE),
        in_specs=[pl.BlockSpec((TM, d), lambda i, j: (i, 0)),
                  pl.BlockSpec((IN_TILE, d), lambda i, j: (rope_tile0 + j, 0)),
                  tab, tab],
        operands=(h, w_t, cos_t, sin_t), out_shape=jax.ShapeDtypeStruct((t, rope_w), BF16), **common)
    return main, rope


def _sgu_kernel(u_ref, v_ref, g_ref, b_ref, w_ref, bias_ref, o_ref):
    row = lax.broadcasted_iota(jnp.int32, (SGU_BLOCK, SGU_BLOCK), 0)
    col = lax.broadcasted_iota(jnp.int32, (SGU_BLOCK, SGU_BLOCK), 1)
    keep = (col // CHUNK) <= (row // CHUNK)
    for g in range(SGU_HEADS):
        sl = slice(g * LANES, (g + 1) * LANES)
        wg = jnp.where(keep, w_ref[g], 0.0).astype(BF16)
        vg = v_ref[:, sl].astype(F32)
        mu = jnp.mean(vg, axis=-1, keepdims=True)
        vc = vg - mu
        y = vc * lax.rsqrt(jnp.mean(vc * vc, axis=-1, keepdims=True) + EPS) * g_ref[:, sl] + b_ref[:, sl]
        yb = y.astype(BF16)
        for n in range(TB_SGU // SGU_BLOCK):
            rs = slice(n * SGU_BLOCK, (n + 1) * SGU_BLOCK)
            mixed = jnp.dot(wg, yb[rs, :], preferred_element_type=F32) + bias_ref[:, sl]
            o_ref[rs, sl] = (u_ref[rs, sl].astype(F32) * mixed).astype(o_ref.dtype)


def sgu_mixer(proj, ln_g, ln_b, w_s, bias_full):
    t = proj.shape[0]
    vec = pl.BlockSpec((1, SGU_WIDTH), lambda i: (0, 0))
    return _pallas(
        _sgu_kernel, name="sgu_mixer", grid=(t // TB_SGU,), semantics=("parallel",),
        in_specs=[pl.BlockSpec((TB_SGU, SGU_WIDTH), lambda i: (i, COL_U // SGU_WIDTH)),
                  pl.BlockSpec((TB_SGU, SGU_WIDTH), lambda i: (i, COL_V // SGU_WIDTH)),
                  vec, vec,
                  pl.BlockSpec((SGU_HEADS, SGU_BLOCK, SGU_BLOCK), lambda i: (0, 0, 0)),
                  pl.BlockSpec((SGU_BLOCK, SGU_WIDTH), lambda i: (0, 0))],
        operands=(proj, proj, ln_g.reshape(1, -1), ln_b.reshape(1, -1), w_s, bias_full),
        out_specs=pl.BlockSpec((TB_SGU, SGU_WIDTH), lambda i: (i, 0)),
        out_shape=jax.ShapeDtypeStruct((t, SGU_WIDTH), BF16),
        temp_bytes=_nbytes((TB_SGU, SGU_WIDTH), F32))


Q_SCALE = (MLA_NOPE + MLA_ROPE) ** -0.5 * math.log2(math.e)
Q_ROPE_TILE = (MLA_HEADS * MLA_NOPE) // IN_TILE


def _q_proj_kernel(c_ref, g_ref, w_ref, cos_ref, sin_ref, o_ref, cn_ref):
    j = pl.program_id(1)

    @pl.when(j == 0)
    def _():
        cn_ref[...] = _rms(c_ref[...].astype(F32), g_ref[...]).astype(cn_ref.dtype)

    acc = jnp.dot(cn_ref[...], w_ref[...], preferred_element_type=F32) * Q_SCALE

    @pl.when(j < Q_ROPE_TILE)
    def _():
        o_ref[...] = acc.astype(o_ref.dtype)

    @pl.when(j >= Q_ROPE_TILE)
    def _():
        for c in range(IN_TILE // LANES):
            sl = slice(c * LANES, (c + 1) * LANES)
            o_ref[:, sl] = _rope_cols(acc[:, sl], cos_ref[...], sin_ref[...]).astype(o_ref.dtype)


def q_proj(proj, g, w, cos_t, sin_t, seq):
    t = proj.shape[0]
    n = w.shape[1]
    pos_blocks = seq // TM_QKV
    tab = pl.BlockSpec((TM_QKV, LANES), lambda i, j: (i % pos_blocks, 0))
    return _pallas(
        _q_proj_kernel, name="q_proj", grid=(t // TM_QKV, n // IN_TILE), semantics=("parallel", "arbitrary"),
        in_specs=[pl.BlockSpec((TM_QKV, MLA_Q_RANK), lambda i, j: (i, COL_CQ // MLA_Q_RANK)),
                  pl.BlockSpec((1, MLA_Q_RANK), lambda i, j: (0, 0)),
                  pl.BlockSpec((MLA_Q_RANK, IN_TILE), lambda i, j: (0, j)),
                  tab, tab],
        operands=(proj, g.reshape(1, -1), w, cos_t, sin_t),
        out_specs=pl.BlockSpec((TM_QKV, IN_TILE), lambda i, j: (i, j)),
        out_shape=jax.ShapeDtypeStruct((t, n), BF16),
        scratch_shapes=[pltpu.VMEM((TM_QKV, MLA_Q_RANK), BF16)],
        temp_bytes=_nbytes((TM_QKV, IN_TILE), F32))


def _kv_proj_kernel(c_ref, g_ref, w_ref, o_ref, cn_ref):
    @pl.when(pl.program_id(1) == 0)
    def _():
        cn_ref[...] = _rms(c_ref[...].astype(F32), g_ref[...]).astype(cn_ref.dtype)

    o_ref[...] = jnp.dot(cn_ref[...], w_ref[...].astype(BF16), preferred_element_type=F32).astype(o_ref.dtype)


def kv_proj(proj_rope, g, w, layer):
    t = proj_rope.shape[0]
    n = w.shape[2]
    return _pallas(
        _kv_proj_kernel, name="kv_proj", grid=(t // TM_QKV, n // IN_TILE), semantics=("parallel", "arbitrary"),
        in_specs=[pl.BlockSpec((TM_QKV, MLA_KV_RANK), lambda i, j: (i, (COL_CKV - ROPE_BASE) // MLA_KV_RANK)),
                  pl.BlockSpec((1, MLA_KV_RANK), lambda i, j: (0, 0)),
                  pl.BlockSpec((None, MLA_KV_RANK, IN_TILE), lambda i, j: (layer, 0, j))],
        operands=(proj_rope, g.reshape(1, -1), w),
        out_specs=pl.BlockSpec((TM_QKV, IN_TILE), lambda i, j: (i, j)),
        out_shape=jax.ShapeDtypeStruct((t, n), BF16),
        scratch_shapes=[pltpu.VMEM((TM_QKV, MLA_KV_RANK), BF16)],
        temp_bytes=_nbytes((TM_QKV, IN_TILE), F32) + _nbytes((MLA_KV_RANK, IN_TILE), BF16))


HEAD_W = MLA_NOPE + MLA_V


def _attn_tiles(qs, kv_ref, kr_ref, base, ntiles, mask_last, states):
    states = list(states)
    ones = jnp.ones((TQ, LANES), BF16)
    for t in range(ntiles):
        st = pl.multiple_of(base + t * TQ, TQ)
        for hd in range(ATTN_HEADS):
            m, l, acc = states[hd]
            kn = kv_ref[pl.ds(st, TQ), hd * HEAD_W:hd * HEAD_W + MLA_NOPE]
            v = kv_ref[pl.ds(st, TQ), hd * HEAD_W + MLA_NOPE:(hd + 1) * HEAD_W]
            par = hd % 2
            k = jnp.concatenate([kn, kr_ref[pl.ds(st, TQ), par * LANES:(par + 1) * LANES]], axis=1)
            s = lax.dot_general(qs[hd], k, (((1,), (1,)), ((), ())), preferred_element_type=F32)
            if mask_last and t == ntiles - 1:
                row = lax.broadcasted_iota(jnp.int32, (TQ, TQ), 0)
                col = lax.broadcasted_iota(jnp.int32, (TQ, TQ), 1)
                s = jnp.where((col // CHUNK) <= (row // CHUNK), s, -jnp.inf)
            m_next = jnp.maximum(m, jnp.max(s, axis=1, keepdims=True))
            alpha = jnp.exp2(m - m_next)
            p = jnp.exp2(s - jnp.tile(m_next, (1, TQ // LANES))).astype(BF16)
            pv = jnp.dot(p, jnp.concatenate([v, ones], axis=1), preferred_element_type=F32)
            states[hd] = (m_next, alpha * l + pv[:, LANES:], alpha * acc + pv[:, :LANES])
    return states


def _attn_kernel(qn_ref, qr_ref, kv_ref, kr_ref, *rest, n_cast):
    w_refs, o_ref = rest[:n_cast], rest[n_cast]
    wb_refs = rest[n_cast + 1:2 * n_cast + 1]
    m_ref, l_ref, acc_ref = rest[2 * n_cast + 1:]
    qi = pl.program_id(2)

    for w_ref, wb_ref in zip(w_refs, wb_refs):
        wb_ref[...] = w_ref[...].astype(wb_ref.dtype)

    qs = [jnp.concatenate([qn_ref[:, hd * LANES:(hd + 1) * LANES],
                           qr_ref[:, (hd // 2) * LANES:(hd // 2 + 1) * LANES]], axis=1)
          for hd in range(ATTN_HEADS)]

    m_ref[...] = jnp.full_like(m_ref, -jnp.inf)
    l_ref[...] = jnp.zeros_like(l_ref)
    acc_ref[...] = jnp.zeros_like(acc_ref)

    def load_states():
        return [(m_ref[hd], l_ref[hd], acc_ref[hd]) for hd in range(ATTN_HEADS)]

    def body(g, carry):
        base = pl.multiple_of(g * (ATTN_GROUP * TQ), ATTN_GROUP * TQ)
        states = _attn_tiles(qs, kv_ref, kr_ref, base, ATTN_GROUP, False, load_states())
        for hd in range(ATTN_HEADS):
            m_ref[hd], l_ref[hd], acc_ref[hd] = states[hd]
        return carry

    lax.fori_loop(0, qi // ATTN_GROUP, body, 0)

    base = pl.multiple_of((qi // ATTN_GROUP) * (ATTN_GROUP * TQ), ATTN_GROUP * TQ)
    for r in range(ATTN_GROUP):
        @pl.when(qi % ATTN_GROUP == r)
        def _():
            states = _attn_tiles(qs, kv_ref, kr_ref, base, r + 1, True, load_states())
            for hd in range(ATTN_HEADS):
                _, l, acc = states[hd]
                o_ref[:, hd * LANES:(hd + 1) * LANES] = (acc / l).astype(o_ref.dtype)


def mla_attention(q, kv, proj_rope, weights, layer, batch, seq):
    t = q.shape[0]
    nq = seq // TQ
    groups = MLA_HEADS // ATTN_HEADS
    steps = batch * groups * nq
    slab_rows = [w.shape[1] // steps for w in weights]
    assert all(r * steps == w.shape[1] and r % BF16_SUBLANES == 0 for r, w in zip(slab_rows, weights))

    def slab(b, g, i):
        return (b * groups + g) * nq + i
    nope_w = ATTN_HEADS * MLA_NOPE
    rope_w = ATTN_HEADS * MLA_ROPE
    kr_w = 2 * LANES
    kr_block = (COL_KR - ROPE_BASE) // kr_w
    rope_block0 = (MLA_HEADS * MLA_NOPE) // rope_w
    return _pallas(
        functools.partial(_attn_kernel, n_cast=len(weights)),
        name="mla_attention", grid=(batch, groups, nq), semantics=("parallel", "parallel", "arbitrary"),
        in_specs=[pl.BlockSpec((TQ, nope_w), lambda b, p, i: (b * nq + i, p)),
                  pl.BlockSpec((TQ, rope_w), lambda b, p, i: (b * nq + i, rope_block0 + p)),
                  pl.BlockSpec((seq, ATTN_HEADS * HEAD_W), lambda b, p, i: (b, p)),
                  pl.BlockSpec((seq, kr_w), lambda b, p, i: (b, kr_block)),
                  *[pl.BlockSpec((None, r, w.shape[2]), lambda b, p, i: (layer, slab(b, p, i), 0))
                    for r, w in zip(slab_rows, weights)]],
        operands=(q, q, kv, proj_rope, *weights),
        out_specs=(pl.BlockSpec((TQ, nope_w), lambda b, p, i: (b * nq + i, p)),
                   *[pl.BlockSpec((r, w.shape[2]), lambda b, p, i: (slab(b, p, i), 0))
                     for r, w in zip(slab_rows, weights)]),
        out_shape=(jax.ShapeDtypeStruct((t, MLA_WIDTH), BF16),
                   *[jax.ShapeDtypeStruct(w.shape[1:], BF16) for w in weights]),
        scratch_shapes=[pltpu.VMEM((ATTN_HEADS, TQ, LANES), F32)] * 3,
        temp_bytes=ATTN_HEADS * (_nbytes((TQ, TQ), F32) + _nbytes((TQ, TQ), BF16)))


def _ret_kernel(q_ref, k_ref, v_ref, g_ref, d_ref, qd_ref, kd_ref, cd_ref, o_ref, s_ref):
    @pl.when(pl.program_id(1) == 0)
    def _():
        s_ref[...] = jnp.zeros_like(s_ref)

    lane = lax.broadcasted_iota(jnp.int32, (RET_L, LANES), 1)
    for p in range(RET_HEADS // 2):
        ls = slice(p * LANES, (p + 1) * LANES)
        q = q_ref[:, ls]
        k = k_ref[:, ls]
        qf = q.astype(F32)
        kdec = (k.astype(F32) * kd_ref[:, ls]).astype(BF16)
        for hh in range(2):
            head = 2 * p + hh
            own = (lane // ROPE_HALF) % 2 == hh
            vs = slice(head * RET_V, (head + 1) * RET_V)
            v = v_ref[:, vs]
            qm = jnp.where(own, q, jnp.zeros_like(q))
            s = lax.dot_general(qm, k, (((1,), (1,)), ((), ())), preferred_element_type=F32)
            sd = (s * d_ref[head]).astype(BF16)
            qx = jnp.where(own, qf * qd_ref[:, ls], 0.0).astype(BF16)
            state = s_ref[head]
            o = (jnp.dot(sd, v, preferred_element_type=F32)
                 + jnp.dot(qx, state.astype(BF16), preferred_element_type=F32))
            s_ref[head] = cd_ref[head] * state + lax.dot_general(
                kdec, v, (((0,), (0,)), ((), ())), preferred_element_type=F32)
            o = o * lax.rsqrt(jnp.mean(o * o, axis=-1, keepdims=True) + EPS)
            gate = g_ref[:, vs].astype(F32)
            o_ref[:, vs] = (gate * jax.nn.sigmoid(gate) * o).astype(o_ref.dtype)


def retention_mixer(proj, proj_rope, d_tab, qd_tab, kd_tab, cd_tab, batch, seq):
    t = proj.shape[0]
    nsc = seq // RET_L
    qk_w = RET_HEADS * RET_QK
    full = lambda b, c: (0, 0)
    full3 = lambda b, c: (0, 0, 0)
    return _pallas(
        _ret_kernel, name="retention_mixer", grid=(batch, nsc), semantics=("parallel", "arbitrary"),
        in_specs=[pl.BlockSpec((RET_L, qk_w), lambda b, c: (b * nsc + c, (COL_RQ - ROPE_BASE) // qk_w)),
                  pl.BlockSpec((RET_L, qk_w), lambda b, c: (b * nsc + c, (COL_RK - ROPE_BASE) // qk_w)),
                  pl.BlockSpec((RET_L, RET_WIDTH), lambda b, c: (b * nsc + c, COL_RV // RET_WIDTH)),
                  pl.BlockSpec((RET_L, RET_WIDTH), lambda b, c: (b * nsc + c, COL_RG // RET_WIDTH)),
                  pl.BlockSpec((RET_HEADS, RET_L, RET_L), full3),
                  pl.BlockSpec((RET_L, qk_w), full),
                  pl.BlockSpec((RET_L, qk_w), full),
                  pl.BlockSpec((RET_HEADS, LANES, RET_V), full3)],
        operands=(proj_rope, proj_rope, proj, proj, d_tab, qd_tab, kd_tab, cd_tab),
        out_specs=pl.BlockSpec((RET_L, RET_WIDTH), lambda b, c: (b * nsc + c, 0)),
        out_shape=jax.ShapeDtypeStruct((t, RET_WIDTH), BF16),
        scratch_shapes=[pltpu.VMEM((RET_HEADS, LANES, RET_V), F32)],
        temp_bytes=RET_HEADS * _nbytes((RET_L, RET_L), F32))


def retention_tables():
    log_gamma = jnp.log1p(-jnp.exp2(-5.0 - jnp.arange(RET_HEADS, dtype=F32)))
    idx = jnp.arange(RET_L, dtype=F32)
    dist = jnp.abs(idx[:, None] - idx[None, :])
    chunk = jnp.arange(RET_L) // CHUNK
    visible = chunk[None, :] <= chunk[:, None]
    d_tab = jnp.where(visible[None], jnp.exp(log_gamma[:, None, None] * dist[None]), 0.0)
    col_head = _pair_rope_order(RET_HEADS) // RET_QK
    lg_lane = log_gamma[col_head]
    qd_tab = jnp.exp(lg_lane[None, :] * (idx[:, None] + 1.0))
    kd_tab = jnp.exp(lg_lane[None, :] * (RET_L - 1.0 - idx[:, None]))
    cd = jnp.exp(log_gamma * RET_L)
    cd_tab = jnp.broadcast_to(cd[:, None, None], (RET_HEADS, LANES, RET_V))
    return d_tab, qd_tab, kd_tab, cd_tab


def _out_proj_kernel(a_ref, b_ref, c_ref, w_ref, *rest):
    if len(rest) == 3:
        wt_ref, o_ref, wtb_ref = rest
        _w_in_prep_kernel(wt_ref, wtb_ref)
    else:
        o_ref, = rest
    ka = a_ref.shape[1]
    kb = b_ref.shape[1]
    acc = jnp.dot(a_ref[...], w_ref[:ka, :], preferred_element_type=F32)
    acc = acc + jnp.dot(b_ref[...], w_ref[ka:ka + kb, :], preferred_element_type=F32)
    acc = acc + jnp.dot(c_ref[...], w_ref[ka + kb:, :], preferred_element_type=F32)
    o_ref[...] = acc.astype(o_ref.dtype)


def out_proj(a, b, c, w, w_in_t=None, next_layer=None):
    t = a.shape[0]
    k, n = w.shape
    nj = n // TN_OUT
    steps = (t // TM) * nj
    in_specs = [pl.BlockSpec((TM, a.shape[1]), lambda i, j: (i, 0)),
                pl.BlockSpec((TM, b.shape[1]), lambda i, j: (i, 0)),
                pl.BlockSpec((TM, c.shape[1]), lambda i, j: (i, 0)),
                pl.BlockSpec((k, TN_OUT), lambda i, j: (0, j))]
    operands = (a, b, c, w)
    out_specs = pl.BlockSpec((TM, TN_OUT), lambda i, j: (i, j))
    out_shape = jax.ShapeDtypeStruct((t, n), BF16)
    if w_in_t is not None:
        _, rows, d = w_in_t.shape
        slab = d // steps
        assert slab * steps == d and slab % LANES == 0
        in_specs.append(pl.BlockSpec((None, rows, slab), lambda i, j: (next_layer, 0, i * nj + j)))
        operands += (w_in_t,)
        out_specs = (out_specs, pl.BlockSpec((IN_WIDTH_PADDED, slab), lambda i, j: (0, i * nj + j)))
        out_shape = (out_shape, jax.ShapeDtypeStruct((IN_WIDTH_PADDED, d), BF16))
    return _pallas(
        _out_proj_kernel, name="out_proj", grid=(t // TM, nj), semantics=("parallel", "arbitrary"),
        in_specs=in_specs, operands=operands, out_specs=out_specs, out_shape=out_shape,
        temp_bytes=_nbytes((TM, TN_OUT), F32))


def _ffn_kernel(h_ref, wu_ref, wd_ref, o_ref):
    @pl.when(pl.program_id(1) == 0)
    def _():
        o_ref[...] = jnp.zeros_like(o_ref)

    a = jnp.dot(h_ref[...], wu_ref[...], preferred_element_type=F32)
    a = jnp.square(jnp.maximum(a, 0.0)).astype(BF16)
    o_ref[...] += jnp.dot(a, wd_ref[...], preferred_element_type=F32)


def ffn(h, wu_b, wd_b):
    t, d = h.shape
    ff = wu_b.shape[1]
    return _pallas(
        _ffn_kernel, name="ffn", grid=(t // TM_FFN, ff // TF_FFN), semantics=("parallel", "arbitrary"),
        in_specs=[pl.BlockSpec((TM_FFN, d), lambda i, j: (i, 0)),
                  pl.BlockSpec((d, TF_FFN), lambda i, j: (0, j)),
                  pl.BlockSpec((TF_FFN, d), lambda i, j: (j, 0))],
        operands=(h, wu_b, wd_b),
        out_specs=pl.BlockSpec((TM_FFN, d), lambda i, j: (i, 0)),
        out_shape=jax.ShapeDtypeStruct((t, d), F32),
        temp_bytes=_nbytes((TM_FFN, TF_FFN), F32) + _nbytes((TM_FFN, TF_FFN), BF16))


_SRC_SIZES = (SGU_WIDTH, SGU_WIDTH, MLA_Q_RANK, MLA_KV_RANK, MLA_ROPE,
              RET_HEADS * RET_QK, RET_HEADS * RET_QK, RET_WIDTH, RET_WIDTH)
(_SRC_U, _SRC_V, _SRC_CQ, _SRC_CKV, _SRC_KR, _SRC_RQ, _SRC_RK, _SRC_RV, _SRC_RG,
 IN_WIDTH) = [int(o) for o in np.concatenate([[0], np.cumsum(_SRC_SIZES)])]
_W_IN_MOVES = (
    (COL_U, _SRC_U, 3 * IN_TILE),
    (COL_RV, _SRC_RV, RET_WIDTH),
    (COL_RG, _SRC_RG, RET_WIDTH),
    (COL_CKV, _SRC_CKV, MLA_KV_RANK),
    (COL_KR, _SRC_KR, ROPE_HALF),
    (COL_KR + LANES // 2, _SRC_KR + ROPE_HALF, ROPE_HALF),
    (COL_KR + LANES + ROPE_HALF, _SRC_KR, ROPE_HALF),
    (COL_KR + LANES + LANES // 2 + ROPE_HALF, _SRC_KR + ROPE_HALF, ROPE_HALF),
    *[(dst0 + i * ROPE_HALF, src0 + int(s0), ROPE_HALF)
      for dst0, src0 in ((COL_RQ, _SRC_RQ), (COL_RK, _SRC_RK))
      for i, s0 in enumerate(_pair_rope_order(RET_HEADS)[::ROPE_HALF])],
)


def _w_in_prep_kernel(w_ref, o_ref):
    o_ref[COL_KR:COL_RQ, :] = jnp.zeros((COL_RQ - COL_KR, o_ref.shape[1]), o_ref.dtype)
    for dst, src, width in _W_IN_MOVES:
        o_ref[dst:dst + width, :] = w_ref[src:src + width, :].astype(o_ref.dtype)


def prep_w_in(w_t, layer):
    _, n, d = w_t.shape
    return _pallas(
        _w_in_prep_kernel, name="prep_w_in", grid=(d // TK_PREP,), semantics=("parallel",),
        in_specs=[pl.BlockSpec((None, n, TK_PREP), lambda i: (layer, 0, i))],
        operands=(w_t,),
        out_specs=pl.BlockSpec((IN_WIDTH_PADDED, TK_PREP), lambda i: (0, i)),
        out_shape=jax.ShapeDtypeStruct((IN_WIDTH_PADDED, d), BF16))


def _prep_wq(w):
    k = w.shape[0]
    w3 = w.reshape(k, MLA_HEADS, MLA_NOPE + MLA_ROPE)
    nope = w3[:, :, :MLA_NOPE].reshape(k, MLA_HEADS * MLA_NOPE)
    rope = w3[:, :, MLA_NOPE:].reshape(k, MLA_HEADS * MLA_ROPE)[:, _pair_rope_order(MLA_HEADS)]
    return jnp.concatenate([nope, rope], axis=1).astype(BF16)


def _rope_lane_tables(seq):
    inv_freq = 1.0 / (ROPE_THETA ** (jnp.arange(0, MLA_ROPE, 2, dtype=F32) / MLA_ROPE))
    ang = jnp.arange(seq, dtype=F32)[:, None] * inv_freq[None, :]
    cos, sin = jnp.cos(ang), jnp.sin(ang)
    cos_t = jnp.concatenate([cos, cos, cos, cos], axis=1)
    sin_t = jnp.concatenate([-sin, -sin, sin, sin], axis=1)
    return cos_t, sin_t


def kernel(x, norm_mix_pre, norm_mix_post, norm_ffn_pre, norm_ffn_post, w_in, sgu_ln_g, sgu_ln_b, sgu_w, sgu_b, mla_q_norm, mla_wq_b, mla_kv_norm, mla_wkv_b, w_out, w_up, w_down):
    batch, seq, d = x.shape
    depth = w_in.shape[0]
    cos_t, sin_t = _rope_lane_tables(seq)
    d_tab, qd_tab, kd_tab, cd_tab = retention_tables()
    w_in_t = jnp.swapaxes(w_in, 1, 2)

    xf = x.reshape(batch * seq, d)
    h = rmsnorm_rows(xf, norm_mix_pre[0])
    w_in_prepped = prep_w_in(w_in_t, 0)
    for l in range(depth):
        proj, proj_rope = in_proj(h, w_in_prepped, cos_t, sin_t, seq)
        bias_full = jnp.repeat(sgu_b[l].T, SGU_WIDTH // SGU_HEADS, axis=1)
        out_a = sgu_mixer(proj, sgu_ln_g[l], sgu_ln_b[l], sgu_w[l], bias_full)
        q = q_proj(proj, mla_q_norm[l], _prep_wq(mla_wq_b[l]), cos_t, sin_t, seq)
        kv = kv_proj(proj_rope, mla_kv_norm[l], mla_wkv_b, l)
        out_b, wu_b, wd_b, wo_b = mla_attention(q, kv, proj_rope, (w_up, w_down, w_out), l, batch, seq)
        out_c = retention_mixer(proj, proj_rope, d_tab, qd_tab, kd_tab, cd_tab, batch, seq)
        if l + 1 < depth:
            mixed, w_in_prepped = out_proj(out_a, out_b, out_c, wo_b, w_in_t, l + 1)
        else:
            mixed = out_proj(out_a, out_b, out_c, wo_b)
        h2 = mix_norm(xf, mixed, norm_mix_post[l], norm_ffn_pre[l])
        f = ffn(h2, wu_b, wd_b)
        g_next = norm_mix_pre[l + 1] if l + 1 < depth else None
        xf, h = layer_residual(xf, mixed, norm_mix_post[l], f, norm_ffn_post[l], g_next)
    return xf.reshape(batch, seq, d)
```

```python
import functools
import math

import numpy as np
import jax
import jax.numpy as jnp
from jax import lax
from jax.experimental import pallas as pl
from jax.experimental.pallas import tpu as pltpu

F32 = jnp.float32
BF16 = jnp.bfloat16

CHUNK = 64
EPS = 1e-6
ROPE_THETA = 10000.0
SGU_BLOCK = 128
SGU_WIDTH = 1024
SGU_HEADS = 8
MLA_HEADS = 16
MLA_NOPE = 128
MLA_ROPE = 64
MLA_V = 128
MLA_Q_RANK = 1024
MLA_KV_RANK = 512
MLA_WIDTH = MLA_HEADS * MLA_V
RET_HEADS = 8
RET_QK = 64
RET_V = 128
RET_WIDTH = RET_HEADS * RET_V

LANES = 128
BF16_SUBLANES = 16
VMEM_CAP = 64 * 1024 * 1024
VMEM_RESERVED = 2 * 1024 * 1024
VMEM_INTERNAL = 6 * 1024 * 1024
PIPELINE_BUFFERS = 2

IN_TILE = 1024
COL_U = 0 * IN_TILE
COL_V = 1 * IN_TILE
COL_CQ = 2 * IN_TILE
COL_RV = 3 * IN_TILE
COL_RG = 4 * IN_TILE
COL_CKV = 5 * IN_TILE
COL_KR = COL_CKV + MLA_KV_RANK
COL_RQ = 6 * IN_TILE
COL_RK = COL_RQ + RET_HEADS * RET_QK
IN_WIDTH_PADDED = 7 * IN_TILE
ROPE_BASE = COL_CKV

TM = 1024
TM_QKV = 2048
NORM_TILES = (512, 256)
TM_FFN = 512
TF_FFN = 1024
TN_OUT = 1024
TK_PREP = 256
TB_SGU = 512
TQ = 512
ATTN_GROUP = 4
ATTN_HEADS = 4
RET_L = 256


def _nbytes(shape, dtype):
    return math.prod(1 if s is None else s for s in shape) * jnp.dtype(dtype).itemsize


def _pallas(kernel, *, name, grid, semantics, in_specs, operands, out_specs, out_shape,
            scratch_shapes=(), temp_bytes=0):
    outs = out_shape if isinstance(out_shape, (tuple, list)) else (out_shape,)
    ospecs = out_specs if isinstance(out_specs, (tuple, list)) else (out_specs,)
    pipelined = sum(_nbytes(s.block_shape, a.dtype) for s, a in zip(in_specs, operands) if s.block_shape)
    pipelined += sum(_nbytes(s.block_shape, o.dtype) for s, o in zip(ospecs, outs) if s.block_shape)
    scratch = sum(_nbytes(s.shape, s.dtype) for s in scratch_shapes if s.memory_space == pltpu.VMEM)
    need = PIPELINE_BUFFERS * pipelined + scratch + temp_bytes + VMEM_INTERNAL
    params = pltpu.CompilerParams(dimension_semantics=semantics,
                                  vmem_limit_bytes=int(min(need, VMEM_CAP - VMEM_RESERVED)))
    return pl.pallas_call(kernel, out_shape=out_shape, grid=grid, in_specs=in_specs, out_specs=out_specs,
                          scratch_shapes=scratch_shapes, compiler_params=params, name=name)(*operands)


def _rms(x, g):
    return x * lax.rsqrt(jnp.mean(x * x, axis=-1, keepdims=True) + EPS) * g


def _rmsnorm_kernel(x_ref, g_ref, o_ref):
    o_ref[...] = _rms(x_ref[...], g_ref[...]).astype(o_ref.dtype)


def _norm_tile(d, row_dtypes):
    for tile in NORM_TILES:
        need = PIPELINE_BUFFERS * sum(_nbytes((tile, d), dt) for dt in row_dtypes)
        if need + _nbytes((tile, d), F32) + VMEM_INTERNAL <= VMEM_CAP - VMEM_RESERVED:
            return tile
    raise ValueError("no row tile fits VMEM")


def rmsnorm_rows(x, g):
    t, d = x.shape
    tm = _norm_tile(d, (x.dtype, BF16))
    return _pallas(
        _rmsnorm_kernel, name="rmsnorm_rows", grid=(t // tm,), semantics=("parallel",),
        in_specs=[pl.BlockSpec((tm, d), lambda i: (i, 0)),
                  pl.BlockSpec((1, d), lambda i: (0, 0))],
        operands=(x, g.reshape(1, d)),
        out_specs=pl.BlockSpec((tm, d), lambda i: (i, 0)),
        out_shape=jax.ShapeDtypeStruct((t, d), BF16),
        temp_bytes=_nbytes((tm, d), F32))


def _mix_norm_kernel(x_ref, y_ref, gp_ref, gn_ref, ho_ref):
    xm = x_ref[...] + _rms(y_ref[...].astype(F32), gp_ref[...])
    ho_ref[...] = _rms(xm, gn_ref[...]).astype(ho_ref.dtype)


def mix_norm(x, y, g_post, g_next):
    t, d = x.shape
    tm = _norm_tile(d, (x.dtype, y.dtype, BF16))
    row = pl.BlockSpec((tm, d), lambda i: (i, 0))
    vec = pl.BlockSpec((1, d), lambda i: (0, 0))
    return _pallas(_mix_norm_kernel, name="mix_norm", grid=(t // tm,), semantics=("parallel",),
                   in_specs=[row, row, vec, vec],
                   operands=(x, y, g_post.reshape(1, d), g_next.reshape(1, d)),
                   out_specs=row, out_shape=jax.ShapeDtypeStruct((t, d), BF16),
                   temp_bytes=_nbytes((tm, d), F32))


def _layer_residual(x_ref, y_ref, gy_ref, f_ref, gf_ref):
    xm = x_ref[...] + _rms(y_ref[...].astype(F32), gy_ref[...])
    return xm + _rms(f_ref[...].astype(F32), gf_ref[...])


def _layer_residual_norm_kernel(x_ref, y_ref, gy_ref, f_ref, gf_ref, gn_ref, xo_ref, ho_ref):
    xn = _layer_residual(x_ref, y_ref, gy_ref, f_ref, gf_ref)
    xo_ref[...] = xn
    ho_ref[...] = _rms(xn, gn_ref[...]).astype(ho_ref.dtype)


def _layer_residual_kernel(x_ref, y_ref, gy_ref, f_ref, gf_ref, xo_ref):
    xo_ref[...] = _layer_residual(x_ref, y_ref, gy_ref, f_ref, gf_ref)


def layer_residual(x, y, g_y, f, g_f, g_next):
    t, d = x.shape
    out_dtypes = (F32,) if g_next is None else (F32, BF16)
    tm = _norm_tile(d, (x.dtype, y.dtype, f.dtype) + out_dtypes)
    row = pl.BlockSpec((tm, d), lambda i: (i, 0))
    vec = pl.BlockSpec((1, d), lambda i: (0, 0))
    common = dict(grid=(t // tm,), semantics=("parallel",), temp_bytes=_nbytes((tm, d), F32))
    if g_next is None:
        return _pallas(_layer_residual_kernel, name="layer_residual", in_specs=[row, row, vec, row, vec],
                       operands=(x, y, g_y.reshape(1, d), f, g_f.reshape(1, d)), out_specs=row,
                       out_shape=jax.ShapeDtypeStruct((t, d), F32), **common), None
    return _pallas(_layer_residual_norm_kernel, name="layer_residual_norm",
                   in_specs=[row, row, vec, row, vec, vec],
                   operands=(x, y, g_y.reshape(1, d), f, g_f.reshape(1, d), g_next.reshape(1, d)),
                   out_specs=(row, row),
                   out_shape=(jax.ShapeDtypeStruct((t, d), F32), jax.ShapeDtypeStruct((t, d), BF16)),
                   **common)


ROPE_HALF = MLA_ROPE // 2


def _rope_cols(x, cos, sin):
    return x * cos + pltpu.roll(x, LANES // 2, 1) * sin


def _pair_rope_order(n_heads):
    src = np.arange(n_heads * MLA_ROPE).reshape(n_heads // 2, 2, 2, ROPE_HALF)
    return src.transpose(0, 2, 1, 3).reshape(-1)


def _in_proj_main_kernel(h_ref, w_ref, o_ref):
    o_ref[...] = lax.dot_general(h_ref[...], w_ref[...], (((1,), (1,)), ((), ())),
                                 preferred_element_type=F32).astype(o_ref.dtype)


def _in_proj_rope_kernel(h_ref, w_ref, cos_ref, sin_ref, o_ref):
    j = pl.program_id(1)
    acc = lax.dot_general(h_ref[...], w_ref[...], (((1,), (1,)), ((), ())), preferred_element_type=F32)

    @pl.when(j == 0)
    def _():
        lo = COL_KR - COL_CKV
        o_ref[:, :lo] = acc[:, :lo].astype(o_ref.dtype)
        for c in range(lo // LANES, lo // LANES + 2):
            sl = slice(c * LANES, (c + 1) * LANES)
            o_ref[:, sl] = _rope_cols(acc[:, sl], cos_ref[...], sin_ref[...]).astype(o_ref.dtype)
        o_ref[:, lo + 2 * LANES:] = acc[:, lo + 2 * LANES:].astype(o_ref.dtype)

    @pl.when(j == 1)
    def _():
        q_cols = (RET_HEADS * RET_QK) // LANES
        for c in range(IN_TILE // LANES):
            sl = slice(c * LANES, (c + 1) * LANES)
            r = _rope_cols(acc[:, sl], cos_ref[...], sin_ref[...])
            if c < q_cols:
                r = r * (RET_QK ** -0.5)
            o_ref[:, sl] = r.astype(o_ref.dtype)


def in_proj(h, w_t, cos_t, sin_t, seq):
    t, d = h.shape
    pos_blocks = seq // TM
    tab = pl.BlockSpec((TM, LANES), lambda i, j: (i % pos_blocks, 0))
    rope_tile0 = ROPE_BASE // IN_TILE
    rope_w = IN_WIDTH_PADDED - ROPE_BASE
    common = dict(semantics=("parallel", "arbitrary"), temp_bytes=_nbytes((TM, IN_TILE), F32),
                  out_specs=pl.BlockSpec((TM, IN_TILE), lambda i, j: (i, j)))
    main = _pallas(
        _in_proj_main_kernel, name="in_proj_main", grid=(t // TM, rope_tile0),
        in_specs=[pl.BlockSpec((TM, d), lambda i, j: (i, 0)),
                  pl.BlockSpec((IN_TILE, d), lambda i, j: (j, 0))],
        operands=(h, w_t), out_shape=jax.ShapeDtypeStruct((t, ROPE_BASE), BF16), **common)
    rope = _pallas(
        _in_proj_rope_kernel, name="in_proj_rope", grid=(t // TM, rope_w // IN_TILE),
        in_specs=[pl.BlockSpec((TM, d), lambda i, j: (i, 0)),
                  pl.BlockSpec((IN_TILE, d), lambda i, j: (rope_tile0 + j, 0)),
                  tab, tab],
        operands=(h, w_t, cos_t, sin_t), out_shape=jax.ShapeDtypeStruct((t, rope_w), BF16), **common)
    return main, rope


def _sgu_kernel(u_ref, v_ref, g_ref, b_ref, w_ref, bias_ref, o_ref):
    row = lax.broadcasted_iota(jnp.int32, (SGU_BLOCK, SGU_BLOCK), 0)
    col = lax.broadcasted_iota(jnp.int32, (SGU_BLOCK, SGU_BLOCK), 1)
    keep = (col // CHUNK) <= (row // CHUNK)
    for g in range(SGU_HEADS):
        sl = slice(g * LANES, (g + 1) * LANES)
        wg = jnp.where(keep, w_ref[g], 0.0).astype(BF16)
        vg = v_ref[:, sl].astype(F32)
        mu = jnp.mean(vg, axis=-1, keepdims=True)
        vc = vg - mu
        y = vc * lax.rsqrt(jnp.mean(vc * vc, axis=-1, keepdims=True) + EPS) * g_ref[:, sl] + b_ref[:, sl]
        yb = y.astype(BF16)
        for n in range(TB_SGU // SGU_BLOCK):
            rs = slice(n * SGU_BLOCK, (n + 1) * SGU_BLOCK)
            mixed = jnp.dot(wg, yb[rs, :], preferred_element_type=F32) + bias_ref[:, sl]
            o_ref[rs, sl] = (u_ref[rs, sl].astype(F32) * mixed).astype(o_ref.dtype)


def sgu_mixer(proj, ln_g, ln_b, w_s, bias_full):
    t = proj.shape[0]
    vec = pl.BlockSpec((1, SGU_WIDTH), lambda i: (0, 0))
    return _pallas(
        _sgu_kernel, name="sgu_mixer", grid=(t // TB_SGU,), semantics=("parallel",),
        in_specs=[pl.BlockSpec((TB_SGU, SGU_WIDTH), lambda i: (i, COL_U // SGU_WIDTH)),
                  pl.BlockSpec((TB_SGU, SGU_WIDTH), lambda i: (i, COL_V // SGU_WIDTH)),
                  vec, vec,
                  pl.BlockSpec((SGU_HEADS, SGU_BLOCK, SGU_BLOCK), lambda i: (0, 0, 0)),
                  pl.BlockSpec((SGU_BLOCK, SGU_WIDTH), lambda i: (0, 0))],
        operands=(proj, proj, ln_g.reshape(1, -1), ln_b.reshape(1, -1), w_s, bias_full),
        out_specs=pl.BlockSpec((TB_SGU, SGU_WIDTH), lambda i: (i, 0)),
        out_shape=jax.ShapeDtypeStruct((t, SGU_WIDTH), BF16),
        temp_bytes=_nbytes((TB_SGU, SGU_WIDTH), F32))


Q_SCALE = (MLA_NOPE + MLA_ROPE) ** -0.5 * math.log2(math.e)
Q_ROPE_TILE = (MLA_HEADS * MLA_NOPE) // IN_TILE


def _q_proj_kernel(c_ref, g_ref, w_ref, cos_ref, sin_ref, o_ref, cn_ref):
    j = pl.program_id(1)

    @pl.when(j == 0)
    def _():
        cn_ref[...] = _rms(c_ref[...].astype(F32), g_ref[...]).astype(cn_ref.dtype)

    acc = jnp.dot(cn_ref[...], w_ref[...], preferred_element_type=F32) * Q_SCALE

    @pl.when(j < Q_ROPE_TILE)
    def _():
        o_ref[...] = acc.astype(o_ref.dtype)

    @pl.when(j >= Q_ROPE_TILE)
    def _():
        for c in range(IN_TILE // LANES):
            sl = slice(c * LANES, (c + 1) * LANES)
            o_ref[:, sl] = _rope_cols(acc[:, sl], cos_ref[...], sin_ref[...]).astype(o_ref.dtype)


def q_proj(proj, g, w, cos_t, sin_t, seq):
    t = proj.shape[0]
    n = w.shape[1]
    pos_blocks = seq // TM_QKV
    tab = pl.BlockSpec((TM_QKV, LANES), lambda i, j: (i % pos_blocks, 0))
    return _pallas(
        _q_proj_kernel, name="q_proj", grid=(t // TM_QKV, n // IN_TILE), semantics=("parallel", "arbitrary"),
        in_specs=[pl.BlockSpec((TM_QKV, MLA_Q_RANK), lambda i, j: (i, COL_CQ // MLA_Q_RANK)),
                  pl.BlockSpec((1, MLA_Q_RANK), lambda i, j: (0, 0)),
                  pl.BlockSpec((MLA_Q_RANK, IN_TILE), lambda i, j: (0, j)),
                  tab, tab],
        operands=(proj, g.reshape(1, -1), w, cos_t, sin_t),
        out_specs=pl.BlockSpec((TM_QKV, IN_TILE), lambda i, j: (i, j)),
        out_shape=jax.ShapeDtypeStruct((t, n), BF16),
        scratch_shapes=[pltpu.VMEM((TM_QKV, MLA_Q_RANK), BF16)],
        temp_bytes=_nbytes((TM_QKV, IN_TILE), F32))


def _kv_proj_kernel(c_ref, g_ref, w_ref, o_ref, cn_ref):
    @pl.when(pl.program_id(1) == 0)
    def _():
        cn_ref[...] = _rms(c_ref[...].astype(F32), g_ref[...]).astype(cn_ref.dtype)

    o_ref[...] = jnp.dot(cn_ref[...], w_ref[...].astype(BF16), preferred_element_type=F32).astype(o_ref.dtype)


def kv_proj(proj_rope, g, w, layer):
    t = proj_rope.shape[0]
    n = w.shape[2]
    return _pallas(
        _kv_proj_kernel, name="kv_proj", grid=(t // TM_QKV, n // IN_TILE), semantics=("parallel", "arbitrary"),
        in_specs=[pl.BlockSpec((TM_QKV, MLA_KV_RANK), lambda i, j: (i, (COL_CKV - ROPE_BASE) // MLA_KV_RANK)),
                  pl.BlockSpec((1, MLA_KV_RANK), lambda i, j: (0, 0)),
                  pl.BlockSpec((None, MLA_KV_RANK, IN_TILE), lambda i, j: (layer, 0, j))],
        operands=(proj_rope, g.reshape(1, -1), w),
        out_specs=pl.BlockSpec((TM_QKV, IN_TILE), lambda i, j: (i, j)),
        out_shape=jax.ShapeDtypeStruct((t, n), BF16),
        scratch_shapes=[pltpu.VMEM((TM_QKV, MLA_KV_RANK), BF16)],
        temp_bytes=_nbytes((TM_QKV, IN_TILE), F32) + _nbytes((MLA_KV_RANK, IN_TILE), BF16))


HEAD_W = MLA_NOPE + MLA_V


def _attn_tiles(qs, kv_ref, kr_ref, base, ntiles, mask_last, states):
    states = list(states)
    ones = jnp.ones((TQ, LANES), BF16)
    for t in range(ntiles):
        st = pl.multiple_of(base + t * TQ, TQ)
        for hd in range(ATTN_HEADS):
            m, l, acc = states[hd]
            kn = kv_ref[pl.ds(st, TQ), hd * HEAD_W:hd * HEAD_W + MLA_NOPE]
            v = kv_ref[pl.ds(st, TQ), hd * HEAD_W + MLA_NOPE:(hd + 1) * HEAD_W]
            par = hd % 2
            k = jnp.concatenate([kn, kr_ref[pl.ds(st, TQ), par * LANES:(par + 1) * LANES]], axis=1)
            s = lax.dot_general(qs[hd], k, (((1,), (1,)), ((), ())), preferred_element_type=F32)
            if mask_last and t == ntiles - 1:
                row = lax.broadcasted_iota(jnp.int32, (TQ, TQ), 0)
                col = lax.broadcasted_iota(jnp.int32, (TQ, TQ), 1)
                s = jnp.where((col // CHUNK) <= (row // CHUNK), s, -jnp.inf)
            m_next = jnp.maximum(m, jnp.max(s, axis=1, keepdims=True))
            alpha = jnp.exp2(m - m_next)
            p = jnp.exp2(s - jnp.tile(m_next, (1, TQ // LANES))).astype(BF16)
            pv = jnp.dot(p, jnp.concatenate([v, ones], axis=1), preferred_element_type=F32)
            states[hd] = (m_next, alpha * l + pv[:, LANES:], alpha * acc + pv[:, :LANES])
    return states


def _attn_kernel(qn_ref, qr_ref, kv_ref, kr_ref, *rest, n_cast, layer, cast_from):
    w_hbm, o_ref = rest[:n_cast], rest[n_cast]
    wb_hbm = rest[n_cast + 1:2 * n_cast + 1]
    m_ref, l_ref, acc_ref = rest[2 * n_cast + 1:2 * n_cast + 4]
    ring = rest[2 * n_cast + 4:]
    in_bufs, out_bufs = ring[0::4], ring[1::4]
    in_sems, out_sems = ring[2::4], ring[3::4]
    b, g, qi = pl.program_id(0), pl.program_id(1), pl.program_id(2)
    groups, nq = pl.num_programs(1), pl.num_programs(2)
    longs_per_group = nq - cast_from
    n_long = pl.num_programs(0) * groups * longs_per_group
    long_idx = (b * groups + g) * longs_per_group + (qi - cast_from)
    is_long = qi >= cast_from

    def in_copy(k, chunk, slot):
        rows = in_bufs[k].shape[1]
        return pltpu.make_async_copy(w_hbm[k].at[layer, pl.ds(chunk * rows, rows), :],
                                     in_bufs[k].at[slot], in_sems[k].at[slot])

    def out_copy(k, chunk, slot):
        rows = out_bufs[k].shape[1]
        return pltpu.make_async_copy(out_bufs[k].at[slot], wb_hbm[k].at[pl.ds(chunk * rows, rows), :],
                                     out_sems[k].at[slot])

    def cast_chunk(slot, chunk, start_next):
        for k in range(n_cast):
            in_copy(k, chunk, slot).wait()

        @pl.when(long_idx > 0)
        def _():
            for k in range(n_cast):
                out_copy(k, chunk - 2, slot).wait()

        for k in range(n_cast):
            out_bufs[k][slot] = in_bufs[k][slot].astype(out_bufs[k].dtype)
            out_copy(k, chunk, slot).start()

        @pl.when(start_next)
        def _():
            for k in range(n_cast):
                in_copy(k, chunk + 1, 1 - slot).start()

    @pl.when(jnp.logical_and(jnp.logical_and(b == 0, g == 0), qi == 0))
    def _():
        for k in range(n_cast):
            in_copy(k, 0, 0).start()

    @pl.when(is_long)
    def _():
        cast_chunk(0, 2 * long_idx, is_long)

    qs = [jnp.concatenate([qn_ref[:, hd * LANES:(hd + 1) * LANES],
                           qr_ref[:, (hd // 2) * LANES:(hd // 2 + 1) * LANES]], axis=1)
          for hd in range(ATTN_HEADS)]

    m_ref[...] = jnp.full_like(m_ref, -jnp.inf)
    l_ref[...] = jnp.zeros_like(l_ref)
    acc_ref[...] = jnp.zeros_like(acc_ref)

    def load_states():
        return [(m_ref[hd], l_ref[hd], acc_ref[hd]) for hd in range(ATTN_HEADS)]

    def body(g, carry):
        base = pl.multiple_of(g * (ATTN_GROUP * TQ), ATTN_GROUP * TQ)
        states = _attn_tiles(qs, kv_ref, kr_ref, base, ATTN_GROUP, False, load_states())
        for hd in range(ATTN_HEADS):
            m_ref[hd], l_ref[hd], acc_ref[hd] = states[hd]
        return carry

    lax.fori_loop(0, qi // ATTN_GROUP, body, 0)

    @pl.when(is_long)
    def _():
        cast_chunk(1, 2 * long_idx + 1, long_idx + 1 < n_long)

    base = pl.multiple_of((qi // ATTN_GROUP) * (ATTN_GROUP * TQ), ATTN_GROUP * TQ)
    for r in range(ATTN_GROUP):
        @pl.when(qi % ATTN_GROUP == r)
        def _():
            states = _attn_tiles(qs, kv_ref, kr_ref, base, r + 1, True, load_states())
            for hd in range(ATTN_HEADS):
                _, l, acc = states[hd]
                o_ref[:, hd * LANES:(hd + 1) * LANES] = (acc / l).astype(o_ref.dtype)

    @pl.when(long_idx == n_long - 1)
    def _():
        for k in range(n_cast):
            out_copy(k, 2 * long_idx, 0).wait()
            out_copy(k, 2 * long_idx + 1, 1).wait()


def mla_attention(q, kv, proj_rope, weights, layer, batch, seq):
    t = q.shape[0]
    nq = seq // TQ
    groups = MLA_HEADS // ATTN_HEADS
    cast_from = nq // 2
    assert cast_from >= ATTN_GROUP
    n_chunks = 2 * batch * groups * (nq - cast_from)
    chunk_rows = [w.shape[1] // n_chunks for w in weights]
    assert all(r * n_chunks == w.shape[1] and r % BF16_SUBLANES == 0 for r, w in zip(chunk_rows, weights))
    any_space = pl.BlockSpec(memory_space=pl.ANY)
    ring = []
    for r, w in zip(chunk_rows, weights):
        ring += [pltpu.VMEM((2, r, w.shape[2]), w.dtype), pltpu.VMEM((2, r, w.shape[2]), BF16),
                 pltpu.SemaphoreType.DMA((2,)), pltpu.SemaphoreType.DMA((2,))]
    nope_w = ATTN_HEADS * MLA_NOPE
    rope_w = ATTN_HEADS * MLA_ROPE
    kr_w = 2 * LANES
    kr_block = (COL_KR - ROPE_BASE) // kr_w
    rope_block0 = (MLA_HEADS * MLA_NOPE) // rope_w
    return _pallas(
        functools.partial(_attn_kernel, n_cast=len(weights), layer=layer, cast_from=cast_from),
        name="mla_attention", grid=(batch, groups, nq), semantics=("arbitrary", "arbitrary", "arbitrary"),
        in_specs=[pl.BlockSpec((TQ, nope_w), lambda b, p, i: (b * nq + i, p)),
                  pl.BlockSpec((TQ, rope_w), lambda b, p, i: (b * nq + i, rope_block0 + p)),
                  pl.BlockSpec((seq, ATTN_HEADS * HEAD_W), lambda b, p, i: (b, p)),
                  pl.BlockSpec((seq, kr_w), lambda b, p, i: (b, kr_block)),
                  *[any_space for _ in weights]],
        operands=(q, q, kv, proj_rope, *weights),
        out_specs=(pl.BlockSpec((TQ, nope_w), lambda b, p, i: (b * nq + i, p)),
                   *[any_space for _ in weights]),
        out_shape=(jax.ShapeDtypeStruct((t, MLA_WIDTH), BF16),
                   *[jax.ShapeDtypeStruct(w.shape[1:], BF16) for w in weights]),
        scratch_shapes=[pltpu.VMEM((ATTN_HEADS, TQ, LANES), F32)] * 3 + ring,
        temp_bytes=ATTN_HEADS * (_nbytes((TQ, TQ), F32) + _nbytes((TQ, TQ), BF16)))


def _ret_kernel(q_ref, k_ref, v_ref, g_ref, d_ref, qd_ref, kd_ref, cd_ref, o_ref, s_ref):
    @pl.when(pl.program_id(1) == 0)
    def _():
        s_ref[...] = jnp.zeros_like(s_ref)

    lane = lax.broadcasted_iota(jnp.int32, (RET_L, LANES), 1)
    for p in range(RET_HEADS // 2):
        ls = slice(p * LANES, (p + 1) * LANES)
        q = q_ref[:, ls]
        k = k_ref[:, ls]
        qf = q.astype(F32)
        kdec = (k.astype(F32) * kd_ref[:, ls]).astype(BF16)
        for hh in range(2):
            head = 2 * p + hh
            own = (lane // ROPE_HALF) % 2 == hh
            vs = slice(head * RET_V, (head + 1) * RET_V)
            v = v_ref[:, vs]
            qm = jnp.where(own, q, jnp.zeros_like(q))
            s = lax.dot_general(qm, k, (((1,), (1,)), ((), ())), preferred_element_type=F32)
            sd = (s * d_ref[head]).astype(BF16)
            qx = jnp.where(own, qf * qd_ref[:, ls], 0.0).astype(BF16)
            state = s_ref[head]
            o = (jnp.dot(sd, v, preferred_element_type=F32)
                 + jnp.dot(qx, state.astype(BF16), preferred_element_type=F32))
            s_ref[head] = cd_ref[head] * state + lax.dot_general(
                kdec, v, (((0,), (0,)), ((), ())), preferred_element_type=F32)
            o = o * lax.rsqrt(jnp.mean(o * o, axis=-1, keepdims=True) + EPS)
            gate = g_ref[:, vs].astype(F32)
            o_ref[:, vs] = (gate * jax.nn.sigmoid(gate) * o).astype(o_ref.dtype)


def retention_mixer(proj, proj_rope, d_tab, qd_tab, kd_tab, cd_tab, batch, seq):
    t = proj.shape[0]
    nsc = seq // RET_L
    qk_w = RET_HEADS * RET_QK
    full = lambda b, c: (0, 0)
    full3 = lambda b, c: (0, 0, 0)
    return _pallas(
        _ret_kernel, name="retention_mixer", grid=(batch, nsc), semantics=("parallel", "arbitrary"),
        in_specs=[pl.BlockSpec((RET_L, qk_w), lambda b, c: (b * nsc + c, (COL_RQ - ROPE_BASE) // qk_w)),
                  pl.BlockSpec((RET_L, qk_w), lambda b, c: (b * nsc + c, (COL_RK - ROPE_BASE) // qk_w)),
                  pl.BlockSpec((RET_L, RET_WIDTH), lambda b, c: (b * nsc + c, COL_RV // RET_WIDTH)),
                  pl.BlockSpec((RET_L, RET_WIDTH), lambda b, c: (b * nsc + c, COL_RG // RET_WIDTH)),
                  pl.BlockSpec((RET_HEADS, RET_L, RET_L), full3),
                  pl.BlockSpec((RET_L, qk_w), full),
                  pl.BlockSpec((RET_L, qk_w), full),
                  pl.BlockSpec((RET_HEADS, LANES, RET_V), full3)],
        operands=(proj_rope, proj_rope, proj, proj, d_tab, qd_tab, kd_tab, cd_tab),
        out_specs=pl.BlockSpec((RET_L, RET_WIDTH), lambda b, c: (b * nsc + c, 0)),
        out_shape=jax.ShapeDtypeStruct((t, RET_WIDTH), BF16),
        scratch_shapes=[pltpu.VMEM((RET_HEADS, LANES, RET_V), F32)],
        temp_bytes=RET_HEADS * _nbytes((RET_L, RET_L), F32))


def retention_tables():
    log_gamma = jnp.log1p(-jnp.exp2(-5.0 - jnp.arange(RET_HEADS, dtype=F32)))
    idx = jnp.arange(RET_L, dtype=F32)
    dist = jnp.abs(idx[:, None] - idx[None, :])
    chunk = jnp.arange(RET_L) // CHUNK
    visible = chunk[None, :] <= chunk[:, None]
    d_tab = jnp.where(visible[None], jnp.exp(log_gamma[:, None, None] * dist[None]), 0.0)
    col_head = _pair_rope_order(RET_HEADS) // RET_QK
    lg_lane = log_gamma[col_head]
    qd_tab = jnp.exp(lg_lane[None, :] * (idx[:, None] + 1.0))
    kd_tab = jnp.exp(lg_lane[None, :] * (RET_L - 1.0 - idx[:, None]))
    cd = jnp.exp(log_gamma * RET_L)
    cd_tab = jnp.broadcast_to(cd[:, None, None], (RET_HEADS, LANES, RET_V))
    return d_tab, qd_tab, kd_tab, cd_tab


def _out_proj_kernel(a_ref, b_ref, c_ref, w_ref, *rest):
    if len(rest) == 3:
        wt_ref, o_ref, wtb_ref = rest
        _w_in_prep_kernel(wt_ref, wtb_ref)
    else:
        o_ref, = rest
    ka = a_ref.shape[1]
    kb = b_ref.shape[1]
    acc = jnp.dot(a_ref[...], w_ref[:ka, :], preferred_element_type=F32)
    acc = acc + jnp.dot(b_ref[...], w_ref[ka:ka + kb, :], preferred_element_type=F32)
    acc = acc + jnp.dot(c_ref[...], w_ref[ka + kb:, :], preferred_element_type=F32)
    o_ref[...] = acc.astype(o_ref.dtype)


def out_proj(a, b, c, w, w_in_t=None, next_layer=None):
    t = a.shape[0]
    k, n = w.shape
    nj = n // TN_OUT
    steps = (t // TM) * nj
    in_specs = [pl.BlockSpec((TM, a.shape[1]), lambda i, j: (i, 0)),
                pl.BlockSpec((TM, b.shape[1]), lambda i, j: (i, 0)),
                pl.BlockSpec((TM, c.shape[1]), lambda i, j: (i, 0)),
                pl.BlockSpec((k, TN_OUT), lambda i, j: (0, j))]
    operands = (a, b, c, w)
    out_specs = pl.BlockSpec((TM, TN_OUT), lambda i, j: (i, j))
    out_shape = jax.ShapeDtypeStruct((t, n), BF16)
    if w_in_t is not None:
        _, rows, d = w_in_t.shape
        slab = d // steps
        assert slab * steps == d and slab % LANES == 0
        in_specs.append(pl.BlockSpec((None, rows, slab), lambda i, j: (next_layer, 0, i * nj + j)))
        operands += (w_in_t,)
        out_specs = (out_specs, pl.BlockSpec((IN_WIDTH_PADDED, slab), lambda i, j: (0, i * nj + j)))
        out_shape = (out_shape, jax.ShapeDtypeStruct((IN_WIDTH_PADDED, d), BF16))
    return _pallas(
        _out_proj_kernel, name="out_proj", grid=(t // TM, nj), semantics=("parallel", "arbitrary"),
        in_specs=in_specs, operands=operands, out_specs=out_specs, out_shape=out_shape,
        temp_bytes=_nbytes((TM, TN_OUT), F32))


def _ffn_kernel(h_ref, wu_ref, wd_ref, o_ref):
    @pl.when(pl.program_id(1) == 0)
    def _():
        o_ref[...] = jnp.zeros_like(o_ref)

    a = jnp.dot(h_ref[...], wu_ref[...], preferred_element_type=F32)
    a = jnp.square(jnp.maximum(a, 0.0)).astype(BF16)
    o_ref[...] += jnp.dot(a, wd_ref[...], preferred_element_type=F32)


def ffn(h, wu_b, wd_b):
    t, d = h.shape
    ff = wu_b.shape[1]
    return _pallas(
        _ffn_kernel, name="ffn", grid=(t // TM_FFN, ff // TF_FFN), semantics=("parallel", "arbitrary"),
        in_specs=[pl.BlockSpec((TM_FFN, d), lambda i, j: (i, 0)),
                  pl.BlockSpec((d, TF_FFN), lambda i, j: (0, j)),
                  pl.BlockSpec((TF_FFN, d), lambda i, j: (j, 0))],
        operands=(h, wu_b, wd_b),
        out_specs=pl.BlockSpec((TM_FFN, d), lambda i, j: (i, 0)),
        out_shape=jax.ShapeDtypeStruct((t, d), F32),
        temp_bytes=_nbytes((TM_FFN, TF_FFN), F32) + _nbytes((TM_FFN, TF_FFN), BF16))


_SRC_SIZES = (SGU_WIDTH, SGU_WIDTH, MLA_Q_RANK, MLA_KV_RANK, MLA_ROPE,
              RET_HEADS * RET_QK, RET_HEADS * RET_QK, RET_WIDTH, RET_WIDTH)
(_SRC_U, _SRC_V, _SRC_CQ, _SRC_CKV, _SRC_KR, _SRC_RQ, _SRC_RK, _SRC_RV, _SRC_RG,
 IN_WIDTH) = [int(o) for o in np.concatenate([[0], np.cumsum(_SRC_SIZES)])]
_W_IN_MOVES = (
    (COL_U, _SRC_U, 3 * IN_TILE),
    (COL_RV, _SRC_RV, RET_WIDTH),
    (COL_RG, _SRC_RG, RET_WIDTH),
    (COL_CKV, _SRC_CKV, MLA_KV_RANK),
    (COL_KR, _SRC_KR, ROPE_HALF),
    (COL_KR + LANES // 2, _SRC_KR + ROPE_HALF, ROPE_HALF),
    (COL_KR + LANES + ROPE_HALF, _SRC_KR, ROPE_HALF),
    (COL_KR + LANES + LANES // 2 + ROPE_HALF, _SRC_KR + ROPE_HALF, ROPE_HALF),
    *[(dst0 + i * ROPE_HALF, src0 + int(s0), ROPE_HALF)
      for dst0, src0 in ((COL_RQ, _SRC_RQ), (COL_RK, _SRC_RK))
      for i, s0 in enumerate(_pair_rope_order(RET_HEADS)[::ROPE_HALF])],
)


def _w_in_prep_kernel(w_ref, o_ref):
    o_ref[COL_KR:COL_RQ, :] = jnp.zeros((COL_RQ - COL_KR, o_ref.shape[1]), o_ref.dtype)
    for dst, src, width in _W_IN_MOVES:
        o_ref[dst:dst + width, :] = w_ref[src:src + width, :].astype(o_ref.dtype)


def prep_w_in(w_t, layer):
    _, n, d = w_t.shape
    return _pallas(
        _w_in_prep_kernel, name="prep_w_in", grid=(d // TK_PREP,), semantics=("parallel",),
        in_specs=[pl.BlockSpec((None, n, TK_PREP), lambda i: (layer, 0, i))],
        operands=(w_t,),
        out_specs=pl.BlockSpec((IN_WIDTH_PADDED, TK_PREP), lambda i: (0, i)),
        out_shape=jax.ShapeDtypeStruct((IN_WIDTH_PADDED, d), BF16))


def _prep_wq(w):
    k = w.shape[0]
    w3 = w.reshape(k, MLA_HEADS, MLA_NOPE + MLA_ROPE)
    nope = w3[:, :, :MLA_NOPE].reshape(k, MLA_HEADS * MLA_NOPE)
    rope = w3[:, :, MLA_NOPE:].reshape(k, MLA_HEADS * MLA_ROPE)[:, _pair_rope_order(MLA_HEADS)]
    return jnp.concatenate([nope, rope], axis=1).astype(BF16)


def _rope_lane_tables(seq):
    inv_freq = 1.0 / (ROPE_THETA ** (jnp.arange(0, MLA_ROPE, 2, dtype=F32) / MLA_ROPE))
    ang = jnp.arange(seq, dtype=F32)[:, None] * inv_freq[None, :]
    cos, sin = jnp.cos(ang), jnp.sin(ang)
    cos_t = jnp.concatenate([cos, cos, cos, cos], axis=1)
    sin_t = jnp.concatenate([-sin, -sin, sin, sin], axis=1)
    return cos_t, sin_t


def kernel(x, norm_mix_pre, norm_mix_post, norm_ffn_pre, norm_ffn_post, w_in, sgu_ln_g, sgu_ln_b, sgu_w, sgu_b, mla_q_norm, mla_wq_b, mla_kv_norm, mla_wkv_b, w_out, w_up, w_down):
    batch, seq, d = x.shape
    depth = w_in.shape[0]
    cos_t, sin_t = _rope_lane_tables(seq)
    d_tab, qd_tab, kd_tab, cd_tab = retention_tables()
    w_in_t = jnp.swapaxes(w_in, 1, 2)

    xf = x.reshape(batch * seq, d)
    h = rmsnorm_rows(xf, norm_mix_pre[0])
    w_in_prepped = prep_w_in(w_in_t, 0)
    for l in range(depth):
        proj, proj_rope = in_proj(h, w_in_prepped, cos_t, sin_t, seq)
        bias_full = jnp.repeat(sgu_b[l].T, SGU_WIDTH // SGU_HEADS, axis=1)
        out_a = sgu_mixer(proj, sgu_ln_g[l], sgu_ln_b[l], sgu_w[l], bias_full)
        q = q_proj(proj, mla_q_norm[l], _prep_wq(mla_wq_b[l]), cos_t, sin_t, seq)
        kv = kv_proj(proj_rope, mla_kv_norm[l], mla_wkv_b, l)
        out_b, wu_b, wd_b, wo_b = mla_attention(q, kv, proj_rope, (w_up, w_down, w_out), l, batch, seq)
        out_c = retention_mixer(proj, proj_rope, d_tab, qd_tab, kd_tab, cd_tab, batch, seq)
        if l + 1 < depth:
            mixed, w_in_prepped = out_proj(out_a, out_b, out_c, wo_b, w_in_t, l + 1)
        else:
            mixed = out_proj(out_a, out_b, out_c, wo_b)
        h2 = mix_norm(xf, mixed, norm_mix_post[l], norm_ffn_pre[l])
        f = ffn(h2, wu_b, wd_b)
        g_next = norm_mix_pre[l + 1] if l + 1 < depth else None
        xf, h = layer_residual(xf, mixed, norm_mix_post[l], f, norm_ffn_post[l], g_next)
    return xf.reshape(batch, seq, d)
```

```python
import functools
import math

import numpy as np
import jax
import jax.numpy as jnp
from jax import lax
from jax.experimental import pallas as pl
from jax.experimental.pallas import tpu as pltpu

F32 = jnp.float32
BF16 = jnp.bfloat16

CHUNK = 64
EPS = 1e-6
ROPE_THETA = 10000.0
SGU_BLOCK = 128
SGU_WIDTH = 1024
SGU_HEADS = 8
MLA_HEADS = 16
MLA_NOPE = 128
MLA_ROPE = 64
MLA_V = 128
MLA_Q_RANK = 1024
MLA_KV_RANK = 512
MLA_WIDTH = MLA_HEADS * MLA_V
RET_HEADS = 8
RET_QK = 64
RET_V = 128
RET_WIDTH = RET_HEADS * RET_V

LANES = 128
BF16_SUBLANES = 16
VMEM_CAP = 64 * 1024 * 1024
VMEM_RESERVED = 2 * 1024 * 1024
VMEM_INTERNAL = 6 * 1024 * 1024
PIPELINE_BUFFERS = 2

IN_TILE = 1024
COL_U = 0 * IN_TILE
COL_V = 1 * IN_TILE
COL_CQ = 2 * IN_TILE
COL_RV = 3 * IN_TILE
COL_RG = 4 * IN_TILE
COL_CKV = 5 * IN_TILE
COL_KR = COL_CKV + MLA_KV_RANK
COL_RQ = 6 * IN_TILE
COL_RK = COL_RQ + RET_HEADS * RET_QK
IN_WIDTH_PADDED = 7 * IN_TILE
ROPE_BASE = COL_CKV

TM = 1024
TM_FROM_F32 = 512
TM_QKV = 2048
NORM_TILES = (512, 256)
TM_FFN = 512
TF_FFN = 1024
TN_OUT = 1024
TK_PREP = 256
TB_SGU = 512
TQ = 512
ATTN_GROUP = 4
ATTN_HEADS = 4
RET_L = 256


def _nbytes(shape, dtype):
    return math.prod(1 if s is None else s for s in shape) * jnp.dtype(dtype).itemsize


def _pallas(kernel, *, name, grid, semantics, in_specs, operands, out_specs, out_shape,
            scratch_shapes=(), temp_bytes=0):
    outs = out_shape if isinstance(out_shape, (tuple, list)) else (out_shape,)
    ospecs = out_specs if isinstance(out_specs, (tuple, list)) else (out_specs,)
    pipelined = sum(_nbytes(s.block_shape, a.dtype) for s, a in zip(in_specs, operands))
    pipelined += sum(_nbytes(s.block_shape, o.dtype) for s, o in zip(ospecs, outs))
    scratch = sum(_nbytes(s.shape, s.dtype) for s in scratch_shapes)
    need = PIPELINE_BUFFERS * pipelined + scratch + temp_bytes + VMEM_INTERNAL
    params = pltpu.CompilerParams(dimension_semantics=semantics,
                                  vmem_limit_bytes=int(min(need, VMEM_CAP - VMEM_RESERVED)))
    return pl.pallas_call(kernel, out_shape=out_shape, grid=grid, in_specs=in_specs, out_specs=out_specs,
                          scratch_shapes=scratch_shapes, compiler_params=params, name=name)(*operands)


def _rms(x, g):
    return x * lax.rsqrt(jnp.mean(x * x, axis=-1, keepdims=True) + EPS) * g


def _rmsnorm_kernel(x_ref, g_ref, o_ref):
    o_ref[...] = _rms(x_ref[...], g_ref[...]).astype(o_ref.dtype)


def _norm_tile(d, row_dtypes):
    for tile in NORM_TILES:
        need = PIPELINE_BUFFERS * sum(_nbytes((tile, d), dt) for dt in row_dtypes)
        if need + _nbytes((tile, d), F32) + VMEM_INTERNAL <= VMEM_CAP - VMEM_RESERVED:
            return tile
    raise ValueError("no row tile fits VMEM")


def rmsnorm_rows(x, g):
    t, d = x.shape
    tm = _norm_tile(d, (x.dtype, BF16))
    return _pallas(
        _rmsnorm_kernel, name="rmsnorm_rows", grid=(t // tm,), semantics=("parallel",),
        in_specs=[pl.BlockSpec((tm, d), lambda i: (i, 0)),
                  pl.BlockSpec((1, d), lambda i: (0, 0))],
        operands=(x, g.reshape(1, d)),
        out_specs=pl.BlockSpec((tm, d), lambda i: (i, 0)),
        out_shape=jax.ShapeDtypeStruct((t, d), BF16),
        temp_bytes=_nbytes((tm, d), F32))


def _mix_norm_kernel(x_ref, y_ref, gp_ref, gn_ref, ho_ref):
    xm = x_ref[...] + _rms(y_ref[...].astype(F32), gp_ref[...])
    ho_ref[...] = _rms(xm, gn_ref[...]).astype(ho_ref.dtype)


def mix_norm(x, y, g_post, g_next):
    t, d = x.shape
    tm = _norm_tile(d, (x.dtype, y.dtype, BF16))
    row = pl.BlockSpec((tm, d), lambda i: (i, 0))
    vec = pl.BlockSpec((1, d), lambda i: (0, 0))
    return _pallas(_mix_norm_kernel, name="mix_norm", grid=(t // tm,), semantics=("parallel",),
                   in_specs=[row, row, vec, vec],
                   operands=(x, y, g_post.reshape(1, d), g_next.reshape(1, d)),
                   out_specs=row, out_shape=jax.ShapeDtypeStruct((t, d), BF16),
                   temp_bytes=_nbytes((tm, d), F32))


def _layer_residual(x_ref, y_ref, gy_ref, f_ref, gf_ref):
    xm = x_ref[...] + _rms(y_ref[...].astype(F32), gy_ref[...])
    return xm + _rms(f_ref[...].astype(F32), gf_ref[...])


def _layer_residual_norm_kernel(x_ref, y_ref, gy_ref, f_ref, gf_ref, gn_ref, xo_ref, ho_ref):
    xn = _layer_residual(x_ref, y_ref, gy_ref, f_ref, gf_ref)
    xo_ref[...] = xn
    ho_ref[...] = _rms(xn, gn_ref[...]).astype(ho_ref.dtype)


def _layer_residual_kernel(x_ref, y_ref, gy_ref, f_ref, gf_ref, xo_ref):
    xo_ref[...] = _layer_residual(x_ref, y_ref, gy_ref, f_ref, gf_ref)


def layer_residual(x, y, g_y, f, g_f, g_next):
    t, d = x.shape
    out_dtypes = (F32,) if g_next is None else (F32, BF16)
    tm = _norm_tile(d, (x.dtype, y.dtype, f.dtype) + out_dtypes)
    row = pl.BlockSpec((tm, d), lambda i: (i, 0))
    vec = pl.BlockSpec((1, d), lambda i: (0, 0))
    common = dict(grid=(t // tm,), semantics=("parallel",), temp_bytes=_nbytes((tm, d), F32))
    if g_next is None:
        return _pallas(_layer_residual_kernel, name="layer_residual", in_specs=[row, row, vec, row, vec],
                       operands=(x, y, g_y.reshape(1, d), f, g_f.reshape(1, d)), out_specs=row,
                       out_shape=jax.ShapeDtypeStruct((t, d), F32), **common), None
    return _pallas(_layer_residual_norm_kernel, name="layer_residual_norm",
                   in_specs=[row, row, vec, row, vec, vec],
                   operands=(x, y, g_y.reshape(1, d), f, g_f.reshape(1, d), g_next.reshape(1, d)),
                   out_specs=(row, row),
                   out_shape=(jax.ShapeDtypeStruct((t, d), F32), jax.ShapeDtypeStruct((t, d), BF16)),
                   **common)


ROPE_HALF = MLA_ROPE // 2


def _rope_cols(x, cos, sin):
    return x * cos + pltpu.roll(x, LANES // 2, 1) * sin


def _pair_rope_order(n_heads):
    src = np.arange(n_heads * MLA_ROPE).reshape(n_heads // 2, 2, 2, ROPE_HALF)
    return src.transpose(0, 2, 1, 3).reshape(-1)


def _in_proj_main_kernel(h_ref, w_ref, o_ref):
    o_ref[...] = lax.dot_general(h_ref[...], w_ref[...], (((1,), (1,)), ((), ())),
                                 preferred_element_type=F32).astype(o_ref.dtype)


def _in_proj_main_norm_kernel(x_ref, g_ref, w_ref, o_ref, h_ref):
    @pl.when(pl.program_id(1) == 0)
    def _():
        h_ref[...] = _rms(x_ref[...], g_ref[...]).astype(h_ref.dtype)

    _in_proj_main_kernel(h_ref, w_ref, o_ref)


def _in_proj_rope_kernel(h_ref, w_ref, cos_ref, sin_ref, o_ref):
    j = pl.program_id(1)
    acc = lax.dot_general(h_ref[...], w_ref[...], (((1,), (1,)), ((), ())), preferred_element_type=F32)

    @pl.when(j == 0)
    def _():
        lo = COL_KR - COL_CKV
        o_ref[:, :lo] = acc[:, :lo].astype(o_ref.dtype)
        for c in range(lo // LANES, lo // LANES + 2):
            sl = slice(c * LANES, (c + 1) * LANES)
            o_ref[:, sl] = _rope_cols(acc[:, sl], cos_ref[...], sin_ref[...]).astype(o_ref.dtype)
        o_ref[:, lo + 2 * LANES:] = acc[:, lo + 2 * LANES:].astype(o_ref.dtype)

    @pl.when(j == 1)
    def _():
        q_cols = (RET_HEADS * RET_QK) // LANES
        for c in range(IN_TILE // LANES):
            sl = slice(c * LANES, (c + 1) * LANES)
            r = _rope_cols(acc[:, sl], cos_ref[...], sin_ref[...])
            if c < q_cols:
                r = r * (RET_QK ** -0.5)
            o_ref[:, sl] = r.astype(o_ref.dtype)


def in_proj(h, w_t, cos_t, sin_t, seq, x=None, g=None):
    t, d = (x if h is None else h).shape
    pos_blocks = seq // TM
    tab = pl.BlockSpec((TM, LANES), lambda i, j: (i % pos_blocks, 0))
    rope_tile0 = ROPE_BASE // IN_TILE
    rope_w = IN_WIDTH_PADDED - ROPE_BASE
    common = dict(semantics=("parallel", "arbitrary"), temp_bytes=_nbytes((TM, IN_TILE), F32),
                  out_specs=pl.BlockSpec((TM, IN_TILE), lambda i, j: (i, j)))
    if h is None:
        tm = TM_FROM_F32
        main, h = _pallas(
            _in_proj_main_norm_kernel, name="in_proj_main_norm", grid=(t // tm, rope_tile0),
            semantics=("parallel", "arbitrary"),
            in_specs=[pl.BlockSpec((tm, d), lambda i, j: (i, 0)),
                      pl.BlockSpec((1, d), lambda i, j: (0, 0)),
                      pl.BlockSpec((IN_TILE, d), lambda i, j: (j, 0))],
            operands=(x, g.reshape(1, d), w_t),
            out_specs=(pl.BlockSpec((tm, IN_TILE), lambda i, j: (i, j)),
                       pl.BlockSpec((tm, d), lambda i, j: (i, 0))),
            out_shape=(jax.ShapeDtypeStruct((t, ROPE_BASE), BF16), jax.ShapeDtypeStruct((t, d), BF16)),
            temp_bytes=_nbytes((tm, IN_TILE), F32) + _nbytes((tm, d), F32))
    else:
        main = _pallas(
            _in_proj_main_kernel, name="in_proj_main", grid=(t // TM, rope_tile0),
            in_specs=[pl.BlockSpec((TM, d), lambda i, j: (i, 0)),
                      pl.BlockSpec((IN_TILE, d), lambda i, j: (j, 0))],
            operands=(h, w_t), out_shape=jax.ShapeDtypeStruct((t, ROPE_BASE), BF16), **common)
    rope = _pallas(
        _in_proj_rope_kernel, name="in_proj_rope", grid=(t // TM, rope_w // IN_TILE),
        in_specs=[pl.BlockSpec((TM, d), lambda i, j: (i, 0)),
                  pl.BlockSpec((IN_TILE, d), lambda i, j: (rope_tile0 + j, 0)),
                  tab, tab],
        operands=(h, w_t, cos_t, sin_t), out_shape=jax.ShapeDtypeStruct((t, rope_w), BF16), **common)
    return main, rope


def _sgu_kernel(u_ref, v_ref, g_ref, b_ref, w_ref, bias_ref, o_ref):
    row = lax.broadcasted_iota(jnp.int32, (SGU_BLOCK, SGU_BLOCK), 0)
    col = lax.broadcasted_iota(jnp.int32, (SGU_BLOCK, SGU_BLOCK), 1)
    keep = (col // CHUNK) <= (row // CHUNK)
    for g in range(SGU_HEADS):
        sl = slice(g * LANES, (g + 1) * LANES)
        wg = jnp.where(keep, w_ref[g], 0.0).astype(BF16)
        vg = v_ref[:, sl].astype(F32)
        mu = jnp.mean(vg, axis=-1, keepdims=True)
        vc = vg - mu
        y = vc * lax.rsqrt(jnp.mean(vc * vc, axis=-1, keepdims=True) + EPS) * g_ref[:, sl] + b_ref[:, sl]
        yb = y.astype(BF16)
        for n in range(TB_SGU // SGU_BLOCK):
            rs = slice(n * SGU_BLOCK, (n + 1) * SGU_BLOCK)
            mixed = jnp.dot(wg, yb[rs, :], preferred_element_type=F32) + bias_ref[:, sl]
            o_ref[rs, sl] = (u_ref[rs, sl].astype(F32) * mixed).astype(o_ref.dtype)


def sgu_mixer(proj, ln_g, ln_b, w_s, bias_full):
    t = proj.shape[0]
    vec = pl.BlockSpec((1, SGU_WIDTH), lambda i: (0, 0))
    return _pallas(
        _sgu_kernel, name="sgu_mixer", grid=(t // TB_SGU,), semantics=("parallel",),
        in_specs=[pl.BlockSpec((TB_SGU, SGU_WIDTH), lambda i: (i, COL_U // SGU_WIDTH)),
                  pl.BlockSpec((TB_SGU, SGU_WIDTH), lambda i: (i, COL_V // SGU_WIDTH)),
                  vec, vec,
                  pl.BlockSpec((SGU_HEADS, SGU_BLOCK, SGU_BLOCK), lambda i: (0, 0, 0)),
                  pl.BlockSpec((SGU_BLOCK, SGU_WIDTH), lambda i: (0, 0))],
        operands=(proj, proj, ln_g.reshape(1, -1), ln_b.reshape(1, -1), w_s, bias_full),
        out_specs=pl.BlockSpec((TB_SGU, SGU_WIDTH), lambda i: (i, 0)),
        out_shape=jax.ShapeDtypeStruct((t, SGU_WIDTH), BF16),
        temp_bytes=_nbytes((TB_SGU, SGU_WIDTH), F32))


Q_SCALE = (MLA_NOPE + MLA_ROPE) ** -0.5 * math.log2(math.e)
Q_ROPE_TILE = (MLA_HEADS * MLA_NOPE) // IN_TILE


def _q_proj_kernel(c_ref, g_ref, w_ref, cos_ref, sin_ref, o_ref, cn_ref):
    j = pl.program_id(1)

    @pl.when(j == 0)
    def _():
        cn_ref[...] = _rms(c_ref[...].astype(F32), g_ref[...]).astype(cn_ref.dtype)

    acc = jnp.dot(cn_ref[...], w_ref[...], preferred_element_type=F32) * Q_SCALE

    @pl.when(j < Q_ROPE_TILE)
    def _():
        o_ref[...] = acc.astype(o_ref.dtype)

    @pl.when(j >= Q_ROPE_TILE)
    def _():
        for c in range(IN_TILE // LANES):
            sl = slice(c * LANES, (c + 1) * LANES)
            o_ref[:, sl] = _rope_cols(acc[:, sl], cos_ref[...], sin_ref[...]).astype(o_ref.dtype)


def q_proj(proj, g, w, cos_t, sin_t, seq):
    t = proj.shape[0]
    n = w.shape[1]
    pos_blocks = seq // TM_QKV
    tab = pl.BlockSpec((TM_QKV, LANES), lambda i, j: (i % pos_blocks, 0))
    return _pallas(
        _q_proj_kernel, name="q_proj", grid=(t // TM_QKV, n // IN_TILE), semantics=("parallel", "arbitrary"),
        in_specs=[pl.BlockSpec((TM_QKV, MLA_Q_RANK), lambda i, j: (i, COL_CQ // MLA_Q_RANK)),
                  pl.BlockSpec((1, MLA_Q_RANK), lambda i, j: (0, 0)),
                  pl.BlockSpec((MLA_Q_RANK, IN_TILE), lambda i, j: (0, j)),
                  tab, tab],
        operands=(proj, g.reshape(1, -1), w, cos_t, sin_t),
        out_specs=pl.BlockSpec((TM_QKV, IN_TILE), lambda i, j: (i, j)),
        out_shape=jax.ShapeDtypeStruct((t, n), BF16),
        scratch_shapes=[pltpu.VMEM((TM_QKV, MLA_Q_RANK), BF16)],
        temp_bytes=_nbytes((TM_QKV, IN_TILE), F32))


def _kv_proj_kernel(c_ref, g_ref, w_ref, o_ref, cn_ref):
    @pl.when(pl.program_id(1) == 0)
    def _():
        cn_ref[...] = _rms(c_ref[...].astype(F32), g_ref[...]).astype(cn_ref.dtype)

    o_ref[...] = jnp.dot(cn_ref[...], w_ref[...].astype(BF16), preferred_element_type=F32).astype(o_ref.dtype)


def kv_proj(proj_rope, g, w, layer):
    t = proj_rope.shape[0]
    n = w.shape[2]
    return _pallas(
        _kv_proj_kernel, name="kv_proj", grid=(t // TM_QKV, n // IN_TILE), semantics=("parallel", "arbitrary"),
        in_specs=[pl.BlockSpec((TM_QKV, MLA_KV_RANK), lambda i, j: (i, (COL_CKV - ROPE_BASE) // MLA_KV_RANK)),
                  pl.BlockSpec((1, MLA_KV_RANK), lambda i, j: (0, 0)),
                  pl.BlockSpec((None, MLA_KV_RANK, IN_TILE), lambda i, j: (layer, 0, j))],
        operands=(proj_rope, g.reshape(1, -1), w),
        out_specs=pl.BlockSpec((TM_QKV, IN_TILE), lambda i, j: (i, j)),
        out_shape=jax.ShapeDtypeStruct((t, n), BF16),
        scratch_shapes=[pltpu.VMEM((TM_QKV, MLA_KV_RANK), BF16)],
        temp_bytes=_nbytes((TM_QKV, IN_TILE), F32) + _nbytes((MLA_KV_RANK, IN_TILE), BF16))


HEAD_W = MLA_NOPE + MLA_V


def _attn_tiles(qs, kv_ref, kr_ref, base, ntiles, mask_last, states):
    states = list(states)
    ones = jnp.ones((TQ, LANES), BF16)
    for t in range(ntiles):
        st = pl.multiple_of(base + t * TQ, TQ)
        for hd in range(ATTN_HEADS):
            m, l, acc = states[hd]
            kn = kv_ref[pl.ds(st, TQ), hd * HEAD_W:hd * HEAD_W + MLA_NOPE]
            v = kv_ref[pl.ds(st, TQ), hd * HEAD_W + MLA_NOPE:(hd + 1) * HEAD_W]
            par = hd % 2
            k = jnp.concatenate([kn, kr_ref[pl.ds(st, TQ), par * LANES:(par + 1) * LANES]], axis=1)
            s = lax.dot_general(qs[hd], k, (((1,), (1,)), ((), ())), preferred_element_type=F32)
            if mask_last and t == ntiles - 1:
                row = lax.broadcasted_iota(jnp.int32, (TQ, TQ), 0)
                col = lax.broadcasted_iota(jnp.int32, (TQ, TQ), 1)
                s = jnp.where((col // CHUNK) <= (row // CHUNK), s, -jnp.inf)
            m_next = jnp.maximum(m, jnp.max(s, axis=1, keepdims=True))
            alpha = jnp.exp2(m - m_next)
            p = jnp.exp2(s - jnp.tile(m_next, (1, TQ // LANES))).astype(BF16)
            pv = jnp.dot(p, jnp.concatenate([v, ones], axis=1), preferred_element_type=F32)
            states[hd] = (m_next, alpha * l + pv[:, LANES:], alpha * acc + pv[:, :LANES])
    return states


def _attn_kernel(qn_ref, qr_ref, kv_ref, kr_ref, *rest, n_cast):
    w_refs, o_ref = rest[:n_cast], rest[n_cast]
    wb_refs = rest[n_cast + 1:2 * n_cast + 1]
    m_ref, l_ref, acc_ref = rest[2 * n_cast + 1:]
    qi = pl.program_id(2)

    for w_ref, wb_ref in zip(w_refs, wb_refs):
        wb_ref[...] = w_ref[...].astype(wb_ref.dtype)

    qs = [jnp.concatenate([qn_ref[:, hd * LANES:(hd + 1) * LANES],
                           qr_ref[:, (hd // 2) * LANES:(hd // 2 + 1) * LANES]], axis=1)
          for hd in range(ATTN_HEADS)]

    m_ref[...] = jnp.full_like(m_ref, -jnp.inf)
    l_ref[...] = jnp.zeros_like(l_ref)
    acc_ref[...] = jnp.zeros_like(acc_ref)

    def load_states():
        return [(m_ref[hd], l_ref[hd], acc_ref[hd]) for hd in range(ATTN_HEADS)]

    def body(g, carry):
        base = pl.multiple_of(g * (ATTN_GROUP * TQ), ATTN_GROUP * TQ)
        states = _attn_tiles(qs, kv_ref, kr_ref, base, ATTN_GROUP, False, load_states())
        for hd in range(ATTN_HEADS):
            m_ref[hd], l_ref[hd], acc_ref[hd] = states[hd]
        return carry

    lax.fori_loop(0, qi // ATTN_GROUP, body, 0)

    base = pl.multiple_of((qi // ATTN_GROUP) * (ATTN_GROUP * TQ), ATTN_GROUP * TQ)
    for r in range(ATTN_GROUP):
        @pl.when(qi % ATTN_GROUP == r)
        def _():
            states = _attn_tiles(qs, kv_ref, kr_ref, base, r + 1, True, load_states())
            for hd in range(ATTN_HEADS):
                _, l, acc = states[hd]
                o_ref[:, hd * LANES:(hd + 1) * LANES] = (acc / l).astype(o_ref.dtype)


def mla_attention(q, kv, proj_rope, weights, layer, batch, seq):
    t = q.shape[0]
    nq = seq // TQ
    groups = MLA_HEADS // ATTN_HEADS
    steps = batch * groups * nq
    slab_rows = [w.shape[1] // steps for w in weights]
    assert all(r * steps == w.shape[1] and r % BF16_SUBLANES == 0 for r, w in zip(slab_rows, weights))

    def slab(b, g, i):
        return (b * groups + g) * nq + i
    nope_w = ATTN_HEADS * MLA_NOPE
    rope_w = ATTN_HEADS * MLA_ROPE
    kr_w = 2 * LANES
    kr_block = (COL_KR - ROPE_BASE) // kr_w
    rope_block0 = (MLA_HEADS * MLA_NOPE) // rope_w
    return _pallas(
        functools.partial(_attn_kernel, n_cast=len(weights)),
        name="mla_attention", grid=(batch, groups, nq), semantics=("parallel", "parallel", "arbitrary"),
        in_specs=[pl.BlockSpec((TQ, nope_w), lambda b, p, i: (b * nq + i, p)),
                  pl.BlockSpec((TQ, rope_w), lambda b, p, i: (b * nq + i, rope_block0 + p)),
                  pl.BlockSpec((seq, ATTN_HEADS * HEAD_W), lambda b, p, i: (b, p)),
                  pl.BlockSpec((seq, kr_w), lambda b, p, i: (b, kr_block)),
                  *[pl.BlockSpec((None, r, w.shape[2]), lambda b, p, i: (layer, slab(b, p, i), 0))
                    for r, w in zip(slab_rows, weights)]],
        operands=(q, q, kv, proj_rope, *weights),
        out_specs=(pl.BlockSpec((TQ, nope_w), lambda b, p, i: (b * nq + i, p)),
                   *[pl.BlockSpec((r, w.shape[2]), lambda b, p, i: (slab(b, p, i), 0))
                     for r, w in zip(slab_rows, weights)]),
        out_shape=(jax.ShapeDtypeStruct((t, MLA_WIDTH), BF16),
                   *[jax.ShapeDtypeStruct(w.shape[1:], BF16) for w in weights]),
        scratch_shapes=[pltpu.VMEM((ATTN_HEADS, TQ, LANES), F32)] * 3,
        temp_bytes=ATTN_HEADS * (_nbytes((TQ, TQ), F32) + _nbytes((TQ, TQ), BF16)))


def _ret_kernel(q_ref, k_ref, v_ref, g_ref, d_ref, qd_ref, kd_ref, cd_ref, o_ref, s_ref):
    @pl.when(pl.program_id(1) == 0)
    def _():
        s_ref[...] = jnp.zeros_like(s_ref)

    lane = lax.broadcasted_iota(jnp.int32, (RET_L, LANES), 1)
    for p in range(RET_HEADS // 2):
        ls = slice(p * LANES, (p + 1) * LANES)
        q = q_ref[:, ls]
        k = k_ref[:, ls]
        qf = q.astype(F32)
        kdec = (k.astype(F32) * kd_ref[:, ls]).astype(BF16)
        for hh in range(2):
            head = 2 * p + hh
            own = (lane // ROPE_HALF) % 2 == hh
            vs = slice(head * RET_V, (head + 1) * RET_V)
            v = v_ref[:, vs]
            qm = jnp.where(own, q, jnp.zeros_like(q))
            s = lax.dot_general(qm, k, (((1,), (1,)), ((), ())), preferred_element_type=F32)
            sd = (s * d_ref[head]).astype(BF16)
            qx = jnp.where(own, qf * qd_ref[:, ls], 0.0).astype(BF16)
            state = s_ref[head]
            o = (jnp.dot(sd, v, preferred_element_type=F32)
                 + jnp.dot(qx, state.astype(BF16), preferred_element_type=F32))
            s_ref[head] = cd_ref[head] * state + lax.dot_general(
                kdec, v, (((0,), (0,)), ((), ())), preferred_element_type=F32)
            o = o * lax.rsqrt(jnp.mean(o * o, axis=-1, keepdims=True) + EPS)
            gate = g_ref[:, vs].astype(F32)
            o_ref[:, vs] = (gate * jax.nn.sigmoid(gate) * o).astype(o_ref.dtype)


def retention_mixer(proj, proj_rope, d_tab, qd_tab, kd_tab, cd_tab, batch, seq):
    t = proj.shape[0]
    nsc = seq // RET_L
    qk_w = RET_HEADS * RET_QK
    full = lambda b, c: (0, 0)
    full3 = lambda b, c: (0, 0, 0)
    return _pallas(
        _ret_kernel, name="retention_mixer", grid=(batch, nsc), semantics=("parallel", "arbitrary"),
        in_specs=[pl.BlockSpec((RET_L, qk_w), lambda b, c: (b * nsc + c, (COL_RQ - ROPE_BASE) // qk_w)),
                  pl.BlockSpec((RET_L, qk_w), lambda b, c: (b * nsc + c, (COL_RK - ROPE_BASE) // qk_w)),
                  pl.BlockSpec((RET_L, RET_WIDTH), lambda b, c: (b * nsc + c, COL_RV // RET_WIDTH)),
                  pl.BlockSpec((RET_L, RET_WIDTH), lambda b, c: (b * nsc + c, COL_RG // RET_WIDTH)),
                  pl.BlockSpec((RET_HEADS, RET_L, RET_L), full3),
                  pl.BlockSpec((RET_L, qk_w), full),
                  pl.BlockSpec((RET_L, qk_w), full),
                  pl.BlockSpec((RET_HEADS, LANES, RET_V), full3)],
        operands=(proj_rope, proj_rope, proj, proj, d_tab, qd_tab, kd_tab, cd_tab),
        out_specs=pl.BlockSpec((RET_L, RET_WIDTH), lambda b, c: (b * nsc + c, 0)),
        out_shape=jax.ShapeDtypeStruct((t, RET_WIDTH), BF16),
        scratch_shapes=[pltpu.VMEM((RET_HEADS, LANES, RET_V), F32)],
        temp_bytes=RET_HEADS * _nbytes((RET_L, RET_L), F32))


def retention_tables():
    log_gamma = jnp.log1p(-jnp.exp2(-5.0 - jnp.arange(RET_HEADS, dtype=F32)))
    idx = jnp.arange(RET_L, dtype=F32)
    dist = jnp.abs(idx[:, None] - idx[None, :])
    chunk = jnp.arange(RET_L) // CHUNK
    visible = chunk[None, :] <= chunk[:, None]
    d_tab = jnp.where(visible[None], jnp.exp(log_gamma[:, None, None] * dist[None]), 0.0)
    col_head = _pair_rope_order(RET_HEADS) // RET_QK
    lg_lane = log_gamma[col_head]
    qd_tab = jnp.exp(lg_lane[None, :] * (idx[:, None] + 1.0))
    kd_tab = jnp.exp(lg_lane[None, :] * (RET_L - 1.0 - idx[:, None]))
    cd = jnp.exp(log_gamma * RET_L)
    cd_tab = jnp.broadcast_to(cd[:, None, None], (RET_HEADS, LANES, RET_V))
    return d_tab, qd_tab, kd_tab, cd_tab


def _out_proj_kernel(a_ref, b_ref, c_ref, w_ref, *rest):
    if len(rest) == 3:
        wt_ref, o_ref, wtb_ref = rest
        _w_in_prep_kernel(wt_ref, wtb_ref)
    else:
        o_ref, = rest
    ka = a_ref.shape[1]
    kb = b_ref.shape[1]
    acc = jnp.dot(a_ref[...], w_ref[:ka, :], preferred_element_type=F32)
    acc = acc + jnp.dot(b_ref[...], w_ref[ka:ka + kb, :], preferred_element_type=F32)
    acc = acc + jnp.dot(c_ref[...], w_ref[ka + kb:, :], preferred_element_type=F32)
    o_ref[...] = acc.astype(o_ref.dtype)


def out_proj(a, b, c, w, w_in_t=None, next_layer=None):
    t = a.shape[0]
    k, n = w.shape
    nj = n // TN_OUT
    steps = (t // TM) * nj
    in_specs = [pl.BlockSpec((TM, a.shape[1]), lambda i, j: (i, 0)),
                pl.BlockSpec((TM, b.shape[1]), lambda i, j: (i, 0)),
                pl.BlockSpec((TM, c.shape[1]), lambda i, j: (i, 0)),
                pl.BlockSpec((k, TN_OUT), lambda i, j: (0, j))]
    operands = (a, b, c, w)
    out_specs = pl.BlockSpec((TM, TN_OUT), lambda i, j: (i, j))
    out_shape = jax.ShapeDtypeStruct((t, n), BF16)
    if w_in_t is not None:
        _, rows, d = w_in_t.shape
        slab = d // steps
        assert slab * steps == d and slab % LANES == 0
        in_specs.append(pl.BlockSpec((None, rows, slab), lambda i, j: (next_layer, 0, i * nj + j)))
        operands += (w_in_t,)
        out_specs = (out_specs, pl.BlockSpec((IN_WIDTH_PADDED, slab), lambda i, j: (0, i * nj + j)))
        out_shape = (out_shape, jax.ShapeDtypeStruct((IN_WIDTH_PADDED, d), BF16))
    return _pallas(
        _out_proj_kernel, name="out_proj", grid=(t // TM, nj), semantics=("parallel", "arbitrary"),
        in_specs=in_specs, operands=operands, out_specs=out_specs, out_shape=out_shape,
        temp_bytes=_nbytes((TM, TN_OUT), F32))


def _ffn_kernel(h_ref, wu_ref, wd_ref, o_ref):
    @pl.when(pl.program_id(1) == 0)
    def _():
        o_ref[...] = jnp.zeros_like(o_ref)

    a = jnp.dot(h_ref[...], wu_ref[...], preferred_element_type=F32)
    a = jnp.square(jnp.maximum(a, 0.0)).astype(BF16)
    o_ref[...] += jnp.dot(a, wd_ref[...], preferred_element_type=F32)


def ffn(h, wu_b, wd_b):
    t, d = h.shape
    ff = wu_b.shape[1]
    return _pallas(
        _ffn_kernel, name="ffn", grid=(t // TM_FFN, ff // TF_FFN), semantics=("parallel", "arbitrary"),
        in_specs=[pl.BlockSpec((TM_FFN, d), lambda i, j: (i, 0)),
                  pl.BlockSpec((d, TF_FFN), lambda i, j: (0, j)),
                  pl.BlockSpec((TF_FFN, d), lambda i, j: (j, 0))],
        operands=(h, wu_b, wd_b),
        out_specs=pl.BlockSpec((TM_FFN, d), lambda i, j: (i, 0)),
        out_shape=jax.ShapeDtypeStruct((t, d), F32),
        temp_bytes=_nbytes((TM_FFN, TF_FFN), F32) + _nbytes((TM_FFN, TF_FFN), BF16))


_SRC_SIZES = (SGU_WIDTH, SGU_WIDTH, MLA_Q_RANK, MLA_KV_RANK, MLA_ROPE,
              RET_HEADS * RET_QK, RET_HEADS * RET_QK, RET_WIDTH, RET_WIDTH)
(_SRC_U, _SRC_V, _SRC_CQ, _SRC_CKV, _SRC_KR, _SRC_RQ, _SRC_RK, _SRC_RV, _SRC_RG,
 IN_WIDTH) = [int(o) for o in np.concatenate([[0], np.cumsum(_SRC_SIZES)])]
_W_IN_MOVES = (
    (COL_U, _SRC_U, 3 * IN_TILE),
    (COL_RV, _SRC_RV, RET_WIDTH),
    (COL_RG, _SRC_RG, RET_WIDTH),
    (COL_CKV, _SRC_CKV, MLA_KV_RANK),
    (COL_KR, _SRC_KR, ROPE_HALF),
    (COL_KR + LANES // 2, _SRC_KR + ROPE_HALF, ROPE_HALF),
    (COL_KR + LANES + ROPE_HALF, _SRC_KR, ROPE_HALF),
    (COL_KR + LANES + LANES // 2 + ROPE_HALF, _SRC_KR + ROPE_HALF, ROPE_HALF),
    *[(dst0 + i * ROPE_HALF, src0 + int(s0), ROPE_HALF)
      for dst0, src0 in ((COL_RQ, _SRC_RQ), (COL_RK, _SRC_RK))
      for i, s0 in enumerate(_pair_rope_order(RET_HEADS)[::ROPE_HALF])],
)


def _w_in_prep_kernel(w_ref, o_ref):
    o_ref[COL_KR:COL_RQ, :] = jnp.zeros((COL_RQ - COL_KR, o_ref.shape[1]), o_ref.dtype)
    for dst, src, width in _W_IN_MOVES:
        o_ref[dst:dst + width, :] = w_ref[src:src + width, :].astype(o_ref.dtype)


def prep_w_in(w_t, layer):
    _, n, d = w_t.shape
    return _pallas(
        _w_in_prep_kernel, name="prep_w_in", grid=(d // TK_PREP,), semantics=("parallel",),
        in_specs=[pl.BlockSpec((None, n, TK_PREP), lambda i: (layer, 0, i))],
        operands=(w_t,),
        out_specs=pl.BlockSpec((IN_WIDTH_PADDED, TK_PREP), lambda i: (0, i)),
        out_shape=jax.ShapeDtypeStruct((IN_WIDTH_PADDED, d), BF16))


def _prep_wq(w):
    k = w.shape[0]
    w3 = w.reshape(k, MLA_HEADS, MLA_NOPE + MLA_ROPE)
    nope = w3[:, :, :MLA_NOPE].reshape(k, MLA_HEADS * MLA_NOPE)
    rope = w3[:, :, MLA_NOPE:].reshape(k, MLA_HEADS * MLA_ROPE)[:, _pair_rope_order(MLA_HEADS)]
    return jnp.concatenate([nope, rope], axis=1).astype(BF16)


def _rope_lane_tables(seq):
    inv_freq = 1.0 / (ROPE_THETA ** (jnp.arange(0, MLA_ROPE, 2, dtype=F32) / MLA_ROPE))
    ang = jnp.arange(seq, dtype=F32)[:, None] * inv_freq[None, :]
    cos, sin = jnp.cos(ang), jnp.sin(ang)
    cos_t = jnp.concatenate([cos, cos, cos, cos], axis=1)
    sin_t = jnp.concatenate([-sin, -sin, sin, sin], axis=1)
    return cos_t, sin_t


def kernel(x, norm_mix_pre, norm_mix_post, norm_ffn_pre, norm_ffn_post, w_in, sgu_ln_g, sgu_ln_b, sgu_w, sgu_b, mla_q_norm, mla_wq_b, mla_kv_norm, mla_wkv_b, w_out, w_up, w_down):
    batch, seq, d = x.shape
    depth = w_in.shape[0]
    cos_t, sin_t = _rope_lane_tables(seq)
    d_tab, qd_tab, kd_tab, cd_tab = retention_tables()
    w_in_t = jnp.swapaxes(w_in, 1, 2)

    xf = x.reshape(batch * seq, d)
    h = None
    w_in_prepped = prep_w_in(w_in_t, 0)
    for l in range(depth):
        proj, proj_rope = in_proj(h, w_in_prepped, cos_t, sin_t, seq, x=xf, g=norm_mix_pre[l])
        bias_full = jnp.repeat(sgu_b[l].T, SGU_WIDTH // SGU_HEADS, axis=1)
        out_a = sgu_mixer(proj, sgu_ln_g[l], sgu_ln_b[l], sgu_w[l], bias_full)
        q = q_proj(proj, mla_q_norm[l], _prep_wq(mla_wq_b[l]), cos_t, sin_t, seq)
        kv = kv_proj(proj_rope, mla_kv_norm[l], mla_wkv_b, l)
        out_b, wu_b, wd_b, wo_b = mla_attention(q, kv, proj_rope, (w_up, w_down, w_out), l, batch, seq)
        out_c = retention_mixer(proj, proj_rope, d_tab, qd_tab, kd_tab, cd_tab, batch, seq)
        if l + 1 < depth:
            mixed, w_in_prepped = out_proj(out_a, out_b, out_c, wo_b, w_in_t, l + 1)
        else:
            mixed = out_proj(out_a, out_b, out_c, wo_b)
        h2 = mix_norm(xf, mixed, norm_mix_post[l], norm_ffn_pre[l])
        f = ffn(h2, wu_b, wd_b)
        g_next = norm_mix_pre[l + 1] if l + 1 < depth else None
        xf, h = layer_residual(xf, mixed, norm_mix_post[l], f, norm_ffn_post[l], g_next)
    return xf.reshape(batch, seq, d)
```
